```python
import math
import jax
import jax.numpy as jnp
from jax import lax
import numpy as np

D_MODEL = 4096
BATCH = 4
SEQ = 4096
DEPTH = 4

SSD_HEAD_DIM = 64
SSD_WIDTH = 3 * D_MODEL // 8
SSD_HEADS = SSD_WIDTH // SSD_HEAD_DIM
SSD_GROUPS = 4
SSD_HEADS_PER_GROUP = SSD_HEADS // SSD_GROUPS
SSD_STATE = 128
SSD_CONV = 4
SSD_CHUNK = 128
SSD_CONV_DIM = SSD_WIDTH + 2 * SSD_GROUPS * SSD_STATE
NSA_HEAD_DIM = 128
NSA_WIDTH = 3 * D_MODEL // 8
NSA_HEADS = NSA_WIDTH // NSA_HEAD_DIM
NSA_KV_HEADS = 4
NSA_GQA = NSA_HEADS // NSA_KV_HEADS
NSA_KV_WIDTH = NSA_KV_HEADS * NSA_HEAD_DIM
CMP_BLOCK = 32
CMP_STRIDE = 16
CMP_HIDDEN = 256
SEL_BLOCK = 64
SEL_TOPK = 16
SEL_LOCAL = 2
WINDOW = 512
Q_BLOCK = 128
FORCE_BONUS = 1e4
HGRN_WIDTH = D_MODEL - SSD_WIDTH - NSA_WIDTH
HGRN_HEADS = 8
HGRN_DIM = HGRN_WIDTH // HGRN_HEADS
HGRN_CHUNK = 32
REL_BUCKETS = 32
REL_MAX_DIST = 128
D_FF = 11008
FFN_CONV = 3

EPS = 1e-6
NEG_INF = -1e30

IN_SIZES = (SSD_WIDTH, SSD_CONV_DIM, SSD_HEADS,
            NSA_WIDTH,
            NSA_KV_WIDTH, NSA_KV_WIDTH,
            NSA_KV_WIDTH, NSA_KV_WIDTH,
            NSA_KV_WIDTH, NSA_KV_WIDTH,
            3 * NSA_HEADS,
            HGRN_WIDTH, HGRN_WIDTH, HGRN_WIDTH, HGRN_WIDTH)
IN_WIDTH = sum(IN_SIZES)

kernel_name = "hybrid_ssd_nsa_hgrn2_trunk"


def rms_norm(x, w):
    xf = x.astype(jnp.float32)
    y = xf * lax.rsqrt(jnp.mean(jnp.square(xf), axis=-1, keepdims=True) + EPS)
    return (y * w.astype(jnp.float32)).astype(x.dtype)


def causal_dwconv(u, w, b):
    k = w.shape[0]
    out = lax.conv_general_dilated(u, w[:, None, :].astype(u.dtype), window_strides=(1,),
                                   padding=[(k - 1, 0)], dimension_numbers=('NWC', 'WIO', 'NWC'),
                                   feature_group_count=u.shape[-1])
    return out + b.astype(u.dtype)


def rel_bucket(dist):
    max_exact = REL_BUCKETS // 2
    d = jnp.maximum(dist, 0)
    log_ratio = jnp.log(jnp.maximum(d, 1).astype(jnp.float32) / max_exact) / math.log(REL_MAX_DIST / max_exact)
    large = jnp.minimum(max_exact + (log_ratio * (REL_BUCKETS - max_exact)).astype(jnp.int32), REL_BUCKETS - 1)
    return jnp.where(d < max_exact, d, large)


def masked_softmax(s, mask):
    s = jnp.where(mask, s, NEG_INF)
    m = jnp.max(s, axis=-1, keepdims=True)
    p = jnp.where(mask, jnp.exp(s - m), 0.0)
    return p / jnp.maximum(jnp.sum(p, axis=-1, keepdims=True), 1e-30)


def ssd_mixer(z, xbc, dt_raw, conv_w, conv_b, dt_bias, a_log, d_skip, norm_w):
    f32 = jnp.float32
    bsz, seq, _ = z.shape
    nc = seq // SSD_CHUNK
    xbc = jax.nn.silu(causal_dwconv(xbc, conv_w, conv_b)).astype(f32)
    xs, bm, cm = jnp.split(xbc, [SSD_WIDTH, SSD_WIDTH + SSD_GROUPS * SSD_STATE], axis=-1)
    shp = (bsz, nc, SSD_CHUNK, SSD_GROUPS)
    xs = xs.reshape(*shp, SSD_HEADS_PER_GROUP, SSD_HEAD_DIM)
    bm = bm.reshape(*shp, SSD_STATE)
    cm = cm.reshape(*shp, SSD_STATE)
    dt = jax.nn.softplus(dt_raw.astype(f32) + dt_bias.astype(f32)).reshape(*shp, SSD_HEADS_PER_GROUP)
    a = dt * (-jnp.exp(a_log.astype(f32))).reshape(SSD_GROUPS, SSD_HEADS_PER_GROUP)
    a_cs = jnp.cumsum(a, axis=2)
    xdt = xs * dt[..., None]
    causal = np.tril(np.ones((SSD_CHUNK, SSD_CHUNK), dtype=bool))
    seg = a_cs[:, :, :, None] - a_cs[:, :, None]
    decay = jnp.exp(jnp.where(causal[None, None, :, :, None, None], seg, -jnp.inf))
    cb = jnp.einsum('bclgn,bcsgn->bclsg', cm, bm)
    y = jnp.einsum('bclsg,bclsgh,bcsghp->bclghp', cb, decay, xdt)
    decay_end = jnp.exp(a_cs[:, :, -1:] - a_cs)
    states = jnp.einsum('bclgn,bclgh,bclghp->bcghpn', bm, decay_end, xdt)
    chunk_decay = jnp.exp(a_cs[:, :, -1])

    def step(h, inp):
        st, cd = inp
        return h * cd[..., None, None] + st, h

    _, prev = lax.scan(step, jnp.zeros_like(states[:, 0]),
                       (jnp.moveaxis(states, 1, 0), jnp.moveaxis(chunk_decay, 1, 0)))
    prev = jnp.moveaxis(prev, 0, 1)
    y = y + jnp.einsum('bclgn,bcghpn,bclgh->bclghp', cm, prev, jnp.exp(a_cs))
    y = y + xs * d_skip.astype(f32).reshape(SSD_GROUPS, SSD_HEADS_PER_GROUP, 1)
    y = y.reshape(bsz, seq, SSD_GROUPS, SSD_WIDTH // SSD_GROUPS)
    u = y * jax.nn.silu(z.astype(f32)).reshape(bsz, seq, SSD_GROUPS, SSD_WIDTH // SSD_GROUPS)
    u = u * lax.rsqrt(jnp.mean(jnp.square(u), axis=-1, keepdims=True) + EPS)
    return u.reshape(bsz, seq, SSD_WIDTH) * norm_w.astype(f32)


def nsa_mixer(q, kc, vc, ks, vs, kw, vw, gate_logits, cmp_pe, cmp_w1, cmp_w2, rel_bias):
    f32 = jnp.float32
    bsz, seq, _ = q.shape
    scale = NSA_HEAD_DIM ** -0.5

    def kv(u):
        return u.reshape(bsz, seq, NSA_KV_HEADS, NSA_HEAD_DIM)

    q = q.reshape(bsz, seq, NSA_KV_HEADS, NSA_GQA, NSA_HEAD_DIM)
    gates = jax.nn.sigmoid(gate_logits.astype(f32)).reshape(bsz, seq, NSA_KV_HEADS, NSA_GQA, 3)

    n_cmp = (seq - CMP_BLOCK) // CMP_STRIDE + 1
    cmp_idx = np.arange(n_cmp)[:, None] * CMP_STRIDE + np.arange(CMP_BLOCK)[None, :]
    cmp_end = jnp.asarray(cmp_idx[:, -1], dtype=jnp.int32)

    def compress(u, pe, w1, w2):
        blocks = kv(u)[:, cmp_idx] + pe[:, None, :]
        flat = blocks.transpose(0, 1, 3, 2, 4).reshape(bsz, n_cmp, NSA_KV_HEADS, CMP_BLOCK * NSA_HEAD_DIM)
        return jax.nn.silu(flat @ w1) @ w2

    k_cmp = compress(kc, cmp_pe[0], cmp_w1[0], cmp_w2[0])
    v_cmp = compress(vc, cmp_pe[1], cmp_w1[1], cmp_w2[1])

    n_sel = seq // SEL_BLOCK
    sel_start = np.arange(n_sel)[None, :] * SEL_BLOCK
    c_start = cmp_idx[:, :1]
    overlap = np.clip(np.minimum(c_start + CMP_BLOCK, sel_start + SEL_BLOCK) - np.maximum(c_start, sel_start), 0, None)
    cmp_to_sel = jnp.asarray(overlap / CMP_BLOCK, dtype=f32)
    k_blk = kv(ks).reshape(bsz, n_sel, SEL_BLOCK, NSA_KV_HEADS, NSA_HEAD_DIM).transpose(0, 3, 1, 2, 4)
    v_blk = kv(vs).reshape(bsz, n_sel, SEL_BLOCK, NSA_KV_HEADS, NSA_HEAD_DIM).transpose(0, 3, 1, 2, 4)
    n_top = min(SEL_TOPK, n_sel)
    sel_ids = jnp.arange(n_sel)

    k_win = jnp.pad(kv(kw), ((0, 0), (WINDOW, 0), (0, 0), (0, 0)))
    v_win = jnp.pad(kv(vw), ((0, 0), (WINDOW, 0), (0, 0), (0, 0)))

    rb = rel_bias.astype(f32)
    rb_grouped = rb.reshape(REL_BUCKETS, NSA_KV_HEADS, NSA_GQA).transpose(1, 0, 2)

    def head_bias(bucket):
        return rb[bucket].transpose(2, 0, 1).reshape(NSA_KV_HEADS, NSA_GQA, *bucket.shape)

    gather_blocks = jax.vmap(jax.vmap(lambda blk, ix: blk[ix]))
    sel_bias = jax.vmap(lambda tbl, bk: tbl[bk], in_axes=(0, 1), out_axes=1)

    def block(args):
        c, qb, gb = args
        t = c * Q_BLOCK + jnp.arange(Q_BLOCK)
        s = jnp.einsum('bqhgd,bnhd->bhgqn', qb, k_cmp).astype(f32) * scale
        s = s + head_bias(rel_bucket(t[:, None] - cmp_end[None, :]))
        p_cmp = masked_softmax(s, cmp_end[None, :] <= t[:, None])
        o_cmp = jnp.einsum('bhgqn,bnhd->bqhgd', p_cmp.astype(v_cmp.dtype), v_cmp)
        imp = jnp.einsum('bhgqn,nj->bhqj', p_cmp, cmp_to_sel)
        q_blk = (t // SEL_BLOCK)[:, None]
        back = q_blk - sel_ids[None, :]
        valid = sel_ids[None, :] * SEL_BLOCK <= t[:, None]
        force = (sel_ids[None, :] == 0) | ((back >= 0) & (back < SEL_LOCAL))
        score = jnp.where(valid, imp + jnp.where(force, FORCE_BONUS, 0.0), NEG_INF)
        top_val, top_idx = lax.top_k(score, n_top)
        sel_ok = top_val > 0.5 * NEG_INF
        kg = gather_blocks(k_blk, top_idx)
        vg = gather_blocks(v_blk, top_idx)
        kpos = top_idx[..., None] * SEL_BLOCK + jnp.arange(SEL_BLOCK)
        dist = t[:, None, None] - kpos
        mask = sel_ok[..., None] & (dist >= 0)
        s = jnp.einsum('bqhgd,bhqkld->bhgqkl', qb, kg).astype(f32) * scale
        s = s + sel_bias(rb_grouped, rel_bucket(dist)).transpose(0, 1, 5, 2, 3, 4)
        sh = s.shape
        p_sel = masked_softmax(s.reshape(*sh[:4], -1),
                               mask.reshape(bsz, NSA_KV_HEADS, 1, Q_BLOCK, -1)).reshape(sh)
        o_sel = jnp.einsum('bhgqkl,bhqkld->bqhgd', p_sel.astype(vg.dtype), vg)
        kwb = lax.dynamic_slice_in_dim(k_win, c * Q_BLOCK, Q_BLOCK + WINDOW, axis=1)
        vwb = lax.dynamic_slice_in_dim(v_win, c * Q_BLOCK, Q_BLOCK + WINDOW, axis=1)
        kpos_w = c * Q_BLOCK - WINDOW + jnp.arange(Q_BLOCK + WINDOW)
        dist_w = t[:, None] - kpos_w[None, :]
        mask_w = (dist_w >= 0) & (dist_w < WINDOW) & (kpos_w[None, :] >= 0)
        s = jnp.einsum('bqhgd,bkhd->bhgqk', qb, kwb).astype(f32) * scale + head_bias(rel_bucket(dist_w))
        p_win = masked_softmax(s, mask_w)
        o_win = jnp.einsum('bhgqk,bkhd->bqhgd', p_win.astype(vwb.dtype), vwb)
        return gb[..., 0:1] * o_cmp + gb[..., 1:2] * o_sel + gb[..., 2:3] * o_win

    n_q = seq // Q_BLOCK
    q_blocks = jnp.moveaxis(q.reshape(bsz, n_q, Q_BLOCK, NSA_KV_HEADS, NSA_GQA, NSA_HEAD_DIM), 1, 0)
    g_blocks = jnp.moveaxis(gates.reshape(bsz, n_q, Q_BLOCK, NSA_KV_HEADS, NSA_GQA, 3), 1, 0)
    out = lax.map(block, (jnp.arange(n_q), q_blocks, g_blocks))
    return jnp.moveaxis(out, 0, 1).reshape(bsz, seq, NSA_WIDTH)


def hgrn2_mixer(q_raw, f_raw, i_raw, g_raw, lb, norm_w):
    f32 = jnp.float32
    bsz, seq, _ = q_raw.shape
    nc = seq // HGRN_CHUNK
    shp = (bsz, nc, HGRN_CHUNK, HGRN_HEADS, HGRN_DIM)
    lb = lb.astype(f32)
    f = lb + (1.0 - lb) * jax.nn.sigmoid(f_raw.astype(f32))
    log_f = jnp.log(f).reshape(shp)
    k = (1.0 - f).reshape(shp)
    q = jax.nn.silu(q_raw.astype(f32)).reshape(shp)
    v = i_raw.astype(f32).reshape(shp)
    b = jnp.cumsum(log_f, axis=2)
    mid = HGRN_CHUNK // 2
    b_ref = b[:, :, mid:mid + 1]
    causal = np.tril(np.ones((HGRN_CHUNK, HGRN_CHUNK), dtype=bool))
    att = jnp.einsum('bclhd,bcshd->bchls', q * jnp.exp(b - b_ref), k * jnp.exp(b_ref - b))
    att = jnp.where(causal, att, 0.0)
    o = jnp.einsum('bchls,bcshe->bclhe', att, v)
    b_last = b[:, :, -1:]
    states = jnp.einsum('bcshd,bcshe->bchde', k * jnp.exp(b_last - b), v)
    chunk_decay = jnp.exp(b_last[:, :, 0])

    def step(s, inp):
        st, cd = inp
        return s * cd[..., None] + st, s

    _, prev = lax.scan(step, jnp.zeros_like(states[:, 0]),
                       (jnp.moveaxis(states, 1, 0), jnp.moveaxis(chunk_decay, 1, 0)))
    prev = jnp.moveaxis(prev, 0, 1)
    o = o + jnp.einsum('bclhd,bchde->bclhe', q * jnp.exp(b), prev)
    o = o.reshape(bsz, seq, HGRN_HEADS, HGRN_DIM)
    o = o * lax.rsqrt(jnp.mean(jnp.square(o), axis=-1, keepdims=True) + EPS) * norm_w.astype(f32)
    return o.reshape(bsz, seq, HGRN_WIDTH) * jax.nn.silu(g_raw.astype(f32))


def conv_ffn(h, w_gu, conv_w, conv_b, w_down):
    gate, up = jnp.split(h @ w_gu, 2, axis=-1)
    gate = causal_dwconv(gate, conv_w, conv_b)
    return (jax.nn.silu(gate) * up) @ w_down


def setup_inputs(seed: int = 0) -> dict:
    key = jax.random.key(seed)
    k = jax.random.split(key, 24)
    f32 = jnp.float32

    def nrm(kk, shape, s):
        return jax.random.normal(kk, shape, f32) * s

    res = (2 * DEPTH) ** -0.5
    dt0 = jnp.exp(jax.random.uniform(k[4], (DEPTH, SSD_HEADS), f32, math.log(1e-3), math.log(1e-1)))
    dt_bias = dt0 + jnp.log(-jnp.expm1(-dt0))
    return {
        "x": nrm(k[0], (BATCH, SEQ, D_MODEL), 1.0),
        "norm_mix_w": 1.0 + nrm(k[1], (DEPTH, D_MODEL), 0.02),
        "w_in": nrm(k[2], (DEPTH, D_MODEL, IN_WIDTH), D_MODEL ** -0.5),
        "ssd_conv_w": nrm(k[3], (DEPTH, SSD_CONV, SSD_CONV_DIM), SSD_CONV ** -0.5),
        "ssd_conv_b": nrm(k[5], (DEPTH, SSD_CONV_DIM), 0.02),
        "ssd_dt_bias": dt_bias,
        "ssd_a_log": jnp.log(jax.random.uniform(k[6], (DEPTH, SSD_HEADS), f32, 1.0, 16.0)),
        "ssd_d": 1.0 + nrm(k[7], (DEPTH, SSD_HEADS), 0.1),
        "ssd_norm_w": 1.0 + nrm(k[8], (DEPTH, SSD_WIDTH), 0.02),
        "nsa_cmp_pe": nrm(k[9], (DEPTH, 2, CMP_BLOCK, NSA_HEAD_DIM), 0.1),
        "nsa_cmp_w1": nrm(k[10], (DEPTH, 2, CMP_BLOCK * NSA_HEAD_DIM, CMP_HIDDEN), (CMP_BLOCK * NSA_HEAD_DIM) ** -0.5),
        "nsa_cmp_w2": nrm(k[11], (DEPTH, 2, CMP_HIDDEN, NSA_HEAD_DIM), CMP_HIDDEN ** -0.5),
        "rel_bias": nrm(k[12], (REL_BUCKETS, NSA_HEADS), 0.5),
        "hgrn_lb_logits": nrm(k[13], (DEPTH, HGRN_WIDTH), 0.1),
        "hgrn_norm_w": 1.0 + nrm(k[14], (DEPTH, HGRN_DIM), 0.02),
        "w_out": nrm(k[15], (DEPTH, D_MODEL, D_MODEL), D_MODEL ** -0.5 * res),
        "norm_ffn_w": 1.0 + nrm(k[16], (DEPTH, D_MODEL), 0.02),
        "ffn_w_gu": nrm(k[17], (DEPTH, D_MODEL, 2 * D_FF), D_MODEL ** -0.5),
        "ffn_conv_w": nrm(k[18], (DEPTH, FFN_CONV, D_FF), FFN_CONV ** -0.5),
        "ffn_conv_b": nrm(k[19], (DEPTH, D_FF), 0.02),
        "ffn_w_down": nrm(k[20], (DEPTH, D_FF, D_MODEL), D_FF ** -0.5 * res),
        "norm_f_w": 1.0 + nrm(k[21], (D_MODEL,), 0.02),
    }


def reference(x, norm_mix_w, w_in, ssd_conv_w, ssd_conv_b, ssd_dt_bias, ssd_a_log, ssd_d, ssd_norm_w,
              nsa_cmp_pe, nsa_cmp_w1, nsa_cmp_w2, rel_bias, hgrn_lb_logits, hgrn_norm_w, w_out,
              norm_ffn_w, ffn_w_gu, ffn_conv_w, ffn_conv_b, ffn_w_down, norm_f_w):
    lb_cum = jnp.cumsum(jax.nn.softmax(hgrn_lb_logits.astype(jnp.float32), axis=0), axis=0)
    lower_bounds = lb_cum - lb_cum[0:1]
    splits = np.cumsum(IN_SIZES)[:-1].tolist()
    for l in range(DEPTH):
        n = rms_norm(x, norm_mix_w[l])
        (z, xbc, dt, q, kc, vc, k_s, v_s, kw, vw, gl, hq, hf, hi, hg) = jnp.split(n @ w_in[l], splits, axis=-1)
        y_ssd = ssd_mixer(z, xbc, dt, ssd_conv_w[l], ssd_conv_b[l], ssd_dt_bias[l], ssd_a_log[l], ssd_d[l], ssd_norm_w[l])
        y_nsa = nsa_mixer(q, kc, vc, k_s, v_s, kw, vw, gl, nsa_cmp_pe[l], nsa_cmp_w1[l], nsa_cmp_w2[l], rel_bias)
        y_hgrn = hgrn2_mixer(hq, hf, hi, hg, lower_bounds[l], hgrn_norm_w[l])
        mix = jnp.concatenate([y_ssd.astype(x.dtype), y_nsa.astype(x.dtype), y_hgrn.astype(x.dtype)], axis=-1)
        x = x + mix @ w_out[l]
        x = x + conv_ffn(rms_norm(x, norm_ffn_w[l]), ffn_w_gu[l], ffn_conv_w[l], ffn_conv_b[l], ffn_w_down[l])
    return rms_norm(x, norm_f_w)
```

```python
import functools
import math

import numpy as np
import jax
import jax.numpy as jnp
from jax import lax
from jax.experimental import pallas as pl
from jax.experimental.pallas import tpu as pltpu

F32 = jnp.float32
BF16 = jnp.bfloat16

D_MODEL = 4096
SSD_HEAD_DIM = 64
SSD_WIDTH = 1536
SSD_HEADS = 24
SSD_GROUPS = 4
SSD_HPG = 6
SSD_STATE = 128
SSD_CONV = 4
SSD_CHUNK = 128
SSD_GW = SSD_WIDTH // SSD_GROUPS
NSA_HEAD_DIM = 128
NSA_WIDTH = 1536
NSA_HEADS = 12
NSA_KV_HEADS = 4
NSA_GQA = 3
NSA_KV_WIDTH = 512
CMP_BLOCK = 32
CMP_STRIDE = 16
CMP_HIDDEN = 256
SEL_BLOCK = 64
SEL_TOPK = 16
SEL_LOCAL = 2
WINDOW = 512
Q_BLOCK = 128
FORCE_BONUS = 1e4
HGRN_WIDTH = 1024
HGRN_HEADS = 8
HGRN_DIM = 128
HGRN_CHUNK = 32
REL_BUCKETS = 32
REL_MAX_DIST = 128
D_FF = 11008
FFN_CONV = 3
EPS = 1e-6
NEG_INF = -1e30

IN_SIZES = (SSD_WIDTH, SSD_WIDTH + 2 * SSD_GROUPS * SSD_STATE, SSD_HEADS, NSA_WIDTH,
            NSA_KV_WIDTH, NSA_KV_WIDTH, NSA_KV_WIDTH, NSA_KV_WIDTH, NSA_KV_WIDTH, NSA_KV_WIDTH,
            3 * NSA_HEADS, HGRN_WIDTH, HGRN_WIDTH, HGRN_WIDTH, HGRN_WIDTH)

LANES = 128
SUBLANES = 8
VMEM_LIMIT = 56 * 1024 * 1024

COL_Z = 0
COL_XS = 1536
COL_Q = 3072
COL_B = 4608
COL_C = 5120
COL_KC = 5632
COL_VC = 6144
COL_KS = 6656
COL_VS = 7168
COL_KW = 7680
COL_VW = 8192
COL_DT = 8704
COL_GATE = 8832
COL_HQ = 9216
COL_HF = 10240
COL_HI = 11264
COL_HG = 12288
IN_PAD = 13312

N_WIN_BLOCKS = WINDOW // Q_BLOCK + 1
BIAS_FAR = N_WIN_BLOCKS


def _params(semantics):
    return pltpu.CompilerParams(dimension_semantics=semantics, vmem_limit_bytes=VMEM_LIMIT)


def _sigmoid(x):
    return 1.0 / (1.0 + jnp.exp(-x))


def _silu(x):
    return x * _sigmoid(x)


def _dot(a, b):
    return jnp.dot(a, b, preferred_element_type=F32)


def _dot_nt(a, b):
    return lax.dot_general(a, b, (((1,), (1,)), ((), ())), preferred_element_type=F32)


def _split3(x):
    hi = x.astype(BF16)
    r1 = x - hi.astype(F32)
    mid = r1.astype(BF16)
    lo = (r1 - mid.astype(F32)).astype(BF16)
    return hi, mid, lo


def _rmsnorm_kernel(x_ref, w_ref, o_ref):
    x = x_ref[...]
    ms = jnp.mean(x * x, axis=-1, keepdims=True)
    o_ref[...] = (x * lax.rsqrt(ms + EPS) * w_ref[...]).astype(o_ref.dtype)


def _rmsnorm(x, w, out_dtype, tm=256):
    m, d = x.shape
    return pl.pallas_call(
        _rmsnorm_kernel,
        grid=(m // tm,),
        in_specs=[pl.BlockSpec((tm, d), lambda i: (i, 0)), pl.BlockSpec((1, d), lambda i: (0, 0))],
        out_specs=pl.BlockSpec((tm, d), lambda i: (i, 0)),
        out_shape=jax.ShapeDtypeStruct((m, d), out_dtype),
        compiler_params=_params(("parallel",)),
        name="rmsnorm",
    )(x, w.reshape(1, d).astype(F32))


def _mm_kernel(a_ref, w_ref, o_ref):
    o_ref[...] = _dot(a_ref[...], w_ref[...]).astype(o_ref.dtype)


def _mm_res_kernel(a_ref, w_ref, r_ref, o_ref):
    o_ref[...] = r_ref[...] + _dot(a_ref[...], w_ref[...])


def _matmul(a, w, out_dtype, tm, tn, name):
    m, k = a.shape
    n = w.shape[1]
    return pl.pallas_call(
        _mm_kernel,
        grid=(m // tm, n // tn),
        in_specs=[pl.BlockSpec((tm, k), lambda i, j: (i, 0)), pl.BlockSpec((k, tn), lambda i, j: (0, j))],
        out_specs=pl.BlockSpec((tm, tn), lambda i, j: (i, j)),
        out_shape=jax.ShapeDtypeStruct((m, n), out_dtype),
        compiler_params=_params(("parallel", "arbitrary")),
        name=name,
    )(a, w)


def _matmul_residual(a, w, res, tm, tn, name):
    m, k = a.shape
    n = w.shape[1]
    return pl.pallas_call(
        _mm_res_kernel,
        grid=(m // tm, n // tn),
        in_specs=[pl.BlockSpec((tm, k), lambda i, j: (i, 0)), pl.BlockSpec((k, tn), lambda i, j: (0, j)),
                  pl.BlockSpec((tm, tn), lambda i, j: (i, j))],
        out_specs=pl.BlockSpec((tm, tn), lambda i, j: (i, j)),
        out_shape=jax.ShapeDtypeStruct((m, n), F32),
        compiler_params=_params(("parallel", "arbitrary")),
        name=name,
    )(a, w, res)


def _matmul_residual_wres(a, w, res, tm, tn, name):
    m, k = a.shape
    n = w.shape[1]
    return pl.pallas_call(
        _mm_res_kernel,
        grid=(n // tn, m // tm),
        in_specs=[pl.BlockSpec((tm, k), lambda j, i: (i, 0)),
                  pl.BlockSpec((k, tn), lambda j, i: (0, j), pipeline_mode=pl.Buffered(1)),
                  pl.BlockSpec((tm, tn), lambda j, i: (i, j))],
        out_specs=pl.BlockSpec((tm, tn), lambda j, i: (i, j)),
        out_shape=jax.ShapeDtypeStruct((m, n), F32),
        compiler_params=_params(("arbitrary", "arbitrary")),
        name=name,
    )(a, w, res)


def _gu_kernel(h_ref, w_ref, cw_ref, cb_ref, o_ref, ext_ref, halo_ref, *, tm, tn, tiles_per_seq):
    i = pl.program_id(0)
    j = pl.program_id(1)
    gu = _dot(h_ref[...], w_ref[...])
    g = gu[:, :tn]
    up = gu[:, tn:]
    prev = halo_ref[j]
    ext_ref[0:SUBLANES, :] = jnp.where(i % tiles_per_seq == 0, 0.0, prev)
    ext_ref[SUBLANES:, :] = g
    halo_ref[j] = g[tm - SUBLANES:, :]
    cw = cw_ref[...]
    acc = cb_ref[...] + cw[FFN_CONV - 1:FFN_CONV, :] * g
    for sh in range(1, FFN_CONV):
        acc = acc + cw[FFN_CONV - 1 - sh:FFN_CONV - sh, :] * ext_ref[pl.ds(SUBLANES - sh, tm), :]
    o_ref[...] = (_silu(acc) * up).astype(o_ref.dtype)


def _gate_up(h, w_gu_tiled, conv_w, conv_b, seq, tm, tn):
    m, k = h.shape
    nf = conv_w.shape[1]
    nj = nf // tn
    kern = functools.partial(_gu_kernel, tm=tm, tn=tn, tiles_per_seq=seq // tm)
    return pl.pallas_call(
        kern,
        grid=(m // tm, nj),
        in_specs=[pl.BlockSpec((tm, k), lambda i, j: (i, 0)),
                  pl.BlockSpec((k, 2 * tn), lambda i, j: (0, j)),
                  pl.BlockSpec((FFN_CONV, tn), lambda i, j: (0, j)),
                  pl.BlockSpec((1, tn), lambda i, j: (0, j))],
        out_specs=pl.BlockSpec((tm, tn), lambda i, j: (i, j)),
        out_shape=jax.ShapeDtypeStruct((m, nf), BF16),
        scratch_shapes=[pltpu.VMEM((tm + SUBLANES, tn), F32), pltpu.VMEM((nj, SUBLANES, tn), F32)],
        compiler_params=_params(("arbitrary", "arbitrary")),
        name="ffn_gate_up_conv",
    )(h, w_gu_tiled, conv_w, conv_b.reshape(1, nf))


def _ssd_kernel(z_ref, xs_ref, b_ref, c_ref, dt_ref, cwx_ref, cwb_ref, cwc_ref, cbx_ref, cbb_ref, cbc_ref,
                dtb_ref, alog_ref, dsk_ref, nw_ref, tri_ref, o_ref,
                extx_ref, extb_ref, extc_ref, state_ref, *, ts):
    s = pl.program_id(1)

    @pl.when(s == 0)
    def _():
        extx_ref[...] = jnp.zeros_like(extx_ref)
        extb_ref[...] = jnp.zeros_like(extb_ref)
        extc_ref[...] = jnp.zeros_like(extc_ref)
        state_ref[...] = jnp.zeros_like(state_ref)

    def conv_silu(u_ref, ext_ref, w_ref, bias_ref):
        ext_ref[0:SUBLANES, :] = ext_ref[ts:ts + SUBLANES, :]
        ext_ref[SUBLANES:, :] = u_ref[...]
        w = w_ref[...]
        acc = bias_ref[...] + w[SSD_CONV - 1:SSD_CONV, :] * u_ref[...]
        for sh in range(1, SSD_CONV):
            acc = acc + w[SSD_CONV - 1 - sh:SSD_CONV - sh, :] * ext_ref[pl.ds(SUBLANES - sh, ts), :]
        return _silu(acc)

    xs = conv_silu(xs_ref, extx_ref, cwx_ref, cbx_ref)
    bm = conv_silu(b_ref, extb_ref, cwb_ref, cbb_ref)
    cm = conv_silu(c_ref, extc_ref, cwc_ref, cbc_ref)
    dtr = dt_ref[...] + dtb_ref[...]
    dt = jnp.maximum(dtr, 0.0) + jnp.log1p(jnp.exp(-jnp.abs(dtr)))
    a = dt * (-jnp.exp(alog_ref[...]))
    z = z_ref[...]
    tri = tri_ref[...]
    dsk = dsk_ref[...]
    nw = nw_ref[...]
    li = lax.broadcasted_iota(jnp.int32, (SSD_CHUNK, SSD_CHUNK), 0)
    si = lax.broadcasted_iota(jnp.int32, (SSD_CHUNK, SSD_CHUNK), 1)
    causal = li >= si

    for ck in range(ts // SSD_CHUNK):
        r0 = ck * SSD_CHUNK
        a_c = a[r0:r0 + SSD_CHUNK]
        hi, mid, lo = _split3(a_c)
        acs = _dot(tri, hi) + _dot(tri, mid) + _dot(tri, lo)
        acs_t = acs.T
        dt_c = dt[r0:r0 + SSD_CHUNK]
        a_last = acs[SSD_CHUNK - 1:SSD_CHUNK, :]
        for g in range(SSD_GROUPS):
            cg = cm[r0:r0 + SSD_CHUNK, g * SSD_STATE:(g + 1) * SSD_STATE]
            bg = bm[r0:r0 + SSD_CHUNK, g * SSD_STATE:(g + 1) * SSD_STATE]
            cgb = cg.astype(BF16)
            cb = _dot_nt(cgb, bg.astype(BF16))
            bgt = bg.T.astype(BF16)
            ys = []
            for h in range(SSD_HPG):
                hh = g * SSD_HPG + h
                col = acs[:, hh:hh + 1]
                row = acs_t[hh:hh + 1, :]
                seg = col - row
                decay = jnp.where(causal, jnp.exp(jnp.where(causal, seg, 0.0)), 0.0)
                xs_h = xs[r0:r0 + SSD_CHUNK, hh * SSD_HEAD_DIM:(hh + 1) * SSD_HEAD_DIM]
                xdt = xs_h * dt_c[:, hh:hh + 1]
                y = _dot((cb * decay).astype(BF16), xdt.astype(BF16))
                al = a_last[:, hh:hh + 1]
                xw = xdt * jnp.exp(al - col)
                prev = state_ref[hh]
                y = y + _dot(cgb, prev.astype(BF16)) * jnp.exp(col)
                state_ref[hh] = prev * jnp.exp(al) + _dot(bgt, xw.astype(BF16))
                y = y + xs_h * dsk[:, hh:hh + 1]
                ys.append(y)
            yg = jnp.concatenate(ys, axis=1)
            u = yg * _silu(z[r0:r0 + SSD_CHUNK, g * SSD_GW:(g + 1) * SSD_GW])
            u = u * lax.rsqrt(jnp.mean(u * u, axis=-1, keepdims=True) + EPS)
            o_ref[r0:r0 + SSD_CHUNK, g * SSD_GW:(g + 1) * SSD_GW] = (
                u * nw[:, g * SSD_GW:(g + 1) * SSD_GW]).astype(o_ref.dtype)


def _pad_lanes(v, n=LANES):
    v = v.reshape(1, -1).astype(F32)
    return jnp.pad(v, ((0, 0), (0, n - v.shape[1])))


def _ssd_mixer(proj, bsz, seq, conv_w, conv_b, dt_bias, a_log, d_skip, norm_w, ts=256):
    nsteps = seq // ts
    row = lambda b, s: b * nsteps + s
    xw, bw, cw = conv_w[:, :SSD_WIDTH], conv_w[:, SSD_WIDTH:SSD_WIDTH + 512], conv_w[:, SSD_WIDTH + 512:]
    xb, bb, cb = conv_b[:SSD_WIDTH], conv_b[SSD_WIDTH:SSD_WIDTH + 512], conv_b[SSD_WIDTH + 512:]
    tri = jnp.asarray(np.tril(np.ones((SSD_CHUNK, SSD_CHUNK), np.float32)), BF16)
    const = lambda shape: pl.BlockSpec(shape, lambda b, s: (0,) * len(shape))
    kern = functools.partial(_ssd_kernel, ts=ts)
    return pl.pallas_call(
        kern,
        grid=(bsz, nsteps),
        in_specs=[pl.BlockSpec((ts, SSD_WIDTH), lambda b, s: (row(b, s), COL_Z // SSD_WIDTH)),
                  pl.BlockSpec((ts, SSD_WIDTH), lambda b, s: (row(b, s), COL_XS // SSD_WIDTH)),
                  pl.BlockSpec((ts, 512), lambda b, s: (row(b, s), COL_B // 512)),
                  pl.BlockSpec((ts, 512), lambda b, s: (row(b, s), COL_C // 512)),
                  pl.BlockSpec((ts, LANES), lambda b, s: (row(b, s), COL_DT // LANES)),
                  const((SSD_CONV, SSD_WIDTH)), const((SSD_CONV, 512)), const((SSD_CONV, 512)),
                  const((1, SSD_WIDTH)), const((1, 512)), const((1, 512)),
                  const((1, LANES)), const((1, LANES)), const((1, LANES)), const((1, SSD_WIDTH)),
                  const((SSD_CHUNK, SSD_CHUNK))],
        out_specs=pl.BlockSpec((ts, SSD_WIDTH), lambda b, s: (row(b, s), 0)),
        out_shape=jax.ShapeDtypeStruct((bsz * seq, SSD_WIDTH), BF16),
        scratch_shapes=[pltpu.VMEM((ts + SUBLANES, SSD_WIDTH), F32), pltpu.VMEM((ts + SUBLANES, 512), F32),
                        pltpu.VMEM((ts + SUBLANES, 512), F32),
                        pltpu.VMEM((SSD_HEADS, SSD_STATE, SSD_HEAD_DIM), F32)],
        compiler_params=_params(("arbitrary", "arbitrary")),
        name="ssd_mixer",
    )(proj, proj, proj, proj, proj, xw, bw, cw, xb.reshape(1, -1), bb.reshape(1, -1), cb.reshape(1, -1),
      _pad_lanes(dt_bias), _pad_lanes(a_log), _pad_lanes(d_skip), norm_w.reshape(1, -1), tri)


def _hgrn_kernel(q_ref, f_ref, i_ref, g_ref, lbl_ref, nw_ref, cum_ref, o_ref, state_ref, *, ts, layer):
    s = pl.program_id(1)

    @pl.when(s == 0)
    def _():
        state_ref[...] = jnp.zeros_like(state_ref)

    nck = ts // HGRN_CHUNK
    lg = lbl_ref[...]
    e = jnp.exp(lg - jnp.max(lg, axis=0, keepdims=True))
    sm = e / jnp.sum(e, axis=0, keepdims=True)
    ridx = lax.broadcasted_iota(jnp.int32, lg.shape, 0)
    lb = jnp.sum(jnp.where((ridx >= 1) & (ridx <= layer), sm, 0.0), axis=0, keepdims=True)

    cum = cum_ref[...]
    nw = nw_ref[...]
    ti = lax.broadcasted_iota(jnp.int32, (ts, ts), 0)
    tj = lax.broadcasted_iota(jnp.int32, (ts, ts), 1)
    blockcausal = (ti // HGRN_CHUNK == tj // HGRN_CHUNK) & (ti >= tj)
    tok = lax.broadcasted_iota(jnp.int32, (1, ts), 1)

    for h in range(HGRN_HEADS):
        c0 = h * HGRN_DIM
        lbh = lb[:, c0:c0 + HGRN_DIM]
        f = lbh + (1.0 - lbh) * _sigmoid(f_ref[:, c0:c0 + HGRN_DIM])
        logf = jnp.log(f)
        k = 1.0 - f
        q = _silu(q_ref[:, c0:c0 + HGRN_DIM])
        v = i_ref[:, c0:c0 + HGRN_DIM]
        hi, mid, lo = _split3(logf)
        r = _dot(cum, hi) + _dot(cum, mid) + _dot(cum, lo)
        b = r[0:ts]
        bref = r[ts:2 * ts]
        blast = r[2 * ts:3 * ts]
        qe = (q * jnp.exp(b - bref)).astype(BF16)
        ke = (k * jnp.exp(bref - b)).astype(BF16)
        kl = (k * jnp.exp(blast - b)).astype(BF16)
        qb = (q * jnp.exp(b)).astype(BF16)
        vb = v.astype(BF16)
        att = jnp.where(blockcausal, _dot_nt(qe, ke), 0.0)
        o = _dot(att.astype(BF16), vb)
        vt = v.T
        lhs = jnp.concatenate(
            [jnp.where(tok // HGRN_CHUNK == c, vt, 0.0) for c in range(nck)], axis=0).astype(BF16)
        st = _dot(lhs, kl)
        state = state_ref[h]
        outs = []
        for c in range(nck):
            t0 = c * HGRN_CHUNK
            outs.append(_dot_nt(qb[t0:t0 + HGRN_CHUNK], state.astype(BF16)))
            cd = jnp.exp(blast[t0:t0 + 1, :])
            state = state * cd + st[c * HGRN_DIM:(c + 1) * HGRN_DIM]
        state_ref[h] = state
        o = o + jnp.concatenate(outs, axis=0)
        o = o * lax.rsqrt(jnp.mean(o * o, axis=-1, keepdims=True) + EPS) * nw
        o_ref[:, c0:c0 + HGRN_DIM] = (o * _silu(g_ref[:, c0:c0 + HGRN_DIM])).astype(o_ref.dtype)


def _hgrn_cum_matrix(ts):
    t = np.arange(ts)
    same = (t[:, None] // HGRN_CHUNK) == (t[None, :] // HGRN_CHUNK)
    incl = same & (t[None, :] <= t[:, None])
    mid = same & (t[None, :] <= (t[:, None] // HGRN_CHUNK) * HGRN_CHUNK + HGRN_CHUNK // 2)
    return np.concatenate([incl, mid, same], axis=0).astype(np.float32)


def _hgrn_mixer(proj, bsz, seq, lb_logits, norm_w, layer, ts=256):
    nsteps = seq // ts
    row = lambda b, s: b * nsteps + s
    depth = lb_logits.shape[0]
    cum = jnp.asarray(_hgrn_cum_matrix(ts), BF16)
    kern = functools.partial(_hgrn_kernel, ts=ts, layer=layer)
    blk = lambda col: pl.BlockSpec((ts, HGRN_WIDTH), lambda b, s: (row(b, s), col // HGRN_WIDTH))
    return pl.pallas_call(
        kern,
        grid=(bsz, nsteps),
        in_specs=[blk(COL_HQ), blk(COL_HF), blk(COL_HI), blk(COL_HG),
                  pl.BlockSpec((depth, HGRN_WIDTH), lambda b, s: (0, 0)),
                  pl.BlockSpec((1, HGRN_DIM), lambda b, s: (0, 0)),
                  pl.BlockSpec((3 * ts, ts), lambda b, s: (0, 0))],
        out_specs=pl.BlockSpec((ts, HGRN_WIDTH), lambda b, s: (row(b, s), 0)),
        out_shape=jax.ShapeDtypeStruct((bsz * seq, HGRN_WIDTH), BF16),
        scratch_shapes=[pltpu.VMEM((HGRN_HEADS, HGRN_DIM, HGRN_DIM), F32)],
        compiler_params=_params(("arbitrary", "arbitrary")),
        name="hgrn2_mixer",
    )(proj, proj, proj, proj, lb_logits.astype(F32), norm_w.reshape(1, HGRN_DIM).astype(F32), cum)


def _rel_bucket_np(dist):
    max_exact = REL_BUCKETS // 2
    d = np.maximum(dist, 0)
    ratio = np.maximum(d, 1).astype(np.float32) / np.float32(max_exact)
    log_ratio = np.log(ratio).astype(np.float32) / np.float32(math.log(REL_MAX_DIST / max_exact))
    large = np.minimum(max_exact + (log_ratio * np.float32(REL_BUCKETS - max_exact)).astype(np.int32),
                       REL_BUCKETS - 1)
    return np.where(d < max_exact, d, large).astype(np.int32)


def _bias_expand_kernel(rb_ref, bmap_ref, o_ref):
    h = pl.program_id(0)
    bm = bmap_ref[...]
    out = jnp.full(bm.shape, NEG_INF, F32)
    for k in range(REL_BUCKETS):
        out = jnp.where(bm == k, rb_ref[k, h], out)
    o_ref[...] = out


def _bias_expand(rel_bias, bmap, tr):
    rows, cols = bmap.shape
    return pl.pallas_call(
        _bias_expand_kernel,
        grid=(NSA_HEADS, rows // tr),
        in_specs=[pl.BlockSpec(memory_space=pltpu.SMEM),
                  pl.BlockSpec((tr, cols), lambda h, i: (i, 0))],
        out_specs=pl.BlockSpec((None, tr, cols), lambda h, i: (h, i, 0)),
        out_shape=jax.ShapeDtypeStruct((NSA_HEADS, rows, cols), F32),
        compiler_params=_params(("arbitrary", "arbitrary")),
        name="nsa_bias_expand",
    )(rel_bias.astype(F32), bmap)


def _nsa_bias_tables(rel_bias, seq):
    ncp = seq // CMP_STRIDE
    t = np.arange(seq)[:, None]
    cmp_end = np.arange(ncp)[None, :] * CMP_STRIDE + CMP_BLOCK - 1
    d = t - cmp_end
    bmap_cmp = np.where(d >= 0, _rel_bucket_np(d), -1).astype(np.int32)
    l = np.arange(Q_BLOCK)[:, None]
    j = np.arange(Q_BLOCK)[None, :]
    tabs = []
    for i in range(N_WIN_BLOCKS):
        dist = l - j + Q_BLOCK * (N_WIN_BLOCKS - 1 - i)
        tabs.append(np.where((dist >= 0) & (dist < WINDOW), _rel_bucket_np(dist), -1))
    far = _rel_bucket_np(np.arange(Q_BLOCK + 1, 8 * seq))
    assert (far == far[0]).all(), "distances beyond one query block must share a single bucket"
    tabs.append(np.full((Q_BLOCK, Q_BLOCK), far[0]))
    bmap_tab = np.concatenate(tabs, axis=0).astype(np.int32)
    bias_cmp = _bias_expand(rel_bias, jnp.asarray(bmap_cmp), 512)
    bias_tab = _bias_expand(rel_bias, jnp.asarray(bmap_tab), bmap_tab.shape[0])
    return bias_cmp, bias_tab


def _cmp_kernel(u_ref, pe_ref, w1_ref, w2_ref, o_ref, *, ncp):
    half = CMP_BLOCK // 2
    pe = pe_ref[...]
    top = jnp.zeros((ncp, CMP_HIDDEN), F32)
    bot = jnp.zeros((ncp, CMP_HIDDEN), F32)
    for l in range(half):
        x = u_ref[pl.ds(l, ncp, stride=half), :]
        top = top + _dot((x + pe[l:l + 1, :]).astype(BF16), w1_ref[l])
        bot = bot + _dot((x + pe[half + l:half + l + 1, :]).astype(BF16), w1_ref[half + l])
    hid = top + pltpu.roll(bot, ncp - 1, 0)
    o_ref[...] = _dot(_silu(hid).astype(BF16), w2_ref[...])


def _nsa_compress(proj, bsz, seq, pe, w1, w2):
    ncp = seq // CMP_STRIDE
    kern = functools.partial(_cmp_kernel, ncp=ncp)
    return pl.pallas_call(
        kern,
        grid=(bsz, 2, NSA_KV_HEADS),
        in_specs=[pl.BlockSpec((seq, NSA_HEAD_DIM), lambda b, t, h: (b, COL_KC // NSA_HEAD_DIM + NSA_KV_HEADS * t + h)),
                  pl.BlockSpec((None, CMP_BLOCK, NSA_HEAD_DIM), lambda b, t, h: (t, 0, 0)),
                  pl.BlockSpec((None, CMP_BLOCK, NSA_HEAD_DIM, CMP_HIDDEN), lambda b, t, h: (t, 0, 0, 0)),
                  pl.BlockSpec((None, CMP_HIDDEN, NSA_HEAD_DIM), lambda b, t, h: (t, 0, 0))],
        out_specs=pl.BlockSpec((None, None, None, ncp, NSA_HEAD_DIM), lambda b, t, h: (b, t, h, 0, 0)),
        out_shape=jax.ShapeDtypeStruct((bsz, 2, NSA_KV_HEADS, ncp, NSA_HEAD_DIM), F32),
        compiler_params=_params(("arbitrary", "arbitrary", "arbitrary")),
        name="nsa_compress",
    )(proj, pe.astype(F32), w1.reshape(2, CMP_BLOCK, NSA_HEAD_DIM, CMP_HIDDEN).astype(BF16), w2.astype(BF16))


def _flash_step(qb, kblk, vblk, bias, carry):
    m, l, acc = carry
    s = _dot_nt(qb, kblk) + bias
    m_new = jnp.maximum(m, jnp.max(s, axis=-1, keepdims=True))
    alpha = jnp.exp(m - m_new)
    p = jnp.exp(s - m_new)
    l = alpha * l + jnp.sum(p, axis=-1, keepdims=True)
    acc = alpha * acc + _dot(p.astype(BF16), vblk)
    return m_new, l, acc


def _flash_finish(carry):
    m, l, acc = carry
    return jnp.where(m > 0.5 * NEG_INF, acc / jnp.maximum(l, 1e-30), 0.0)


def _nsa_kernel(q_ref, gate_ref, kc_ref, vc_ref, ks_ref, vs_ref, kw_ref, vw_ref, bcmp_ref, btab_ref,
                onehot_ref, c2s_ref, o_ref, kaug_ref, vsb_ref, kwb_ref, vwb_ref, *, n_sel):
    hk = pl.program_id(1)
    c = pl.program_id(2)
    rows = NSA_GQA * Q_BLOCK

    @pl.when(c == 0)
    def _():
        kaug_ref[:, 0:NSA_HEAD_DIM] = ks_ref[...].astype(BF16)
        kaug_ref[:, NSA_HEAD_DIM:] = onehot_ref[...]
        vsb_ref[...] = vs_ref[...].astype(BF16)
        kwb_ref[...] = kw_ref[...].astype(BF16)
        vwb_ref[...] = vw_ref[...].astype(BF16)

    q = q_ref[...] * (NSA_HEAD_DIM ** -0.5)
    q3 = jnp.concatenate([q[:, g * NSA_HEAD_DIM:(g + 1) * NSA_HEAD_DIM] for g in range(NSA_GQA)],
                         axis=0).astype(BF16)

    bc = bcmp_ref[...].reshape(rows, bcmp_ref.shape[-1])
    valid = bc > 0.5 * NEG_INF
    s = _dot_nt(q3, kc_ref[...].astype(BF16)) + bc
    m = jnp.max(s, axis=-1, keepdims=True)
    p = jnp.where(valid, jnp.exp(s - m), 0.0)
    p = p / jnp.maximum(jnp.sum(p, axis=-1, keepdims=True), 1e-30)
    o_cmp = _dot(p.astype(BF16), vc_ref[...].astype(BF16))

    psum = p[0:Q_BLOCK] + p[Q_BLOCK:2 * Q_BLOCK] + p[2 * Q_BLOCK:3 * Q_BLOCK]
    ph = psum.astype(BF16)
    plo = (psum - ph.astype(F32)).astype(BF16)
    c2s = c2s_ref[...]
    imp_t = _dot_nt(c2s, ph) + _dot_nt(c2s, plo)
    nsp = -(-n_sel // SUBLANES) * SUBLANES
    imp_t = imp_t[0:nsp]
    tq = c * Q_BLOCK + lax.broadcasted_iota(jnp.int32, (nsp, Q_BLOCK), 1)
    jj = lax.broadcasted_iota(jnp.int32, (nsp, Q_BLOCK), 0)
    valid_sel = (jj < n_sel) & (jj * SEL_BLOCK <= tq)
    back = tq // SEL_BLOCK - jj
    force = (jj == 0) | ((back >= 0) & (back < SEL_LOCAL))
    score = jnp.where(valid_sel, imp_t + jnp.where(force, FORCE_BONUS, 0.0), NEG_INF)
    cnt = jnp.zeros((nsp, Q_BLOCK), F32)
    for jp in range(n_sel):
        r = score[jp:jp + 1, :]
        ge = jnp.where(r >= score, 1.0, 0.0)
        gt = jnp.where(r > score, 1.0, 0.0)
        cnt = cnt + jnp.where(jj > jp, ge, gt)
    keep = valid_sel & (cnt < float(min(SEL_TOPK, n_sel)))
    selneg = jnp.where(keep, 0.0, NEG_INF)
    selneg = jnp.concatenate([selneg, jnp.full((LANES - nsp, Q_BLOCK), NEG_INF, F32)], axis=0)
    selneg_t = selneg.T.astype(BF16)
    qaug = jnp.concatenate([q3, jnp.concatenate([selneg_t] * NSA_GQA, axis=0)], axis=1)

    init = (jnp.full((rows, 1), NEG_INF, F32), jnp.zeros((rows, 1), F32), jnp.zeros((rows, NSA_HEAD_DIM), F32))

    def sel_body(kb, carry):
        off = pl.multiple_of(kb * Q_BLOCK, Q_BLOCK)
        d = c - kb
        bidx = jnp.where(d == 0, N_WIN_BLOCKS - 1, jnp.where(d == 1, N_WIN_BLOCKS - 2, BIAS_FAR))
        bias = btab_ref[:, pl.ds(pl.multiple_of(bidx * Q_BLOCK, Q_BLOCK), Q_BLOCK), :].reshape(rows, Q_BLOCK)
        return _flash_step(qaug, kaug_ref[pl.ds(off, Q_BLOCK), :], vsb_ref[pl.ds(off, Q_BLOCK), :], bias, carry)

    o_sel = _flash_finish(lax.fori_loop(0, c + 1, sel_body, init))

    carry = init
    for i in range(N_WIN_BLOCKS):
        kb = c - (N_WIN_BLOCKS - 1) + i
        off = pl.multiple_of(jnp.maximum(kb, 0) * Q_BLOCK, Q_BLOCK)
        bias = btab_ref[:, i * Q_BLOCK:(i + 1) * Q_BLOCK, :].reshape(rows, Q_BLOCK)
        bias = jnp.where(kb >= 0, bias, NEG_INF)
        carry = _flash_step(q3, kwb_ref[pl.ds(off, Q_BLOCK), :], vwb_ref[pl.ds(off, Q_BLOCK), :], bias, carry)
    o_win = _flash_finish(carry)

    gates = _sigmoid(gate_ref[...])
    lane = lax.broadcasted_iota(jnp.int32, gates.shape, 1)

    def gate_col(g, br):
        return jnp.sum(jnp.where(lane == hk * (3 * NSA_GQA) + g * 3 + br, gates, 0.0), axis=-1, keepdims=True)

    for g in range(NSA_GQA):
        r0 = g * Q_BLOCK
        out = (gate_col(g, 0) * o_cmp[r0:r0 + Q_BLOCK] + gate_col(g, 1) * o_sel[r0:r0 + Q_BLOCK]
               + gate_col(g, 2) * o_win[r0:r0 + Q_BLOCK])
        o_ref[:, g * NSA_HEAD_DIM:(g + 1) * NSA_HEAD_DIM] = out.astype(o_ref.dtype)


def _nsa_attention(proj, cmp_kv, bias_cmp, bias_tab, bsz, seq):
    nq = seq // Q_BLOCK
    ncp = seq // CMP_STRIDE
    n_sel = seq // SEL_BLOCK
    assert n_sel <= LANES and NSA_HEAD_DIM == LANES
    key = np.arange(seq)
    onehot = jnp.asarray((key[:, None] // SEL_BLOCK) == np.arange(LANES)[None, :], BF16)
    c_start = np.arange(ncp)[None, :] * CMP_STRIDE
    s_start = np.arange(LANES)[:, None] * SEL_BLOCK
    overlap = np.clip(np.minimum(c_start + CMP_BLOCK, s_start + SEL_BLOCK) - np.maximum(c_start, s_start), 0, None)
    overlap = np.where(np.arange(LANES)[:, None] < n_sel, overlap, 0)
    c2s_t = jnp.asarray(overlap / CMP_BLOCK, BF16)
    gqa_w = NSA_GQA * NSA_HEAD_DIM
    full = lambda col: pl.BlockSpec((seq, NSA_HEAD_DIM), lambda b, h, c: (b, col // NSA_HEAD_DIM + h))
    kern = functools.partial(_nsa_kernel, n_sel=n_sel)
    return pl.pallas_call(
        kern,
        grid=(bsz, NSA_KV_HEADS, nq),
        in_specs=[pl.BlockSpec((Q_BLOCK, gqa_w), lambda b, h, c: (b * nq + c, COL_Q // gqa_w + h)),
                  pl.BlockSpec((Q_BLOCK, LANES), lambda b, h, c: (b * nq + c, COL_GATE // LANES)),
                  pl.BlockSpec((None, None, None, ncp, NSA_HEAD_DIM), lambda b, h, c: (b, 0, h, 0, 0)),
                  pl.BlockSpec((None, None, None, ncp, NSA_HEAD_DIM), lambda b, h, c: (b, 1, h, 0, 0)),
                  full(COL_KS), full(COL_VS), full(COL_KW), full(COL_VW),
                  pl.BlockSpec((NSA_GQA, Q_BLOCK, ncp), lambda b, h, c: (h, c, 0)),
                  pl.BlockSpec((NSA_GQA, (N_WIN_BLOCKS + 1) * Q_BLOCK, Q_BLOCK), lambda b, h, c: (h, 0, 0)),
                  pl.BlockSpec((seq, LANES), lambda b, h, c: (0, 0)),
                  pl.BlockSpec((LANES, ncp), lambda b, h, c: (0, 0))],
        out_specs=pl.BlockSpec((Q_BLOCK, gqa_w), lambda b, h, c: (b * nq + c, h)),
        out_shape=jax.ShapeDtypeStruct((bsz * seq, NSA_WIDTH), BF16),
        scratch_shapes=[pltpu.VMEM((seq, 2 * NSA_HEAD_DIM), BF16), pltpu.VMEM((seq, NSA_HEAD_DIM), BF16),
                        pltpu.VMEM((seq, NSA_HEAD_DIM), BF16), pltpu.VMEM((seq, NSA_HEAD_DIM), BF16)],
        compiler_params=_params(("arbitrary", "arbitrary", "arbitrary")),
        name="nsa_attention",
    )(proj, proj, cmp_kv, cmp_kv, proj, proj, proj, proj, bias_cmp, bias_tab, onehot, c2s_t)


def _relayout_w_in(w):
    offs = np.cumsum((0,) + IN_SIZES)
    part = lambda i: w[:, offs[i]:offs[i + 1]]
    z, xbc, dt, q, kc, vc, ks, vs, kw, vw, gl, hq, hf, hi, hg = [part(i) for i in range(len(IN_SIZES))]
    padto = lambda a, n: jnp.pad(a, ((0, 0), (0, n - a.shape[1])))
    cols = [z, xbc[:, :SSD_WIDTH], q, xbc[:, SSD_WIDTH:SSD_WIDTH + 512], xbc[:, SSD_WIDTH + 512:],
            kc, vc, ks, vs, kw, vw, padto(dt, LANES), padto(gl, COL_HQ - COL_GATE), hq, hf, hi, hg]
    out = jnp.concatenate(cols, axis=1).astype(BF16)
    assert out.shape[1] == IN_PAD
    return out


def _tile_w_gu(w, tn):
    d = w.shape[0]
    nf = w.shape[1] // 2
    return w.reshape(d, 2, nf // tn, tn).transpose(0, 2, 1, 3).reshape(d, 2 * nf).astype(BF16)


GU_TN = 256


def kernel(x, norm_mix_w, w_in, ssd_conv_w, ssd_conv_b, ssd_dt_bias, ssd_a_log, ssd_d, ssd_norm_w, nsa_cmp_pe, nsa_cmp_w1, nsa_cmp_w2, rel_bias, hgrn_lb_logits, hgrn_norm_w, w_out, norm_ffn_w, ffn_w_gu, ffn_conv_w, ffn_conv_b, ffn_w_down, norm_f_w):
    bsz, seq, d = x.shape
    depth = w_in.shape[0]
    xr = x.reshape(bsz * seq, d).astype(F32)
    bias_cmp, bias_tab = _nsa_bias_tables(rel_bias, seq)
    for l in range(depth):
        n = _rmsnorm(xr, norm_mix_w[l], BF16)
        proj = _matmul(n, _relayout_w_in(w_in[l]), F32, tm=1024, tn=1024, name="in_proj")
        y_ssd = _ssd_mixer(proj, bsz, seq, ssd_conv_w[l].astype(F32), ssd_conv_b[l].astype(F32), ssd_dt_bias[l],
                           ssd_a_log[l], ssd_d[l], ssd_norm_w[l].astype(F32))
        cmp_kv = _nsa_compress(proj, bsz, seq, nsa_cmp_pe[l], nsa_cmp_w1[l], nsa_cmp_w2[l])
        y_nsa = _nsa_attention(proj, cmp_kv, bias_cmp, bias_tab, bsz, seq)
        y_hgrn = _hgrn_mixer(proj, bsz, seq, hgrn_lb_logits, hgrn_norm_w[l], l)
        mix = jnp.concatenate([y_ssd, y_nsa, y_hgrn], axis=1)
        xr = _matmul_residual(mix, w_out[l].astype(BF16), xr, tm=1024, tn=512, name="out_proj")
        hn = _rmsnorm(xr, norm_ffn_w[l], BF16)
        act = _gate_up(hn, _tile_w_gu(ffn_w_gu[l], GU_TN), ffn_conv_w[l].astype(F32), ffn_conv_b[l].astype(F32),
                       seq, tm=1024, tn=GU_TN)
        xr = _matmul_residual_wres(act, ffn_w_down[l].astype(BF16), xr, tm=256, tn=1024, name="ffn_down")
    out = _rmsnorm(xr, norm_f_w, x.dtype)
    return out.reshape(bsz, seq, d)
```

```python
import functools
import math

import numpy as np
import jax
import jax.numpy as jnp
from jax import lax
from jax.experimental import pallas as pl
from jax.experimental.pallas import tpu as pltpu

F32 = jnp.float32
BF16 = jnp.bfloat16

D_MODEL = 4096
SSD_HEAD_DIM = 64
SSD_WIDTH = 1536
SSD_HEADS = 24
SSD_GROUPS = 4
SSD_HPG = 6
SSD_STATE = 128
SSD_CONV = 4
SSD_CHUNK = 128
SSD_GW = SSD_WIDTH // SSD_GROUPS
NSA_HEAD_DIM = 128
NSA_WIDTH = 1536
NSA_HEADS = 12
NSA_KV_HEADS = 4
NSA_GQA = 3
NSA_KV_WIDTH = 512
CMP_BLOCK = 32
CMP_STRIDE = 16
CMP_HIDDEN = 256
SEL_BLOCK = 64
SEL_TOPK = 16
SEL_LOCAL = 2
WINDOW = 512
Q_BLOCK = 128
FORCE_BONUS = 1e4
HGRN_WIDTH = 1024
HGRN_HEADS = 8
HGRN_DIM = 128
HGRN_CHUNK = 32
REL_BUCKETS = 32
REL_MAX_DIST = 128
D_FF = 11008
FFN_CONV = 3
EPS = 1e-6
NEG_INF = -1e30

IN_SIZES = (SSD_WIDTH, SSD_WIDTH + 2 * SSD_GROUPS * SSD_STATE, SSD_HEADS, NSA_WIDTH,
            NSA_KV_WIDTH, NSA_KV_WIDTH, NSA_KV_WIDTH, NSA_KV_WIDTH, NSA_KV_WIDTH, NSA_KV_WIDTH,
            3 * NSA_HEADS, HGRN_WIDTH, HGRN_WIDTH, HGRN_WIDTH, HGRN_WIDTH)

LANES = 128
SUBLANES = 8
VMEM_LIMIT = 56 * 1024 * 1024

COL_Z = 0
COL_XS = 1536
COL_Q = 3072
COL_B = 4608
COL_C = 5120
COL_KC = 5632
COL_VC = 6144
COL_KS = 6656
COL_VS = 7168
COL_KW = 7680
COL_VW = 8192
COL_DT = 8704
COL_GATE = 8832
COL_HQ = 9216
COL_HF = 10240
COL_HI = 11264
COL_HG = 12288
IN_PAD = 13312

WIN_KEYS = WINDOW + Q_BLOCK
NEAR_KEYS = 2 * Q_BLOCK
FAR_KEYS = 4 * Q_BLOCK


def _params(semantics):
    return pltpu.CompilerParams(dimension_semantics=semantics, vmem_limit_bytes=VMEM_LIMIT)


def _sigmoid(x):
    return 1.0 / (1.0 + jnp.exp(-x))


def _silu(x):
    return x * _sigmoid(x)


def _dot(a, b):
    return jnp.dot(a, b, preferred_element_type=F32)


def _dot_nt(a, b):
    return lax.dot_general(a, b, (((1,), (1,)), ((), ())), preferred_element_type=F32)


def _split3(x):
    hi = x.astype(BF16)
    r1 = x - hi.astype(F32)
    mid = r1.astype(BF16)
    lo = (r1 - mid.astype(F32)).astype(BF16)
    return hi, mid, lo


def _rmsnorm_kernel(x_ref, w_ref, o_ref):
    x = x_ref[...]
    ms = jnp.mean(x * x, axis=-1, keepdims=True)
    o_ref[...] = (x * lax.rsqrt(ms + EPS) * w_ref[...]).astype(o_ref.dtype)


def _rmsnorm(x, w, out_dtype, tm=256):
    m, d = x.shape
    return pl.pallas_call(
        _rmsnorm_kernel,
        grid=(m // tm,),
        in_specs=[pl.BlockSpec((tm, d), lambda i: (i, 0)), pl.BlockSpec((1, d), lambda i: (0, 0))],
        out_specs=pl.BlockSpec((tm, d), lambda i: (i, 0)),
        out_shape=jax.ShapeDtypeStruct((m, d), out_dtype),
        compiler_params=_params(("parallel",)),
        name="rmsnorm",
    )(x, w.reshape(1, d).astype(F32))


def _mm_kernel(a_ref, w_ref, o_ref):
    o_ref[...] = _dot(a_ref[...], w_ref[...]).astype(o_ref.dtype)


def _mm_res_kernel(a_ref, w_ref, r_ref, o_ref):
    o_ref[...] = r_ref[...] + _dot(a_ref[...], w_ref[...])


def _matmul(a, w, layer, out_dtype, tm, tn, name):
    m, k = a.shape
    n = w.shape[2]
    return pl.pallas_call(
        _mm_kernel,
        grid=(m // tm, n // tn),
        in_specs=[pl.BlockSpec((tm, k), lambda i, j: (i, 0)),
                  pl.BlockSpec((None, k, tn), lambda i, j: (layer, 0, j))],
        out_specs=pl.BlockSpec((tm, tn), lambda i, j: (i, j)),
        out_shape=jax.ShapeDtypeStruct((m, n), out_dtype),
        compiler_params=_params(("parallel", "arbitrary")),
        name=name,
    )(a, w)


def _matmul_residual(a, w, layer, res, tm, tn, name):
    m, k = a.shape
    n = w.shape[2]
    return pl.pallas_call(
        _mm_res_kernel,
        grid=(m // tm, n // tn),
        in_specs=[pl.BlockSpec((tm, k), lambda i, j: (i, 0)),
                  pl.BlockSpec((None, k, tn), lambda i, j: (layer, 0, j)),
                  pl.BlockSpec((tm, tn), lambda i, j: (i, j))],
        out_specs=pl.BlockSpec((tm, tn), lambda i, j: (i, j)),
        out_shape=jax.ShapeDtypeStruct((m, n), F32),
        compiler_params=_params(("parallel", "arbitrary")),
        name=name,
    )(a, w, res)


def _matmul_residual_wres(a, w, layer, res, tm, tn, name):
    m, k = a.shape
    n = w.shape[2]
    return pl.pallas_call(
        _mm_res_kernel,
        grid=(n // tn, m // tm),
        in_specs=[pl.BlockSpec((tm, k), lambda j, i: (i, 0)),
                  pl.BlockSpec((None, k, tn), lambda j, i: (layer, 0, j), pipeline_mode=pl.Buffered(1)),
                  pl.BlockSpec((tm, tn), lambda j, i: (i, j))],
        out_specs=pl.BlockSpec((tm, tn), lambda j, i: (i, j)),
        out_shape=jax.ShapeDtypeStruct((m, n), F32),
        compiler_params=_params(("arbitrary", "arbitrary")),
        name=name,
    )(a, w, res)


def _gu_kernel(h_ref, wg_ref, wu_ref, cw_ref, cb_ref, o_ref, ext_ref, halo_ref, *, tm, tn, tiles_per_seq):
    i = pl.program_id(0)
    j = pl.program_id(1)
    h = h_ref[...]
    g = _dot(h, wg_ref[...])
    up = _dot(h, wu_ref[...])
    prev = halo_ref[j]
    ext_ref[0:SUBLANES, :] = jnp.where(i % tiles_per_seq == 0, 0.0, prev)
    ext_ref[SUBLANES:, :] = g
    halo_ref[j] = g[tm - SUBLANES:, :]
    cw = cw_ref[...]
    acc = cb_ref[...] + cw[FFN_CONV - 1:FFN_CONV, :] * g
    for sh in range(1, FFN_CONV):
        acc = acc + cw[FFN_CONV - 1 - sh:FFN_CONV - sh, :] * ext_ref[pl.ds(SUBLANES - sh, tm), :]
    o_ref[...] = (_silu(acc) * up).astype(o_ref.dtype)


def _gate_up(h, w_gu, layer, conv_w, conv_b, seq, tm, tn):
    m, k = h.shape
    nf = conv_w.shape[1]
    nj = nf // tn
    kern = functools.partial(_gu_kernel, tm=tm, tn=tn, tiles_per_seq=seq // tm)
    return pl.pallas_call(
        kern,
        grid=(m // tm, nj),
        in_specs=[pl.BlockSpec((tm, k), lambda i, j: (i, 0)),
                  pl.BlockSpec((None, k, tn), lambda i, j: (layer, 0, j)),
                  pl.BlockSpec((None, k, tn), lambda i, j: (layer, 0, j + nj)),
                  pl.BlockSpec((FFN_CONV, tn), lambda i, j: (0, j)),
                  pl.BlockSpec((1, tn), lambda i, j: (0, j))],
        out_specs=pl.BlockSpec((tm, tn), lambda i, j: (i, j)),
        out_shape=jax.ShapeDtypeStruct((m, nf), BF16),
        scratch_shapes=[pltpu.VMEM((tm + SUBLANES, tn), F32), pltpu.VMEM((nj, SUBLANES, tn), F32)],
        compiler_params=_params(("arbitrary", "arbitrary")),
        name="ffn_gate_up_conv",
    )(h, w_gu, w_gu, conv_w, conv_b.reshape(1, nf))


def _ssd_kernel(z_ref, xs_ref, b_ref, c_ref, dt_ref, cwx_ref, cwb_ref, cwc_ref, cbx_ref, cbb_ref, cbc_ref,
                dtb_ref, alog_ref, dsk_ref, nw_ref, tri_ref, o_ref,
                extx_ref, extb_ref, extc_ref, state_ref, *, ts):
    s = pl.program_id(1)

    @pl.when(s == 0)
    def _():
        extx_ref[...] = jnp.zeros_like(extx_ref)
        extb_ref[...] = jnp.zeros_like(extb_ref)
        extc_ref[...] = jnp.zeros_like(extc_ref)
        state_ref[...] = jnp.zeros_like(state_ref)

    def conv_silu(u_ref, ext_ref, w_ref, bias_ref):
        ext_ref[0:SUBLANES, :] = ext_ref[ts:ts + SUBLANES, :]
        ext_ref[SUBLANES:, :] = u_ref[...]
        w = w_ref[...]
        acc = bias_ref[...] + w[SSD_CONV - 1:SSD_CONV, :] * u_ref[...]
        for sh in range(1, SSD_CONV):
            acc = acc + w[SSD_CONV - 1 - sh:SSD_CONV - sh, :] * ext_ref[pl.ds(SUBLANES - sh, ts), :]
        return _silu(acc)

    xs = conv_silu(xs_ref, extx_ref, cwx_ref, cbx_ref)
    bm = conv_silu(b_ref, extb_ref, cwb_ref, cbb_ref)
    cm = conv_silu(c_ref, extc_ref, cwc_ref, cbc_ref)
    dtr = dt_ref[...] + dtb_ref[...]
    dt = jnp.maximum(dtr, 0.0) + jnp.log1p(jnp.exp(-jnp.abs(dtr)))
    a = dt * (-jnp.exp(alog_ref[...]))
    z = z_ref[...]
    tri = tri_ref[...]
    dsk = dsk_ref[...]
    nw = nw_ref[...]
    li = lax.broadcasted_iota(jnp.int32, (SSD_CHUNK, SSD_CHUNK), 0)
    si = lax.broadcasted_iota(jnp.int32, (SSD_CHUNK, SSD_CHUNK), 1)
    causal = li >= si

    for ck in range(ts // SSD_CHUNK):
        r0 = ck * SSD_CHUNK
        a_c = a[r0:r0 + SSD_CHUNK]
        hi, mid, lo = _split3(a_c)
        acs = _dot(tri, hi) + _dot(tri, mid) + _dot(tri, lo)
        acs_t = acs.T
        dt_c = dt[r0:r0 + SSD_CHUNK]
        a_last = acs[SSD_CHUNK - 1:SSD_CHUNK, :]
        for g in range(SSD_GROUPS):
            cg = cm[r0:r0 + SSD_CHUNK, g * SSD_STATE:(g + 1) * SSD_STATE]
            bg = bm[r0:r0 + SSD_CHUNK, g * SSD_STATE:(g + 1) * SSD_STATE]
            cgb = cg.astype(BF16)
            cb = _dot_nt(cgb, bg.astype(BF16))
            bgt = bg.T.astype(BF16)
            ys = []
            for h in range(SSD_HPG):
                hh = g * SSD_HPG + h
                col = acs[:, hh:hh + 1]
                row = acs_t[hh:hh + 1, :]
                seg = col - row
                decay = jnp.where(causal, jnp.exp(jnp.where(causal, seg, 0.0)), 0.0)
                xs_h = xs[r0:r0 + SSD_CHUNK, hh * SSD_HEAD_DIM:(hh + 1) * SSD_HEAD_DIM]
                xdt = xs_h * dt_c[:, hh:hh + 1]
                y = _dot((cb * decay).astype(BF16), xdt.astype(BF16))
                al = a_last[:, hh:hh + 1]
                xw = xdt * jnp.exp(al - col)
                prev = state_ref[hh]
                y = y + _dot(cgb, prev.astype(BF16)) * jnp.exp(col)
                state_ref[hh] = prev * jnp.exp(al) + _dot(bgt, xw.astype(BF16))
                y = y + xs_h * dsk[:, hh:hh + 1]
                ys.append(y)
            yg = jnp.concatenate(ys, axis=1)
            u = yg * _silu(z[r0:r0 + SSD_CHUNK, g * SSD_GW:(g + 1) * SSD_GW])
            u = u * lax.rsqrt(jnp.mean(u * u, axis=-1, keepdims=True) + EPS)
            o_ref[r0:r0 + SSD_CHUNK, g * SSD_GW:(g + 1) * SSD_GW] = (
                u * nw[:, g * SSD_GW:(g + 1) * SSD_GW]).astype(o_ref.dtype)


def _pad_lanes(v, n=LANES):
    v = v.reshape(1, -1).astype(F32)
    return jnp.pad(v, ((0, 0), (0, n - v.shape[1])))


def _ssd_mixer(proj, bsz, seq, conv_w, conv_b, dt_bias, a_log, d_skip, norm_w, ts=256):
    nsteps = seq // ts
    row = lambda b, s: b * nsteps + s
    xw, bw, cw = conv_w[:, :SSD_WIDTH], conv_w[:, SSD_WIDTH:SSD_WIDTH + 512], conv_w[:, SSD_WIDTH + 512:]
    xb, bb, cb = conv_b[:SSD_WIDTH], conv_b[SSD_WIDTH:SSD_WIDTH + 512], conv_b[SSD_WIDTH + 512:]
    tri = jnp.asarray(np.tril(np.ones((SSD_CHUNK, SSD_CHUNK), np.float32)), BF16)
    const = lambda shape: pl.BlockSpec(shape, lambda b, s: (0,) * len(shape))
    kern = functools.partial(_ssd_kernel, ts=ts)
    return pl.pallas_call(
        kern,
        grid=(bsz, nsteps),
        in_specs=[pl.BlockSpec((ts, SSD_WIDTH), lambda b, s: (row(b, s), COL_Z // SSD_WIDTH)),
                  pl.BlockSpec((ts, SSD_WIDTH), lambda b, s: (row(b, s), COL_XS // SSD_WIDTH)),
                  pl.BlockSpec((ts, 512), lambda b, s: (row(b, s), COL_B // 512)),
                  pl.BlockSpec((ts, 512), lambda b, s: (row(b, s), COL_C // 512)),
                  pl.BlockSpec((ts, LANES), lambda b, s: (row(b, s), COL_DT // LANES)),
                  const((SSD_CONV, SSD_WIDTH)), const((SSD_CONV, 512)), const((SSD_CONV, 512)),
                  const((1, SSD_WIDTH)), const((1, 512)), const((1, 512)),
                  const((1, LANES)), const((1, LANES)), const((1, LANES)), const((1, SSD_WIDTH)),
                  const((SSD_CHUNK, SSD_CHUNK))],
        out_specs=pl.BlockSpec((ts, SSD_WIDTH), lambda b, s: (row(b, s), 0)),
        out_shape=jax.ShapeDtypeStruct((bsz * seq, SSD_WIDTH), BF16),
        scratch_shapes=[pltpu.VMEM((ts + SUBLANES, SSD_WIDTH), F32), pltpu.VMEM((ts + SUBLANES, 512), F32),
                        pltpu.VMEM((ts + SUBLANES, 512), F32),
                        pltpu.VMEM((SSD_HEADS, SSD_STATE, SSD_HEAD_DIM), F32)],
        compiler_params=_params(("arbitrary", "arbitrary")),
        name="ssd_mixer",
    )(proj, proj, proj, proj, proj, xw, bw, cw, xb.reshape(1, -1), bb.reshape(1, -1), cb.reshape(1, -1),
      _pad_lanes(dt_bias), _pad_lanes(a_log), _pad_lanes(d_skip), norm_w.reshape(1, -1), tri)


def _hgrn_kernel(q_ref, f_ref, i_ref, g_ref, lbl_ref, nw_ref, cum_ref, o_ref, state_ref, *, ts, layer):
    s = pl.program_id(1)

    @pl.when(s == 0)
    def _():
        state_ref[...] = jnp.zeros_like(state_ref)

    nck = ts // HGRN_CHUNK
    lg = lbl_ref[...]
    e = jnp.exp(lg - jnp.max(lg, axis=0, keepdims=True))
    sm = e / jnp.sum(e, axis=0, keepdims=True)
    ridx = lax.broadcasted_iota(jnp.int32, lg.shape, 0)
    lb = jnp.sum(jnp.where((ridx >= 1) & (ridx <= layer), sm, 0.0), axis=0, keepdims=True)

    cum = cum_ref[...]
    nw = nw_ref[...]
    ti = lax.broadcasted_iota(jnp.int32, (ts, ts), 0)
    tj = lax.broadcasted_iota(jnp.int32, (ts, ts), 1)
    blockcausal = (ti // HGRN_CHUNK == tj // HGRN_CHUNK) & (ti >= tj)
    tok = lax.broadcasted_iota(jnp.int32, (1, ts), 1)

    for h in range(HGRN_HEADS):
        c0 = h * HGRN_DIM
        lbh = lb[:, c0:c0 + HGRN_DIM]
        f = lbh + (1.0 - lbh) * _sigmoid(f_ref[:, c0:c0 + HGRN_DIM])
        logf = jnp.log(f)
        k = 1.0 - f
        q = _silu(q_ref[:, c0:c0 + HGRN_DIM])
        v = i_ref[:, c0:c0 + HGRN_DIM]
        hi, mid, lo = _split3(logf)
        r = _dot(cum, hi) + _dot(cum, mid) + _dot(cum, lo)
        b = r[0:ts]
        bref = r[ts:2 * ts]
        blast = r[2 * ts:3 * ts]
        qe = (q * jnp.exp(b - bref)).astype(BF16)
        ke = (k * jnp.exp(bref - b)).astype(BF16)
        kl = (k * jnp.exp(blast - b)).astype(BF16)
        qb = (q * jnp.exp(b)).astype(BF16)
        vb = v.astype(BF16)
        att = jnp.where(blockcausal, _dot_nt(qe, ke), 0.0)
        o = _dot(att.astype(BF16), vb)
        vt = v.T
        lhs = jnp.concatenate(
            [jnp.where(tok // HGRN_CHUNK == c, vt, 0.0) for c in range(nck)], axis=0).astype(BF16)
        st = _dot(lhs, kl)
        state = state_ref[h]
        outs = []
        for c in range(nck):
            t0 = c * HGRN_CHUNK
            outs.append(_dot_nt(qb[t0:t0 + HGRN_CHUNK], state.astype(BF16)))
            cd = jnp.exp(blast[t0:t0 + 1, :])
            state = state * cd + st[c * HGRN_DIM:(c + 1) * HGRN_DIM]
        state_ref[h] = state
        o = o + jnp.concatenate(outs, axis=0)
        o = o * lax.rsqrt(jnp.mean(o * o, axis=-1, keepdims=True) + EPS) * nw
        o_ref[:, c0:c0 + HGRN_DIM] = (o * _silu(g_ref[:, c0:c0 + HGRN_DIM])).astype(o_ref.dtype)


def _hgrn_cum_matrix(ts):
    t = np.arange(ts)
    same = (t[:, None] // HGRN_CHUNK) == (t[None, :] // HGRN_CHUNK)
    incl = same & (t[None, :] <= t[:, None])
    mid = same & (t[None, :] <= (t[:, None] // HGRN_CHUNK) * HGRN_CHUNK + HGRN_CHUNK // 2)
    return np.concatenate([incl, mid, same], axis=0).astype(np.float32)


def _hgrn_mixer(proj, bsz, seq, lb_logits, norm_w, layer, ts=256):
    nsteps = seq // ts
    row = lambda b, s: b * nsteps + s
    depth = lb_logits.shape[0]
    cum = jnp.asarray(_hgrn_cum_matrix(ts), BF16)
    kern = functools.partial(_hgrn_kernel, ts=ts, layer=layer)
    blk = lambda col: pl.BlockSpec((ts, HGRN_WIDTH), lambda b, s: (row(b, s), col // HGRN_WIDTH))
    return pl.pallas_call(
        kern,
        grid=(bsz, nsteps),
        in_specs=[blk(COL_HQ), blk(COL_HF), blk(COL_HI), blk(COL_HG),
                  pl.BlockSpec((depth, HGRN_WIDTH), lambda b, s: (0, 0)),
                  pl.BlockSpec((1, HGRN_DIM), lambda b, s: (0, 0)),
                  pl.BlockSpec((3 * ts, ts), lambda b, s: (0, 0))],
        out_specs=pl.BlockSpec((ts, HGRN_WIDTH), lambda b, s: (row(b, s), 0)),
        out_shape=jax.ShapeDtypeStruct((bsz * seq, HGRN_WIDTH), BF16),
        scratch_shapes=[pltpu.VMEM((HGRN_HEADS, HGRN_DIM, HGRN_DIM), F32)],
        compiler_params=_params(("arbitrary", "arbitrary")),
        name="hgrn2_mixer",
    )(proj, proj, proj, proj, lb_logits.astype(F32), norm_w.reshape(1, HGRN_DIM).astype(F32), cum)


def _rel_bucket_np(dist):
    max_exact = REL_BUCKETS // 2
    d = np.maximum(dist, 0)
    ratio = np.maximum(d, 1).astype(np.float32) / np.float32(max_exact)
    log_ratio = np.log(ratio).astype(np.float32) / np.float32(math.log(REL_MAX_DIST / max_exact))
    large = np.minimum(max_exact + (log_ratio * np.float32(REL_BUCKETS - max_exact)).astype(np.int32),
                       REL_BUCKETS - 1)
    return np.where(d < max_exact, d, large).astype(np.int32)


def _bias_expand_kernel(rb_ref, bmap_ref, o_ref):
    h = pl.program_id(0)
    bm = bmap_ref[...]
    out = jnp.full(bm.shape, NEG_INF, F32)
    for k in range(REL_BUCKETS):
        out = jnp.where(bm == k, rb_ref[k, h], out)
    o_ref[...] = out


def _bias_expand(rel_bias, bmap, tr):
    rows, cols = bmap.shape
    return pl.pallas_call(
        _bias_expand_kernel,
        grid=(NSA_HEADS, rows // tr),
        in_specs=[pl.BlockSpec(memory_space=pltpu.SMEM),
                  pl.BlockSpec((tr, cols), lambda h, i: (i, 0))],
        out_specs=pl.BlockSpec((None, tr, cols), lambda h, i: (h, i, 0)),
        out_shape=jax.ShapeDtypeStruct((NSA_HEADS, rows, cols), F32),
        compiler_params=_params(("arbitrary", "arbitrary")),
        name="nsa_bias_expand",
    )(rel_bias.astype(F32), bmap)


def _nsa_bias_tables(rel_bias, seq):
    ncp = seq // CMP_STRIDE
    t = np.arange(seq)[:, None]
    cmp_end = np.arange(ncp)[None, :] * CMP_STRIDE + CMP_BLOCK - 1
    d = t - cmp_end
    bmap_cmp = np.where(d >= 0, _rel_bucket_np(d), -1).astype(np.int32)
    l = np.arange(Q_BLOCK)[:, None]
    j = np.arange(WIN_KEYS)[None, :]
    dist = l - j + WINDOW
    win = np.where((dist >= 0) & (dist < WINDOW), _rel_bucket_np(dist), -1)
    far = _rel_bucket_np(np.arange(Q_BLOCK + 1, 8 * seq))
    assert (far == far[0]).all(), "distances beyond one query block must share a single bucket"
    bmap_tab = np.concatenate([win, np.full((Q_BLOCK, Q_BLOCK), far[0])], axis=1).astype(np.int32)
    bias_cmp = _bias_expand(rel_bias, jnp.asarray(bmap_cmp), 512)
    bias_tab = _bias_expand(rel_bias, jnp.asarray(bmap_tab), Q_BLOCK)
    return bias_cmp, bias_tab


def _cmp_kernel(u_ref, pe_ref, w1_ref, w2_ref, o_ref, *, ncp):
    half = CMP_BLOCK // 2
    pe = pe_ref[...]
    top = jnp.zeros((ncp, CMP_HIDDEN), F32)
    bot = jnp.zeros((ncp, CMP_HIDDEN), F32)
    for l in range(half):
        x = u_ref[pl.ds(l, ncp, stride=half), :]
        top = top + _dot((x + pe[l:l + 1, :]).astype(BF16), w1_ref[l])
        bot = bot + _dot((x + pe[half + l:half + l + 1, :]).astype(BF16), w1_ref[half + l])
    hid = top + pltpu.roll(bot, ncp - 1, 0)
    o_ref[...] = _dot(_silu(hid).astype(BF16), w2_ref[...])


def _nsa_compress(proj, bsz, seq, pe, w1, w2):
    ncp = seq // CMP_STRIDE
    kern = functools.partial(_cmp_kernel, ncp=ncp)
    return pl.pallas_call(
        kern,
        grid=(bsz, 2, NSA_KV_HEADS),
        in_specs=[pl.BlockSpec((seq, NSA_HEAD_DIM), lambda b, t, h: (b, COL_KC // NSA_HEAD_DIM + NSA_KV_HEADS * t + h)),
                  pl.BlockSpec((None, CMP_BLOCK, NSA_HEAD_DIM), lambda b, t, h: (t, 0, 0)),
                  pl.BlockSpec((None, CMP_BLOCK, NSA_HEAD_DIM, CMP_HIDDEN), lambda b, t, h: (t, 0, 0, 0)),
                  pl.BlockSpec((None, CMP_HIDDEN, NSA_HEAD_DIM), lambda b, t, h: (t, 0, 0))],
        out_specs=pl.BlockSpec((None, None, None, ncp, NSA_HEAD_DIM), lambda b, t, h: (b, t, h, 0, 0)),
        out_shape=jax.ShapeDtypeStruct((bsz, 2, NSA_KV_HEADS, ncp, NSA_HEAD_DIM), F32),
        compiler_params=_params(("arbitrary", "arbitrary", "arbitrary")),
        name="nsa_compress",
    )(proj, pe.astype(F32), w1.reshape(2, CMP_BLOCK, NSA_HEAD_DIM, CMP_HIDDEN).astype(BF16), w2.astype(BF16))


def _softmax_start(s, v):
    m = jnp.max(s, axis=-1, keepdims=True)
    p = jnp.exp(s - m)
    return m, jnp.sum(p, axis=-1, keepdims=True), _dot(p.astype(BF16), v)


def _softmax_update(s, v, carry):
    m, l, acc = carry
    m_new = jnp.maximum(m, jnp.max(s, axis=-1, keepdims=True))
    alpha = jnp.exp(m - m_new)
    p = jnp.exp(s - m_new)
    return m_new, alpha * l + jnp.sum(p, axis=-1, keepdims=True), alpha * acc + _dot(p.astype(BF16), v)


def _softmax_finish(carry):
    m, l, acc = carry
    return jnp.where(m > 0.5 * NEG_INF, acc / jnp.maximum(l, 1e-30), 0.0)


def _nsa_kernel(q_ref, gate_ref, kc_ref, vc_ref, ks_ref, vs_ref, kw_ref, vw_ref, bcmp_ref, btab_ref,
                onehot_ref, c2s_ref, o_ref, kaug_ref, vsb_ref, kwb_ref, vwb_ref, *, n_sel):
    hk = pl.program_id(1)
    c = pl.program_id(2)
    rows = NSA_GQA * Q_BLOCK

    @pl.when(c == 0)
    def _():
        kaug_ref[0:Q_BLOCK, :] = jnp.zeros((Q_BLOCK, 2 * NSA_HEAD_DIM), BF16)
        kaug_ref[Q_BLOCK:, 0:NSA_HEAD_DIM] = ks_ref[...].astype(BF16)
        kaug_ref[Q_BLOCK:, NSA_HEAD_DIM:] = onehot_ref[...]
        vsb_ref[0:Q_BLOCK, :] = jnp.zeros((Q_BLOCK, NSA_HEAD_DIM), BF16)
        vsb_ref[Q_BLOCK:, :] = vs_ref[...].astype(BF16)
        kwb_ref[0:WINDOW, :] = jnp.zeros((WINDOW, NSA_HEAD_DIM), BF16)
        kwb_ref[WINDOW:, :] = kw_ref[...].astype(BF16)
        vwb_ref[0:WINDOW, :] = jnp.zeros((WINDOW, NSA_HEAD_DIM), BF16)
        vwb_ref[WINDOW:, :] = vw_ref[...].astype(BF16)

    q = q_ref[...] * (NSA_HEAD_DIM ** -0.5)
    q3 = jnp.concatenate([q[:, g * NSA_HEAD_DIM:(g + 1) * NSA_HEAD_DIM] for g in range(NSA_GQA)],
                         axis=0).astype(BF16)

    bc = bcmp_ref[...].reshape(rows, bcmp_ref.shape[-1])
    valid = bc > 0.5 * NEG_INF
    s = _dot_nt(q3, kc_ref[...].astype(BF16)) + bc
    m = jnp.max(s, axis=-1, keepdims=True)
    p = jnp.where(valid, jnp.exp(s - m), 0.0)
    p = p / jnp.maximum(jnp.sum(p, axis=-1, keepdims=True), 1e-30)
    o_cmp = _dot(p.astype(BF16), vc_ref[...].astype(BF16))

    psum = p[0:Q_BLOCK] + p[Q_BLOCK:2 * Q_BLOCK] + p[2 * Q_BLOCK:3 * Q_BLOCK]
    ph = psum.astype(BF16)
    plo = (psum - ph.astype(F32)).astype(BF16)
    c2s = c2s_ref[...]
    imp_t = _dot_nt(c2s, ph) + _dot_nt(c2s, plo)
    nsp = -(-n_sel // SUBLANES) * SUBLANES
    imp_t = imp_t[0:nsp]
    tq = c * Q_BLOCK + lax.broadcasted_iota(jnp.int32, (nsp, Q_BLOCK), 1)
    jj = lax.broadcasted_iota(jnp.int32, (nsp, Q_BLOCK), 0)
    valid_sel = (jj < n_sel) & (jj * SEL_BLOCK <= tq)
    back = tq // SEL_BLOCK - jj
    force = (jj == 0) | ((back >= 0) & (back < SEL_LOCAL))
    score = jnp.where(valid_sel, imp_t + jnp.where(force, FORCE_BONUS, 0.0), NEG_INF)
    cnt = jnp.zeros((nsp, Q_BLOCK), F32)
    for jp in range(n_sel):
        r = score[jp:jp + 1, :]
        ge = jnp.where(r >= score, 1.0, 0.0)
        gt = jnp.where(r > score, 1.0, 0.0)
        cnt = cnt + jnp.where(jj > jp, ge, gt)
    keep = valid_sel & (cnt < float(min(SEL_TOPK, n_sel)))
    selneg = jnp.where(keep, 0.0, NEG_INF)
    selneg = jnp.concatenate([selneg, jnp.full((LANES - nsp, Q_BLOCK), NEG_INF, F32)], axis=0)
    selneg_t = selneg.T.astype(BF16)
    qaug = jnp.concatenate([q3, jnp.concatenate([selneg_t] * NSA_GQA, axis=0)], axis=1)

    btab = btab_ref[...].reshape(rows, WIN_KEYS + Q_BLOCK)
    bias_win = btab[:, 0:WIN_KEYS]

    near_col = lax.broadcasted_iota(jnp.int32, (rows, NEAR_KEYS), 1)
    bias_near = jnp.where(near_col >= Q_BLOCK - c * Q_BLOCK, bias_win[:, WIN_KEYS - NEAR_KEYS:], NEG_INF)
    near0 = pl.multiple_of(c * Q_BLOCK, Q_BLOCK)
    s = _dot_nt(qaug, kaug_ref[pl.ds(near0, NEAR_KEYS), :]) + bias_near
    carry = _softmax_start(s, vsb_ref[pl.ds(near0, NEAR_KEYS), :])

    far_blocks = FAR_KEYS // Q_BLOCK
    bias_far = jnp.concatenate([btab[:, WIN_KEYS:]] * far_blocks, axis=1)
    far_col = lax.broadcasted_iota(jnp.int32, (rows, FAR_KEYS), 1)
    n_far = c - 1

    def far_body(it, carry):
        r0 = pl.multiple_of(it * FAR_KEYS + Q_BLOCK, Q_BLOCK)
        bias = jnp.where(far_col < (n_far - it * far_blocks) * Q_BLOCK, bias_far, NEG_INF)
        s = _dot_nt(qaug, kaug_ref[pl.ds(r0, FAR_KEYS), :]) + bias
        return _softmax_update(s, vsb_ref[pl.ds(r0, FAR_KEYS), :], carry)

    o_sel = _softmax_finish(lax.fori_loop(0, (n_far + far_blocks - 1) // far_blocks, far_body, carry))

    win_col = lax.broadcasted_iota(jnp.int32, (rows, WIN_KEYS), 1)
    bias_w = jnp.where(win_col >= WINDOW - c * Q_BLOCK, bias_win, NEG_INF)
    win0 = pl.multiple_of(c * Q_BLOCK, Q_BLOCK)
    s = _dot_nt(q3, kwb_ref[pl.ds(win0, WIN_KEYS), :]) + bias_w
    o_win = _softmax_finish(_softmax_start(s, vwb_ref[pl.ds(win0, WIN_KEYS), :]))

    gates = _sigmoid(gate_ref[...])
    lane = lax.broadcasted_iota(jnp.int32, gates.shape, 1)

    def gate_col(g, br):
        return jnp.sum(jnp.where(lane == hk * (3 * NSA_GQA) + g * 3 + br, gates, 0.0), axis=-1, keepdims=True)

    for g in range(NSA_GQA):
        r0 = g * Q_BLOCK
        out = (gate_col(g, 0) * o_cmp[r0:r0 + Q_BLOCK] + gate_col(g, 1) * o_sel[r0:r0 + Q_BLOCK]
               + gate_col(g, 2) * o_win[r0:r0 + Q_BLOCK])
        o_ref[:, g * NSA_HEAD_DIM:(g + 1) * NSA_HEAD_DIM] = out.astype(o_ref.dtype)


def _nsa_attention(proj, cmp_kv, bias_cmp, bias_tab, bsz, seq):
    nq = seq // Q_BLOCK
    ncp = seq // CMP_STRIDE
    n_sel = seq // SEL_BLOCK
    assert n_sel <= LANES and NSA_HEAD_DIM == LANES
    key = np.arange(seq)
    onehot = jnp.asarray((key[:, None] // SEL_BLOCK) == np.arange(LANES)[None, :], BF16)
    c_start = np.arange(ncp)[None, :] * CMP_STRIDE
    s_start = np.arange(LANES)[:, None] * SEL_BLOCK
    overlap = np.clip(np.minimum(c_start + CMP_BLOCK, s_start + SEL_BLOCK) - np.maximum(c_start, s_start), 0, None)
    overlap = np.where(np.arange(LANES)[:, None] < n_sel, overlap, 0)
    c2s_t = jnp.asarray(overlap / CMP_BLOCK, BF16)
    gqa_w = NSA_GQA * NSA_HEAD_DIM
    full = lambda col: pl.BlockSpec((seq, NSA_HEAD_DIM), lambda b, h, c: (b, col // NSA_HEAD_DIM + h))
    kern = functools.partial(_nsa_kernel, n_sel=n_sel)
    return pl.pallas_call(
        kern,
        grid=(bsz, NSA_KV_HEADS, nq),
        in_specs=[pl.BlockSpec((Q_BLOCK, gqa_w), lambda b, h, c: (b * nq + c, COL_Q // gqa_w + h)),
                  pl.BlockSpec((Q_BLOCK, LANES), lambda b, h, c: (b * nq + c, COL_GATE // LANES)),
                  pl.BlockSpec((None, None, None, ncp, NSA_HEAD_DIM), lambda b, h, c: (b, 0, h, 0, 0)),
                  pl.BlockSpec((None, None, None, ncp, NSA_HEAD_DIM), lambda b, h, c: (b, 1, h, 0, 0)),
                  full(COL_KS), full(COL_VS), full(COL_KW), full(COL_VW),
                  pl.BlockSpec((NSA_GQA, Q_BLOCK, ncp), lambda b, h, c: (h, c, 0)),
                  pl.BlockSpec((NSA_GQA, Q_BLOCK, WIN_KEYS + Q_BLOCK), lambda b, h, c: (h, 0, 0)),
                  pl.BlockSpec((seq, LANES), lambda b, h, c: (0, 0)),
                  pl.BlockSpec((LANES, ncp), lambda b, h, c: (0, 0))],
        out_specs=pl.BlockSpec((Q_BLOCK, gqa_w), lambda b, h, c: (b * nq + c, h)),
        out_shape=jax.ShapeDtypeStruct((bsz * seq, NSA_WIDTH), BF16),
        scratch_shapes=[pltpu.VMEM((seq + Q_BLOCK, 2 * NSA_HEAD_DIM), BF16),
                        pltpu.VMEM((seq + Q_BLOCK, NSA_HEAD_DIM), BF16),
                        pltpu.VMEM((seq + WINDOW, NSA_HEAD_DIM), BF16), pltpu.VMEM((seq + WINDOW, NSA_HEAD_DIM), BF16)],
        compiler_params=_params(("arbitrary", "arbitrary", "arbitrary")),
        name="nsa_attention",
    )(proj, proj, cmp_kv, cmp_kv, proj, proj, proj, proj, bias_cmp, bias_tab, onehot, c2s_t)


_IN_OFF = tuple(int(v) for v in np.cumsum((0,) + IN_SIZES))
_W_IN_COPIES = ((_IN_OFF[0], SSD_WIDTH, COL_Z), (_IN_OFF[1], SSD_WIDTH, COL_XS),
                (_IN_OFF[1] + SSD_WIDTH, 512, COL_B), (_IN_OFF[1] + SSD_WIDTH + 512, 512, COL_C),
                (_IN_OFF[3], NSA_WIDTH, COL_Q), (_IN_OFF[4], 6 * NSA_KV_WIDTH, COL_KC),
                (_IN_OFF[11], 4 * HGRN_WIDTH, COL_HQ))
_W_IN_NARROW = ((_IN_OFF[2], IN_SIZES[2], COL_DT, COL_GATE - COL_DT), (_IN_OFF[10], IN_SIZES[10], COL_GATE, COL_HQ - COL_GATE))


def _relayout_kernel(w_ref, o_ref):
    for src, width, dst in _W_IN_COPIES:
        o_ref[:, dst:dst + width] = w_ref[:, src:src + width].astype(BF16)
    for src, valid, dst, padded in _W_IN_NARROW:
        tile = w_ref[:, src:src + LANES]
        lane = lax.broadcasted_iota(jnp.int32, tile.shape, 1)
        o_ref[:, dst:dst + LANES] = jnp.where(lane < valid, tile, 0.0).astype(BF16)
        if padded > LANES:
            o_ref[:, dst + LANES:dst + padded] = jnp.zeros((tile.shape[0], padded - LANES), BF16)


def _relayout_w_in(w, tr=128):
    depth, d, n = w.shape
    return pl.pallas_call(
        _relayout_kernel,
        grid=(depth, d // tr),
        in_specs=[pl.BlockSpec((None, tr, n), lambda l, i: (l, i, 0))],
        out_specs=pl.BlockSpec((None, tr, IN_PAD), lambda l, i: (l, i, 0)),
        out_shape=jax.ShapeDtypeStruct((depth, d, IN_PAD), BF16),
        compiler_params=_params(("parallel", "parallel")),
        name="w_in_relayout",
    )(w)


def _tiles(seq):
    return dict(in_proj=(1024, 1024), out_proj=(1024, 512), gate_up=(min(1024, seq), 256), down=(256, 1024))


def kernel(x, norm_mix_w, w_in, ssd_conv_w, ssd_conv_b, ssd_dt_bias, ssd_a_log, ssd_d, ssd_norm_w, nsa_cmp_pe, nsa_cmp_w1, nsa_cmp_w2, rel_bias, hgrn_lb_logits, hgrn_norm_w, w_out, norm_ffn_w, ffn_w_gu, ffn_conv_w, ffn_conv_b, ffn_w_down, norm_f_w):
    bsz, seq, d = x.shape
    depth = w_in.shape[0]
    xr = x.reshape(bsz * seq, d).astype(F32)
    bias_cmp, bias_tab = _nsa_bias_tables(rel_bias, seq)
    tiles = _tiles(seq)
    w_in_b = _relayout_w_in(w_in.astype(F32))
    w_out_b = w_out.astype(BF16)
    w_gu_b = ffn_w_gu.astype(BF16)
    w_down_b = ffn_w_down.astype(BF16)
    for l in range(depth):
        n = _rmsnorm(xr, norm_mix_w[l], BF16)
        proj = _matmul(n, w_in_b, l, F32, *tiles["in_proj"], name="in_proj")
        y_ssd = _ssd_mixer(proj, bsz, seq, ssd_conv_w[l].astype(F32), ssd_conv_b[l].astype(F32), ssd_dt_bias[l],
                           ssd_a_log[l], ssd_d[l], ssd_norm_w[l].astype(F32))
        cmp_kv = _nsa_compress(proj, bsz, seq, nsa_cmp_pe[l], nsa_cmp_w1[l], nsa_cmp_w2[l])
        y_nsa = _nsa_attention(proj, cmp_kv, bias_cmp, bias_tab, bsz, seq)
        y_hgrn = _hgrn_mixer(proj, bsz, seq, hgrn_lb_logits, hgrn_norm_w[l], l)
        mix = jnp.concatenate([y_ssd, y_nsa, y_hgrn], axis=1)
        xr = _matmul_residual(mix, w_out_b, l, xr, *tiles["out_proj"], name="out_proj")
        hn = _rmsnorm(xr, norm_ffn_w[l], BF16)
        act = _gate_up(hn, w_gu_b, l, ffn_conv_w[l].astype(F32), ffn_conv_b[l].astype(F32), seq, *tiles["gate_up"])
        xr = _matmul_residual_wres(act, w_down_b, l, xr, *tiles["down"], name="ffn_down")
    out = _rmsnorm(xr, norm_f_w, x.dtype)
    return out.reshape(bsz, seq, d)
```

```python
import functools
import math

import numpy as np
import jax
import jax.numpy as jnp
from jax import lax
from jax.experimental import pallas as pl
from jax.experimental.pallas import tpu as pltpu

F32 = jnp.float32
BF16 = jnp.bfloat16

D_MODEL = 4096
SSD_HEAD_DIM = 64
SSD_WIDTH = 1536
SSD_HEADS = 24
SSD_GROUPS = 4
SSD_HPG = 6
SSD_STATE = 128
SSD_CONV = 4
SSD_CHUNK = 128
SSD_GW = SSD_WIDTH // SSD_GROUPS
NSA_HEAD_DIM = 128
NSA_WIDTH = 1536
NSA_HEADS = 12
NSA_KV_HEADS = 4
NSA_GQA = 3
NSA_KV_WIDTH = 512
CMP_BLOCK = 32
CMP_STRIDE = 16
CMP_HIDDEN = 256
SEL_BLOCK = 64
SEL_TOPK = 16
SEL_LOCAL = 2
WINDOW = 512
Q_BLOCK = 128
FORCE_BONUS = 1e4
HGRN_WIDTH = 1024
HGRN_HEADS = 8
HGRN_DIM = 128
HGRN_CHUNK = 32
REL_BUCKETS = 32
REL_MAX_DIST = 128
D_FF = 11008
FFN_CONV = 3
EPS = 1e-6
NEG_INF = -1e30

IN_SIZES = (SSD_WIDTH, SSD_WIDTH + 2 * SSD_GROUPS * SSD_STATE, SSD_HEADS, NSA_WIDTH,
            NSA_KV_WIDTH, NSA_KV_WIDTH, NSA_KV_WIDTH, NSA_KV_WIDTH, NSA_KV_WIDTH, NSA_KV_WIDTH,
            3 * NSA_HEADS, HGRN_WIDTH, HGRN_WIDTH, HGRN_WIDTH, HGRN_WIDTH)

LANES = 128
SUBLANES = 8
VMEM_LIMIT = 56 * 1024 * 1024

COL_Z = 0
COL_XS = 1536
COL_Q = 3072
COL_B = 4608
COL_C = 5120
COL_KC = 5632
COL_VC = 6144
COL_KS = 6656
COL_VS = 7168
COL_KW = 7680
COL_VW = 8192
COL_DT = 8704
COL_GATE = 8832
COL_HQ = 9216
COL_HF = 10240
COL_HI = 11264
COL_HG = 12288
IN_PAD = 13312

LOG2E = math.log2(math.e)
AUG_BIAS_HI = 64
AUG_BIAS_LO = 65
AUG_PAD = 66
WIN_KEYS = WINDOW + Q_BLOCK
NEAR_KEYS = 2 * Q_BLOCK
FAR_KEYS = 4 * Q_BLOCK


def _params(semantics):
    return pltpu.CompilerParams(dimension_semantics=semantics, vmem_limit_bytes=VMEM_LIMIT)


def _sigmoid(x):
    return 1.0 / (1.0 + jnp.exp(-x))


def _silu(x):
    return x * _sigmoid(x)


def _dot(a, b):
    return jnp.dot(a, b, preferred_element_type=F32)


def _dot_nt(a, b):
    return lax.dot_general(a, b, (((1,), (1,)), ((), ())), preferred_element_type=F32)


def _split3(x):
    hi = x.astype(BF16)
    r1 = x - hi.astype(F32)
    mid = r1.astype(BF16)
    lo = (r1 - mid.astype(F32)).astype(BF16)
    return hi, mid, lo


def _rmsnorm_kernel(x_ref, w_ref, o_ref):
    x = x_ref[...]
    ms = jnp.mean(x * x, axis=-1, keepdims=True)
    o_ref[...] = (x * lax.rsqrt(ms + EPS) * w_ref[...]).astype(o_ref.dtype)


def _rmsnorm(x, w, out_dtype, tm=256):
    m, d = x.shape
    return pl.pallas_call(
        _rmsnorm_kernel,
        grid=(m // tm,),
        in_specs=[pl.BlockSpec((tm, d), lambda i: (i, 0)), pl.BlockSpec((1, d), lambda i: (0, 0))],
        out_specs=pl.BlockSpec((tm, d), lambda i: (i, 0)),
        out_shape=jax.ShapeDtypeStruct((m, d), out_dtype),
        compiler_params=_params(("parallel",)),
        name="rmsnorm",
    )(x, w.reshape(1, d).astype(F32))


def _mm_kernel(a_ref, w_ref, o_ref):
    o_ref[...] = _dot(a_ref[...], w_ref[...]).astype(o_ref.dtype)


def _mm_res_kernel(a_ref, w_ref, r_ref, o_ref):
    o_ref[...] = r_ref[...] + _dot(a_ref[...], w_ref[...])


def _matmul(a, w, layer, out_dtype, tm, tn, name):
    m, k = a.shape
    n = w.shape[2]
    return pl.pallas_call(
        _mm_kernel,
        grid=(m // tm, n // tn),
        in_specs=[pl.BlockSpec((tm, k), lambda i, j: (i, 0)),
                  pl.BlockSpec((None, k, tn), lambda i, j: (layer, 0, j))],
        out_specs=pl.BlockSpec((tm, tn), lambda i, j: (i, j)),
        out_shape=jax.ShapeDtypeStruct((m, n), out_dtype),
        compiler_params=_params(("parallel", "arbitrary")),
        name=name,
    )(a, w)


def _matmul_residual(a, w, layer, res, tm, tn, name):
    m, k = a.shape
    n = w.shape[2]
    return pl.pallas_call(
        _mm_res_kernel,
        grid=(m // tm, n // tn),
        in_specs=[pl.BlockSpec((tm, k), lambda i, j: (i, 0)),
                  pl.BlockSpec((None, k, tn), lambda i, j: (layer, 0, j)),
                  pl.BlockSpec((tm, tn), lambda i, j: (i, j))],
        out_specs=pl.BlockSpec((tm, tn), lambda i, j: (i, j)),
        out_shape=jax.ShapeDtypeStruct((m, n), F32),
        compiler_params=_params(("parallel", "arbitrary")),
        name=name,
    )(a, w, res)


def _matmul_residual_wres(a, w, layer, res, tm, tn, name):
    m, k = a.shape
    n = w.shape[2]
    return pl.pallas_call(
        _mm_res_kernel,
        grid=(n // tn, m // tm),
        in_specs=[pl.BlockSpec((tm, k), lambda j, i: (i, 0)),
                  pl.BlockSpec((None, k, tn), lambda j, i: (layer, 0, j), pipeline_mode=pl.Buffered(1)),
                  pl.BlockSpec((tm, tn), lambda j, i: (i, j))],
        out_specs=pl.BlockSpec((tm, tn), lambda j, i: (i, j)),
        out_shape=jax.ShapeDtypeStruct((m, n), F32),
        compiler_params=_params(("arbitrary", "arbitrary")),
        name=name,
    )(a, w, res)


def _gu_kernel(h_ref, wg_ref, wu_ref, cw_ref, cb_ref, o_ref, ext_ref, halo_ref, *, tm, tn, tiles_per_seq):
    i = pl.program_id(0)
    j = pl.program_id(1)
    h = h_ref[...]
    g = _dot(h, wg_ref[...])
    up = _dot(h, wu_ref[...])
    prev = halo_ref[j]
    ext_ref[0:SUBLANES, :] = jnp.where(i % tiles_per_seq == 0, 0.0, prev)
    ext_ref[SUBLANES:, :] = g
    halo_ref[j] = g[tm - SUBLANES:, :]
    cw = cw_ref[...]
    acc = cb_ref[...] + cw[FFN_CONV - 1:FFN_CONV, :] * g
    for sh in range(1, FFN_CONV):
        acc = acc + cw[FFN_CONV - 1 - sh:FFN_CONV - sh, :] * ext_ref[pl.ds(SUBLANES - sh, tm), :]
    o_ref[...] = (_silu(acc) * up).astype(o_ref.dtype)


def _gate_up(h, w_gu, layer, conv_w, conv_b, seq, tm, tn):
    m, k = h.shape
    nf = conv_w.shape[1]
    nj = nf // tn
    kern = functools.partial(_gu_kernel, tm=tm, tn=tn, tiles_per_seq=seq // tm)
    return pl.pallas_call(
        kern,
        grid=(m // tm, nj),
        in_specs=[pl.BlockSpec((tm, k), lambda i, j: (i, 0)),
                  pl.BlockSpec((None, k, tn), lambda i, j: (layer, 0, j)),
                  pl.BlockSpec((None, k, tn), lambda i, j: (layer, 0, j + nj)),
                  pl.BlockSpec((FFN_CONV, tn), lambda i, j: (0, j)),
                  pl.BlockSpec((1, tn), lambda i, j: (0, j))],
        out_specs=pl.BlockSpec((tm, tn), lambda i, j: (i, j)),
        out_shape=jax.ShapeDtypeStruct((m, nf), BF16),
        scratch_shapes=[pltpu.VMEM((tm + SUBLANES, tn), F32), pltpu.VMEM((nj, SUBLANES, tn), F32)],
        compiler_params=_params(("arbitrary", "arbitrary")),
        name="ffn_gate_up_conv",
    )(h, w_gu, w_gu, conv_w, conv_b.reshape(1, nf))


def _ssd_kernel(z_ref, xs_ref, b_ref, c_ref, dt_ref, cwx_ref, cwb_ref, cwc_ref, cbx_ref, cbb_ref, cbc_ref,
                dtb_ref, alog_ref, dsk_ref, nw_ref, tri_ref, o_ref,
                extx_ref, extb_ref, extc_ref, state_ref, *, ts):
    s = pl.program_id(1)

    @pl.when(s == 0)
    def _():
        extx_ref[...] = jnp.zeros_like(extx_ref)
        extb_ref[...] = jnp.zeros_like(extb_ref)
        extc_ref[...] = jnp.zeros_like(extc_ref)
        state_ref[...] = jnp.zeros_like(state_ref)

    def conv_silu(u_ref, ext_ref, w_ref, bias_ref):
        ext_ref[0:SUBLANES, :] = ext_ref[ts:ts + SUBLANES, :]
        ext_ref[SUBLANES:, :] = u_ref[...]
        w = w_ref[...]
        acc = bias_ref[...] + w[SSD_CONV - 1:SSD_CONV, :] * u_ref[...]
        for sh in range(1, SSD_CONV):
            acc = acc + w[SSD_CONV - 1 - sh:SSD_CONV - sh, :] * ext_ref[pl.ds(SUBLANES - sh, ts), :]
        return _silu(acc)

    xs = conv_silu(xs_ref, extx_ref, cwx_ref, cbx_ref)
    bm = conv_silu(b_ref, extb_ref, cwb_ref, cbb_ref)
    cm = conv_silu(c_ref, extc_ref, cwc_ref, cbc_ref)
    dtr = dt_ref[...] + dtb_ref[...]
    dt = jnp.maximum(dtr, 0.0) + jnp.log1p(jnp.exp(-jnp.abs(dtr)))
    a = dt * (-jnp.exp(alog_ref[...]))
    z = z_ref[...]
    tri = tri_ref[...]
    dsk = dsk_ref[...]
    nw = nw_ref[...]
    li = lax.broadcasted_iota(jnp.int32, (SSD_CHUNK, SSD_CHUNK), 0)
    si = lax.broadcasted_iota(jnp.int32, (SSD_CHUNK, SSD_CHUNK), 1)
    causal = li >= si

    for ck in range(ts // SSD_CHUNK):
        r0 = ck * SSD_CHUNK
        a_c = a[r0:r0 + SSD_CHUNK]
        hi, mid, lo = _split3(a_c)
        acs = _dot(tri, hi) + _dot(tri, mid) + _dot(tri, lo)
        acs_t = acs.T
        dt_c = dt[r0:r0 + SSD_CHUNK]
        a_last = acs[SSD_CHUNK - 1:SSD_CHUNK, :]
        for g in range(SSD_GROUPS):
            cg = cm[r0:r0 + SSD_CHUNK, g * SSD_STATE:(g + 1) * SSD_STATE]
            bg = bm[r0:r0 + SSD_CHUNK, g * SSD_STATE:(g + 1) * SSD_STATE]
            cgb = cg.astype(BF16)
            cb = _dot_nt(cgb, bg.astype(BF16))
            bgt = bg.T.astype(BF16)
            ys = []
            for h in range(SSD_HPG):
                hh = g * SSD_HPG + h
                col = acs[:, hh:hh + 1]
                row = acs_t[hh:hh + 1, :]
                seg = col - row
                decay = jnp.where(causal, jnp.exp(jnp.where(causal, seg, 0.0)), 0.0)
                xs_h = xs[r0:r0 + SSD_CHUNK, hh * SSD_HEAD_DIM:(hh + 1) * SSD_HEAD_DIM]
                xdt = xs_h * dt_c[:, hh:hh + 1]
                y = _dot((cb * decay).astype(BF16), xdt.astype(BF16))
                al = a_last[:, hh:hh + 1]
                xw = xdt * jnp.exp(al - col)
                prev = state_ref[hh]
                y = y + _dot(cgb, prev.astype(BF16)) * jnp.exp(col)
                state_ref[hh] = prev * jnp.exp(al) + _dot(bgt, xw.astype(BF16))
                y = y + xs_h * dsk[:, hh:hh + 1]
                ys.append(y)
            yg = jnp.concatenate(ys, axis=1)
            u = yg * _silu(z[r0:r0 + SSD_CHUNK, g * SSD_GW:(g + 1) * SSD_GW])
            u = u * lax.rsqrt(jnp.mean(u * u, axis=-1, keepdims=True) + EPS)
            o_ref[r0:r0 + SSD_CHUNK, g * SSD_GW:(g + 1) * SSD_GW] = (
                u * nw[:, g * SSD_GW:(g + 1) * SSD_GW]).astype(o_ref.dtype)


def _pad_lanes(v, n=LANES):
    v = v.reshape(1, -1).astype(F32)
    return jnp.pad(v, ((0, 0), (0, n - v.shape[1])))


def _ssd_mixer(proj, bsz, seq, conv_w, conv_b, dt_bias, a_log, d_skip, norm_w, ts=256):
    nsteps = seq // ts
    row = lambda b, s: b * nsteps + s
    xw, bw, cw = conv_w[:, :SSD_WIDTH], conv_w[:, SSD_WIDTH:SSD_WIDTH + 512], conv_w[:, SSD_WIDTH + 512:]
    xb, bb, cb = conv_b[:SSD_WIDTH], conv_b[SSD_WIDTH:SSD_WIDTH + 512], conv_b[SSD_WIDTH + 512:]
    tri = jnp.asarray(np.tril(np.ones((SSD_CHUNK, SSD_CHUNK), np.float32)), BF16)
    const = lambda shape: pl.BlockSpec(shape, lambda b, s: (0,) * len(shape))
    kern = functools.partial(_ssd_kernel, ts=ts)
    return pl.pallas_call(
        kern,
        grid=(bsz, nsteps),
        in_specs=[pl.BlockSpec((ts, SSD_WIDTH), lambda b, s: (row(b, s), COL_Z // SSD_WIDTH)),
                  pl.BlockSpec((ts, SSD_WIDTH), lambda b, s: (row(b, s), COL_XS // SSD_WIDTH)),
                  pl.BlockSpec((ts, 512), lambda b, s: (row(b, s), COL_B // 512)),
                  pl.BlockSpec((ts, 512), lambda b, s: (row(b, s), COL_C // 512)),
                  pl.BlockSpec((ts, LANES), lambda b, s: (row(b, s), COL_DT // LANES)),
                  const((SSD_CONV, SSD_WIDTH)), const((SSD_CONV, 512)), const((SSD_CONV, 512)),
                  const((1, SSD_WIDTH)), const((1, 512)), const((1, 512)),
                  const((1, LANES)), const((1, LANES)), const((1, LANES)), const((1, SSD_WIDTH)),
                  const((SSD_CHUNK, SSD_CHUNK))],
        out_specs=pl.BlockSpec((ts, SSD_WIDTH), lambda b, s: (row(b, s), 0)),
        out_shape=jax.ShapeDtypeStruct((bsz * seq, SSD_WIDTH), BF16),
        scratch_shapes=[pltpu.VMEM((ts + SUBLANES, SSD_WIDTH), F32), pltpu.VMEM((ts + SUBLANES, 512), F32),
                        pltpu.VMEM((ts + SUBLANES, 512), F32),
                        pltpu.VMEM((SSD_HEADS, SSD_STATE, SSD_HEAD_DIM), F32)],
        compiler_params=_params(("arbitrary", "arbitrary")),
        name="ssd_mixer",
    )(proj, proj, proj, proj, proj, xw, bw, cw, xb.reshape(1, -1), bb.reshape(1, -1), cb.reshape(1, -1),
      _pad_lanes(dt_bias), _pad_lanes(a_log), _pad_lanes(d_skip), norm_w.reshape(1, -1), tri)


def _hgrn_kernel(q_ref, f_ref, i_ref, g_ref, lbl_ref, nw_ref, cum_ref, o_ref, state_ref, *, ts, layer):
    s = pl.program_id(1)

    @pl.when(s == 0)
    def _():
        state_ref[...] = jnp.zeros_like(state_ref)

    nck = ts // HGRN_CHUNK
    lg = lbl_ref[...]
    e = jnp.exp(lg - jnp.max(lg, axis=0, keepdims=True))
    sm = e / jnp.sum(e, axis=0, keepdims=True)
    ridx = lax.broadcasted_iota(jnp.int32, lg.shape, 0)
    lb = jnp.sum(jnp.where((ridx >= 1) & (ridx <= layer), sm, 0.0), axis=0, keepdims=True)

    cum = cum_ref[...]
    nw = nw_ref[...]
    ti = lax.broadcasted_iota(jnp.int32, (ts, ts), 0)
    tj = lax.broadcasted_iota(jnp.int32, (ts, ts), 1)
    blockcausal = (ti // HGRN_CHUNK == tj // HGRN_CHUNK) & (ti >= tj)
    tok = lax.broadcasted_iota(jnp.int32, (1, ts), 1)

    for h in range(HGRN_HEADS):
        c0 = h * HGRN_DIM
        lbh = lb[:, c0:c0 + HGRN_DIM]
        f = lbh + (1.0 - lbh) * _sigmoid(f_ref[:, c0:c0 + HGRN_DIM])
        logf = jnp.log(f)
        k = 1.0 - f
        q = _silu(q_ref[:, c0:c0 + HGRN_DIM])
        v = i_ref[:, c0:c0 + HGRN_DIM]
        hi, mid, lo = _split3(logf)
        r = _dot(cum, hi) + _dot(cum, mid) + _dot(cum, lo)
        b = r[0:ts]
        bref = r[ts:2 * ts]
        blast = r[2 * ts:3 * ts]
        qe = (q * jnp.exp(b - bref)).astype(BF16)
        ke = (k * jnp.exp(bref - b)).astype(BF16)
        kl = (k * jnp.exp(blast - b)).astype(BF16)
        qb = (q * jnp.exp(b)).astype(BF16)
        vb = v.astype(BF16)
        att = jnp.where(blockcausal, _dot_nt(qe, ke), 0.0)
        o = _dot(att.astype(BF16), vb)
        vt = v.T
        lhs = jnp.concatenate(
            [jnp.where(tok // HGRN_CHUNK == c, vt, 0.0) for c in range(nck)], axis=0).astype(BF16)
        st = _dot(lhs, kl)
        state = state_ref[h]
        outs = []
        for c in range(nck):
            t0 = c * HGRN_CHUNK
            outs.append(_dot_nt(qb[t0:t0 + HGRN_CHUNK], state.astype(BF16)))
            cd = jnp.exp(blast[t0:t0 + 1, :])
            state = state * cd + st[c * HGRN_DIM:(c + 1) * HGRN_DIM]
        state_ref[h] = state
        o = o + jnp.concatenate(outs, axis=0)
        o = o * lax.rsqrt(jnp.mean(o * o, axis=-1, keepdims=True) + EPS) * nw
        o_ref[:, c0:c0 + HGRN_DIM] = (o * _silu(g_ref[:, c0:c0 + HGRN_DIM])).astype(o_ref.dtype)


def _hgrn_cum_matrix(ts):
    t = np.arange(ts)
    same = (t[:, None] // HGRN_CHUNK) == (t[None, :] // HGRN_CHUNK)
    incl = same & (t[None, :] <= t[:, None])
    mid = same & (t[None, :] <= (t[:, None] // HGRN_CHUNK) * HGRN_CHUNK + HGRN_CHUNK // 2)
    return np.concatenate([incl, mid, same], axis=0).astype(np.float32)


def _hgrn_mixer(proj, bsz, seq, lb_logits, norm_w, layer, ts=256):
    nsteps = seq // ts
    row = lambda b, s: b * nsteps + s
    depth = lb_logits.shape[0]
    cum = jnp.asarray(_hgrn_cum_matrix(ts), BF16)
    kern = functools.partial(_hgrn_kernel, ts=ts, layer=layer)
    blk = lambda col: pl.BlockSpec((ts, HGRN_WIDTH), lambda b, s: (row(b, s), col // HGRN_WIDTH))
    return pl.pallas_call(
        kern,
        grid=(bsz, nsteps),
        in_specs=[blk(COL_HQ), blk(COL_HF), blk(COL_HI), blk(COL_HG),
                  pl.BlockSpec((depth, HGRN_WIDTH), lambda b, s: (0, 0)),
                  pl.BlockSpec((1, HGRN_DIM), lambda b, s: (0, 0)),
                  pl.BlockSpec((3 * ts, ts), lambda b, s: (0, 0))],
        out_specs=pl.BlockSpec((ts, HGRN_WIDTH), lambda b, s: (row(b, s), 0)),
        out_shape=jax.ShapeDtypeStruct((bsz * seq, HGRN_WIDTH), BF16),
        scratch_shapes=[pltpu.VMEM((HGRN_HEADS, HGRN_DIM, HGRN_DIM), F32)],
        compiler_params=_params(("arbitrary", "arbitrary")),
        name="hgrn2_mixer",
    )(proj, proj, proj, proj, lb_logits.astype(F32), norm_w.reshape(1, HGRN_DIM).astype(F32), cum)


def _rel_bucket_np(dist):
    max_exact = REL_BUCKETS // 2
    d = np.maximum(dist, 0)
    ratio = np.maximum(d, 1).astype(np.float32) / np.float32(max_exact)
    log_ratio = np.log(ratio).astype(np.float32) / np.float32(math.log(REL_MAX_DIST / max_exact))
    large = np.minimum(max_exact + (log_ratio * np.float32(REL_BUCKETS - max_exact)).astype(np.int32),
                       REL_BUCKETS - 1)
    return np.where(d < max_exact, d, large).astype(np.int32)


def _bias_expand_kernel(rb_ref, bmap_ref, o_ref):
    h = pl.program_id(0)
    bm = bmap_ref[...]
    out = jnp.full(bm.shape, NEG_INF, F32)
    for k in range(REL_BUCKETS):
        out = jnp.where(bm == k, rb_ref[k, h] * LOG2E, out)
    o_ref[...] = out


def _bias_expand(rel_bias, bmap, tr):
    rows, cols = bmap.shape
    return pl.pallas_call(
        _bias_expand_kernel,
        grid=(NSA_HEADS, rows // tr),
        in_specs=[pl.BlockSpec(memory_space=pltpu.SMEM),
                  pl.BlockSpec((tr, cols), lambda h, i: (i, 0))],
        out_specs=pl.BlockSpec((None, tr, cols), lambda h, i: (h, i, 0)),
        out_shape=jax.ShapeDtypeStruct((NSA_HEADS, rows, cols), F32),
        compiler_params=_params(("arbitrary", "arbitrary")),
        name="nsa_bias_expand",
    )(rel_bias.astype(F32), bmap)


def _nsa_bias_tables(rel_bias, seq):
    ncp = seq // CMP_STRIDE
    t = np.arange(seq)[:, None]
    cmp_end = np.arange(ncp)[None, :] * CMP_STRIDE + CMP_BLOCK - 1
    d = t - cmp_end
    bmap_cmp = np.where(d >= 0, _rel_bucket_np(d), -1).astype(np.int32)
    l = np.arange(Q_BLOCK)[:, None]
    j = np.arange(WIN_KEYS)[None, :]
    dist = l - j + WINDOW
    win = np.where((dist >= 0) & (dist < WINDOW), _rel_bucket_np(dist), -1)
    far = _rel_bucket_np(np.arange(Q_BLOCK + 1, 8 * seq))
    assert (far == far[0]).all(), "distances beyond one query block must share a single bucket"
    bmap_tab = np.concatenate([win, np.full((Q_BLOCK, Q_BLOCK), far[0])], axis=1).astype(np.int32)
    bias_cmp = _bias_expand(rel_bias, jnp.asarray(bmap_cmp), 512)
    bias_tab = _bias_expand(rel_bias, jnp.asarray(bmap_tab), Q_BLOCK)
    return bias_cmp, bias_tab


def _cmp_kernel(u_ref, pe_ref, w1_ref, w2_ref, o_ref, *, ncp):
    half = CMP_BLOCK // 2
    pe = pe_ref[...]
    top = jnp.zeros((ncp, CMP_HIDDEN), F32)
    bot = jnp.zeros((ncp, CMP_HIDDEN), F32)
    for l in range(half):
        x = u_ref[pl.ds(l, ncp, stride=half), :]
        top = top + _dot((x + pe[l:l + 1, :]).astype(BF16), w1_ref[l])
        bot = bot + _dot((x + pe[half + l:half + l + 1, :]).astype(BF16), w1_ref[half + l])
    hid = top + pltpu.roll(bot, ncp - 1, 0)
    o_ref[...] = _dot(_silu(hid).astype(BF16), w2_ref[...])


def _nsa_compress(proj, bsz, seq, pe, w1, w2):
    ncp = seq // CMP_STRIDE
    kern = functools.partial(_cmp_kernel, ncp=ncp)
    return pl.pallas_call(
        kern,
        grid=(bsz, 2, NSA_KV_HEADS),
        in_specs=[pl.BlockSpec((seq, NSA_HEAD_DIM), lambda b, t, h: (b, COL_KC // NSA_HEAD_DIM + NSA_KV_HEADS * t + h)),
                  pl.BlockSpec((None, CMP_BLOCK, NSA_HEAD_DIM), lambda b, t, h: (t, 0, 0)),
                  pl.BlockSpec((None, CMP_BLOCK, NSA_HEAD_DIM, CMP_HIDDEN), lambda b, t, h: (t, 0, 0, 0)),
                  pl.BlockSpec((None, CMP_HIDDEN, NSA_HEAD_DIM), lambda b, t, h: (t, 0, 0))],
        out_specs=pl.BlockSpec((None, None, None, ncp, NSA_HEAD_DIM), lambda b, t, h: (b, t, h, 0, 0)),
        out_shape=jax.ShapeDtypeStruct((bsz, 2, NSA_KV_HEADS, ncp, NSA_HEAD_DIM), F32),
        compiler_params=_params(("arbitrary", "arbitrary", "arbitrary")),
        name="nsa_compress",
    )(proj, pe.astype(F32), w1.reshape(2, CMP_BLOCK, NSA_HEAD_DIM, CMP_HIDDEN).astype(BF16), w2.astype(BF16))


def _softmax_start(s, v):
    m = jnp.max(s, axis=-1, keepdims=True)
    p = jnp.exp2(s - m)
    return m, jnp.sum(p, axis=-1, keepdims=True), _dot(p.astype(BF16), v)


def _softmax_update(s, v, carry):
    m, l, acc = carry
    m_new = jnp.maximum(m, jnp.max(s, axis=-1, keepdims=True))
    alpha = jnp.exp2(m - m_new)
    p = jnp.exp2(s - m_new)
    return m_new, alpha * l + jnp.sum(p, axis=-1, keepdims=True), alpha * acc + _dot(p.astype(BF16), v)


def _softmax_finish(carry):
    m, l, acc = carry
    return jnp.where(m > 0.5 * NEG_INF, acc / jnp.maximum(l, 1e-30), 0.0)


def _nsa_kernel(q_ref, gate_ref, kc_ref, vc_ref, ks_ref, vs_ref, kw_ref, vw_ref, bcmp_ref, btab_ref,
                kone_ref, c2s_ref, o_ref, kaug_ref, vsb_ref, kwb_ref, vwb_ref, sa_ref, sb_ref, *, n_sel):
    hk = pl.program_id(1)
    c = pl.program_id(2)
    rows = NSA_GQA * Q_BLOCK
    lane = lax.broadcasted_iota(jnp.int32, (Q_BLOCK, LANES), 1)

    @pl.when(c == 0)
    def _():
        kaug_ref[0:Q_BLOCK, 0:NSA_HEAD_DIM] = jnp.zeros((Q_BLOCK, NSA_HEAD_DIM), BF16)
        kaug_ref[Q_BLOCK:, 0:NSA_HEAD_DIM] = ks_ref[...].astype(BF16)
        kaug_ref[:, NSA_HEAD_DIM:] = kone_ref[...]
        vsb_ref[0:Q_BLOCK, :] = jnp.zeros((Q_BLOCK, NSA_HEAD_DIM), BF16)
        vsb_ref[Q_BLOCK:, :] = vs_ref[...].astype(BF16)
        kwb_ref[0:WINDOW, 0:NSA_HEAD_DIM] = jnp.zeros((WINDOW, NSA_HEAD_DIM), BF16)
        kwb_ref[WINDOW:, 0:NSA_HEAD_DIM] = kw_ref[...].astype(BF16)
        for r0 in range(0, WINDOW, Q_BLOCK):
            kwb_ref[r0:r0 + Q_BLOCK, NSA_HEAD_DIM:] = jnp.where(lane == AUG_PAD, 1.0, 0.0).astype(BF16)
        kwb_ref[WINDOW:, NSA_HEAD_DIM:] = jnp.zeros((kw_ref.shape[0], LANES), BF16)
        vwb_ref[0:WINDOW, :] = jnp.zeros((WINDOW, NSA_HEAD_DIM), BF16)
        vwb_ref[WINDOW:, :] = vw_ref[...].astype(BF16)

    q = q_ref[...] * (NSA_HEAD_DIM ** -0.5 * LOG2E)
    q3 = jnp.concatenate([q[:, g * NSA_HEAD_DIM:(g + 1) * NSA_HEAD_DIM] for g in range(NSA_GQA)],
                         axis=0).astype(BF16)

    bc = bcmp_ref[...].reshape(rows, bcmp_ref.shape[-1])
    valid = bc > 0.5 * NEG_INF
    s = _dot_nt(q3, kc_ref[...].astype(BF16)) + bc
    m = jnp.max(s, axis=-1, keepdims=True)
    p = jnp.where(valid, jnp.exp2(s - m), 0.0)
    p = p / jnp.maximum(jnp.sum(p, axis=-1, keepdims=True), 1e-30)
    o_cmp = _dot(p.astype(BF16), vc_ref[...].astype(BF16))

    psum = p[0:Q_BLOCK] + p[Q_BLOCK:2 * Q_BLOCK] + p[2 * Q_BLOCK:3 * Q_BLOCK]
    ph = psum.astype(BF16)
    plo = (psum - ph.astype(F32)).astype(BF16)
    c2s = c2s_ref[...]
    imp_t = _dot_nt(c2s, ph) + _dot_nt(c2s, plo)
    nsp = -(-n_sel // SUBLANES) * SUBLANES
    imp_t = imp_t[0:nsp]
    tq = c * Q_BLOCK + lax.broadcasted_iota(jnp.int32, (nsp, Q_BLOCK), 1)
    jj = lax.broadcasted_iota(jnp.int32, (nsp, Q_BLOCK), 0)
    valid_sel = (jj < n_sel) & (jj * SEL_BLOCK <= tq)
    back = tq // SEL_BLOCK - jj
    force = (jj == 0) | ((back >= 0) & (back < SEL_LOCAL))
    score = jnp.where(valid_sel, imp_t + jnp.where(force, FORCE_BONUS, 0.0), NEG_INF)
    n_tiles = nsp // SUBLANES
    tiles = [score[v * SUBLANES:(v + 1) * SUBLANES] for v in range(n_tiles)]
    sub = lax.broadcasted_iota(jnp.int32, (SUBLANES, Q_BLOCK), 0)
    cnts = [jnp.zeros((SUBLANES, Q_BLOCK), F32) for _ in range(n_tiles)]
    for jp in range(n_sel):
        v0, r0 = divmod(jp, SUBLANES)
        r = tiles[v0][r0:r0 + 1, :]
        for v in range(n_tiles):
            gt = jnp.where(r > tiles[v], 1.0, 0.0)
            ge = jnp.where(r >= tiles[v], 1.0, 0.0)
            cnts[v] = cnts[v] + (gt if v < v0 else ge if v > v0 else jnp.where(sub > r0, ge, gt))
    cnt = jnp.concatenate(cnts, axis=0)
    keep = valid_sel & (cnt < float(min(SEL_TOPK, n_sel)))

    ext = lax.broadcasted_iota(jnp.int32, (LANES - nsp, Q_BLOCK), 0) + nsp
    aug = jnp.concatenate([jnp.where(keep, 0.0, NEG_INF), jnp.where(ext == AUG_PAD, NEG_INF, 0.0)], axis=0)
    aug_near = aug.T
    aug_far = jnp.where((lane >= 2 * (c - 1)) & (lane < AUG_BIAS_HI), NEG_INF, aug_near)
    btab = btab_ref[...].reshape(rows, WIN_KEYS + Q_BLOCK)
    far_parts = []
    for g in range(NSA_GQA):
        bfar = btab[g * Q_BLOCK:(g + 1) * Q_BLOCK, WIN_KEYS:]
        hi = bfar.astype(BF16).astype(F32)
        far_parts.append(jnp.where(lane == AUG_BIAS_HI, hi, jnp.where(lane == AUG_BIAS_LO, bfar - hi, aug_far)))
    qaug_near = jnp.concatenate([q3, jnp.concatenate([aug_near.astype(BF16)] * NSA_GQA, axis=0)], axis=1)
    qaug_far = jnp.concatenate([q3, jnp.concatenate(far_parts, axis=0).astype(BF16)], axis=1)
    qaug_win = jnp.concatenate(
        [q3, jnp.concatenate([jnp.where(lane == AUG_PAD, NEG_INF, 0.0).astype(BF16)] * NSA_GQA, axis=0)], axis=1)

    near0 = pl.multiple_of(c * Q_BLOCK, Q_BLOCK)
    s = _dot_nt(qaug_near, kaug_ref[pl.ds(near0, NEAR_KEYS), :]) + btab[:, WIN_KEYS - NEAR_KEYS:WIN_KEYS]
    carry = _softmax_start(s, vsb_ref[pl.ds(near0, NEAR_KEYS), :])

    s = _dot_nt(qaug_win, kwb_ref[pl.ds(near0, WIN_KEYS), :]) + btab[:, 0:WIN_KEYS]
    o_win = _softmax_finish(_softmax_start(s, vwb_ref[pl.ds(near0, WIN_KEYS), :]))

    far_blocks = FAR_KEYS // Q_BLOCK
    n_pairs = (c - 1 + 2 * far_blocks - 1) // (2 * far_blocks)
    last_slab = (kaug_ref.shape[0] - Q_BLOCK) // FAR_KEYS - 1

    def far_logits(slab):
        r0 = pl.multiple_of(jnp.minimum(slab, last_slab) * FAR_KEYS + Q_BLOCK, Q_BLOCK)
        return _dot_nt(qaug_far, kaug_ref[pl.ds(r0, FAR_KEYS), :])

    def far_values(slab):
        return vsb_ref[pl.ds(pl.multiple_of(slab * FAR_KEYS + Q_BLOCK, Q_BLOCK), FAR_KEYS), :]

    sa_ref[...] = far_logits(0)

    def far_body(k, carry):
        sb_ref[...] = far_logits(2 * k + 1)
        carry = _softmax_update(sa_ref[...], far_values(2 * k), carry)
        sa_ref[...] = far_logits(2 * k + 2)
        return _softmax_update(sb_ref[...], far_values(2 * k + 1), carry)

    o_sel = _softmax_finish(lax.fori_loop(0, n_pairs, far_body, carry))

    gates = _sigmoid(gate_ref[...])
    lane = lax.broadcasted_iota(jnp.int32, gates.shape, 1)

    def gate_col(g, br):
        return jnp.sum(jnp.where(lane == hk * (3 * NSA_GQA) + g * 3 + br, gates, 0.0), axis=-1, keepdims=True)

    for g in range(NSA_GQA):
        r0 = g * Q_BLOCK
        out = (gate_col(g, 0) * o_cmp[r0:r0 + Q_BLOCK] + gate_col(g, 1) * o_sel[r0:r0 + Q_BLOCK]
               + gate_col(g, 2) * o_win[r0:r0 + Q_BLOCK])
        o_ref[:, g * NSA_HEAD_DIM:(g + 1) * NSA_HEAD_DIM] = out.astype(o_ref.dtype)


def _nsa_attention(proj, cmp_kv, bias_cmp, bias_tab, bsz, seq):
    nq = seq // Q_BLOCK
    ncp = seq // CMP_STRIDE
    n_sel = seq // SEL_BLOCK
    assert n_sel <= AUG_BIAS_HI and NSA_HEAD_DIM == LANES
    assert seq % (2 * FAR_KEYS) == 0, "far steps walk the keys two 512-key slabs at a time"
    key = np.arange(-Q_BLOCK, seq)[:, None]
    ln = np.arange(LANES)[None, :]
    kone = np.where(key < 0, ln == AUG_PAD,
                    ((ln < AUG_BIAS_HI) & (key // SEL_BLOCK == ln)) | (ln == AUG_BIAS_HI) | (ln == AUG_BIAS_LO))
    kone = jnp.asarray(kone, BF16)
    c_start = np.arange(ncp)[None, :] * CMP_STRIDE
    s_start = np.arange(LANES)[:, None] * SEL_BLOCK
    overlap = np.clip(np.minimum(c_start + CMP_BLOCK, s_start + SEL_BLOCK) - np.maximum(c_start, s_start), 0, None)
    overlap = np.where(np.arange(LANES)[:, None] < n_sel, overlap, 0)
    c2s_t = jnp.asarray(overlap / CMP_BLOCK, BF16)
    gqa_w = NSA_GQA * NSA_HEAD_DIM
    full = lambda col: pl.BlockSpec((seq, NSA_HEAD_DIM), lambda b, h, c: (b, col // NSA_HEAD_DIM + h))
    kern = functools.partial(_nsa_kernel, n_sel=n_sel)
    return pl.pallas_call(
        kern,
        grid=(bsz, NSA_KV_HEADS, nq),
        in_specs=[pl.BlockSpec((Q_BLOCK, gqa_w), lambda b, h, c: (b * nq + c, COL_Q // gqa_w + h)),
                  pl.BlockSpec((Q_BLOCK, LANES), lambda b, h, c: (b * nq + c, COL_GATE // LANES)),
                  pl.BlockSpec((None, None, None, ncp, NSA_HEAD_DIM), lambda b, h, c: (b, 0, h, 0, 0)),
                  pl.BlockSpec((None, None, None, ncp, NSA_HEAD_DIM), lambda b, h, c: (b, 1, h, 0, 0)),
                  full(COL_KS), full(COL_VS), full(COL_KW), full(COL_VW),
                  pl.BlockSpec((NSA_GQA, Q_BLOCK, ncp), lambda b, h, c: (h, c, 0)),
                  pl.BlockSpec((NSA_GQA, Q_BLOCK, WIN_KEYS + Q_BLOCK), lambda b, h, c: (h, 0, 0)),
                  pl.BlockSpec((seq + Q_BLOCK, LANES), lambda b, h, c: (0, 0)),
                  pl.BlockSpec((LANES, ncp), lambda b, h, c: (0, 0))],
        out_specs=pl.BlockSpec((Q_BLOCK, gqa_w), lambda b, h, c: (b * nq + c, h)),
        out_shape=jax.ShapeDtypeStruct((bsz * seq, NSA_WIDTH), BF16),
        scratch_shapes=[pltpu.VMEM((seq + Q_BLOCK, 2 * NSA_HEAD_DIM), BF16),
                        pltpu.VMEM((seq + Q_BLOCK, NSA_HEAD_DIM), BF16),
                        pltpu.VMEM((seq + WINDOW, 2 * NSA_HEAD_DIM), BF16),
                        pltpu.VMEM((seq + WINDOW, NSA_HEAD_DIM), BF16),
                        pltpu.VMEM((NSA_GQA * Q_BLOCK, FAR_KEYS), F32), pltpu.VMEM((NSA_GQA * Q_BLOCK, FAR_KEYS), F32)],
        compiler_params=_params(("arbitrary", "arbitrary", "arbitrary")),
        name="nsa_attention",
    )(proj, proj, cmp_kv, cmp_kv, proj, proj, proj, proj, bias_cmp, bias_tab, kone, c2s_t)


_IN_OFF = tuple(int(v) for v in np.cumsum((0,) + IN_SIZES))
_W_IN_COPIES = ((_IN_OFF[0], SSD_WIDTH, COL_Z), (_IN_OFF[1], SSD_WIDTH, COL_XS),
                (_IN_OFF[1] + SSD_WIDTH, 512, COL_B), (_IN_OFF[1] + SSD_WIDTH + 512, 512, COL_C),
                (_IN_OFF[3], NSA_WIDTH, COL_Q), (_IN_OFF[4], 6 * NSA_KV_WIDTH, COL_KC),
                (_IN_OFF[11], 4 * HGRN_WIDTH, COL_HQ))
_W_IN_NARROW = ((_IN_OFF[2], IN_SIZES[2], COL_DT, COL_GATE - COL_DT), (_IN_OFF[10], IN_SIZES[10], COL_GATE, COL_HQ - COL_GATE))


def _relayout_kernel(w_ref, o_ref):
    for src, width, dst in _W_IN_COPIES:
        o_ref[:, dst:dst + width] = w_ref[:, src:src + width].astype(BF16)
    for src, valid, dst, padded in _W_IN_NARROW:
        tile = w_ref[:, src:src + LANES]
        lane = lax.broadcasted_iota(jnp.int32, tile.shape, 1)
        o_ref[:, dst:dst + LANES] = jnp.where(lane < valid, tile, 0.0).astype(BF16)
        if padded > LANES:
            o_ref[:, dst + LANES:dst + padded] = jnp.zeros((tile.shape[0], padded - LANES), BF16)


def _relayout_w_in(w, tr=128):
    depth, d, n = w.shape
    return pl.pallas_call(
        _relayout_kernel,
        grid=(depth, d // tr),
        in_specs=[pl.BlockSpec((None, tr, n), lambda l, i: (l, i, 0))],
        out_specs=pl.BlockSpec((None, tr, IN_PAD), lambda l, i: (l, i, 0)),
        out_shape=jax.ShapeDtypeStruct((depth, d, IN_PAD), BF16),
        compiler_params=_params(("parallel", "parallel")),
        name="w_in_relayout",
    )(w)


def _tiles(seq):
    return dict(in_proj=(1024, 1024), out_proj=(1024, 512), gate_up=(min(1024, seq), 256), down=(256, 1024))


def kernel(x, norm_mix_w, w_in, ssd_conv_w, ssd_conv_b, ssd_dt_bias, ssd_a_log, ssd_d, ssd_norm_w, nsa_cmp_pe, nsa_cmp_w1, nsa_cmp_w2, rel_bias, hgrn_lb_logits, hgrn_norm_w, w_out, norm_ffn_w, ffn_w_gu, ffn_conv_w, ffn_conv_b, ffn_w_down, norm_f_w):
    bsz, seq, d = x.shape
    depth = w_in.shape[0]
    xr = x.reshape(bsz * seq, d).astype(F32)
    bias_cmp, bias_tab = _nsa_bias_tables(rel_bias, seq)
    tiles = _tiles(seq)
    w_in_b = _relayout_w_in(w_in.astype(F32))
    w_out_b = w_out.astype(BF16)
    w_gu_b = ffn_w_gu.astype(BF16)
    w_down_b = ffn_w_down.astype(BF16)
    for l in range(depth):
        n = _rmsnorm(xr, norm_mix_w[l], BF16)
        proj = _matmul(n, w_in_b, l, F32, *tiles["in_proj"], name="in_proj")
        y_ssd = _ssd_mixer(proj, bsz, seq, ssd_conv_w[l].astype(F32), ssd_conv_b[l].astype(F32), ssd_dt_bias[l],
                           ssd_a_log[l], ssd_d[l], ssd_norm_w[l].astype(F32))
        cmp_kv = _nsa_compress(proj, bsz, seq, nsa_cmp_pe[l], nsa_cmp_w1[l], nsa_cmp_w2[l])
        y_nsa = _nsa_attention(proj, cmp_kv, bias_cmp, bias_tab, bsz, seq)
        y_hgrn = _hgrn_mixer(proj, bsz, seq, hgrn_lb_logits, hgrn_norm_w[l], l)
        mix = jnp.concatenate([y_ssd, y_nsa, y_hgrn], axis=1)
        xr = _matmul_residual(mix, w_out_b, l, xr, *tiles["out_proj"], name="out_proj")
        hn = _rmsnorm(xr, norm_ffn_w[l], BF16)
        act = _gate_up(hn, w_gu_b, l, ffn_conv_w[l].astype(F32), ffn_conv_b[l].astype(F32), seq, *tiles["gate_up"])
        xr = _matmul_residual_wres(act, w_down_b, l, xr, *tiles["down"], name="ffn_down")
    out = _rmsnorm(xr, norm_f_w, x.dtype)
    return out.reshape(bsz, seq, d)
```

```python
import functools
import math

import numpy as np
import jax
import jax.numpy as jnp
from jax import lax
from jax.experimental import pallas as pl
from jax.experimental.pallas import tpu as pltpu

F32 = jnp.float32
BF16 = jnp.bfloat16

D_MODEL = 4096
SSD_HEAD_DIM = 64
SSD_WIDTH = 1536
SSD_HEADS = 24
SSD_GROUPS = 4
SSD_HPG = 6
SSD_STATE = 128
SSD_CONV = 4
SSD_CHUNK = 128
SSD_GW = SSD_WIDTH // SSD_GROUPS
NSA_HEAD_DIM = 128
NSA_WIDTH = 1536
NSA_HEADS = 12
NSA_KV_HEADS = 4
NSA_GQA = 3
NSA_KV_WIDTH = 512
CMP_BLOCK = 32
CMP_STRIDE = 16
CMP_HIDDEN = 256
SEL_BLOCK = 64
SEL_TOPK = 16
SEL_LOCAL = 2
WINDOW = 512
Q_BLOCK = 128
FORCE_BONUS = 1e4
HGRN_WIDTH = 1024
HGRN_HEADS = 8
HGRN_DIM = 128
HGRN_CHUNK = 32
REL_BUCKETS = 32
REL_MAX_DIST = 128
D_FF = 11008
FFN_CONV = 3
EPS = 1e-6
NEG_INF = -1e30

IN_SIZES = (SSD_WIDTH, SSD_WIDTH + 2 * SSD_GROUPS * SSD_STATE, SSD_HEADS, NSA_WIDTH,
            NSA_KV_WIDTH, NSA_KV_WIDTH, NSA_KV_WIDTH, NSA_KV_WIDTH, NSA_KV_WIDTH, NSA_KV_WIDTH,
            3 * NSA_HEADS, HGRN_WIDTH, HGRN_WIDTH, HGRN_WIDTH, HGRN_WIDTH)

LANES = 128
SUBLANES = 8
VMEM_LIMIT = 56 * 1024 * 1024

COL_Z = 0
COL_XS = 1536
COL_Q = 3072
COL_B = 4608
COL_C = 5120
COL_KC = 5632
COL_VC = 6144
COL_KS = 6656
COL_VS = 7168
COL_KW = 7680
COL_VW = 8192
COL_DT = 8704
COL_GATE = 8832
COL_HQ = 9216
COL_HF = 10240
COL_HI = 11264
COL_HG = 12288
IN_PAD = 13312

LOG2E = math.log2(math.e)
AUG_BIAS_HI = 64
AUG_BIAS_LO = 65
AUG_PAD = 66
WIN_KEYS = WINDOW + Q_BLOCK
NEAR_KEYS = 2 * Q_BLOCK
FAR_KEYS = 4 * Q_BLOCK
NSA_Q_BLOCKS_PER_STEP = 2


def _params(semantics):
    return pltpu.CompilerParams(dimension_semantics=semantics, vmem_limit_bytes=VMEM_LIMIT)


def _sigmoid(x):
    return 1.0 / (1.0 + jnp.exp(-x))


def _silu(x):
    return x * _sigmoid(x)


def _dot(a, b):
    return jnp.dot(a, b, preferred_element_type=F32)


def _dot_nt(a, b):
    return lax.dot_general(a, b, (((1,), (1,)), ((), ())), preferred_element_type=F32)


def _split3(x):
    hi = x.astype(BF16)
    r1 = x - hi.astype(F32)
    mid = r1.astype(BF16)
    lo = (r1 - mid.astype(F32)).astype(BF16)
    return hi, mid, lo


def _rmsnorm_kernel(x_ref, w_ref, o_ref):
    x = x_ref[...]
    ms = jnp.mean(x * x, axis=-1, keepdims=True)
    o_ref[...] = (x * lax.rsqrt(ms + EPS) * w_ref[...]).astype(o_ref.dtype)


def _rmsnorm(x, w, out_dtype, tm=256):
    m, d = x.shape
    return pl.pallas_call(
        _rmsnorm_kernel,
        grid=(m // tm,),
        in_specs=[pl.BlockSpec((tm, d), lambda i: (i, 0)), pl.BlockSpec((1, d), lambda i: (0, 0))],
        out_specs=pl.BlockSpec((tm, d), lambda i: (i, 0)),
        out_shape=jax.ShapeDtypeStruct((m, d), out_dtype),
        compiler_params=_params(("parallel",)),
        name="rmsnorm",
    )(x, w.reshape(1, d).astype(F32))


def _mm_kernel(a_ref, w_ref, o_ref):
    o_ref[...] = _dot(a_ref[...], w_ref[...]).astype(o_ref.dtype)


def _mm_res_kernel(a_ref, w_ref, r_ref, o_ref):
    o_ref[...] = r_ref[...] + _dot(a_ref[...], w_ref[...])


def _matmul(a, w, layer, out_dtype, tm, tn, name):
    m, k = a.shape
    n = w.shape[2]
    return pl.pallas_call(
        _mm_kernel,
        grid=(m // tm, n // tn),
        in_specs=[pl.BlockSpec((tm, k), lambda i, j: (i, 0)),
                  pl.BlockSpec((None, k, tn), lambda i, j: (layer, 0, j))],
        out_specs=pl.BlockSpec((tm, tn), lambda i, j: (i, j)),
        out_shape=jax.ShapeDtypeStruct((m, n), out_dtype),
        compiler_params=_params(("parallel", "arbitrary")),
        name=name,
    )(a, w)


def _matmul_residual(a, w, layer, res, tm, tn, name):
    m, k = a.shape
    n = w.shape[2]
    return pl.pallas_call(
        _mm_res_kernel,
        grid=(m // tm, n // tn),
        in_specs=[pl.BlockSpec((tm, k), lambda i, j: (i, 0)),
                  pl.BlockSpec((None, k, tn), lambda i, j: (layer, 0, j)),
                  pl.BlockSpec((tm, tn), lambda i, j: (i, j))],
        out_specs=pl.BlockSpec((tm, tn), lambda i, j: (i, j)),
        out_shape=jax.ShapeDtypeStruct((m, n), F32),
        compiler_params=_params(("parallel", "arbitrary")),
        name=name,
    )(a, w, res)


def _matmul_residual_wres(a, w, layer, res, tm, tn, name):
    m, k = a.shape
    n = w.shape[2]
    return pl.pallas_call(
        _mm_res_kernel,
        grid=(n // tn, m // tm),
        in_specs=[pl.BlockSpec((tm, k), lambda j, i: (i, 0)),
                  pl.BlockSpec((None, k, tn), lambda j, i: (layer, 0, j), pipeline_mode=pl.Buffered(1)),
                  pl.BlockSpec((tm, tn), lambda j, i: (i, j))],
        out_specs=pl.BlockSpec((tm, tn), lambda j, i: (i, j)),
        out_shape=jax.ShapeDtypeStruct((m, n), F32),
        compiler_params=_params(("arbitrary", "arbitrary")),
        name=name,
    )(a, w, res)


def _gu_kernel(h_ref, wg_ref, wu_ref, cw_ref, cb_ref, o_ref, ext_ref, halo_ref, *, tm, tn, tiles_per_seq):
    i = pl.program_id(0)
    j = pl.program_id(1)
    h = h_ref[...]
    g = _dot(h, wg_ref[...])
    up = _dot(h, wu_ref[...])
    prev = halo_ref[j]
    ext_ref[0:SUBLANES, :] = jnp.where(i % tiles_per_seq == 0, 0.0, prev)
    ext_ref[SUBLANES:, :] = g
    halo_ref[j] = g[tm - SUBLANES:, :]
    cw = cw_ref[...]
    acc = cb_ref[...] + cw[FFN_CONV - 1:FFN_CONV, :] * g
    for sh in range(1, FFN_CONV):
        acc = acc + cw[FFN_CONV - 1 - sh:FFN_CONV - sh, :] * ext_ref[pl.ds(SUBLANES - sh, tm), :]
    o_ref[...] = (_silu(acc) * up).astype(o_ref.dtype)


def _gate_up(h, w_gu, layer, conv_w, conv_b, seq, tm, tn):
    m, k = h.shape
    nf = conv_w.shape[1]
    nj = nf // tn
    kern = functools.partial(_gu_kernel, tm=tm, tn=tn, tiles_per_seq=seq // tm)
    return pl.pallas_call(
        kern,
        grid=(m // tm, nj),
        in_specs=[pl.BlockSpec((tm, k), lambda i, j: (i, 0)),
                  pl.BlockSpec((None, k, tn), lambda i, j: (layer, 0, j)),
                  pl.BlockSpec((None, k, tn), lambda i, j: (layer, 0, j + nj)),
                  pl.BlockSpec((FFN_CONV, tn), lambda i, j: (0, j)),
                  pl.BlockSpec((1, tn), lambda i, j: (0, j))],
        out_specs=pl.BlockSpec((tm, tn), lambda i, j: (i, j)),
        out_shape=jax.ShapeDtypeStruct((m, nf), BF16),
        scratch_shapes=[pltpu.VMEM((tm + SUBLANES, tn), F32), pltpu.VMEM((nj, SUBLANES, tn), F32)],
        compiler_params=_params(("arbitrary", "arbitrary")),
        name="ffn_gate_up_conv",
    )(h, w_gu, w_gu, conv_w, conv_b.reshape(1, nf))


def _ssd_kernel(z_ref, xs_ref, b_ref, c_ref, dt_ref, cwx_ref, cwb_ref, cwc_ref, cbx_ref, cbb_ref, cbc_ref,
                dtb_ref, alog_ref, dsk_ref, nw_ref, tri_ref, o_ref,
                extx_ref, extb_ref, extc_ref, state_ref, *, ts):
    s = pl.program_id(1)

    @pl.when(s == 0)
    def _():
        extx_ref[...] = jnp.zeros_like(extx_ref)
        extb_ref[...] = jnp.zeros_like(extb_ref)
        extc_ref[...] = jnp.zeros_like(extc_ref)
        state_ref[...] = jnp.zeros_like(state_ref)

    def conv_silu(u_ref, ext_ref, w_ref, bias_ref):
        ext_ref[0:SUBLANES, :] = ext_ref[ts:ts + SUBLANES, :]
        ext_ref[SUBLANES:, :] = u_ref[...]
        w = w_ref[...]
        acc = bias_ref[...] + w[SSD_CONV - 1:SSD_CONV, :] * u_ref[...]
        for sh in range(1, SSD_CONV):
            acc = acc + w[SSD_CONV - 1 - sh:SSD_CONV - sh, :] * ext_ref[pl.ds(SUBLANES - sh, ts), :]
        return _silu(acc)

    xs = conv_silu(xs_ref, extx_ref, cwx_ref, cbx_ref)
    bm = conv_silu(b_ref, extb_ref, cwb_ref, cbb_ref)
    cm = conv_silu(c_ref, extc_ref, cwc_ref, cbc_ref)
    dtr = dt_ref[...] + dtb_ref[...]
    dt = jnp.maximum(dtr, 0.0) + jnp.log1p(jnp.exp(-jnp.abs(dtr)))
    a = dt * (-jnp.exp(alog_ref[...]))
    z = z_ref[...]
    tri = tri_ref[...]
    dsk = dsk_ref[...]
    nw = nw_ref[...]
    li = lax.broadcasted_iota(jnp.int32, (SSD_CHUNK, SSD_CHUNK), 0)
    si = lax.broadcasted_iota(jnp.int32, (SSD_CHUNK, SSD_CHUNK), 1)
    causal = li >= si

    for ck in range(ts // SSD_CHUNK):
        r0 = ck * SSD_CHUNK
        a_c = a[r0:r0 + SSD_CHUNK]
        hi, mid, lo = _split3(a_c)
        acs = _dot(tri, hi) + _dot(tri, mid) + _dot(tri, lo)
        acs_t = acs.T
        dt_c = dt[r0:r0 + SSD_CHUNK]
        a_last = acs[SSD_CHUNK - 1:SSD_CHUNK, :]
        for g in range(SSD_GROUPS):
            cg = cm[r0:r0 + SSD_CHUNK, g * SSD_STATE:(g + 1) * SSD_STATE]
            bg = bm[r0:r0 + SSD_CHUNK, g * SSD_STATE:(g + 1) * SSD_STATE]
            cgb = cg.astype(BF16)
            cb = _dot_nt(cgb, bg.astype(BF16))
            bgt = bg.T.astype(BF16)
            ys = []
            for h in range(SSD_HPG):
                hh = g * SSD_HPG + h
                col = acs[:, hh:hh + 1]
                row = acs_t[hh:hh + 1, :]
                seg = col - row
                decay = jnp.where(causal, jnp.exp(jnp.where(causal, seg, 0.0)), 0.0)
                xs_h = xs[r0:r0 + SSD_CHUNK, hh * SSD_HEAD_DIM:(hh + 1) * SSD_HEAD_DIM]
                xdt = xs_h * dt_c[:, hh:hh + 1]
                y = _dot((cb * decay).astype(BF16), xdt.astype(BF16))
                al = a_last[:, hh:hh + 1]
                xw = xdt * jnp.exp(al - col)
                prev = state_ref[hh]
                y = y + _dot(cgb, prev.astype(BF16)) * jnp.exp(col)
                state_ref[hh] = prev * jnp.exp(al) + _dot(bgt, xw.astype(BF16))
                y = y + xs_h * dsk[:, hh:hh + 1]
                ys.append(y)
            yg = jnp.concatenate(ys, axis=1)
            u = yg * _silu(z[r0:r0 + SSD_CHUNK, g * SSD_GW:(g + 1) * SSD_GW])
            u = u * lax.rsqrt(jnp.mean(u * u, axis=-1, keepdims=True) + EPS)
            o_ref[r0:r0 + SSD_CHUNK, g * SSD_GW:(g + 1) * SSD_GW] = (
                u * nw[:, g * SSD_GW:(g + 1) * SSD_GW]).astype(o_ref.dtype)


def _pad_lanes(v, n=LANES):
    v = v.reshape(1, -1).astype(F32)
    return jnp.pad(v, ((0, 0), (0, n - v.shape[1])))


def _ssd_mixer(proj, bsz, seq, conv_w, conv_b, dt_bias, a_log, d_skip, norm_w, ts=256):
    nsteps = seq // ts
    row = lambda b, s: b * nsteps + s
    xw, bw, cw = conv_w[:, :SSD_WIDTH], conv_w[:, SSD_WIDTH:SSD_WIDTH + 512], conv_w[:, SSD_WIDTH + 512:]
    xb, bb, cb = conv_b[:SSD_WIDTH], conv_b[SSD_WIDTH:SSD_WIDTH + 512], conv_b[SSD_WIDTH + 512:]
    tri = jnp.asarray(np.tril(np.ones((SSD_CHUNK, SSD_CHUNK), np.float32)), BF16)
    const = lambda shape: pl.BlockSpec(shape, lambda b, s: (0,) * len(shape))
    kern = functools.partial(_ssd_kernel, ts=ts)
    return pl.pallas_call(
        kern,
        grid=(bsz, nsteps),
        in_specs=[pl.BlockSpec((ts, SSD_WIDTH), lambda b, s: (row(b, s), COL_Z // SSD_WIDTH)),
                  pl.BlockSpec((ts, SSD_WIDTH), lambda b, s: (row(b, s), COL_XS // SSD_WIDTH)),
                  pl.BlockSpec((ts, 512), lambda b, s: (row(b, s), COL_B // 512)),
                  pl.BlockSpec((ts, 512), lambda b, s: (row(b, s), COL_C // 512)),
                  pl.BlockSpec((ts, LANES), lambda b, s: (row(b, s), COL_DT // LANES)),
                  const((SSD_CONV, SSD_WIDTH)), const((SSD_CONV, 512)), const((SSD_CONV, 512)),
                  const((1, SSD_WIDTH)), const((1, 512)), const((1, 512)),
                  const((1, LANES)), const((1, LANES)), const((1, LANES)), const((1, SSD_WIDTH)),
                  const((SSD_CHUNK, SSD_CHUNK))],
        out_specs=pl.BlockSpec((ts, SSD_WIDTH), lambda b, s: (row(b, s), 0)),
        out_shape=jax.ShapeDtypeStruct((bsz * seq, SSD_WIDTH), BF16),
        scratch_shapes=[pltpu.VMEM((ts + SUBLANES, SSD_WIDTH), F32), pltpu.VMEM((ts + SUBLANES, 512), F32),
                        pltpu.VMEM((ts + SUBLANES, 512), F32),
                        pltpu.VMEM((SSD_HEADS, SSD_STATE, SSD_HEAD_DIM), F32)],
        compiler_params=_params(("arbitrary", "arbitrary")),
        name="ssd_mixer",
    )(proj, proj, proj, proj, proj, xw, bw, cw, xb.reshape(1, -1), bb.reshape(1, -1), cb.reshape(1, -1),
      _pad_lanes(dt_bias), _pad_lanes(a_log), _pad_lanes(d_skip), norm_w.reshape(1, -1), tri)


def _hgrn_kernel(q_ref, f_ref, i_ref, g_ref, lbl_ref, nw_ref, cum_ref, o_ref, state_ref, *, ts, layer):
    s = pl.program_id(1)

    @pl.when(s == 0)
    def _():
        state_ref[...] = jnp.zeros_like(state_ref)

    nck = ts // HGRN_CHUNK
    lg = lbl_ref[...]
    e = jnp.exp(lg - jnp.max(lg, axis=0, keepdims=True))
    sm = e / jnp.sum(e, axis=0, keepdims=True)
    ridx = lax.broadcasted_iota(jnp.int32, lg.shape, 0)
    lb = jnp.sum(jnp.where((ridx >= 1) & (ridx <= layer), sm, 0.0), axis=0, keepdims=True)

    cum = cum_ref[...]
    nw = nw_ref[...]
    ti = lax.broadcasted_iota(jnp.int32, (ts, ts), 0)
    tj = lax.broadcasted_iota(jnp.int32, (ts, ts), 1)
    blockcausal = (ti // HGRN_CHUNK == tj // HGRN_CHUNK) & (ti >= tj)
    tok = lax.broadcasted_iota(jnp.int32, (1, ts), 1)

    for h in range(HGRN_HEADS):
        c0 = h * HGRN_DIM
        lbh = lb[:, c0:c0 + HGRN_DIM]
        f = lbh + (1.0 - lbh) * _sigmoid(f_ref[:, c0:c0 + HGRN_DIM])
        logf = jnp.log(f)
        k = 1.0 - f
        q = _silu(q_ref[:, c0:c0 + HGRN_DIM])
        v = i_ref[:, c0:c0 + HGRN_DIM]
        hi, mid, lo = _split3(logf)
        b = _dot(cum, hi) + _dot(cum, mid) + _dot(cum, lo)
        b3 = b.reshape(nck, HGRN_CHUNK, HGRN_DIM)
        bref = jnp.broadcast_to(b3[:, HGRN_CHUNK // 2:HGRN_CHUNK // 2 + 1, :], b3.shape).reshape(ts, HGRN_DIM)
        blast = jnp.broadcast_to(b3[:, HGRN_CHUNK - 1:HGRN_CHUNK, :], b3.shape).reshape(ts, HGRN_DIM)
        qe = (q * jnp.exp(b - bref)).astype(BF16)
        ke = (k * jnp.exp(bref - b)).astype(BF16)
        kl = (k * jnp.exp(blast - b)).astype(BF16)
        qb = (q * jnp.exp(b)).astype(BF16)
        vb = v.astype(BF16)
        att = jnp.where(blockcausal, _dot_nt(qe, ke), 0.0)
        o = _dot(att.astype(BF16), vb)
        vt = v.T
        lhs = jnp.concatenate(
            [jnp.where(tok // HGRN_CHUNK == c, vt, 0.0) for c in range(nck)], axis=0).astype(BF16)
        st = _dot(lhs, kl)
        state = state_ref[h]
        outs = []
        for c in range(nck):
            t0 = c * HGRN_CHUNK
            outs.append(_dot_nt(qb[t0:t0 + HGRN_CHUNK], state.astype(BF16)))
            cd = jnp.exp(blast[t0:t0 + 1, :])
            state = state * cd + st[c * HGRN_DIM:(c + 1) * HGRN_DIM]
        state_ref[h] = state
        o = o + jnp.concatenate(outs, axis=0)
        o = o * lax.rsqrt(jnp.mean(o * o, axis=-1, keepdims=True) + EPS) * nw
        o_ref[:, c0:c0 + HGRN_DIM] = (o * _silu(g_ref[:, c0:c0 + HGRN_DIM])).astype(o_ref.dtype)


def _hgrn_cum_matrix(ts):
    t = np.arange(ts)
    same = (t[:, None] // HGRN_CHUNK) == (t[None, :] // HGRN_CHUNK)
    return (same & (t[None, :] <= t[:, None])).astype(np.float32)


def _hgrn_mixer(proj, bsz, seq, lb_logits, norm_w, layer, ts=256):
    nsteps = seq // ts
    row = lambda b, s: b * nsteps + s
    depth = lb_logits.shape[0]
    cum = jnp.asarray(_hgrn_cum_matrix(ts), BF16)
    kern = functools.partial(_hgrn_kernel, ts=ts, layer=layer)
    blk = lambda col: pl.BlockSpec((ts, HGRN_WIDTH), lambda b, s: (row(b, s), col // HGRN_WIDTH))
    return pl.pallas_call(
        kern,
        grid=(bsz, nsteps),
        in_specs=[blk(COL_HQ), blk(COL_HF), blk(COL_HI), blk(COL_HG),
                  pl.BlockSpec((depth, HGRN_WIDTH), lambda b, s: (0, 0)),
                  pl.BlockSpec((1, HGRN_DIM), lambda b, s: (0, 0)),
                  pl.BlockSpec((ts, ts), lambda b, s: (0, 0))],
        out_specs=pl.BlockSpec((ts, HGRN_WIDTH), lambda b, s: (row(b, s), 0)),
        out_shape=jax.ShapeDtypeStruct((bsz * seq, HGRN_WIDTH), BF16),
        scratch_shapes=[pltpu.VMEM((HGRN_HEADS, HGRN_DIM, HGRN_DIM), F32)],
        compiler_params=_params(("arbitrary", "arbitrary")),
        name="hgrn2_mixer",
    )(proj, proj, proj, proj, lb_logits.astype(F32), norm_w.reshape(1, HGRN_DIM).astype(F32), cum)


def _rel_bucket_np(dist):
    max_exact = REL_BUCKETS // 2
    d = np.maximum(dist, 0)
    ratio = np.maximum(d, 1).astype(np.float32) / np.float32(max_exact)
    log_ratio = np.log(ratio).astype(np.float32) / np.float32(math.log(REL_MAX_DIST / max_exact))
    large = np.minimum(max_exact + (log_ratio * np.float32(REL_BUCKETS - max_exact)).astype(np.int32),
                       REL_BUCKETS - 1)
    return np.where(d < max_exact, d, large).astype(np.int32)


def _bias_expand_kernel(rb_ref, bmap_ref, o_ref):
    h = pl.program_id(0)
    bm = bmap_ref[...]
    out = jnp.full(bm.shape, NEG_INF, F32)
    for k in range(REL_BUCKETS):
        out = jnp.where(bm == k, rb_ref[k, h] * LOG2E, out)
    o_ref[...] = out


def _bias_expand(rel_bias, bmap, tr):
    rows, cols = bmap.shape
    return pl.pallas_call(
        _bias_expand_kernel,
        grid=(NSA_HEADS, rows // tr),
        in_specs=[pl.BlockSpec(memory_space=pltpu.SMEM),
                  pl.BlockSpec((tr, cols), lambda h, i: (i, 0))],
        out_specs=pl.BlockSpec((None, tr, cols), lambda h, i: (h, i, 0)),
        out_shape=jax.ShapeDtypeStruct((NSA_HEADS, rows, cols), F32),
        compiler_params=_params(("arbitrary", "arbitrary")),
        name="nsa_bias_expand",
    )(rel_bias.astype(F32), bmap)


def _nsa_bias_tables(rel_bias, seq):
    ncp = seq // CMP_STRIDE
    t = np.arange(seq)[:, None]
    cmp_end = np.arange(ncp)[None, :] * CMP_STRIDE + CMP_BLOCK - 1
    d = t - cmp_end
    bmap_cmp = np.where(d >= 0, _rel_bucket_np(d), -1).astype(np.int32)
    l = np.arange(Q_BLOCK)[:, None]
    j = np.arange(WIN_KEYS)[None, :]
    dist = l - j + WINDOW
    win = np.where((dist >= 0) & (dist < WINDOW), _rel_bucket_np(dist), -1)
    far = _rel_bucket_np(np.arange(Q_BLOCK + 1, 8 * seq))
    assert (far == far[0]).all(), "distances beyond one query block must share a single bucket"
    bmap_tab = np.concatenate([win, np.full((Q_BLOCK, Q_BLOCK), far[0])], axis=1).astype(np.int32)
    bias_cmp = _bias_expand(rel_bias, jnp.asarray(bmap_cmp), 512)
    bias_tab = _bias_expand(rel_bias, jnp.asarray(bmap_tab), Q_BLOCK)
    return bias_cmp, bias_tab


def _cmp_kernel(u_ref, pe_ref, w1_ref, w2_ref, o_ref, *, ncp):
    half = CMP_BLOCK // 2
    pe = pe_ref[...]
    top = jnp.zeros((ncp, CMP_HIDDEN), F32)
    bot = jnp.zeros((ncp, CMP_HIDDEN), F32)
    for l in range(half):
        x = u_ref[pl.ds(l, ncp, stride=half), :]
        top = top + _dot((x + pe[l:l + 1, :]).astype(BF16), w1_ref[l])
        bot = bot + _dot((x + pe[half + l:half + l + 1, :]).astype(BF16), w1_ref[half + l])
    hid = top + pltpu.roll(bot, ncp - 1, 0)
    o_ref[...] = _dot(_silu(hid).astype(BF16), w2_ref[...])


def _nsa_compress(proj, bsz, seq, pe, w1, w2):
    ncp = seq // CMP_STRIDE
    kern = functools.partial(_cmp_kernel, ncp=ncp)
    return pl.pallas_call(
        kern,
        grid=(bsz, 2, NSA_KV_HEADS),
        in_specs=[pl.BlockSpec((seq, NSA_HEAD_DIM), lambda b, t, h: (b, COL_KC // NSA_HEAD_DIM + NSA_KV_HEADS * t + h)),
                  pl.BlockSpec((None, CMP_BLOCK, NSA_HEAD_DIM), lambda b, t, h: (t, 0, 0)),
                  pl.BlockSpec((None, CMP_BLOCK, NSA_HEAD_DIM, CMP_HIDDEN), lambda b, t, h: (t, 0, 0, 0)),
                  pl.BlockSpec((None, CMP_HIDDEN, NSA_HEAD_DIM), lambda b, t, h: (t, 0, 0))],
        out_specs=pl.BlockSpec((None, None, None, ncp, NSA_HEAD_DIM), lambda b, t, h: (b, t, h, 0, 0)),
        out_shape=jax.ShapeDtypeStruct((bsz, 2, NSA_KV_HEADS, ncp, NSA_HEAD_DIM), F32),
        compiler_params=_params(("arbitrary", "arbitrary", "arbitrary")),
        name="nsa_compress",
    )(proj, pe.astype(F32), w1.reshape(2, CMP_BLOCK, NSA_HEAD_DIM, CMP_HIDDEN).astype(BF16), w2.astype(BF16))


def _softmax_start(s, v):
    m = jnp.max(s, axis=-1, keepdims=True)
    p = jnp.exp2(s - m)
    return m, jnp.sum(p, axis=-1, keepdims=True), _dot(p.astype(BF16), v)


def _softmax_update(s, v, carry):
    m, l, acc = carry
    m_new = jnp.maximum(m, jnp.max(s, axis=-1, keepdims=True))
    alpha = jnp.exp2(m - m_new)
    p = jnp.exp2(s - m_new)
    return m_new, alpha * l + jnp.sum(p, axis=-1, keepdims=True), alpha * acc + _dot(p.astype(BF16), v)


def _softmax_finish(carry):
    m, l, acc = carry
    return jnp.where(m > 0.5 * NEG_INF, acc / jnp.maximum(l, 1e-30), 0.0)


def _nsa_kernel(q_ref, gate_ref, kc_ref, vc_ref, ks_ref, vs_ref, kw_ref, vw_ref, bcmp_ref, btab_ref,
                kone_ref, c2s_ref, o_ref, kaug_ref, vsb_ref, kwb_ref, vwb_ref, sa_ref, sb_ref, *, n_sel, nqb):
    hk = pl.program_id(1)
    c0 = pl.program_id(2) * nqb
    rows = NSA_GQA * Q_BLOCK
    lane = lax.broadcasted_iota(jnp.int32, (Q_BLOCK, LANES), 1)

    @pl.when(c0 == 0)
    def _():
        kaug_ref[0:Q_BLOCK, 0:NSA_HEAD_DIM] = jnp.zeros((Q_BLOCK, NSA_HEAD_DIM), BF16)
        kaug_ref[Q_BLOCK:, 0:NSA_HEAD_DIM] = ks_ref[...].astype(BF16)
        kaug_ref[:, NSA_HEAD_DIM:] = kone_ref[...]
        vsb_ref[0:Q_BLOCK, :] = jnp.zeros((Q_BLOCK, NSA_HEAD_DIM), BF16)
        vsb_ref[Q_BLOCK:, :] = vs_ref[...].astype(BF16)
        kwb_ref[0:WINDOW, 0:NSA_HEAD_DIM] = jnp.zeros((WINDOW, NSA_HEAD_DIM), BF16)
        kwb_ref[WINDOW:, 0:NSA_HEAD_DIM] = kw_ref[...].astype(BF16)
        for r0 in range(0, WINDOW, Q_BLOCK):
            kwb_ref[r0:r0 + Q_BLOCK, NSA_HEAD_DIM:] = jnp.where(lane == AUG_PAD, 1.0, 0.0).astype(BF16)
        kwb_ref[WINDOW:, NSA_HEAD_DIM:] = jnp.zeros((kw_ref.shape[0], LANES), BF16)
        vwb_ref[0:WINDOW, :] = jnp.zeros((WINDOW, NSA_HEAD_DIM), BF16)
        vwb_ref[WINDOW:, :] = vw_ref[...].astype(BF16)

    gens = [_nsa_unit(c0 + u, q_ref[u * Q_BLOCK:(u + 1) * Q_BLOCK, :], bcmp_ref[:, u * Q_BLOCK:(u + 1) * Q_BLOCK, :],
                      btab_ref, kc_ref, vc_ref, c2s_ref, kaug_ref, vsb_ref, kwb_ref, vwb_ref, n_sel)
            for u in range(nqb)]
    units = [None] * nqb
    while any(un is None for un in units):
        for u, gen in enumerate(gens):
            if units[u] is None:
                units[u] = next(gen)
    qaug_far = jnp.concatenate([un[3] for un in units], axis=0)
    carry = tuple(jnp.concatenate([un[2][i] for un in units], axis=0) for i in range(3))

    far_blocks = FAR_KEYS // Q_BLOCK
    n_pairs = (c0 + nqb - 2 + 2 * far_blocks - 1) // (2 * far_blocks)
    last_slab = (kaug_ref.shape[0] - Q_BLOCK) // FAR_KEYS - 1

    def far_logits(slab):
        r0 = pl.multiple_of(jnp.minimum(slab, last_slab) * FAR_KEYS + Q_BLOCK, Q_BLOCK)
        return _dot_nt(qaug_far, kaug_ref[pl.ds(r0, FAR_KEYS), :])

    def far_values(slab):
        return vsb_ref[pl.ds(pl.multiple_of(slab * FAR_KEYS + Q_BLOCK, Q_BLOCK), FAR_KEYS), :]

    sa_ref[...] = far_logits(0)

    def far_body(k, carry):
        sb_ref[...] = far_logits(2 * k + 1)
        carry = _softmax_update(sa_ref[...], far_values(2 * k), carry)
        sa_ref[...] = far_logits(2 * k + 2)
        return _softmax_update(sb_ref[...], far_values(2 * k + 1), carry)

    o_sel = _softmax_finish(lax.fori_loop(0, n_pairs, far_body, carry))

    for u in range(nqb):
        o_cmp, o_win = units[u][0], units[u][1]
        gates = _sigmoid(gate_ref[u * Q_BLOCK:(u + 1) * Q_BLOCK, :])
        for g in range(NSA_GQA):
            def gate_col(br):
                return jnp.sum(jnp.where(lane == hk * (3 * NSA_GQA) + g * 3 + br, gates, 0.0), axis=-1, keepdims=True)

            r0 = g * Q_BLOCK
            out = (gate_col(0) * o_cmp[r0:r0 + Q_BLOCK] + gate_col(1) * o_sel[u * rows + r0:u * rows + r0 + Q_BLOCK]
                   + gate_col(2) * o_win[r0:r0 + Q_BLOCK])
            o_ref[u * Q_BLOCK:(u + 1) * Q_BLOCK, g * NSA_HEAD_DIM:(g + 1) * NSA_HEAD_DIM] = out.astype(o_ref.dtype)


def _nsa_unit(c, q_in, bc_in, btab_ref, kc_ref, vc_ref, c2s_ref, kaug_ref, vsb_ref, kwb_ref, vwb_ref, n_sel):
    rows = NSA_GQA * Q_BLOCK
    lane = lax.broadcasted_iota(jnp.int32, (Q_BLOCK, LANES), 1)
    q = q_in * (NSA_HEAD_DIM ** -0.5 * LOG2E)
    q3 = jnp.concatenate([q[:, g * NSA_HEAD_DIM:(g + 1) * NSA_HEAD_DIM] for g in range(NSA_GQA)],
                         axis=0).astype(BF16)

    btab = btab_ref[...].reshape(rows, WIN_KEYS + Q_BLOCK)
    near0 = pl.multiple_of(c * Q_BLOCK, Q_BLOCK)

    qaug_win = jnp.concatenate(
        [q3, jnp.concatenate([jnp.where(lane == AUG_PAD, NEG_INF, 0.0).astype(BF16)] * NSA_GQA, axis=0)], axis=1)
    s_win = _dot_nt(qaug_win, kwb_ref[pl.ds(near0, WIN_KEYS), :]) + btab[:, 0:WIN_KEYS]
    yield None

    bc = bc_in.reshape(rows, bc_in.shape[-1])
    valid = bc > 0.5 * NEG_INF
    s = _dot_nt(q3, kc_ref[...].astype(BF16)) + bc
    yield None
    o_win = _softmax_finish(_softmax_start(s_win, vwb_ref[pl.ds(near0, WIN_KEYS), :]))
    yield None
    m = jnp.max(s, axis=-1, keepdims=True)
    p = jnp.where(valid, jnp.exp2(s - m), 0.0)
    p = p / jnp.maximum(jnp.sum(p, axis=-1, keepdims=True), 1e-30)
    o_cmp = _dot(p.astype(BF16), vc_ref[...].astype(BF16))

    psum = p[0:Q_BLOCK] + p[Q_BLOCK:2 * Q_BLOCK] + p[2 * Q_BLOCK:3 * Q_BLOCK]
    ph = psum.astype(BF16)
    plo = (psum - ph.astype(F32)).astype(BF16)
    c2s = c2s_ref[...]
    imp_t = _dot_nt(c2s, ph) + _dot_nt(c2s, plo)
    nsp = -(-n_sel // SUBLANES) * SUBLANES
    imp_t = imp_t[0:nsp]
    tq = c * Q_BLOCK + lax.broadcasted_iota(jnp.int32, (nsp, Q_BLOCK), 1)
    jj = lax.broadcasted_iota(jnp.int32, (nsp, Q_BLOCK), 0)
    valid_sel = (jj < n_sel) & (jj * SEL_BLOCK <= tq)
    back = tq // SEL_BLOCK - jj
    force = (jj == 0) | ((back >= 0) & (back < SEL_LOCAL))
    score = jnp.where(valid_sel, imp_t + jnp.where(force, FORCE_BONUS, 0.0), NEG_INF)
    yield None
    n_tiles = nsp // SUBLANES
    tiles = [score[v * SUBLANES:(v + 1) * SUBLANES] for v in range(n_tiles)]
    sub = lax.broadcasted_iota(jnp.int32, (SUBLANES, Q_BLOCK), 0)
    cnts = [jnp.zeros((SUBLANES, Q_BLOCK), F32) for _ in range(n_tiles)]
    for jp in range(n_sel):
        v0, r0 = divmod(jp, SUBLANES)
        r = tiles[v0][r0:r0 + 1, :]
        for v in range(n_tiles):
            gt = jnp.where(r > tiles[v], 1.0, 0.0)
            ge = jnp.where(r >= tiles[v], 1.0, 0.0)
            cnts[v] = cnts[v] + (gt if v < v0 else ge if v > v0 else jnp.where(sub > r0, ge, gt))
        if jp % (2 * SUBLANES) == 2 * SUBLANES - 1:
            yield None
    cnt = jnp.concatenate(cnts, axis=0)
    keep = valid_sel & (cnt < float(min(SEL_TOPK, n_sel)))

    ext = lax.broadcasted_iota(jnp.int32, (LANES - nsp, Q_BLOCK), 0) + nsp
    aug = jnp.concatenate([jnp.where(keep, 0.0, NEG_INF), jnp.where(ext == AUG_PAD, NEG_INF, 0.0)], axis=0)
    aug_near = aug.T
    aug_far = jnp.where((lane >= 2 * (c - 1)) & (lane < AUG_BIAS_HI), NEG_INF, aug_near)
    far_parts = []
    for g in range(NSA_GQA):
        bfar = btab[g * Q_BLOCK:(g + 1) * Q_BLOCK, WIN_KEYS:]
        hi = bfar.astype(BF16).astype(F32)
        far_parts.append(jnp.where(lane == AUG_BIAS_HI, hi, jnp.where(lane == AUG_BIAS_LO, bfar - hi, aug_far)))
    qaug_near = jnp.concatenate([q3, jnp.concatenate([aug_near.astype(BF16)] * NSA_GQA, axis=0)], axis=1)
    qaug_far = jnp.concatenate([q3, jnp.concatenate(far_parts, axis=0).astype(BF16)], axis=1)

    s = _dot_nt(qaug_near, kaug_ref[pl.ds(near0, NEAR_KEYS), :]) + btab[:, WIN_KEYS - NEAR_KEYS:WIN_KEYS]
    yield None
    carry = _softmax_start(s, vsb_ref[pl.ds(near0, NEAR_KEYS), :])
    yield o_cmp, o_win, carry, qaug_far


def _nsa_attention(proj, cmp_kv, bias_cmp, bias_tab, bsz, seq):
    nq = seq // Q_BLOCK
    ncp = seq // CMP_STRIDE
    n_sel = seq // SEL_BLOCK
    assert n_sel <= AUG_BIAS_HI and NSA_HEAD_DIM == LANES
    assert seq % (2 * FAR_KEYS) == 0, "far steps walk the keys two 512-key slabs at a time"
    key = np.arange(-Q_BLOCK, seq)[:, None]
    ln = np.arange(LANES)[None, :]
    kone = np.where(key < 0, ln == AUG_PAD,
                    ((ln < AUG_BIAS_HI) & (key // SEL_BLOCK == ln)) | (ln == AUG_BIAS_HI) | (ln == AUG_BIAS_LO))
    kone = jnp.asarray(kone, BF16)
    c_start = np.arange(ncp)[None, :] * CMP_STRIDE
    s_start = np.arange(LANES)[:, None] * SEL_BLOCK
    overlap = np.clip(np.minimum(c_start + CMP_BLOCK, s_start + SEL_BLOCK) - np.maximum(c_start, s_start), 0, None)
    overlap = np.where(np.arange(LANES)[:, None] < n_sel, overlap, 0)
    c2s_t = jnp.asarray(overlap / CMP_BLOCK, BF16)
    gqa_w = NSA_GQA * NSA_HEAD_DIM
    full = lambda col: pl.BlockSpec((seq, NSA_HEAD_DIM), lambda b, h, c: (b, col // NSA_HEAD_DIM + h))
    nqb = NSA_Q_BLOCKS_PER_STEP
    ns = nq // nqb
    qr = nqb * Q_BLOCK
    kern = functools.partial(_nsa_kernel, n_sel=n_sel, nqb=nqb)
    return pl.pallas_call(
        kern,
        grid=(bsz, NSA_KV_HEADS, ns),
        in_specs=[pl.BlockSpec((qr, gqa_w), lambda b, h, c: (b * ns + c, COL_Q // gqa_w + h)),
                  pl.BlockSpec((qr, LANES), lambda b, h, c: (b * ns + c, COL_GATE // LANES)),
                  pl.BlockSpec((None, None, None, ncp, NSA_HEAD_DIM), lambda b, h, c: (b, 0, h, 0, 0)),
                  pl.BlockSpec((None, None, None, ncp, NSA_HEAD_DIM), lambda b, h, c: (b, 1, h, 0, 0)),
                  full(COL_KS), full(COL_VS), full(COL_KW), full(COL_VW),
                  pl.BlockSpec((NSA_GQA, qr, ncp), lambda b, h, c: (h, c, 0)),
                  pl.BlockSpec((NSA_GQA, Q_BLOCK, WIN_KEYS + Q_BLOCK), lambda b, h, c: (h, 0, 0)),
                  pl.BlockSpec((seq + Q_BLOCK, LANES), lambda b, h, c: (0, 0)),
                  pl.BlockSpec((LANES, ncp), lambda b, h, c: (0, 0))],
        out_specs=pl.BlockSpec((qr, gqa_w), lambda b, h, c: (b * ns + c, h)),
        out_shape=jax.ShapeDtypeStruct((bsz * seq, NSA_WIDTH), BF16),
        scratch_shapes=[pltpu.VMEM((seq + Q_BLOCK, 2 * NSA_HEAD_DIM), BF16),
                        pltpu.VMEM((seq + Q_BLOCK, NSA_HEAD_DIM), BF16),
                        pltpu.VMEM((seq + WINDOW, 2 * NSA_HEAD_DIM), BF16),
                        pltpu.VMEM((seq + WINDOW, NSA_HEAD_DIM), BF16),
                        pltpu.VMEM((nqb * NSA_GQA * Q_BLOCK, FAR_KEYS), F32),
                        pltpu.VMEM((nqb * NSA_GQA * Q_BLOCK, FAR_KEYS), F32)],
        compiler_params=_params(("arbitrary", "arbitrary", "arbitrary")),
        name="nsa_attention",
    )(proj, proj, cmp_kv, cmp_kv, proj, proj, proj, proj, bias_cmp, bias_tab, kone, c2s_t)


_IN_OFF = tuple(int(v) for v in np.cumsum((0,) + IN_SIZES))
_W_IN_COPIES = ((_IN_OFF[0], SSD_WIDTH, COL_Z), (_IN_OFF[1], SSD_WIDTH, COL_XS),
                (_IN_OFF[1] + SSD_WIDTH, 512, COL_B), (_IN_OFF[1] + SSD_WIDTH + 512, 512, COL_C),
                (_IN_OFF[3], NSA_WIDTH, COL_Q), (_IN_OFF[4], 6 * NSA_KV_WIDTH, COL_KC),
                (_IN_OFF[11], 4 * HGRN_WIDTH, COL_HQ))
_W_IN_NARROW = ((_IN_OFF[2], IN_SIZES[2], COL_DT, COL_GATE - COL_DT), (_IN_OFF[10], IN_SIZES[10], COL_GATE, COL_HQ - COL_GATE))


def _relayout_kernel(w_ref, o_ref):
    for src, width, dst in _W_IN_COPIES:
        o_ref[:, dst:dst + width] = w_ref[:, src:src + width].astype(BF16)
    for src, valid, dst, padded in _W_IN_NARROW:
        tile = w_ref[:, src:src + LANES]
        lane = lax.broadcasted_iota(jnp.int32, tile.shape, 1)
        o_ref[:, dst:dst + LANES] = jnp.where(lane < valid, tile, 0.0).astype(BF16)
        if padded > LANES:
            o_ref[:, dst + LANES:dst + padded] = jnp.zeros((tile.shape[0], padded - LANES), BF16)


def _relayout_w_in(w, tr=128):
    depth, d, n = w.shape
    return pl.pallas_call(
        _relayout_kernel,
        grid=(depth, d // tr),
        in_specs=[pl.BlockSpec((None, tr, n), lambda l, i: (l, i, 0))],
        out_specs=pl.BlockSpec((None, tr, IN_PAD), lambda l, i: (l, i, 0)),
        out_shape=jax.ShapeDtypeStruct((depth, d, IN_PAD), BF16),
        compiler_params=_params(("parallel", "parallel")),
        name="w_in_relayout",
    )(w)


def _tiles(seq):
    return dict(in_proj=(1024, 1024), out_proj=(1024, 512), gate_up=(min(1024, seq), 256), down=(256, 1024))


def kernel(x, norm_mix_w, w_in, ssd_conv_w, ssd_conv_b, ssd_dt_bias, ssd_a_log, ssd_d, ssd_norm_w, nsa_cmp_pe, nsa_cmp_w1, nsa_cmp_w2, rel_bias, hgrn_lb_logits, hgrn_norm_w, w_out, norm_ffn_w, ffn_w_gu, ffn_conv_w, ffn_conv_b, ffn_w_down, norm_f_w):
    bsz, seq, d = x.shape
    depth = w_in.shape[0]
    xr = x.reshape(bsz * seq, d).astype(F32)
    bias_cmp, bias_tab = _nsa_bias_tables(rel_bias, seq)
    tiles = _tiles(seq)
    w_in_b = _relayout_w_in(w_in.astype(BF16))
    w_out_b = w_out.astype(BF16)
    w_gu_b = ffn_w_gu.astype(BF16)
    w_down_b = ffn_w_down.astype(BF16)
    for l in range(depth):
        n = _rmsnorm(xr, norm_mix_w[l], BF16)
        proj = _matmul(n, w_in_b, l, F32, *tiles["in_proj"], name="in_proj")
        y_ssd = _ssd_mixer(proj, bsz, seq, ssd_conv_w[l].astype(F32), ssd_conv_b[l].astype(F32), ssd_dt_bias[l],
                           ssd_a_log[l], ssd_d[l], ssd_norm_w[l].astype(F32))
        cmp_kv = _nsa_compress(proj, bsz, seq, nsa_cmp_pe[l], nsa_cmp_w1[l], nsa_cmp_w2[l])
        y_nsa = _nsa_attention(proj, cmp_kv, bias_cmp, bias_tab, bsz, seq)
        y_hgrn = _hgrn_mixer(proj, bsz, seq, hgrn_lb_logits, hgrn_norm_w[l], l)
        mix = jnp.concatenate([y_ssd, y_nsa, y_hgrn], axis=1)
        xr = _matmul_residual(mix, w_out_b, l, xr, *tiles["out_proj"], name="out_proj")
        hn = _rmsnorm(xr, norm_ffn_w[l], BF16)
        act = _gate_up(hn, w_gu_b, l, ffn_conv_w[l].astype(F32), ffn_conv_b[l].astype(F32), seq, *tiles["gate_up"])
        xr = _matmul_residual_wres(act, w_down_b, l, xr, *tiles["down"], name="ffn_down")
    out = _rmsnorm(xr, norm_f_w, x.dtype)
    return out.reshape(bsz, seq, d)
```

```python
import functools
import math

import numpy as np
import jax
import jax.numpy as jnp
from jax import lax
from jax.experimental import pallas as pl
from jax.experimental.pallas import tpu as pltpu

F32 = jnp.float32
BF16 = jnp.bfloat16

D_MODEL = 4096
SSD_HEAD_DIM = 64
SSD_WIDTH = 1536
SSD_HEADS = 24
SSD_GROUPS = 4
SSD_HPG = 6
SSD_STATE = 128
SSD_CONV = 4
SSD_CHUNK = 128
SSD_GW = SSD_WIDTH // SSD_GROUPS
NSA_HEAD_DIM = 128
NSA_WIDTH = 1536
NSA_HEADS = 12
NSA_KV_HEADS = 4
NSA_GQA = 3
NSA_KV_WIDTH = 512
CMP_BLOCK = 32
CMP_STRIDE = 16
CMP_HIDDEN = 256
SEL_BLOCK = 64
SEL_TOPK = 16
SEL_LOCAL = 2
WINDOW = 512
Q_BLOCK = 128
FORCE_BONUS = 1e4
HGRN_WIDTH = 1024
HGRN_HEADS = 8
HGRN_DIM = 128
HGRN_CHUNK = 32
REL_BUCKETS = 32
REL_MAX_DIST = 128
D_FF = 11008
FFN_CONV = 3
EPS = 1e-6
NEG_INF = -1e30

IN_SIZES = (SSD_WIDTH, SSD_WIDTH + 2 * SSD_GROUPS * SSD_STATE, SSD_HEADS, NSA_WIDTH,
            NSA_KV_WIDTH, NSA_KV_WIDTH, NSA_KV_WIDTH, NSA_KV_WIDTH, NSA_KV_WIDTH, NSA_KV_WIDTH,
            3 * NSA_HEADS, HGRN_WIDTH, HGRN_WIDTH, HGRN_WIDTH, HGRN_WIDTH)

LANES = 128
SUBLANES = 8
VMEM_LIMIT = 56 * 1024 * 1024

COL_Z = 0
COL_XS = 1536
COL_Q = 3072
COL_B = 4608
COL_C = 5120
COL_KC = 5632
COL_VC = 6144
COL_KS = 6656
COL_VS = 7168
COL_KW = 7680
COL_VW = 8192
COL_DT = 8704
COL_GATE = 8832
COL_HQ = 9216
COL_HF = 10240
COL_HI = 11264
COL_HG = 12288
IN_PAD = 13312

LOG2E = math.log2(math.e)
AUG_BIAS_HI = 64
AUG_BIAS_LO = 65
AUG_PAD = 66
WIN_KEYS = WINDOW + Q_BLOCK
NEAR_KEYS = 2 * Q_BLOCK
FAR_KEYS = 4 * Q_BLOCK
NSA_Q_BLOCKS_PER_STEP = 2


def _params(semantics):
    return pltpu.CompilerParams(dimension_semantics=semantics, vmem_limit_bytes=VMEM_LIMIT)


def _sigmoid(x):
    return 1.0 / (1.0 + jnp.exp(-x))


def _silu(x):
    return x * _sigmoid(x)


def _dot(a, b):
    return jnp.dot(a, b, preferred_element_type=F32)


def _dot_nt(a, b):
    return lax.dot_general(a, b, (((1,), (1,)), ((), ())), preferred_element_type=F32)


def _split3(x):
    hi = x.astype(BF16)
    r1 = x - hi.astype(F32)
    mid = r1.astype(BF16)
    lo = (r1 - mid.astype(F32)).astype(BF16)
    return hi, mid, lo


def _rmsnorm_kernel(x_ref, w_ref, o_ref):
    x = x_ref[...]
    ms = jnp.mean(x * x, axis=-1, keepdims=True)
    o_ref[...] = (x * lax.rsqrt(ms + EPS) * w_ref[...]).astype(o_ref.dtype)


def _rmsnorm(x, w, out_dtype, tm=256):
    m, d = x.shape
    return pl.pallas_call(
        _rmsnorm_kernel,
        grid=(m // tm,),
        in_specs=[pl.BlockSpec((tm, d), lambda i: (i, 0)), pl.BlockSpec((1, d), lambda i: (0, 0))],
        out_specs=pl.BlockSpec((tm, d), lambda i: (i, 0)),
        out_shape=jax.ShapeDtypeStruct((m, d), out_dtype),
        compiler_params=_params(("parallel",)),
        name="rmsnorm",
    )(x, w.reshape(1, d).astype(F32))


def _mm_kernel(a_ref, w_ref, o_ref):
    o_ref[...] = _dot(a_ref[...], w_ref[...]).astype(o_ref.dtype)


def _mm_res_kernel(a_ref, w_ref, r_ref, o_ref):
    o_ref[...] = r_ref[...] + _dot(a_ref[...], w_ref[...])


def _matmul(a, w, layer, out_dtype, tm, tn, name):
    m, k = a.shape
    n = w.shape[2]
    return pl.pallas_call(
        _mm_kernel,
        grid=(m // tm, n // tn),
        in_specs=[pl.BlockSpec((tm, k), lambda i, j: (i, 0)),
                  pl.BlockSpec((None, k, tn), lambda i, j: (layer, 0, j))],
        out_specs=pl.BlockSpec((tm, tn), lambda i, j: (i, j)),
        out_shape=jax.ShapeDtypeStruct((m, n), out_dtype),
        compiler_params=_params(("parallel", "arbitrary")),
        name=name,
    )(a, w)


def _matmul_residual(a, w, layer, res, tm, tn, name):
    m, k = a.shape
    n = w.shape[2]
    return pl.pallas_call(
        _mm_res_kernel,
        grid=(m // tm, n // tn),
        in_specs=[pl.BlockSpec((tm, k), lambda i, j: (i, 0)),
                  pl.BlockSpec((None, k, tn), lambda i, j: (layer, 0, j)),
                  pl.BlockSpec((tm, tn), lambda i, j: (i, j))],
        out_specs=pl.BlockSpec((tm, tn), lambda i, j: (i, j)),
        out_shape=jax.ShapeDtypeStruct((m, n), F32),
        compiler_params=_params(("parallel", "arbitrary")),
        name=name,
    )(a, w, res)


def _matmul_residual_wres(a, w, layer, res, tm, tn, name):
    m, k = a.shape
    n = w.shape[2]
    return pl.pallas_call(
        _mm_res_kernel,
        grid=(n // tn, m // tm),
        in_specs=[pl.BlockSpec((tm, k), lambda j, i: (i, 0)),
                  pl.BlockSpec((None, k, tn), lambda j, i: (layer, 0, j), pipeline_mode=pl.Buffered(1)),
                  pl.BlockSpec((tm, tn), lambda j, i: (i, j))],
        out_specs=pl.BlockSpec((tm, tn), lambda j, i: (i, j)),
        out_shape=jax.ShapeDtypeStruct((m, n), F32),
        compiler_params=_params(("arbitrary", "arbitrary")),
        name=name,
    )(a, w, res)


def _gu_kernel(h_ref, wg_ref, wu_ref, cw_ref, cb_ref, o_ref, ext_ref, halo_ref, *, tm, tn, tiles_per_seq):
    i = pl.program_id(0)
    j = pl.program_id(1)
    h = h_ref[...]
    g = _dot(h, wg_ref[...])
    up = _dot(h, wu_ref[...])
    prev = halo_ref[j]
    ext_ref[0:SUBLANES, :] = jnp.where(i % tiles_per_seq == 0, 0.0, prev)
    ext_ref[SUBLANES:, :] = g
    halo_ref[j] = g[tm - SUBLANES:, :]
    cw = cw_ref[...]
    acc = cb_ref[...] + cw[FFN_CONV - 1:FFN_CONV, :] * g
    for sh in range(1, FFN_CONV):
        acc = acc + cw[FFN_CONV - 1 - sh:FFN_CONV - sh, :] * ext_ref[pl.ds(SUBLANES - sh, tm), :]
    o_ref[...] = (_silu(acc) * up).astype(o_ref.dtype)


def _gate_up(h, w_gu, layer, conv_w, conv_b, seq, tm, tn):
    m, k = h.shape
    nf = conv_w.shape[1]
    nj = nf // tn
    kern = functools.partial(_gu_kernel, tm=tm, tn=tn, tiles_per_seq=seq // tm)
    return pl.pallas_call(
        kern,
        grid=(m // tm, nj),
        in_specs=[pl.BlockSpec((tm, k), lambda i, j: (i, 0)),
                  pl.BlockSpec((None, k, tn), lambda i, j: (layer, 0, j)),
                  pl.BlockSpec((None, k, tn), lambda i, j: (layer, 0, j + nj)),
                  pl.BlockSpec((FFN_CONV, tn), lambda i, j: (0, j)),
                  pl.BlockSpec((1, tn), lambda i, j: (0, j))],
        out_specs=pl.BlockSpec((tm, tn), lambda i, j: (i, j)),
        out_shape=jax.ShapeDtypeStruct((m, nf), BF16),
        scratch_shapes=[pltpu.VMEM((tm + SUBLANES, tn), F32), pltpu.VMEM((nj, SUBLANES, tn), F32)],
        compiler_params=_params(("arbitrary", "arbitrary")),
        name="ffn_gate_up_conv",
    )(h, w_gu, w_gu, conv_w, conv_b.reshape(1, nf))


def _ssd_kernel(z_ref, xs_ref, b_ref, c_ref, dt_ref, cwx_ref, cwb_ref, cwc_ref, cbx_ref, cbb_ref, cbc_ref,
                dtb_ref, alog_ref, dsk_ref, nw_ref, tri_ref, o_ref,
                extx_ref, extb_ref, extc_ref, state_ref, *, ts):
    s = pl.program_id(1)

    @pl.when(s == 0)
    def _():
        extx_ref[...] = jnp.zeros_like(extx_ref)
        extb_ref[...] = jnp.zeros_like(extb_ref)
        extc_ref[...] = jnp.zeros_like(extc_ref)
        state_ref[...] = jnp.zeros_like(state_ref)

    def conv_silu(u_ref, ext_ref, w_ref, bias_ref):
        ext_ref[0:SUBLANES, :] = ext_ref[ts:ts + SUBLANES, :]
        ext_ref[SUBLANES:, :] = u_ref[...]
        w = w_ref[...]
        acc = bias_ref[...] + w[SSD_CONV - 1:SSD_CONV, :] * u_ref[...]
        for sh in range(1, SSD_CONV):
            acc = acc + w[SSD_CONV - 1 - sh:SSD_CONV - sh, :] * ext_ref[pl.ds(SUBLANES - sh, ts), :]
        return _silu(acc)

    xs = conv_silu(xs_ref, extx_ref, cwx_ref, cbx_ref)
    bm = conv_silu(b_ref, extb_ref, cwb_ref, cbb_ref)
    cm = conv_silu(c_ref, extc_ref, cwc_ref, cbc_ref)
    dtr = dt_ref[...] + dtb_ref[...]
    dt = jnp.maximum(dtr, 0.0) + jnp.log1p(jnp.exp(-jnp.abs(dtr)))
    a = dt * (-jnp.exp(alog_ref[...]))
    z = z_ref[...]
    tri = tri_ref[...]
    dsk = dsk_ref[...]
    nw = nw_ref[...]
    li = lax.broadcasted_iota(jnp.int32, (SSD_CHUNK, SSD_CHUNK), 0)
    si = lax.broadcasted_iota(jnp.int32, (SSD_CHUNK, SSD_CHUNK), 1)
    causal = li >= si
    low_half = si < SSD_HEAD_DIM
    low_half_row = low_half[0:1, :]

    for ck in range(ts // SSD_CHUNK):
        r0 = ck * SSD_CHUNK
        a_c = a[r0:r0 + SSD_CHUNK]
        hi, mid, lo = _split3(a_c)
        acs = _dot(tri, hi) + _dot(tri, mid) + _dot(tri, lo)
        acs_t = acs.T
        dt_t = dt[r0:r0 + SSD_CHUNK].T
        a_last_b = jnp.broadcast_to(acs_t[:, SSD_CHUNK - 1:SSD_CHUNK], acs_t.shape)
        dtd_t = dt_t * jnp.exp(a_last_b - acs_t)
        chunk_decay = jnp.exp(acs[SSD_CHUNK - 1:SSD_CHUNK, :])
        for g in range(SSD_GROUPS):
            cg = cm[r0:r0 + SSD_CHUNK, g * SSD_STATE:(g + 1) * SSD_STATE]
            bg = bm[r0:r0 + SSD_CHUNK, g * SSD_STATE:(g + 1) * SSD_STATE]
            cb = _dot_nt(cg.astype(BF16), bg.astype(BF16))
            bgt = bg.T
            ys = []
            for pr in range(SSD_HPG // 2):
                pair = (g * SSD_HPG) // 2 + pr
                c0 = pair * LANES
                xs_p = xs[r0:r0 + SSD_CHUNK, c0:c0 + LANES].astype(BF16)
                prev = state_ref[pair]
                rhs = jnp.concatenate([xs_p, prev.astype(BF16)], axis=0)
                y_h, st_h, cd_h = [], [], []
                for hh in (2 * pair, 2 * pair + 1):
                    colb = jnp.broadcast_to(acs[:, hh:hh + 1], (SSD_CHUNK, SSD_CHUNK))
                    seg = colb - acs_t[hh:hh + 1, :]
                    decay = jnp.where(causal, jnp.exp(jnp.where(causal, seg, 0.0)), 0.0)
                    intra = cb * decay * dt_t[hh:hh + 1, :]
                    inter = cg * jnp.exp(colb)
                    lhs = jnp.concatenate([intra, inter], axis=1).astype(BF16)
                    y_h.append(_dot(lhs, rhs))
                    st_h.append(_dot((bgt * dtd_t[hh:hh + 1, :]).astype(BF16), xs_p))
                    cd_h.append(jnp.broadcast_to(chunk_decay[:, hh:hh + 1], (1, LANES)))
                ys.append(jnp.where(low_half, y_h[0], y_h[1]))
                state_ref[pair] = (prev * jnp.where(low_half_row, cd_h[0], cd_h[1])
                                   + jnp.where(low_half, st_h[0], st_h[1]))
            yg = jnp.concatenate(ys, axis=1)
            yg = yg + xs[r0:r0 + SSD_CHUNK, g * SSD_GW:(g + 1) * SSD_GW] * dsk[:, g * SSD_GW:(g + 1) * SSD_GW]
            u = yg * _silu(z[r0:r0 + SSD_CHUNK, g * SSD_GW:(g + 1) * SSD_GW])
            u = u * lax.rsqrt(jnp.mean(u * u, axis=-1, keepdims=True) + EPS)
            o_ref[r0:r0 + SSD_CHUNK, g * SSD_GW:(g + 1) * SSD_GW] = (
                u * nw[:, g * SSD_GW:(g + 1) * SSD_GW]).astype(o_ref.dtype)


def _pad_lanes(v, n=LANES):
    v = v.reshape(1, -1).astype(F32)
    return jnp.pad(v, ((0, 0), (0, n - v.shape[1])))


def _ssd_mixer(proj, bsz, seq, conv_w, conv_b, dt_bias, a_log, d_skip, norm_w, ts=256):
    nsteps = seq // ts
    row = lambda b, s: b * nsteps + s
    xw, bw, cw = conv_w[:, :SSD_WIDTH], conv_w[:, SSD_WIDTH:SSD_WIDTH + 512], conv_w[:, SSD_WIDTH + 512:]
    xb, bb, cb = conv_b[:SSD_WIDTH], conv_b[SSD_WIDTH:SSD_WIDTH + 512], conv_b[SSD_WIDTH + 512:]
    tri = jnp.asarray(np.tril(np.ones((SSD_CHUNK, SSD_CHUNK), np.float32)), BF16)
    const = lambda shape: pl.BlockSpec(shape, lambda b, s: (0,) * len(shape))
    kern = functools.partial(_ssd_kernel, ts=ts)
    return pl.pallas_call(
        kern,
        grid=(bsz, nsteps),
        in_specs=[pl.BlockSpec((ts, SSD_WIDTH), lambda b, s: (row(b, s), COL_Z // SSD_WIDTH)),
                  pl.BlockSpec((ts, SSD_WIDTH), lambda b, s: (row(b, s), COL_XS // SSD_WIDTH)),
                  pl.BlockSpec((ts, 512), lambda b, s: (row(b, s), COL_B // 512)),
                  pl.BlockSpec((ts, 512), lambda b, s: (row(b, s), COL_C // 512)),
                  pl.BlockSpec((ts, LANES), lambda b, s: (row(b, s), COL_DT // LANES)),
                  const((SSD_CONV, SSD_WIDTH)), const((SSD_CONV, 512)), const((SSD_CONV, 512)),
                  const((1, SSD_WIDTH)), const((1, 512)), const((1, 512)),
                  const((1, LANES)), const((1, LANES)), const((1, SSD_WIDTH)), const((1, SSD_WIDTH)),
                  const((SSD_CHUNK, SSD_CHUNK))],
        out_specs=pl.BlockSpec((ts, SSD_WIDTH), lambda b, s: (row(b, s), 0)),
        out_shape=jax.ShapeDtypeStruct((bsz * seq, SSD_WIDTH), BF16),
        scratch_shapes=[pltpu.VMEM((ts + SUBLANES, SSD_WIDTH), F32), pltpu.VMEM((ts + SUBLANES, 512), F32),
                        pltpu.VMEM((ts + SUBLANES, 512), F32),
                        pltpu.VMEM((SSD_HEADS // 2, SSD_STATE, 2 * SSD_HEAD_DIM), F32)],
        compiler_params=_params(("arbitrary", "arbitrary")),
        name="ssd_mixer",
    )(proj, proj, proj, proj, proj, xw, bw, cw, xb.reshape(1, -1), bb.reshape(1, -1), cb.reshape(1, -1),
      _pad_lanes(dt_bias), _pad_lanes(a_log), jnp.repeat(d_skip.astype(F32), SSD_HEAD_DIM).reshape(1, -1),
      norm_w.reshape(1, -1), tri)


def _hgrn_kernel(q_ref, f_ref, i_ref, g_ref, lbl_ref, nw_ref, cum_ref, o_ref, state_ref, *, ts, layer):
    s = pl.program_id(1)

    @pl.when(s == 0)
    def _():
        state_ref[...] = jnp.zeros_like(state_ref)

    nck = ts // HGRN_CHUNK
    lg = lbl_ref[...]
    e = jnp.exp(lg - jnp.max(lg, axis=0, keepdims=True))
    sm = e / jnp.sum(e, axis=0, keepdims=True)
    ridx = lax.broadcasted_iota(jnp.int32, lg.shape, 0)
    lb = jnp.sum(jnp.where((ridx >= 1) & (ridx <= layer), sm, 0.0), axis=0, keepdims=True)

    cum = cum_ref[...]
    nw = nw_ref[...]
    ti = lax.broadcasted_iota(jnp.int32, (ts, ts), 0)
    tj = lax.broadcasted_iota(jnp.int32, (ts, ts), 1)
    blockcausal = (ti // HGRN_CHUNK == tj // HGRN_CHUNK) & (ti >= tj)
    tok = lax.broadcasted_iota(jnp.int32, (1, ts), 1)

    for h in range(HGRN_HEADS):
        c0 = h * HGRN_DIM
        lbh = lb[:, c0:c0 + HGRN_DIM]
        f = lbh + (1.0 - lbh) * _sigmoid(f_ref[:, c0:c0 + HGRN_DIM])
        logf = jnp.log(f)
        k = 1.0 - f
        q = _silu(q_ref[:, c0:c0 + HGRN_DIM])
        v = i_ref[:, c0:c0 + HGRN_DIM]
        hi, mid, lo = _split3(logf)
        b = _dot(cum, hi) + _dot(cum, mid) + _dot(cum, lo)
        b3 = b.reshape(nck, HGRN_CHUNK, HGRN_DIM)
        bref = jnp.broadcast_to(b3[:, HGRN_CHUNK // 2:HGRN_CHUNK // 2 + 1, :], b3.shape).reshape(ts, HGRN_DIM)
        blast = jnp.broadcast_to(b3[:, HGRN_CHUNK - 1:HGRN_CHUNK, :], b3.shape).reshape(ts, HGRN_DIM)
        qe = (q * jnp.exp(b - bref)).astype(BF16)
        ke = (k * jnp.exp(bref - b)).astype(BF16)
        kl = (k * jnp.exp(blast - b)).astype(BF16)
        qb = (q * jnp.exp(b)).astype(BF16)
        vb = v.astype(BF16)
        att = jnp.where(blockcausal, _dot_nt(qe, ke), 0.0)
        o = _dot(att.astype(BF16), vb)
        vt = v.T
        lhs = jnp.concatenate(
            [jnp.where(tok // HGRN_CHUNK == c, vt, 0.0) for c in range(nck)], axis=0).astype(BF16)
        st = _dot(lhs, kl)
        state = state_ref[h]
        outs = []
        for c in range(nck):
            t0 = c * HGRN_CHUNK
            outs.append(_dot_nt(qb[t0:t0 + HGRN_CHUNK], state.astype(BF16)))
            cd = jnp.exp(blast[t0:t0 + 1, :])
            state = state * cd + st[c * HGRN_DIM:(c + 1) * HGRN_DIM]
        state_ref[h] = state
        o = o + jnp.concatenate(outs, axis=0)
        o = o * lax.rsqrt(jnp.mean(o * o, axis=-1, keepdims=True) + EPS) * nw
        o_ref[:, c0:c0 + HGRN_DIM] = (o * _silu(g_ref[:, c0:c0 + HGRN_DIM])).astype(o_ref.dtype)


def _hgrn_cum_matrix(ts):
    t = np.arange(ts)
    same = (t[:, None] // HGRN_CHUNK) == (t[None, :] // HGRN_CHUNK)
    return (same & (t[None, :] <= t[:, None])).astype(np.float32)


def _hgrn_mixer(proj, bsz, seq, lb_logits, norm_w, layer, ts=256):
    nsteps = seq // ts
    row = lambda b, s: b * nsteps + s
    depth = lb_logits.shape[0]
    cum = jnp.asarray(_hgrn_cum_matrix(ts), BF16)
    kern = functools.partial(_hgrn_kernel, ts=ts, layer=layer)
    blk = lambda col: pl.BlockSpec((ts, HGRN_WIDTH), lambda b, s: (row(b, s), col // HGRN_WIDTH))
    return pl.pallas_call(
        kern,
        grid=(bsz, nsteps),
        in_specs=[blk(COL_HQ), blk(COL_HF), blk(COL_HI), blk(COL_HG),
                  pl.BlockSpec((depth, HGRN_WIDTH), lambda b, s: (0, 0)),
                  pl.BlockSpec((1, HGRN_DIM), lambda b, s: (0, 0)),
                  pl.BlockSpec((ts, ts), lambda b, s: (0, 0))],
        out_specs=pl.BlockSpec((ts, HGRN_WIDTH), lambda b, s: (row(b, s), 0)),
        out_shape=jax.ShapeDtypeStruct((bsz * seq, HGRN_WIDTH), BF16),
        scratch_shapes=[pltpu.VMEM((HGRN_HEADS, HGRN_DIM, HGRN_DIM), F32)],
        compiler_params=_params(("arbitrary", "arbitrary")),
        name="hgrn2_mixer",
    )(proj, proj, proj, proj, lb_logits.astype(F32), norm_w.reshape(1, HGRN_DIM).astype(F32), cum)


def _rel_bucket_np(dist):
    max_exact = REL_BUCKETS // 2
    d = np.maximum(dist, 0)
    ratio = np.maximum(d, 1).astype(np.float32) / np.float32(max_exact)
    log_ratio = np.log(ratio).astype(np.float32) / np.float32(math.log(REL_MAX_DIST / max_exact))
    large = np.minimum(max_exact + (log_ratio * np.float32(REL_BUCKETS - max_exact)).astype(np.int32),
                       REL_BUCKETS - 1)
    return np.where(d < max_exact, d, large).astype(np.int32)


def _bias_expand_kernel(rb_ref, bmap_ref, o_ref):
    h = pl.program_id(0)
    bm = bmap_ref[...]
    out = jnp.full(bm.shape, NEG_INF, F32)
    for k in range(REL_BUCKETS):
        out = jnp.where(bm == k, rb_ref[k, h] * LOG2E, out)
    o_ref[...] = out


def _bias_expand(rel_bias, bmap, tr):
    rows, cols = bmap.shape
    return pl.pallas_call(
        _bias_expand_kernel,
        grid=(NSA_HEADS, rows // tr),
        in_specs=[pl.BlockSpec(memory_space=pltpu.SMEM),
                  pl.BlockSpec((tr, cols), lambda h, i: (i, 0))],
        out_specs=pl.BlockSpec((None, tr, cols), lambda h, i: (h, i, 0)),
        out_shape=jax.ShapeDtypeStruct((NSA_HEADS, rows, cols), F32),
        compiler_params=_params(("arbitrary", "arbitrary")),
        name="nsa_bias_expand",
    )(rel_bias.astype(F32), bmap)


def _nsa_bias_tables(rel_bias, seq):
    ncp = seq // CMP_STRIDE
    t = np.arange(seq)[:, None]
    cmp_end = np.arange(ncp)[None, :] * CMP_STRIDE + CMP_BLOCK - 1
    d = t - cmp_end
    bmap_cmp = np.where(d >= 0, _rel_bucket_np(d), -1).astype(np.int32)
    l = np.arange(Q_BLOCK)[:, None]
    j = np.arange(WIN_KEYS)[None, :]
    dist = l - j + WINDOW
    win = np.where((dist >= 0) & (dist < WINDOW), _rel_bucket_np(dist), -1)
    far = _rel_bucket_np(np.arange(Q_BLOCK + 1, 8 * seq))
    assert (far == far[0]).all(), "distances beyond one query block must share a single bucket"
    bmap_tab = np.concatenate([win, np.full((Q_BLOCK, Q_BLOCK), far[0])], axis=1).astype(np.int32)
    bias_cmp = _bias_expand(rel_bias, jnp.asarray(bmap_cmp), 512)
    bias_tab = _bias_expand(rel_bias, jnp.asarray(bmap_tab), Q_BLOCK)
    return bias_cmp, bias_tab


def _cmp_kernel(u_ref, pe_ref, w1_ref, w2_ref, o_ref, *, ncp):
    half = CMP_BLOCK // 2
    pe = pe_ref[...]
    top = jnp.zeros((ncp, CMP_HIDDEN), F32)
    bot = jnp.zeros((ncp, CMP_HIDDEN), F32)
    for l in range(half):
        x = u_ref[pl.ds(l, ncp, stride=half), :]
        top = top + _dot((x + pe[l:l + 1, :]).astype(BF16), w1_ref[l])
        bot = bot + _dot((x + pe[half + l:half + l + 1, :]).astype(BF16), w1_ref[half + l])
    hid = top + pltpu.roll(bot, ncp - 1, 0)
    o_ref[...] = _dot(_silu(hid).astype(BF16), w2_ref[...])


def _nsa_compress(proj, bsz, seq, pe, w1, w2):
    ncp = seq // CMP_STRIDE
    kern = functools.partial(_cmp_kernel, ncp=ncp)
    return pl.pallas_call(
        kern,
        grid=(bsz, 2, NSA_KV_HEADS),
        in_specs=[pl.BlockSpec((seq, NSA_HEAD_DIM), lambda b, t, h: (b, COL_KC // NSA_HEAD_DIM + NSA_KV_HEADS * t + h)),
                  pl.BlockSpec((None, CMP_BLOCK, NSA_HEAD_DIM), lambda b, t, h: (t, 0, 0)),
                  pl.BlockSpec((None, CMP_BLOCK, NSA_HEAD_DIM, CMP_HIDDEN), lambda b, t, h: (t, 0, 0, 0)),
                  pl.BlockSpec((None, CMP_HIDDEN, NSA_HEAD_DIM), lambda b, t, h: (t, 0, 0))],
        out_specs=pl.BlockSpec((None, None, None, ncp, NSA_HEAD_DIM), lambda b, t, h: (b, t, h, 0, 0)),
        out_shape=jax.ShapeDtypeStruct((bsz, 2, NSA_KV_HEADS, ncp, NSA_HEAD_DIM), F32),
        compiler_params=_params(("arbitrary", "arbitrary", "arbitrary")),
        name="nsa_compress",
    )(proj, pe.astype(F32), w1.reshape(2, CMP_BLOCK, NSA_HEAD_DIM, CMP_HIDDEN).astype(BF16), w2.astype(BF16))


def _softmax_start(s, v):
    m = jnp.max(s, axis=-1, keepdims=True)
    p = jnp.exp2(s - m)
    return m, jnp.sum(p, axis=-1, keepdims=True), _dot(p.astype(BF16), v)


def _softmax_update(s, v, carry):
    m, l, acc = carry
    m_new = jnp.maximum(m, jnp.max(s, axis=-1, keepdims=True))
    alpha = jnp.exp2(m - m_new)
    p = jnp.exp2(s - m_new)
    return m_new, alpha * l + jnp.sum(p, axis=-1, keepdims=True), alpha * acc + _dot(p.astype(BF16), v)


def _softmax_finish(carry):
    m, l, acc = carry
    return jnp.where(m > 0.5 * NEG_INF, acc / jnp.maximum(l, 1e-30), 0.0)


def _nsa_kernel(q_ref, gate_ref, kc_ref, vc_ref, ks_ref, vs_ref, kw_ref, vw_ref, bcmp_ref, btab_ref,
                kone_ref, c2s_ref, o_ref, kaug_ref, vsb_ref, kwb_ref, vwb_ref, sa_ref, sb_ref, *, n_sel, nqb):
    hk = pl.program_id(1)
    c0 = pl.program_id(2) * nqb
    rows = NSA_GQA * Q_BLOCK
    lane = lax.broadcasted_iota(jnp.int32, (Q_BLOCK, LANES), 1)

    @pl.when(c0 == 0)
    def _():
        kaug_ref[0:Q_BLOCK, 0:NSA_HEAD_DIM] = jnp.zeros((Q_BLOCK, NSA_HEAD_DIM), BF16)
        kaug_ref[Q_BLOCK:, 0:NSA_HEAD_DIM] = ks_ref[...].astype(BF16)
        kaug_ref[:, NSA_HEAD_DIM:] = kone_ref[...]
        vsb_ref[0:Q_BLOCK, :] = jnp.zeros((Q_BLOCK, NSA_HEAD_DIM), BF16)
        vsb_ref[Q_BLOCK:, :] = vs_ref[...].astype(BF16)
        kwb_ref[0:WINDOW, 0:NSA_HEAD_DIM] = jnp.zeros((WINDOW, NSA_HEAD_DIM), BF16)
        kwb_ref[WINDOW:, 0:NSA_HEAD_DIM] = kw_ref[...].astype(BF16)
        for r0 in range(0, WINDOW, Q_BLOCK):
            kwb_ref[r0:r0 + Q_BLOCK, NSA_HEAD_DIM:] = jnp.where(lane == AUG_PAD, 1.0, 0.0).astype(BF16)
        kwb_ref[WINDOW:, NSA_HEAD_DIM:] = jnp.zeros((kw_ref.shape[0], LANES), BF16)
        vwb_ref[0:WINDOW, :] = jnp.zeros((WINDOW, NSA_HEAD_DIM), BF16)
        vwb_ref[WINDOW:, :] = vw_ref[...].astype(BF16)

    gens = [_nsa_unit(c0 + u, q_ref[u * Q_BLOCK:(u + 1) * Q_BLOCK, :], bcmp_ref[:, u * Q_BLOCK:(u + 1) * Q_BLOCK, :],
                      btab_ref, kc_ref, vc_ref, c2s_ref, kaug_ref, vsb_ref, kwb_ref, vwb_ref, n_sel)
            for u in range(nqb)]
    units = [None] * nqb
    while any(un is None for un in units):
        for u, gen in enumerate(gens):
            if units[u] is None:
                units[u] = next(gen)
    qaug_far = jnp.concatenate([un[3] for un in units], axis=0)
    carry = tuple(jnp.concatenate([un[2][i] for un in units], axis=0) for i in range(3))

    far_blocks = FAR_KEYS // Q_BLOCK
    n_pairs = (c0 + nqb - 2 + 2 * far_blocks - 1) // (2 * far_blocks)
    last_slab = (kaug_ref.shape[0] - Q_BLOCK) // FAR_KEYS - 1

    def far_logits(slab):
        r0 = pl.multiple_of(jnp.minimum(slab, last_slab) * FAR_KEYS + Q_BLOCK, Q_BLOCK)
        return _dot_nt(qaug_far, kaug_ref[pl.ds(r0, FAR_KEYS), :])

    def far_values(slab):
        return vsb_ref[pl.ds(pl.multiple_of(slab * FAR_KEYS + Q_BLOCK, Q_BLOCK), FAR_KEYS), :]

    sa_ref[...] = far_logits(0)

    def far_body(k, carry):
        sb_ref[...] = far_logits(2 * k + 1)
        carry = _softmax_update(sa_ref[...], far_values(2 * k), carry)
        sa_ref[...] = far_logits(2 * k + 2)
        return _softmax_update(sb_ref[...], far_values(2 * k + 1), carry)

    o_sel = _softmax_finish(lax.fori_loop(0, n_pairs, far_body, carry))

    for u in range(nqb):
        o_cmp, o_win = units[u][0], units[u][1]
        gates = _sigmoid(gate_ref[u * Q_BLOCK:(u + 1) * Q_BLOCK, :])
        for g in range(NSA_GQA):
            def gate_col(br):
                return jnp.sum(jnp.where(lane == hk * (3 * NSA_GQA) + g * 3 + br, gates, 0.0), axis=-1, keepdims=True)

            r0 = g * Q_BLOCK
            out = (gate_col(0) * o_cmp[r0:r0 + Q_BLOCK] + gate_col(1) * o_sel[u * rows + r0:u * rows + r0 + Q_BLOCK]
                   + gate_col(2) * o_win[r0:r0 + Q_BLOCK])
            o_ref[u * Q_BLOCK:(u + 1) * Q_BLOCK, g * NSA_HEAD_DIM:(g + 1) * NSA_HEAD_DIM] = out.astype(o_ref.dtype)


def _nsa_unit(c, q_in, bc_in, btab_ref, kc_ref, vc_ref, c2s_ref, kaug_ref, vsb_ref, kwb_ref, vwb_ref, n_sel):
    rows = NSA_GQA * Q_BLOCK
    lane = lax.broadcasted_iota(jnp.int32, (Q_BLOCK, LANES), 1)
    q = q_in * (NSA_HEAD_DIM ** -0.5 * LOG2E)
    q3 = jnp.concatenate([q[:, g * NSA_HEAD_DIM:(g + 1) * NSA_HEAD_DIM] for g in range(NSA_GQA)],
                         axis=0).astype(BF16)

    btab = btab_ref[...].reshape(rows, WIN_KEYS + Q_BLOCK)
    near0 = pl.multiple_of(c * Q_BLOCK, Q_BLOCK)

    qaug_win = jnp.concatenate(
        [q3, jnp.concatenate([jnp.where(lane == AUG_PAD, NEG_INF, 0.0).astype(BF16)] * NSA_GQA, axis=0)], axis=1)
    s_win = _dot_nt(qaug_win, kwb_ref[pl.ds(near0, WIN_KEYS), :]) + btab[:, 0:WIN_KEYS]
    yield None

    bc = bc_in.reshape(rows, bc_in.shape[-1])
    valid = bc > 0.5 * NEG_INF
    s = _dot_nt(q3, kc_ref[...].astype(BF16)) + bc
    yield None
    o_win = _softmax_finish(_softmax_start(s_win, vwb_ref[pl.ds(near0, WIN_KEYS), :]))
    yield None
    m = jnp.max(s, axis=-1, keepdims=True)
    p = jnp.where(valid, jnp.exp2(s - m), 0.0)
    p = p / jnp.maximum(jnp.sum(p, axis=-1, keepdims=True), 1e-30)
    o_cmp = _dot(p.astype(BF16), vc_ref[...].astype(BF16))

    psum = p[0:Q_BLOCK] + p[Q_BLOCK:2 * Q_BLOCK] + p[2 * Q_BLOCK:3 * Q_BLOCK]
    ph = psum.astype(BF16)
    plo = (psum - ph.astype(F32)).astype(BF16)
    c2s = c2s_ref[...]
    imp_t = _dot_nt(c2s, ph) + _dot_nt(c2s, plo)
    nsp = -(-n_sel // SUBLANES) * SUBLANES
    imp_t = imp_t[0:nsp]
    tq = c * Q_BLOCK + lax.broadcasted_iota(jnp.int32, (nsp, Q_BLOCK), 1)
    jj = lax.broadcasted_iota(jnp.int32, (nsp, Q_BLOCK), 0)
    valid_sel = (jj < n_sel) & (jj * SEL_BLOCK <= tq)
    back = tq // SEL_BLOCK - jj
    force = (jj == 0) | ((back >= 0) & (back < SEL_LOCAL))
    score = jnp.where(valid_sel, imp_t + jnp.where(force, FORCE_BONUS, 0.0), NEG_INF)
    yield None
    n_tiles = nsp // SUBLANES
    tiles = [score[v * SUBLANES:(v + 1) * SUBLANES] for v in range(n_tiles)]
    sub = lax.broadcasted_iota(jnp.int32, (SUBLANES, Q_BLOCK), 0)
    cnts = [jnp.zeros((SUBLANES, Q_BLOCK), F32) for _ in range(n_tiles)]
    for jp in range(n_sel):
        v0, r0 = divmod(jp, SUBLANES)
        r = tiles[v0][r0:r0 + 1, :]
        for v in range(n_tiles):
            gt = jnp.where(r > tiles[v], 1.0, 0.0)
            ge = jnp.where(r >= tiles[v], 1.0, 0.0)
            cnts[v] = cnts[v] + (gt if v < v0 else ge if v > v0 else jnp.where(sub > r0, ge, gt))
        if jp % (2 * SUBLANES) == 2 * SUBLANES - 1:
            yield None
    cnt = jnp.concatenate(cnts, axis=0)
    keep = valid_sel & (cnt < float(min(SEL_TOPK, n_sel)))

    ext = lax.broadcasted_iota(jnp.int32, (LANES - nsp, Q_BLOCK), 0) + nsp
    aug = jnp.concatenate([jnp.where(keep, 0.0, NEG_INF), jnp.where(ext == AUG_PAD, NEG_INF, 0.0)], axis=0)
    aug_near = aug.T
    aug_far = jnp.where((lane >= 2 * (c - 1)) & (lane < AUG_BIAS_HI), NEG_INF, aug_near)
    far_parts = []
    for g in range(NSA_GQA):
        bfar = btab[g * Q_BLOCK:(g + 1) * Q_BLOCK, WIN_KEYS:]
        hi = bfar.astype(BF16).astype(F32)
        far_parts.append(jnp.where(lane == AUG_BIAS_HI, hi, jnp.where(lane == AUG_BIAS_LO, bfar - hi, aug_far)))
    qaug_near = jnp.concatenate([q3, jnp.concatenate([aug_near.astype(BF16)] * NSA_GQA, axis=0)], axis=1)
    qaug_far = jnp.concatenate([q3, jnp.concatenate(far_parts, axis=0).astype(BF16)], axis=1)

    s = _dot_nt(qaug_near, kaug_ref[pl.ds(near0, NEAR_KEYS), :]) + btab[:, WIN_KEYS - NEAR_KEYS:WIN_KEYS]
    yield None
    carry = _softmax_start(s, vsb_ref[pl.ds(near0, NEAR_KEYS), :])
    yield o_cmp, o_win, carry, qaug_far


def _nsa_attention(proj, cmp_kv, bias_cmp, bias_tab, bsz, seq):
    nq = seq // Q_BLOCK
    ncp = seq // CMP_STRIDE
    n_sel = seq // SEL_BLOCK
    assert n_sel <= AUG_BIAS_HI and NSA_HEAD_DIM == LANES
    assert seq % (2 * FAR_KEYS) == 0, "far steps walk the keys two 512-key slabs at a time"
    key = np.arange(-Q_BLOCK, seq)[:, None]
    ln = np.arange(LANES)[None, :]
    kone = np.where(key < 0, ln == AUG_PAD,
                    ((ln < AUG_BIAS_HI) & (key // SEL_BLOCK == ln)) | (ln == AUG_BIAS_HI) | (ln == AUG_BIAS_LO))
    kone = jnp.asarray(kone, BF16)
    c_start = np.arange(ncp)[None, :] * CMP_STRIDE
    s_start = np.arange(LANES)[:, None] * SEL_BLOCK
    overlap = np.clip(np.minimum(c_start + CMP_BLOCK, s_start + SEL_BLOCK) - np.maximum(c_start, s_start), 0, None)
    overlap = np.where(np.arange(LANES)[:, None] < n_sel, overlap, 0)
    c2s_t = jnp.asarray(overlap / CMP_BLOCK, BF16)
    gqa_w = NSA_GQA * NSA_HEAD_DIM
    full = lambda col: pl.BlockSpec((seq, NSA_HEAD_DIM), lambda b, h, c: (b, col // NSA_HEAD_DIM + h))
    nqb = NSA_Q_BLOCKS_PER_STEP
    ns = nq // nqb
    qr = nqb * Q_BLOCK
    kern = functools.partial(_nsa_kernel, n_sel=n_sel, nqb=nqb)
    return pl.pallas_call(
        kern,
        grid=(bsz, NSA_KV_HEADS, ns),
        in_specs=[pl.BlockSpec((qr, gqa_w), lambda b, h, c: (b * ns + c, COL_Q // gqa_w + h)),
                  pl.BlockSpec((qr, LANES), lambda b, h, c: (b * ns + c, COL_GATE // LANES)),
                  pl.BlockSpec((None, None, None, ncp, NSA_HEAD_DIM), lambda b, h, c: (b, 0, h, 0, 0)),
                  pl.BlockSpec((None, None, None, ncp, NSA_HEAD_DIM), lambda b, h, c: (b, 1, h, 0, 0)),
                  full(COL_KS), full(COL_VS), full(COL_KW), full(COL_VW),
                  pl.BlockSpec((NSA_GQA, qr, ncp), lambda b, h, c: (h, c, 0)),
                  pl.BlockSpec((NSA_GQA, Q_BLOCK, WIN_KEYS + Q_BLOCK), lambda b, h, c: (h, 0, 0)),
                  pl.BlockSpec((seq + Q_BLOCK, LANES), lambda b, h, c: (0, 0)),
                  pl.BlockSpec((LANES, ncp), lambda b, h, c: (0, 0))],
        out_specs=pl.BlockSpec((qr, gqa_w), lambda b, h, c: (b * ns + c, h)),
        out_shape=jax.ShapeDtypeStruct((bsz * seq, NSA_WIDTH), BF16),
        scratch_shapes=[pltpu.VMEM((seq + Q_BLOCK, 2 * NSA_HEAD_DIM), BF16),
                        pltpu.VMEM((seq + Q_BLOCK, NSA_HEAD_DIM), BF16),
                        pltpu.VMEM((seq + WINDOW, 2 * NSA_HEAD_DIM), BF16),
                        pltpu.VMEM((seq + WINDOW, NSA_HEAD_DIM), BF16),
                        pltpu.VMEM((nqb * NSA_GQA * Q_BLOCK, FAR_KEYS), F32),
                        pltpu.VMEM((nqb * NSA_GQA * Q_BLOCK, FAR_KEYS), F32)],
        compiler_params=_params(("arbitrary", "arbitrary", "arbitrary")),
        name="nsa_attention",
    )(proj, proj, cmp_kv, cmp_kv, proj, proj, proj, proj, bias_cmp, bias_tab, kone, c2s_t)


_IN_OFF = tuple(int(v) for v in np.cumsum((0,) + IN_SIZES))
_W_IN_COPIES = ((_IN_OFF[0], SSD_WIDTH, COL_Z), (_IN_OFF[1], SSD_WIDTH, COL_XS),
                (_IN_OFF[1] + SSD_WIDTH, 512, COL_B), (_IN_OFF[1] + SSD_WIDTH + 512, 512, COL_C),
                (_IN_OFF[3], NSA_WIDTH, COL_Q), (_IN_OFF[4], 6 * NSA_KV_WIDTH, COL_KC),
                (_IN_OFF[11], 4 * HGRN_WIDTH, COL_HQ))
_W_IN_NARROW = ((_IN_OFF[2], IN_SIZES[2], COL_DT, COL_GATE - COL_DT), (_IN_OFF[10], IN_SIZES[10], COL_GATE, COL_HQ - COL_GATE))


def _relayout_kernel(w_ref, o_ref):
    for src, width, dst in _W_IN_COPIES:
        o_ref[:, dst:dst + width] = w_ref[:, src:src + width].astype(BF16)
    for src, valid, dst, padded in _W_IN_NARROW:
        tile = w_ref[:, src:src + LANES]
        lane = lax.broadcasted_iota(jnp.int32, tile.shape, 1)
        o_ref[:, dst:dst + LANES] = jnp.where(lane < valid, tile, 0.0).astype(BF16)
        if padded > LANES:
            o_ref[:, dst + LANES:dst + padded] = jnp.zeros((tile.shape[0], padded - LANES), BF16)


def _relayout_w_in(w, tr=128):
    depth, d, n = w.shape
    return pl.pallas_call(
        _relayout_kernel,
        grid=(depth, d // tr),
        in_specs=[pl.BlockSpec((None, tr, n), lambda l, i: (l, i, 0))],
        out_specs=pl.BlockSpec((None, tr, IN_PAD), lambda l, i: (l, i, 0)),
        out_shape=jax.ShapeDtypeStruct((depth, d, IN_PAD), BF16),
        compiler_params=_params(("parallel", "parallel")),
        name="w_in_relayout",
    )(w)


def _tiles(seq):
    return dict(in_proj=(1024, 1024), out_proj=(1024, 512), gate_up=(min(1024, seq), 256), down=(256, 1024))


def kernel(x, norm_mix_w, w_in, ssd_conv_w, ssd_conv_b, ssd_dt_bias, ssd_a_log, ssd_d, ssd_norm_w, nsa_cmp_pe, nsa_cmp_w1, nsa_cmp_w2, rel_bias, hgrn_lb_logits, hgrn_norm_w, w_out, norm_ffn_w, ffn_w_gu, ffn_conv_w, ffn_conv_b, ffn_w_down, norm_f_w):
    bsz, seq, d = x.shape
    depth = w_in.shape[0]
    xr = x.reshape(bsz * seq, d).astype(F32)
    bias_cmp, bias_tab = _nsa_bias_tables(rel_bias, seq)
    tiles = _tiles(seq)
    w_in_b = _relayout_w_in(jnp.pad(w_in.astype(BF16), ((0, 0), (0, 0), (0, -w_in.shape[2] % LANES))))
    w_out_b = w_out.astype(BF16)
    w_gu_b = ffn_w_gu.astype(BF16)
    w_down_b = ffn_w_down.astype(BF16)
    for l in range(depth):
        n = _rmsnorm(xr, norm_mix_w[l], BF16)
        proj = _matmul(n, w_in_b, l, F32, *tiles["in_proj"], name="in_proj")
        y_ssd = _ssd_mixer(proj, bsz, seq, ssd_conv_w[l].astype(F32), ssd_conv_b[l].astype(F32), ssd_dt_bias[l],
                           ssd_a_log[l], ssd_d[l], ssd_norm_w[l].astype(F32))
        cmp_kv = _nsa_compress(proj, bsz, seq, nsa_cmp_pe[l], nsa_cmp_w1[l], nsa_cmp_w2[l])
        y_nsa = _nsa_attention(proj, cmp_kv, bias_cmp, bias_tab, bsz, seq)
        y_hgrn = _hgrn_mixer(proj, bsz, seq, hgrn_lb_logits, hgrn_norm_w[l], l)
        mix = jnp.concatenate([y_ssd, y_nsa, y_hgrn], axis=1)
        xr = _matmul_residual(mix, w_out_b, l, xr, *tiles["out_proj"], name="out_proj")
        hn = _rmsnorm(xr, norm_ffn_w[l], BF16)
        act = _gate_up(hn, w_gu_b, l, ffn_conv_w[l].astype(F32), ffn_conv_b[l].astype(F32), seq, *tiles["gate_up"])
        xr = _matmul_residual_wres(act, w_down_b, l, xr, *tiles["down"], name="ffn_down")
    out = _rmsnorm(xr, norm_f_w, x.dtype)
    return out.reshape(bsz, seq, d)
```

```python
import functools
import math

import numpy as np
import jax
import jax.numpy as jnp
from jax import lax
from jax.experimental import pallas as pl
from jax.experimental.pallas import tpu as pltpu

F32 = jnp.float32
BF16 = jnp.bfloat16

D_MODEL = 4096
SSD_HEAD_DIM = 64
SSD_WIDTH = 1536
SSD_HEADS = 24
SSD_GROUPS = 4
SSD_HPG = 6
SSD_STATE = 128
SSD_CONV = 4
SSD_CHUNK = 128
SSD_GW = SSD_WIDTH // SSD_GROUPS
NSA_HEAD_DIM = 128
NSA_WIDTH = 1536
NSA_HEADS = 12
NSA_KV_HEADS = 4
NSA_GQA = 3
NSA_KV_WIDTH = 512
CMP_BLOCK = 32
CMP_STRIDE = 16
CMP_HIDDEN = 256
SEL_BLOCK = 64
SEL_TOPK = 16
SEL_LOCAL = 2
WINDOW = 512
Q_BLOCK = 128
FORCE_BONUS = 1e4
HGRN_WIDTH = 1024
HGRN_HEADS = 8
HGRN_DIM = 128
HGRN_CHUNK = 32
REL_BUCKETS = 32
REL_MAX_DIST = 128
D_FF = 11008
FFN_CONV = 3
EPS = 1e-6
NEG_INF = -1e30

IN_SIZES = (SSD_WIDTH, SSD_WIDTH + 2 * SSD_GROUPS * SSD_STATE, SSD_HEADS, NSA_WIDTH,
            NSA_KV_WIDTH, NSA_KV_WIDTH, NSA_KV_WIDTH, NSA_KV_WIDTH, NSA_KV_WIDTH, NSA_KV_WIDTH,
            3 * NSA_HEADS, HGRN_WIDTH, HGRN_WIDTH, HGRN_WIDTH, HGRN_WIDTH)

LANES = 128
SUBLANES = 8
VMEM_LIMIT = 56 * 1024 * 1024

COL_Z = 0
COL_XS = 1536
COL_Q = 3072
COL_B = 4608
COL_C = 5120
COL_KC = 5632
COL_VC = 6144
COL_KS = 6656
COL_VS = 7168
COL_KW = 7680
COL_VW = 8192
COL_DT = 8704
COL_GATE = 8832
COL_HQ = 9216
COL_HF = 10240
COL_HI = 11264
COL_HG = 12288
IN_PAD = 13312

LOG2E = math.log2(math.e)
AUG_BIAS_HI = 64
AUG_BIAS_LO = 65
AUG_PAD = 66
WIN_KEYS = WINDOW + Q_BLOCK
NEAR_KEYS = 2 * Q_BLOCK
FAR_KEYS = 4 * Q_BLOCK
HGRN_HEADS_INTERLEAVED = 8
NSA_Q_BLOCKS_PER_STEP = 2


def _params(semantics):
    return pltpu.CompilerParams(dimension_semantics=semantics, vmem_limit_bytes=VMEM_LIMIT)


def _sigmoid(x):
    return 1.0 / (1.0 + jnp.exp(-x))


def _silu(x):
    return x * _sigmoid(x)


def _dot(a, b):
    return jnp.dot(a, b, preferred_element_type=F32)


def _dot_nt(a, b):
    return lax.dot_general(a, b, (((1,), (1,)), ((), ())), preferred_element_type=F32)


def _split3(x):
    hi = x.astype(BF16)
    r1 = x - hi.astype(F32)
    mid = r1.astype(BF16)
    lo = (r1 - mid.astype(F32)).astype(BF16)
    return hi, mid, lo


def _rmsnorm_kernel(x_ref, w_ref, o_ref):
    x = x_ref[...]
    ms = jnp.mean(x * x, axis=-1, keepdims=True)
    o_ref[...] = (x * lax.rsqrt(ms + EPS) * w_ref[...]).astype(o_ref.dtype)


def _rmsnorm(x, w, out_dtype, tm=256):
    m, d = x.shape
    return pl.pallas_call(
        _rmsnorm_kernel,
        grid=(m // tm,),
        in_specs=[pl.BlockSpec((tm, d), lambda i: (i, 0)), pl.BlockSpec((1, d), lambda i: (0, 0))],
        out_specs=pl.BlockSpec((tm, d), lambda i: (i, 0)),
        out_shape=jax.ShapeDtypeStruct((m, d), out_dtype),
        compiler_params=_params(("parallel",)),
        name="rmsnorm",
    )(x, w.reshape(1, d).astype(F32))


def _mm_kernel(a_ref, w_ref, o_ref):
    o_ref[...] = _dot(a_ref[...], w_ref[...]).astype(o_ref.dtype)


def _mm_res_kernel(a_ref, w_ref, r_ref, o_ref):
    o_ref[...] = r_ref[...] + _dot(a_ref[...], w_ref[...])


def _matmul(a, w, layer, out_dtype, tm, tn, name):
    m, k = a.shape
    n = w.shape[2]
    return pl.pallas_call(
        _mm_kernel,
        grid=(m // tm, n // tn),
        in_specs=[pl.BlockSpec((tm, k), lambda i, j: (i, 0)),
                  pl.BlockSpec((None, k, tn), lambda i, j: (layer, 0, j))],
        out_specs=pl.BlockSpec((tm, tn), lambda i, j: (i, j)),
        out_shape=jax.ShapeDtypeStruct((m, n), out_dtype),
        compiler_params=_params(("parallel", "arbitrary")),
        name=name,
    )(a, w)


def _out_proj_kernel(*refs):
    *a_refs, w_ref, r_ref, o_ref = refs
    acc = r_ref[...]
    k0 = 0
    for a_ref in a_refs:
        acc = acc + _dot(a_ref[...], w_ref[k0:k0 + a_ref.shape[1], :])
        k0 += a_ref.shape[1]
    o_ref[...] = acc


def _out_proj(parts, w, layer, res, tm, tn):
    m = res.shape[0]
    k, n = w.shape[1], w.shape[2]
    assert sum(p.shape[1] for p in parts) == k
    return pl.pallas_call(
        _out_proj_kernel,
        grid=(m // tm, n // tn),
        in_specs=[pl.BlockSpec((tm, p.shape[1]), lambda i, j: (i, 0)) for p in parts]
        + [pl.BlockSpec((None, k, tn), lambda i, j: (layer, 0, j)),
           pl.BlockSpec((tm, tn), lambda i, j: (i, j))],
        out_specs=pl.BlockSpec((tm, tn), lambda i, j: (i, j)),
        out_shape=jax.ShapeDtypeStruct((m, n), F32),
        compiler_params=_params(("parallel", "arbitrary")),
        name="out_proj",
    )(*parts, w, res)


def _matmul_residual_wres(a, w, layer, res, tm, tn, name):
    m, k = a.shape
    n = w.shape[2]
    return pl.pallas_call(
        _mm_res_kernel,
        grid=(n // tn, m // tm),
        in_specs=[pl.BlockSpec((tm, k), lambda j, i: (i, 0)),
                  pl.BlockSpec((None, k, tn), lambda j, i: (layer, 0, j), pipeline_mode=pl.Buffered(1)),
                  pl.BlockSpec((tm, tn), lambda j, i: (i, j))],
        out_specs=pl.BlockSpec((tm, tn), lambda j, i: (i, j)),
        out_shape=jax.ShapeDtypeStruct((m, n), F32),
        compiler_params=_params(("arbitrary", "arbitrary")),
        name=name,
    )(a, w, res)


def _gu_kernel(h_ref, wg_ref, wu_ref, cw_ref, cb_ref, o_ref, ext_ref, halo_ref, *, tm, tn, tiles_per_seq):
    i = pl.program_id(0)
    j = pl.program_id(1)
    h = h_ref[...]
    g = _dot(h, wg_ref[...])
    up = _dot(h, wu_ref[...])
    prev = halo_ref[j]
    ext_ref[0:SUBLANES, :] = jnp.where(i % tiles_per_seq == 0, 0.0, prev)
    ext_ref[SUBLANES:, :] = g
    halo_ref[j] = g[tm - SUBLANES:, :]
    cw = cw_ref[...]
    acc = cb_ref[...] + cw[FFN_CONV - 1:FFN_CONV, :] * g
    for sh in range(1, FFN_CONV):
        acc = acc + cw[FFN_CONV - 1 - sh:FFN_CONV - sh, :] * ext_ref[pl.ds(SUBLANES - sh, tm), :]
    o_ref[...] = (_silu(acc) * up).astype(o_ref.dtype)


def _gate_up(h, w_gu, layer, conv_w, conv_b, seq, tm, tn):
    m, k = h.shape
    nf = conv_w.shape[1]
    nj = nf // tn
    kern = functools.partial(_gu_kernel, tm=tm, tn=tn, tiles_per_seq=seq // tm)
    return pl.pallas_call(
        kern,
        grid=(m // tm, nj),
        in_specs=[pl.BlockSpec((tm, k), lambda i, j: (i, 0)),
                  pl.BlockSpec((None, k, tn), lambda i, j: (layer, 0, j)),
                  pl.BlockSpec((None, k, tn), lambda i, j: (layer, 0, j + nj)),
                  pl.BlockSpec((FFN_CONV, tn), lambda i, j: (0, j)),
                  pl.BlockSpec((1, tn), lambda i, j: (0, j))],
        out_specs=pl.BlockSpec((tm, tn), lambda i, j: (i, j)),
        out_shape=jax.ShapeDtypeStruct((m, nf), BF16),
        scratch_shapes=[pltpu.VMEM((tm + SUBLANES, tn), F32), pltpu.VMEM((nj, SUBLANES, tn), F32)],
        compiler_params=_params(("arbitrary", "arbitrary")),
        name="ffn_gate_up_conv",
    )(h, w_gu, w_gu, conv_w, conv_b.reshape(1, nf))


def _ssd_kernel(z_ref, xs_ref, b_ref, c_ref, dt_ref, cwx_ref, cwb_ref, cwc_ref, cbx_ref, cbb_ref, cbc_ref,
                dtb_ref, alog_ref, dsk_ref, nw_ref, tri_ref, o_ref,
                extx_ref, extb_ref, extc_ref, state_ref, *, ts):
    s = pl.program_id(1)

    @pl.when(s == 0)
    def _():
        extx_ref[...] = jnp.zeros_like(extx_ref)
        extb_ref[...] = jnp.zeros_like(extb_ref)
        extc_ref[...] = jnp.zeros_like(extc_ref)
        state_ref[...] = jnp.zeros_like(state_ref)

    def conv_silu(u_ref, ext_ref, w_ref, bias_ref):
        ext_ref[0:SUBLANES, :] = ext_ref[ts:ts + SUBLANES, :]
        ext_ref[SUBLANES:, :] = u_ref[...]
        w = w_ref[...]
        acc = bias_ref[...] + w[SSD_CONV - 1:SSD_CONV, :] * u_ref[...]
        for sh in range(1, SSD_CONV):
            acc = acc + w[SSD_CONV - 1 - sh:SSD_CONV - sh, :] * ext_ref[pl.ds(SUBLANES - sh, ts), :]
        return _silu(acc)

    xs = conv_silu(xs_ref, extx_ref, cwx_ref, cbx_ref)
    bm = conv_silu(b_ref, extb_ref, cwb_ref, cbb_ref)
    cm = conv_silu(c_ref, extc_ref, cwc_ref, cbc_ref)
    dtr = dt_ref[...] + dtb_ref[...]
    dt = jnp.maximum(dtr, 0.0) + jnp.log1p(jnp.exp(-jnp.abs(dtr)))
    a = dt * (-jnp.exp(alog_ref[...]))
    z = z_ref[...]
    tri = tri_ref[...]
    dsk = dsk_ref[...]
    nw = nw_ref[...]
    li = lax.broadcasted_iota(jnp.int32, (SSD_CHUNK, SSD_CHUNK), 0)
    si = lax.broadcasted_iota(jnp.int32, (SSD_CHUNK, SSD_CHUNK), 1)
    causal = li >= si
    low_half = si < SSD_HEAD_DIM
    low_half_row = low_half[0:1, :]

    for ck in range(ts // SSD_CHUNK):
        r0 = ck * SSD_CHUNK
        a_c = a[r0:r0 + SSD_CHUNK]
        hi, mid, lo = _split3(a_c)
        acs = _dot(tri, hi) + _dot(tri, mid) + _dot(tri, lo)
        acs_t = acs.T
        dt_t = dt[r0:r0 + SSD_CHUNK].T
        a_last_b = jnp.broadcast_to(acs_t[:, SSD_CHUNK - 1:SSD_CHUNK], acs_t.shape)
        dtd_t = dt_t * jnp.exp(a_last_b - acs_t)
        chunk_decay = jnp.exp(acs[SSD_CHUNK - 1:SSD_CHUNK, :])
        for g in range(SSD_GROUPS):
            cg = cm[r0:r0 + SSD_CHUNK, g * SSD_STATE:(g + 1) * SSD_STATE]
            bg = bm[r0:r0 + SSD_CHUNK, g * SSD_STATE:(g + 1) * SSD_STATE]
            cb = _dot_nt(cg.astype(BF16), bg.astype(BF16))
            bgt = bg.T
            ys = []
            for pr in range(SSD_HPG // 2):
                pair = (g * SSD_HPG) // 2 + pr
                c0 = pair * LANES
                xs_p = xs[r0:r0 + SSD_CHUNK, c0:c0 + LANES].astype(BF16)
                prev = state_ref[pair]
                rhs = jnp.concatenate([xs_p, prev.astype(BF16)], axis=0)
                y_h, st_h, cd_h = [], [], []
                for hh in (2 * pair, 2 * pair + 1):
                    colb = jnp.broadcast_to(acs[:, hh:hh + 1], (SSD_CHUNK, SSD_CHUNK))
                    seg = colb - acs_t[hh:hh + 1, :]
                    decay = jnp.where(causal, jnp.exp(jnp.where(causal, seg, 0.0)), 0.0)
                    intra = cb * decay * dt_t[hh:hh + 1, :]
                    inter = cg * jnp.exp(colb)
                    lhs = jnp.concatenate([intra, inter], axis=1).astype(BF16)
                    y_h.append(_dot(lhs, rhs))
                    st_h.append(_dot((bgt * dtd_t[hh:hh + 1, :]).astype(BF16), xs_p))
                    cd_h.append(jnp.broadcast_to(chunk_decay[:, hh:hh + 1], (1, LANES)))
                ys.append(jnp.where(low_half, y_h[0], y_h[1]))
                state_ref[pair] = (prev * jnp.where(low_half_row, cd_h[0], cd_h[1])
                                   + jnp.where(low_half, st_h[0], st_h[1]))
            yg = jnp.concatenate(ys, axis=1)
            yg = yg + xs[r0:r0 + SSD_CHUNK, g * SSD_GW:(g + 1) * SSD_GW] * dsk[:, g * SSD_GW:(g + 1) * SSD_GW]
            u = yg * _silu(z[r0:r0 + SSD_CHUNK, g * SSD_GW:(g + 1) * SSD_GW])
            u = u * lax.rsqrt(jnp.mean(u * u, axis=-1, keepdims=True) + EPS)
            o_ref[r0:r0 + SSD_CHUNK, g * SSD_GW:(g + 1) * SSD_GW] = (
                u * nw[:, g * SSD_GW:(g + 1) * SSD_GW]).astype(o_ref.dtype)


def _pad_lanes(v, n=LANES):
    v = v.reshape(1, -1).astype(F32)
    return jnp.pad(v, ((0, 0), (0, n - v.shape[1])))


def _ssd_mixer(proj, bsz, seq, conv_w, conv_b, dt_bias, a_log, d_skip, norm_w, ts=256):
    nsteps = seq // ts
    row = lambda b, s: b * nsteps + s
    xw, bw, cw = conv_w[:, :SSD_WIDTH], conv_w[:, SSD_WIDTH:SSD_WIDTH + 512], conv_w[:, SSD_WIDTH + 512:]
    xb, bb, cb = conv_b[:SSD_WIDTH], conv_b[SSD_WIDTH:SSD_WIDTH + 512], conv_b[SSD_WIDTH + 512:]
    tri = jnp.asarray(np.tril(np.ones((SSD_CHUNK, SSD_CHUNK), np.float32)), BF16)
    const = lambda shape: pl.BlockSpec(shape, lambda b, s: (0,) * len(shape))
    kern = functools.partial(_ssd_kernel, ts=ts)
    return pl.pallas_call(
        kern,
        grid=(bsz, nsteps),
        in_specs=[pl.BlockSpec((ts, SSD_WIDTH), lambda b, s: (row(b, s), COL_Z // SSD_WIDTH)),
                  pl.BlockSpec((ts, SSD_WIDTH), lambda b, s: (row(b, s), COL_XS // SSD_WIDTH)),
                  pl.BlockSpec((ts, 512), lambda b, s: (row(b, s), COL_B // 512)),
                  pl.BlockSpec((ts, 512), lambda b, s: (row(b, s), COL_C // 512)),
                  pl.BlockSpec((ts, LANES), lambda b, s: (row(b, s), COL_DT // LANES)),
                  const((SSD_CONV, SSD_WIDTH)), const((SSD_CONV, 512)), const((SSD_CONV, 512)),
                  const((1, SSD_WIDTH)), const((1, 512)), const((1, 512)),
                  const((1, LANES)), const((1, LANES)), const((1, SSD_WIDTH)), const((1, SSD_WIDTH)),
                  const((SSD_CHUNK, SSD_CHUNK))],
        out_specs=pl.BlockSpec((ts, SSD_WIDTH), lambda b, s: (row(b, s), 0)),
        out_shape=jax.ShapeDtypeStruct((bsz * seq, SSD_WIDTH), BF16),
        scratch_shapes=[pltpu.VMEM((ts + SUBLANES, SSD_WIDTH), F32), pltpu.VMEM((ts + SUBLANES, 512), F32),
                        pltpu.VMEM((ts + SUBLANES, 512), F32),
                        pltpu.VMEM((SSD_HEADS // 2, SSD_STATE, 2 * SSD_HEAD_DIM), F32)],
        compiler_params=_params(("arbitrary", "arbitrary")),
        name="ssd_mixer",
    )(proj, proj, proj, proj, proj, xw, bw, cw, xb.reshape(1, -1), bb.reshape(1, -1), cb.reshape(1, -1),
      _pad_lanes(dt_bias), _pad_lanes(a_log), jnp.repeat(d_skip.astype(F32), SSD_HEAD_DIM).reshape(1, -1),
      norm_w.reshape(1, -1), tri)


def _hgrn_kernel(q_ref, f_ref, i_ref, g_ref, lbl_ref, nw_ref, cum_ref, o_ref, state_ref, *, ts, layer):
    s = pl.program_id(1)

    @pl.when(s == 0)
    def _():
        state_ref[...] = jnp.zeros_like(state_ref)

    nck = ts // HGRN_CHUNK
    lg = lbl_ref[...]
    e = jnp.exp(lg - jnp.max(lg, axis=0, keepdims=True))
    sm = e / jnp.sum(e, axis=0, keepdims=True)
    ridx = lax.broadcasted_iota(jnp.int32, lg.shape, 0)
    lb = jnp.sum(jnp.where((ridx >= 1) & (ridx <= layer), sm, 0.0), axis=0, keepdims=True)

    cum = cum_ref[...]
    nw = nw_ref[...]
    ti = lax.broadcasted_iota(jnp.int32, (ts, ts), 0)
    tj = lax.broadcasted_iota(jnp.int32, (ts, ts), 1)
    blockcausal = (ti // HGRN_CHUNK == tj // HGRN_CHUNK) & (ti >= tj)
    tok = lax.broadcasted_iota(jnp.int32, (1, ts), 1)

    def head(h):
        c0 = h * HGRN_DIM
        lbh = lb[:, c0:c0 + HGRN_DIM]
        f = lbh + (1.0 - lbh) * _sigmoid(f_ref[:, c0:c0 + HGRN_DIM])
        logf = jnp.log(f)
        k = 1.0 - f
        q = _silu(q_ref[:, c0:c0 + HGRN_DIM])
        v = i_ref[:, c0:c0 + HGRN_DIM]
        hi, mid, lo = _split3(logf)
        b = _dot(cum, hi) + _dot(cum, mid) + _dot(cum, lo)
        yield
        b3 = b.reshape(nck, HGRN_CHUNK, HGRN_DIM)
        bref = jnp.broadcast_to(b3[:, HGRN_CHUNK // 2:HGRN_CHUNK // 2 + 1, :], b3.shape).reshape(ts, HGRN_DIM)
        blast = jnp.broadcast_to(b3[:, HGRN_CHUNK - 1:HGRN_CHUNK, :], b3.shape).reshape(ts, HGRN_DIM)
        qe = (q * jnp.exp(b - bref)).astype(BF16)
        ke = (k * jnp.exp(bref - b)).astype(BF16)
        kl = (k * jnp.exp(blast - b)).astype(BF16)
        qb = (q * jnp.exp(b)).astype(BF16)
        vb = v.astype(BF16)
        att = _dot_nt(qe, ke)
        vt = v.T
        lhs = jnp.concatenate(
            [jnp.where(tok // HGRN_CHUNK == c, vt, 0.0) for c in range(nck)], axis=0).astype(BF16)
        st = _dot(lhs, kl)
        yield
        o = _dot(jnp.where(blockcausal, att, 0.0).astype(BF16), vb)
        state = state_ref[h]
        outs = []
        for c in range(nck):
            t0 = c * HGRN_CHUNK
            outs.append(_dot_nt(qb[t0:t0 + HGRN_CHUNK], state.astype(BF16)))
            cd = jnp.exp(blast[t0:t0 + 1, :])
            state = state * cd + st[c * HGRN_DIM:(c + 1) * HGRN_DIM]
            if c % 2 == 1:
                yield
        state_ref[h] = state
        o = o + jnp.concatenate(outs, axis=0)
        o = o * lax.rsqrt(jnp.mean(o * o, axis=-1, keepdims=True) + EPS) * nw
        o_ref[:, c0:c0 + HGRN_DIM] = (o * _silu(g_ref[:, c0:c0 + HGRN_DIM])).astype(o_ref.dtype)

    for h0 in range(0, HGRN_HEADS, HGRN_HEADS_INTERLEAVED):
        live = [head(h) for h in range(h0, h0 + HGRN_HEADS_INTERLEAVED)]
        while live:
            live = [gen for gen in live if next(gen, live) is not live]


def _hgrn_cum_matrix(ts):
    t = np.arange(ts)
    same = (t[:, None] // HGRN_CHUNK) == (t[None, :] // HGRN_CHUNK)
    return (same & (t[None, :] <= t[:, None])).astype(np.float32)


def _hgrn_mixer(proj, bsz, seq, lb_logits, norm_w, layer, ts=256):
    nsteps = seq // ts
    row = lambda b, s: b * nsteps + s
    depth = lb_logits.shape[0]
    cum = jnp.asarray(_hgrn_cum_matrix(ts), BF16)
    kern = functools.partial(_hgrn_kernel, ts=ts, layer=layer)
    blk = lambda col: pl.BlockSpec((ts, HGRN_WIDTH), lambda b, s: (row(b, s), col // HGRN_WIDTH))
    return pl.pallas_call(
        kern,
        grid=(bsz, nsteps),
        in_specs=[blk(COL_HQ), blk(COL_HF), blk(COL_HI), blk(COL_HG),
                  pl.BlockSpec((depth, HGRN_WIDTH), lambda b, s: (0, 0)),
                  pl.BlockSpec((1, HGRN_DIM), lambda b, s: (0, 0)),
                  pl.BlockSpec((ts, ts), lambda b, s: (0, 0))],
        out_specs=pl.BlockSpec((ts, HGRN_WIDTH), lambda b, s: (row(b, s), 0)),
        out_shape=jax.ShapeDtypeStruct((bsz * seq, HGRN_WIDTH), BF16),
        scratch_shapes=[pltpu.VMEM((HGRN_HEADS, HGRN_DIM, HGRN_DIM), F32)],
        compiler_params=_params(("arbitrary", "arbitrary")),
        name="hgrn2_mixer",
    )(proj, proj, proj, proj, lb_logits.astype(F32), norm_w.reshape(1, HGRN_DIM).astype(F32), cum)


def _rel_bucket_np(dist):
    max_exact = REL_BUCKETS // 2
    d = np.maximum(dist, 0)
    ratio = np.maximum(d, 1).astype(np.float32) / np.float32(max_exact)
    log_ratio = np.log(ratio).astype(np.float32) / np.float32(math.log(REL_MAX_DIST / max_exact))
    large = np.minimum(max_exact + (log_ratio * np.float32(REL_BUCKETS - max_exact)).astype(np.int32),
                       REL_BUCKETS - 1)
    return np.where(d < max_exact, d, large).astype(np.int32)


def _bias_expand_kernel(rb_ref, bmap_ref, o_ref):
    h = pl.program_id(0)
    bm = bmap_ref[...]
    out = jnp.full(bm.shape, NEG_INF, F32)
    for k in range(REL_BUCKETS):
        out = jnp.where(bm == k, rb_ref[k, h] * LOG2E, out)
    o_ref[...] = out


def _bias_expand(rel_bias, bmap, tr):
    rows, cols = bmap.shape
    return pl.pallas_call(
        _bias_expand_kernel,
        grid=(NSA_HEADS, rows // tr),
        in_specs=[pl.BlockSpec(memory_space=pltpu.SMEM),
                  pl.BlockSpec((tr, cols), lambda h, i: (i, 0))],
        out_specs=pl.BlockSpec((None, tr, cols), lambda h, i: (h, i, 0)),
        out_shape=jax.ShapeDtypeStruct((NSA_HEADS, rows, cols), F32),
        compiler_params=_params(("arbitrary", "arbitrary")),
        name="nsa_bias_expand",
    )(rel_bias.astype(F32), bmap)


def _nsa_bias_tables(rel_bias, seq):
    ncp = seq // CMP_STRIDE
    t = np.arange(seq)[:, None]
    cmp_end = np.arange(ncp)[None, :] * CMP_STRIDE + CMP_BLOCK - 1
    d = t - cmp_end
    bmap_cmp = np.where(d >= 0, _rel_bucket_np(d), -1).astype(np.int32)
    l = np.arange(Q_BLOCK)[:, None]
    j = np.arange(WIN_KEYS)[None, :]
    dist = l - j + WINDOW
    win = np.where((dist >= 0) & (dist < WINDOW), _rel_bucket_np(dist), -1)
    far = _rel_bucket_np(np.arange(Q_BLOCK + 1, 8 * seq))
    assert (far == far[0]).all(), "distances beyond one query block must share a single bucket"
    bmap_tab = np.concatenate([win, np.full((Q_BLOCK, Q_BLOCK), far[0])], axis=1).astype(np.int32)
    bias_cmp = _bias_expand(rel_bias, jnp.asarray(bmap_cmp), 512)
    bias_tab = _bias_expand(rel_bias, jnp.asarray(bmap_tab), Q_BLOCK)
    return bias_cmp, bias_tab


def _cmp_kernel(u_ref, pe_ref, w1_ref, w2_ref, o_ref, *, ncp):
    half = CMP_BLOCK // 2
    pe = pe_ref[...]
    top = jnp.zeros((ncp, CMP_HIDDEN), F32)
    bot = jnp.zeros((ncp, CMP_HIDDEN), F32)
    for l in range(half):
        x = u_ref[pl.ds(l, ncp, stride=half), :]
        top = top + _dot((x + pe[l:l + 1, :]).astype(BF16), w1_ref[l])
        bot = bot + _dot((x + pe[half + l:half + l + 1, :]).astype(BF16), w1_ref[half + l])
    hid = top + pltpu.roll(bot, ncp - 1, 0)
    o_ref[...] = _dot(_silu(hid).astype(BF16), w2_ref[...])


def _nsa_compress(proj, bsz, seq, pe, w1, w2):
    ncp = seq // CMP_STRIDE
    kern = functools.partial(_cmp_kernel, ncp=ncp)
    return pl.pallas_call(
        kern,
        grid=(bsz, 2, NSA_KV_HEADS),
        in_specs=[pl.BlockSpec((seq, NSA_HEAD_DIM), lambda b, t, h: (b, COL_KC // NSA_HEAD_DIM + NSA_KV_HEADS * t + h)),
                  pl.BlockSpec((None, CMP_BLOCK, NSA_HEAD_DIM), lambda b, t, h: (t, 0, 0)),
                  pl.BlockSpec((None, CMP_BLOCK, NSA_HEAD_DIM, CMP_HIDDEN), lambda b, t, h: (t, 0, 0, 0)),
                  pl.BlockSpec((None, CMP_HIDDEN, NSA_HEAD_DIM), lambda b, t, h: (t, 0, 0))],
        out_specs=pl.BlockSpec((None, None, None, ncp, NSA_HEAD_DIM), lambda b, t, h: (b, t, h, 0, 0)),
        out_shape=jax.ShapeDtypeStruct((bsz, 2, NSA_KV_HEADS, ncp, NSA_HEAD_DIM), F32),
        compiler_params=_params(("arbitrary", "arbitrary", "arbitrary")),
        name="nsa_compress",
    )(proj, pe.astype(F32), w1.reshape(2, CMP_BLOCK, NSA_HEAD_DIM, CMP_HIDDEN).astype(BF16), w2.astype(BF16))


def _softmax_start(s, v):
    m = jnp.max(s, axis=-1, keepdims=True)
    p = jnp.exp2(s - m)
    return m, jnp.sum(p, axis=-1, keepdims=True), _dot(p.astype(BF16), v)


def _softmax_update(s, v, carry):
    m, l, acc = carry
    m_new = jnp.maximum(m, jnp.max(s, axis=-1, keepdims=True))
    alpha = jnp.exp2(m - m_new)
    p = jnp.exp2(s - m_new)
    return m_new, alpha * l + jnp.sum(p, axis=-1, keepdims=True), alpha * acc + _dot(p.astype(BF16), v)


def _softmax_finish(carry):
    m, l, acc = carry
    return jnp.where(m > 0.5 * NEG_INF, acc / jnp.maximum(l, 1e-30), 0.0)


def _nsa_kernel(q_ref, gate_ref, kc_ref, vc_ref, ks_ref, vs_ref, kw_ref, vw_ref, bcmp_ref, btab_ref,
                kone_ref, c2s_ref, o_ref, kaug_ref, vsb_ref, kwb_ref, vwb_ref, sa_ref, sb_ref, *, n_sel, nqb):
    hk = pl.program_id(1)
    c0 = pl.program_id(2) * nqb
    rows = NSA_GQA * Q_BLOCK
    lane = lax.broadcasted_iota(jnp.int32, (Q_BLOCK, LANES), 1)

    @pl.when(c0 == 0)
    def _():
        kaug_ref[0:Q_BLOCK, 0:NSA_HEAD_DIM] = jnp.zeros((Q_BLOCK, NSA_HEAD_DIM), BF16)
        kaug_ref[Q_BLOCK:, 0:NSA_HEAD_DIM] = ks_ref[...].astype(BF16)
        kaug_ref[:, NSA_HEAD_DIM:] = kone_ref[...]
        vsb_ref[0:Q_BLOCK, :] = jnp.zeros((Q_BLOCK, NSA_HEAD_DIM), BF16)
        vsb_ref[Q_BLOCK:, :] = vs_ref[...].astype(BF16)
        kwb_ref[0:WINDOW, 0:NSA_HEAD_DIM] = jnp.zeros((WINDOW, NSA_HEAD_DIM), BF16)
        kwb_ref[WINDOW:, 0:NSA_HEAD_DIM] = kw_ref[...].astype(BF16)
        for r0 in range(0, WINDOW, Q_BLOCK):
            kwb_ref[r0:r0 + Q_BLOCK, NSA_HEAD_DIM:] = jnp.where(lane == AUG_PAD, 1.0, 0.0).astype(BF16)
        kwb_ref[WINDOW:, NSA_HEAD_DIM:] = jnp.zeros((kw_ref.shape[0], LANES), BF16)
        vwb_ref[0:WINDOW, :] = jnp.zeros((WINDOW, NSA_HEAD_DIM), BF16)
        vwb_ref[WINDOW:, :] = vw_ref[...].astype(BF16)

    gens = [_nsa_unit(c0 + u, q_ref[u * Q_BLOCK:(u + 1) * Q_BLOCK, :], bcmp_ref[:, u * Q_BLOCK:(u + 1) * Q_BLOCK, :],
                      btab_ref, kc_ref, vc_ref, c2s_ref, kaug_ref, vsb_ref, kwb_ref, vwb_ref, n_sel)
            for u in range(nqb)]
    units = [None] * nqb
    while any(un is None for un in units):
        for u, gen in enumerate(gens):
            if units[u] is None:
                units[u] = next(gen)
    qaug_far = jnp.concatenate([un[3] for un in units], axis=0)
    carry = tuple(jnp.concatenate([un[2][i] for un in units], axis=0) for i in range(3))

    far_blocks = FAR_KEYS // Q_BLOCK
    n_pairs = (c0 + nqb - 2 + 2 * far_blocks - 1) // (2 * far_blocks)
    last_slab = (kaug_ref.shape[0] - Q_BLOCK) // FAR_KEYS - 1

    def far_logits(slab):
        r0 = pl.multiple_of(jnp.minimum(slab, last_slab) * FAR_KEYS + Q_BLOCK, Q_BLOCK)
        return _dot_nt(qaug_far, kaug_ref[pl.ds(r0, FAR_KEYS), :])

    def far_values(slab):
        return vsb_ref[pl.ds(pl.multiple_of(slab * FAR_KEYS + Q_BLOCK, Q_BLOCK), FAR_KEYS), :]

    sa_ref[...] = far_logits(0)

    def far_body(k, carry):
        sb_ref[...] = far_logits(2 * k + 1)
        carry = _softmax_update(sa_ref[...], far_values(2 * k), carry)
        sa_ref[...] = far_logits(2 * k + 2)
        return _softmax_update(sb_ref[...], far_values(2 * k + 1), carry)

    o_sel = _softmax_finish(lax.fori_loop(0, n_pairs, far_body, carry))

    for u in range(nqb):
        o_cmp, o_win = units[u][0], units[u][1]
        gates = _sigmoid(gate_ref[u * Q_BLOCK:(u + 1) * Q_BLOCK, :])
        for g in range(NSA_GQA):
            def gate_col(br):
                return jnp.sum(jnp.where(lane == hk * (3 * NSA_GQA) + g * 3 + br, gates, 0.0), axis=-1, keepdims=True)

            r0 = g * Q_BLOCK
            out = (gate_col(0) * o_cmp[r0:r0 + Q_BLOCK] + gate_col(1) * o_sel[u * rows + r0:u * rows + r0 + Q_BLOCK]
                   + gate_col(2) * o_win[r0:r0 + Q_BLOCK])
            o_ref[u * Q_BLOCK:(u + 1) * Q_BLOCK, g * NSA_HEAD_DIM:(g + 1) * NSA_HEAD_DIM] = out.astype(o_ref.dtype)


def _nsa_unit(c, q_in, bc_in, btab_ref, kc_ref, vc_ref, c2s_ref, kaug_ref, vsb_ref, kwb_ref, vwb_ref, n_sel):
    rows = NSA_GQA * Q_BLOCK
    lane = lax.broadcasted_iota(jnp.int32, (Q_BLOCK, LANES), 1)
    q = q_in * (NSA_HEAD_DIM ** -0.5 * LOG2E)
    q3 = jnp.concatenate([q[:, g * NSA_HEAD_DIM:(g + 1) * NSA_HEAD_DIM] for g in range(NSA_GQA)],
                         axis=0).astype(BF16)

    btab = btab_ref[...].reshape(rows, WIN_KEYS + Q_BLOCK)
    near0 = pl.multiple_of(c * Q_BLOCK, Q_BLOCK)

    qaug_win = jnp.concatenate(
        [q3, jnp.concatenate([jnp.where(lane == AUG_PAD, NEG_INF, 0.0).astype(BF16)] * NSA_GQA, axis=0)], axis=1)
    s_win = _dot_nt(qaug_win, kwb_ref[pl.ds(near0, WIN_KEYS), :]) + btab[:, 0:WIN_KEYS]
    yield None

    bc = bc_in.reshape(rows, bc_in.shape[-1])
    valid = bc > 0.5 * NEG_INF
    s = _dot_nt(q3, kc_ref[...].astype(BF16)) + bc
    yield None
    o_win = _softmax_finish(_softmax_start(s_win, vwb_ref[pl.ds(near0, WIN_KEYS), :]))
    yield None
    m = jnp.max(s, axis=-1, keepdims=True)
    p = jnp.where(valid, jnp.exp2(s - m), 0.0)
    p = p / jnp.maximum(jnp.sum(p, axis=-1, keepdims=True), 1e-30)
    o_cmp = _dot(p.astype(BF16), vc_ref[...].astype(BF16))

    psum = p[0:Q_BLOCK] + p[Q_BLOCK:2 * Q_BLOCK] + p[2 * Q_BLOCK:3 * Q_BLOCK]
    ph = psum.astype(BF16)
    plo = (psum - ph.astype(F32)).astype(BF16)
    c2s = c2s_ref[...]
    imp_t = _dot_nt(c2s, ph) + _dot_nt(c2s, plo)
    nsp = -(-n_sel // SUBLANES) * SUBLANES
    imp_t = imp_t[0:nsp]
    tq = c * Q_BLOCK + lax.broadcasted_iota(jnp.int32, (nsp, Q_BLOCK), 1)
    jj = lax.broadcasted_iota(jnp.int32, (nsp, Q_BLOCK), 0)
    valid_sel = (jj < n_sel) & (jj * SEL_BLOCK <= tq)
    back = tq // SEL_BLOCK - jj
    force = (jj == 0) | ((back >= 0) & (back < SEL_LOCAL))
    score = jnp.where(valid_sel, imp_t + jnp.where(force, FORCE_BONUS, 0.0), NEG_INF)
    yield None
    n_tiles = nsp // SUBLANES
    tiles = [score[v * SUBLANES:(v + 1) * SUBLANES] for v in range(n_tiles)]
    sub = lax.broadcasted_iota(jnp.int32, (SUBLANES, Q_BLOCK), 0)
    cnts = [jnp.zeros((SUBLANES, Q_BLOCK), F32) for _ in range(n_tiles)]
    for jp in range(n_sel):
        v0, r0 = divmod(jp, SUBLANES)
        r = tiles[v0][r0:r0 + 1, :]
        for v in range(n_tiles):
            gt = jnp.where(r > tiles[v], 1.0, 0.0)
            ge = jnp.where(r >= tiles[v], 1.0, 0.0)
            cnts[v] = cnts[v] + (gt if v < v0 else ge if v > v0 else jnp.where(sub > r0, ge, gt))
        if jp % (2 * SUBLANES) == 2 * SUBLANES - 1:
            yield None
    cnt = jnp.concatenate(cnts, axis=0)
    keep = valid_sel & (cnt < float(min(SEL_TOPK, n_sel)))

    ext = lax.broadcasted_iota(jnp.int32, (LANES - nsp, Q_BLOCK), 0) + nsp
    aug = jnp.concatenate([jnp.where(keep, 0.0, NEG_INF), jnp.where(ext == AUG_PAD, NEG_INF, 0.0)], axis=0)
    aug_near = aug.T
    aug_far = jnp.where((lane >= 2 * (c - 1)) & (lane < AUG_BIAS_HI), NEG_INF, aug_near)
    far_parts = []
    for g in range(NSA_GQA):
        bfar = btab[g * Q_BLOCK:(g + 1) * Q_BLOCK, WIN_KEYS:]
        hi = bfar.astype(BF16).astype(F32)
        far_parts.append(jnp.where(lane == AUG_BIAS_HI, hi, jnp.where(lane == AUG_BIAS_LO, bfar - hi, aug_far)))
    qaug_near = jnp.concatenate([q3, jnp.concatenate([aug_near.astype(BF16)] * NSA_GQA, axis=0)], axis=1)
    qaug_far = jnp.concatenate([q3, jnp.concatenate(far_parts, axis=0).astype(BF16)], axis=1)

    s = _dot_nt(qaug_near, kaug_ref[pl.ds(near0, NEAR_KEYS), :]) + btab[:, WIN_KEYS - NEAR_KEYS:WIN_KEYS]
    yield None
    carry = _softmax_start(s, vsb_ref[pl.ds(near0, NEAR_KEYS), :])
    yield o_cmp, o_win, carry, qaug_far


def _nsa_attention(proj, cmp_kv, bias_cmp, bias_tab, bsz, seq):
    nq = seq // Q_BLOCK
    ncp = seq // CMP_STRIDE
    n_sel = seq // SEL_BLOCK
    assert n_sel <= AUG_BIAS_HI and NSA_HEAD_DIM == LANES
    assert seq % (2 * FAR_KEYS) == 0, "far steps walk the keys two 512-key slabs at a time"
    key = np.arange(-Q_BLOCK, seq)[:, None]
    ln = np.arange(LANES)[None, :]
    kone = np.where(key < 0, ln == AUG_PAD,
                    ((ln < AUG_BIAS_HI) & (key // SEL_BLOCK == ln)) | (ln == AUG_BIAS_HI) | (ln == AUG_BIAS_LO))
    kone = jnp.asarray(kone, BF16)
    c_start = np.arange(ncp)[None, :] * CMP_STRIDE
    s_start = np.arange(LANES)[:, None] * SEL_BLOCK
    overlap = np.clip(np.minimum(c_start + CMP_BLOCK, s_start + SEL_BLOCK) - np.maximum(c_start, s_start), 0, None)
    overlap = np.where(np.arange(LANES)[:, None] < n_sel, overlap, 0)
    c2s_t = jnp.asarray(overlap / CMP_BLOCK, BF16)
    gqa_w = NSA_GQA * NSA_HEAD_DIM
    full = lambda col: pl.BlockSpec((seq, NSA_HEAD_DIM), lambda b, h, c: (b, col // NSA_HEAD_DIM + h))
    nqb = NSA_Q_BLOCKS_PER_STEP
    ns = nq // nqb
    qr = nqb * Q_BLOCK
    kern = functools.partial(_nsa_kernel, n_sel=n_sel, nqb=nqb)
    return pl.pallas_call(
        kern,
        grid=(bsz, NSA_KV_HEADS, ns),
        in_specs=[pl.BlockSpec((qr, gqa_w), lambda b, h, c: (b * ns + c, COL_Q // gqa_w + h)),
                  pl.BlockSpec((qr, LANES), lambda b, h, c: (b * ns + c, COL_GATE // LANES)),
                  pl.BlockSpec((None, None, None, ncp, NSA_HEAD_DIM), lambda b, h, c: (b, 0, h, 0, 0)),
                  pl.BlockSpec((None, None, None, ncp, NSA_HEAD_DIM), lambda b, h, c: (b, 1, h, 0, 0)),
                  full(COL_KS), full(COL_VS), full(COL_KW), full(COL_VW),
                  pl.BlockSpec((NSA_GQA, qr, ncp), lambda b, h, c: (h, c, 0)),
                  pl.BlockSpec((NSA_GQA, Q_BLOCK, WIN_KEYS + Q_BLOCK), lambda b, h, c: (h, 0, 0)),
                  pl.BlockSpec((seq + Q_BLOCK, LANES), lambda b, h, c: (0, 0)),
                  pl.BlockSpec((LANES, ncp), lambda b, h, c: (0, 0))],
        out_specs=pl.BlockSpec((qr, gqa_w), lambda b, h, c: (b * ns + c, h)),
        out_shape=jax.ShapeDtypeStruct((bsz * seq, NSA_WIDTH), BF16),
        scratch_shapes=[pltpu.VMEM((seq + Q_BLOCK, 2 * NSA_HEAD_DIM), BF16),
                        pltpu.VMEM((seq + Q_BLOCK, NSA_HEAD_DIM), BF16),
                        pltpu.VMEM((seq + WINDOW, 2 * NSA_HEAD_DIM), BF16),
                        pltpu.VMEM((seq + WINDOW, NSA_HEAD_DIM), BF16),
                        pltpu.VMEM((nqb * NSA_GQA * Q_BLOCK, FAR_KEYS), F32),
                        pltpu.VMEM((nqb * NSA_GQA * Q_BLOCK, FAR_KEYS), F32)],
        compiler_params=_params(("arbitrary", "arbitrary", "arbitrary")),
        name="nsa_attention",
    )(proj, proj, cmp_kv, cmp_kv, proj, proj, proj, proj, bias_cmp, bias_tab, kone, c2s_t)


_IN_OFF = tuple(int(v) for v in np.cumsum((0,) + IN_SIZES))
_W_IN_COPIES = ((_IN_OFF[0], SSD_WIDTH, COL_Z), (_IN_OFF[1], SSD_WIDTH, COL_XS),
                (_IN_OFF[1] + SSD_WIDTH, 512, COL_B), (_IN_OFF[1] + SSD_WIDTH + 512, 512, COL_C),
                (_IN_OFF[3], NSA_WIDTH, COL_Q), (_IN_OFF[4], 6 * NSA_KV_WIDTH, COL_KC),
                (_IN_OFF[11], 4 * HGRN_WIDTH, COL_HQ))
_W_IN_NARROW = ((_IN_OFF[2], IN_SIZES[2], COL_DT, COL_GATE - COL_DT), (_IN_OFF[10], IN_SIZES[10], COL_GATE, COL_HQ - COL_GATE))


def _relayout_kernel(w_ref, o_ref):
    for src, width, dst in _W_IN_COPIES:
        o_ref[:, dst:dst + width] = w_ref[:, src:src + width].astype(BF16)
    for src, valid, dst, padded in _W_IN_NARROW:
        tile = w_ref[:, src:src + LANES]
        lane = lax.broadcasted_iota(jnp.int32, tile.shape, 1)
        o_ref[:, dst:dst + LANES] = jnp.where(lane < valid, tile, 0.0).astype(BF16)
        if padded > LANES:
            o_ref[:, dst + LANES:dst + padded] = jnp.zeros((tile.shape[0], padded - LANES), BF16)


def _relayout_w_in(w, tr=128):
    depth, d, n = w.shape
    return pl.pallas_call(
        _relayout_kernel,
        grid=(depth, d // tr),
        in_specs=[pl.BlockSpec((None, tr, n), lambda l, i: (l, i, 0))],
        out_specs=pl.BlockSpec((None, tr, IN_PAD), lambda l, i: (l, i, 0)),
        out_shape=jax.ShapeDtypeStruct((depth, d, IN_PAD), BF16),
        compiler_params=_params(("parallel", "parallel")),
        name="w_in_relayout",
    )(w)


def _tiles(seq):
    return dict(in_proj=(1024, 1024), out_proj=(1024, 512), gate_up=(min(1024, seq), 256), down=(256, 1024))


def kernel(x, norm_mix_w, w_in, ssd_conv_w, ssd_conv_b, ssd_dt_bias, ssd_a_log, ssd_d, ssd_norm_w, nsa_cmp_pe, nsa_cmp_w1, nsa_cmp_w2, rel_bias, hgrn_lb_logits, hgrn_norm_w, w_out, norm_ffn_w, ffn_w_gu, ffn_conv_w, ffn_conv_b, ffn_w_down, norm_f_w):
    bsz, seq, d = x.shape
    depth = w_in.shape[0]
    xr = x.reshape(bsz * seq, d).astype(F32)
    bias_cmp, bias_tab = _nsa_bias_tables(rel_bias, seq)
    tiles = _tiles(seq)
    w_in_b = _relayout_w_in(jnp.pad(w_in.astype(BF16), ((0, 0), (0, 0), (0, -w_in.shape[2] % LANES))))
    w_out_b = w_out.astype(BF16)
    w_gu_b = ffn_w_gu.astype(BF16)
    w_down_b = ffn_w_down.astype(BF16)
    for l in range(depth):
        n = _rmsnorm(xr, norm_mix_w[l], BF16)
        proj = _matmul(n, w_in_b, l, F32, *tiles["in_proj"], name="in_proj")
        y_ssd = _ssd_mixer(proj, bsz, seq, ssd_conv_w[l].astype(F32), ssd_conv_b[l].astype(F32), ssd_dt_bias[l],
                           ssd_a_log[l], ssd_d[l], ssd_norm_w[l].astype(F32))
        cmp_kv = _nsa_compress(proj, bsz, seq, nsa_cmp_pe[l], nsa_cmp_w1[l], nsa_cmp_w2[l])
        y_nsa = _nsa_attention(proj, cmp_kv, bias_cmp, bias_tab, bsz, seq)
        y_hgrn = _hgrn_mixer(proj, bsz, seq, hgrn_lb_logits, hgrn_norm_w[l], l)
        xr = _out_proj([y_ssd, y_nsa, y_hgrn], w_out_b, l, xr, *tiles["out_proj"])
        hn = _rmsnorm(xr, norm_ffn_w[l], BF16)
        act = _gate_up(hn, w_gu_b, l, ffn_conv_w[l].astype(F32), ffn_conv_b[l].astype(F32), seq, *tiles["gate_up"])
        xr = _matmul_residual_wres(act, w_down_b, l, xr, *tiles["down"], name="ffn_down")
    out = _rmsnorm(xr, norm_f_w, x.dtype)
    return out.reshape(bsz, seq, d)
```

```python
import functools
import math

import numpy as np
import jax
import jax.numpy as jnp
from jax import lax
from jax.experimental import pallas as pl
from jax.experimental.pallas import tpu as pltpu

F32 = jnp.float32
BF16 = jnp.bfloat16

D_MODEL = 4096
SSD_HEAD_DIM = 64
SSD_WIDTH = 1536
SSD_HEADS = 24
SSD_GROUPS = 4
SSD_HPG = 6
SSD_STATE = 128
SSD_CONV = 4
SSD_CHUNK = 128
SSD_GW = SSD_WIDTH // SSD_GROUPS
NSA_HEAD_DIM = 128
NSA_WIDTH = 1536
NSA_HEADS = 12
NSA_KV_HEADS = 4
NSA_GQA = 3
NSA_KV_WIDTH = 512
CMP_BLOCK = 32
CMP_STRIDE = 16
CMP_HIDDEN = 256
SEL_BLOCK = 64
SEL_TOPK = 16
SEL_LOCAL = 2
WINDOW = 512
Q_BLOCK = 128
FORCE_BONUS = 1e4
HGRN_WIDTH = 1024
HGRN_HEADS = 8
HGRN_DIM = 128
HGRN_CHUNK = 32
REL_BUCKETS = 32
REL_MAX_DIST = 128
D_FF = 11008
FFN_CONV = 3
EPS = 1e-6
NEG_INF = -1e30

IN_SIZES = (SSD_WIDTH, SSD_WIDTH + 2 * SSD_GROUPS * SSD_STATE, SSD_HEADS, NSA_WIDTH,
            NSA_KV_WIDTH, NSA_KV_WIDTH, NSA_KV_WIDTH, NSA_KV_WIDTH, NSA_KV_WIDTH, NSA_KV_WIDTH,
            3 * NSA_HEADS, HGRN_WIDTH, HGRN_WIDTH, HGRN_WIDTH, HGRN_WIDTH)

LANES = 128
SUBLANES = 8
VMEM_LIMIT = 56 * 1024 * 1024

COL_Z = 0
COL_XS = 1536
COL_Q = 3072
COL_B = 4608
COL_C = 5120
COL_KC = 5632
COL_VC = 6144
COL_KS = 6656
COL_VS = 7168
COL_KW = 7680
COL_VW = 8192
COL_DT = 8704
COL_GATE = 8832
COL_HQ = 9216
COL_HF = 10240
COL_HI = 11264
COL_HG = 12288
IN_PAD = 13312

LOG2E = math.log2(math.e)
AUG_BIAS_HI = 64
AUG_BIAS_LO = 65
AUG_PAD = 66
WIN_KEYS = WINDOW + Q_BLOCK
NEAR_KEYS = 2 * Q_BLOCK
FAR_KEYS = 4 * Q_BLOCK
HGRN_HEADS_INTERLEAVED = 8
NSA_Q_BLOCKS_PER_STEP = 2


def _params(semantics):
    return pltpu.CompilerParams(dimension_semantics=semantics, vmem_limit_bytes=VMEM_LIMIT)


def _sigmoid(x):
    return 1.0 / (1.0 + jnp.exp(-x))


def _silu(x):
    return x * _sigmoid(x)


def _dot(a, b):
    return jnp.dot(a, b, preferred_element_type=F32)


def _dot_nt(a, b):
    return lax.dot_general(a, b, (((1,), (1,)), ((), ())), preferred_element_type=F32)


def _split3(x):
    hi = x.astype(BF16)
    r1 = x - hi.astype(F32)
    mid = r1.astype(BF16)
    lo = (r1 - mid.astype(F32)).astype(BF16)
    return hi, mid, lo


def _rmsnorm_kernel(x_ref, w_ref, o_ref):
    x = x_ref[...]
    ms = jnp.mean(x * x, axis=-1, keepdims=True)
    o_ref[...] = (x * lax.rsqrt(ms + EPS) * w_ref[...]).astype(o_ref.dtype)


def _rmsnorm(x, w, out_dtype, tm=256):
    m, d = x.shape
    return pl.pallas_call(
        _rmsnorm_kernel,
        grid=(m // tm,),
        in_specs=[pl.BlockSpec((tm, d), lambda i: (i, 0)), pl.BlockSpec((1, d), lambda i: (0, 0))],
        out_specs=pl.BlockSpec((tm, d), lambda i: (i, 0)),
        out_shape=jax.ShapeDtypeStruct((m, d), out_dtype),
        compiler_params=_params(("parallel",)),
        name="rmsnorm",
    )(x, w.reshape(1, d).astype(F32))


def _mm_kernel(a_ref, w_ref, o_ref):
    o_ref[...] = _dot(a_ref[...], w_ref[...]).astype(o_ref.dtype)


def _mm_res_kernel(a_ref, w_ref, r_ref, o_ref):
    o_ref[...] = r_ref[...] + _dot(a_ref[...], w_ref[...])


def _matmul(a, w, layer, out_dtype, tm, tn, name):
    m, k = a.shape
    n = w.shape[2]
    return pl.pallas_call(
        _mm_kernel,
        grid=(m // tm, n // tn),
        in_specs=[pl.BlockSpec((tm, k), lambda i, j: (i, 0)),
                  pl.BlockSpec((None, k, tn), lambda i, j: (layer, 0, j))],
        out_specs=pl.BlockSpec((tm, tn), lambda i, j: (i, j)),
        out_shape=jax.ShapeDtypeStruct((m, n), out_dtype),
        compiler_params=_params(("parallel", "arbitrary")),
        name=name,
    )(a, w)


def _out_proj_kernel(*refs):
    *a_refs, w_ref, r_ref, o_ref = refs
    acc = r_ref[...]
    k0 = 0
    for a_ref in a_refs:
        acc = acc + _dot(a_ref[...], w_ref[k0:k0 + a_ref.shape[1], :])
        k0 += a_ref.shape[1]
    o_ref[...] = acc


def _out_proj(parts, w, layer, res, tm, tn):
    m = res.shape[0]
    k, n = w.shape[1], w.shape[2]
    assert sum(p.shape[1] for p in parts) == k
    return pl.pallas_call(
        _out_proj_kernel,
        grid=(m // tm, n // tn),
        in_specs=[pl.BlockSpec((tm, p.shape[1]), lambda i, j: (i, 0)) for p in parts]
        + [pl.BlockSpec((None, k, tn), lambda i, j: (layer, 0, j)),
           pl.BlockSpec((tm, tn), lambda i, j: (i, j))],
        out_specs=pl.BlockSpec((tm, tn), lambda i, j: (i, j)),
        out_shape=jax.ShapeDtypeStruct((m, n), F32),
        compiler_params=_params(("parallel", "arbitrary")),
        name="out_proj",
    )(*parts, w, res)


def _matmul_residual_wres(a, w, layer, res, tm, tn, name):
    m, k = a.shape
    n = w.shape[2]
    return pl.pallas_call(
        _mm_res_kernel,
        grid=(n // tn, m // tm),
        in_specs=[pl.BlockSpec((tm, k), lambda j, i: (i, 0)),
                  pl.BlockSpec((None, k, tn), lambda j, i: (layer, 0, j), pipeline_mode=pl.Buffered(1)),
                  pl.BlockSpec((tm, tn), lambda j, i: (i, j))],
        out_specs=pl.BlockSpec((tm, tn), lambda j, i: (i, j)),
        out_shape=jax.ShapeDtypeStruct((m, n), F32),
        compiler_params=_params(("arbitrary", "arbitrary")),
        name=name,
    )(a, w, res)


GU_ROW_CHUNKS = 8


def _gu_kernel(h_ref, wg_ref, wu_ref, cw_ref, cb_ref, o_ref, ga_ref, ua_ref, gb_ref, ub_ref, halo_ref,
               *, tm, nj, tiles_per_seq):
    t = pl.program_id(0)
    tp = jnp.maximum(t - 1, 0)
    ip = tp // nj
    jp = tp % nj
    rc = tm // GU_ROW_CHUNKS

    @pl.when(t == 0)
    def _():
        gb_ref[...] = jnp.zeros_like(gb_ref)
        ub_ref[...] = jnp.zeros_like(ub_ref)
        halo_ref[...] = jnp.zeros_like(halo_ref)

    def step(g_cur, u_cur, g_prev, u_prev):
        g_prev[0:SUBLANES, :] = jnp.where(ip % tiles_per_seq == 0, 0.0, halo_ref[jp])
        halo_ref[jp] = g_prev[tm:tm + SUBLANES, :]
        cw = cw_ref[...]
        for r in range(GU_ROW_CHUNKS):
            hr = h_ref[r * rc:(r + 1) * rc, :]
            g_cur[SUBLANES + r * rc:SUBLANES + (r + 1) * rc, :] = _dot(hr, wg_ref[...])
            u_cur[r * rc:(r + 1) * rc, :] = _dot(hr, wu_ref[...])
            acc = cb_ref[...]
            for sh in range(FFN_CONV):
                acc = acc + cw[FFN_CONV - 1 - sh:FFN_CONV - sh, :] * g_prev[pl.ds(SUBLANES - sh + r * rc, rc), :]
            o_ref[r * rc:(r + 1) * rc, :] = (_silu(acc) * u_prev[r * rc:(r + 1) * rc, :]).astype(o_ref.dtype)

    @pl.when(t % 2 == 0)
    def _():
        step(ga_ref, ua_ref, gb_ref, ub_ref)

    @pl.when(t % 2 == 1)
    def _():
        step(gb_ref, ub_ref, ga_ref, ua_ref)


def _gate_up(h, w_gu, layer, conv_w, conv_b, seq, tm, tn):
    m, k = h.shape
    nf = conv_w.shape[1]
    nj = nf // tn
    last = (m // tm) * nj - 1
    cur = lambda t: jnp.minimum(t, last)
    prev = lambda t: jnp.maximum(t - 1, 0)
    kern = functools.partial(_gu_kernel, tm=tm, nj=nj, tiles_per_seq=seq // tm)
    return pl.pallas_call(
        kern,
        grid=(last + 2,),
        in_specs=[pl.BlockSpec((tm, k), lambda t: (cur(t) // nj, 0)),
                  pl.BlockSpec((None, k, tn), lambda t: (layer, 0, cur(t) % nj)),
                  pl.BlockSpec((None, k, tn), lambda t: (layer, 0, cur(t) % nj + nj)),
                  pl.BlockSpec((FFN_CONV, tn), lambda t: (0, prev(t) % nj)),
                  pl.BlockSpec((1, tn), lambda t: (0, prev(t) % nj))],
        out_specs=pl.BlockSpec((tm, tn), lambda t: (prev(t) // nj, prev(t) % nj)),
        out_shape=jax.ShapeDtypeStruct((m, nf), BF16),
        scratch_shapes=[pltpu.VMEM((tm + SUBLANES, tn), F32), pltpu.VMEM((tm, tn), F32),
                        pltpu.VMEM((tm + SUBLANES, tn), F32), pltpu.VMEM((tm, tn), F32),
                        pltpu.VMEM((nj, SUBLANES, tn), F32)],
        compiler_params=_params(("arbitrary",)),
        name="ffn_gate_up_conv",
    )(h, w_gu, w_gu, conv_w, conv_b.reshape(1, nf))


def _ssd_kernel(z_ref, xs_ref, b_ref, c_ref, dt_ref, cwx_ref, cwb_ref, cwc_ref, cbx_ref, cbb_ref, cbc_ref,
                dtb_ref, alog_ref, dsk_ref, nw_ref, tri_ref, o_ref,
                extx_ref, extb_ref, extc_ref, state_ref, *, ts):
    s = pl.program_id(1)

    @pl.when(s == 0)
    def _():
        extx_ref[...] = jnp.zeros_like(extx_ref)
        extb_ref[...] = jnp.zeros_like(extb_ref)
        extc_ref[...] = jnp.zeros_like(extc_ref)
        state_ref[...] = jnp.zeros_like(state_ref)

    def conv_silu(u_ref, ext_ref, w_ref, bias_ref):
        ext_ref[0:SUBLANES, :] = ext_ref[ts:ts + SUBLANES, :]
        ext_ref[SUBLANES:, :] = u_ref[...]
        w = w_ref[...]
        acc = bias_ref[...] + w[SSD_CONV - 1:SSD_CONV, :] * u_ref[...]
        for sh in range(1, SSD_CONV):
            acc = acc + w[SSD_CONV - 1 - sh:SSD_CONV - sh, :] * ext_ref[pl.ds(SUBLANES - sh, ts), :]
        return _silu(acc)

    xs = conv_silu(xs_ref, extx_ref, cwx_ref, cbx_ref)
    bm = conv_silu(b_ref, extb_ref, cwb_ref, cbb_ref)
    cm = conv_silu(c_ref, extc_ref, cwc_ref, cbc_ref)
    dtr = dt_ref[...] + dtb_ref[...]
    dt = jnp.maximum(dtr, 0.0) + jnp.log1p(jnp.exp(-jnp.abs(dtr)))
    a = dt * (-jnp.exp(alog_ref[...]))
    z = z_ref[...]
    tri = tri_ref[...]
    dsk = dsk_ref[...]
    nw = nw_ref[...]
    li = lax.broadcasted_iota(jnp.int32, (SSD_CHUNK, SSD_CHUNK), 0)
    si = lax.broadcasted_iota(jnp.int32, (SSD_CHUNK, SSD_CHUNK), 1)
    causal = li >= si
    low_half = si < SSD_HEAD_DIM
    low_half_row = low_half[0:1, :]

    for ck in range(ts // SSD_CHUNK):
        r0 = ck * SSD_CHUNK
        a_c = a[r0:r0 + SSD_CHUNK]
        hi, mid, lo = _split3(a_c)
        acs = _dot(tri, hi) + _dot(tri, mid) + _dot(tri, lo)
        acs_t = acs.T
        dt_t = dt[r0:r0 + SSD_CHUNK].T
        a_last_b = jnp.broadcast_to(acs_t[:, SSD_CHUNK - 1:SSD_CHUNK], acs_t.shape)
        dtd_t = dt_t * jnp.exp(a_last_b - acs_t)
        chunk_decay = jnp.exp(acs[SSD_CHUNK - 1:SSD_CHUNK, :])
        for g in range(SSD_GROUPS):
            cg = cm[r0:r0 + SSD_CHUNK, g * SSD_STATE:(g + 1) * SSD_STATE]
            bg = bm[r0:r0 + SSD_CHUNK, g * SSD_STATE:(g + 1) * SSD_STATE]
            cb = _dot_nt(cg.astype(BF16), bg.astype(BF16))
            bgt = bg.T
            ys = []
            for pr in range(SSD_HPG // 2):
                pair = (g * SSD_HPG) // 2 + pr
                c0 = pair * LANES
                xs_p = xs[r0:r0 + SSD_CHUNK, c0:c0 + LANES].astype(BF16)
                prev = state_ref[pair]
                rhs = jnp.concatenate([xs_p, prev.astype(BF16)], axis=0)
                y_h, st_h, cd_h = [], [], []
                for hh in (2 * pair, 2 * pair + 1):
                    colb = jnp.broadcast_to(acs[:, hh:hh + 1], (SSD_CHUNK, SSD_CHUNK))
                    seg = colb - acs_t[hh:hh + 1, :]
                    decay = jnp.where(causal, jnp.exp(jnp.where(causal, seg, 0.0)), 0.0)
                    intra = cb * decay * dt_t[hh:hh + 1, :]
                    inter = cg * jnp.exp(colb)
                    lhs = jnp.concatenate([intra, inter], axis=1).astype(BF16)
                    y_h.append(_dot(lhs, rhs))
                    st_h.append(_dot((bgt * dtd_t[hh:hh + 1, :]).astype(BF16), xs_p))
                    cd_h.append(jnp.broadcast_to(chunk_decay[:, hh:hh + 1], (1, LANES)))
                ys.append(jnp.where(low_half, y_h[0], y_h[1]))
                state_ref[pair] = (prev * jnp.where(low_half_row, cd_h[0], cd_h[1])
                                   + jnp.where(low_half, st_h[0], st_h[1]))
            yg = jnp.concatenate(ys, axis=1)
            yg = yg + xs[r0:r0 + SSD_CHUNK, g * SSD_GW:(g + 1) * SSD_GW] * dsk[:, g * SSD_GW:(g + 1) * SSD_GW]
            u = yg * _silu(z[r0:r0 + SSD_CHUNK, g * SSD_GW:(g + 1) * SSD_GW])
            u = u * lax.rsqrt(jnp.mean(u * u, axis=-1, keepdims=True) + EPS)
            o_ref[r0:r0 + SSD_CHUNK, g * SSD_GW:(g + 1) * SSD_GW] = (
                u * nw[:, g * SSD_GW:(g + 1) * SSD_GW]).astype(o_ref.dtype)


def _pad_lanes(v, n=LANES):
    v = v.reshape(1, -1).astype(F32)
    return jnp.pad(v, ((0, 0), (0, n - v.shape[1])))


def _ssd_mixer(proj, bsz, seq, conv_w, conv_b, dt_bias, a_log, d_skip, norm_w, ts=256):
    nsteps = seq // ts
    row = lambda b, s: b * nsteps + s
    xw, bw, cw = conv_w[:, :SSD_WIDTH], conv_w[:, SSD_WIDTH:SSD_WIDTH + 512], conv_w[:, SSD_WIDTH + 512:]
    xb, bb, cb = conv_b[:SSD_WIDTH], conv_b[SSD_WIDTH:SSD_WIDTH + 512], conv_b[SSD_WIDTH + 512:]
    tri = jnp.asarray(np.tril(np.ones((SSD_CHUNK, SSD_CHUNK), np.float32)), BF16)
    const = lambda shape: pl.BlockSpec(shape, lambda b, s: (0,) * len(shape))
    kern = functools.partial(_ssd_kernel, ts=ts)
    return pl.pallas_call(
        kern,
        grid=(bsz, nsteps),
        in_specs=[pl.BlockSpec((ts, SSD_WIDTH), lambda b, s: (row(b, s), COL_Z // SSD_WIDTH)),
                  pl.BlockSpec((ts, SSD_WIDTH), lambda b, s: (row(b, s), COL_XS // SSD_WIDTH)),
                  pl.BlockSpec((ts, 512), lambda b, s: (row(b, s), COL_B // 512)),
                  pl.BlockSpec((ts, 512), lambda b, s: (row(b, s), COL_C // 512)),
                  pl.BlockSpec((ts, LANES), lambda b, s: (row(b, s), COL_DT // LANES)),
                  const((SSD_CONV, SSD_WIDTH)), const((SSD_CONV, 512)), const((SSD_CONV, 512)),
                  const((1, SSD_WIDTH)), const((1, 512)), const((1, 512)),
                  const((1, LANES)), const((1, LANES)), const((1, SSD_WIDTH)), const((1, SSD_WIDTH)),
                  const((SSD_CHUNK, SSD_CHUNK))],
        out_specs=pl.BlockSpec((ts, SSD_WIDTH), lambda b, s: (row(b, s), 0)),
        out_shape=jax.ShapeDtypeStruct((bsz * seq, SSD_WIDTH), BF16),
        scratch_shapes=[pltpu.VMEM((ts + SUBLANES, SSD_WIDTH), F32), pltpu.VMEM((ts + SUBLANES, 512), F32),
                        pltpu.VMEM((ts + SUBLANES, 512), F32),
                        pltpu.VMEM((SSD_HEADS // 2, SSD_STATE, 2 * SSD_HEAD_DIM), F32)],
        compiler_params=_params(("arbitrary", "arbitrary")),
        name="ssd_mixer",
    )(proj, proj, proj, proj, proj, xw, bw, cw, xb.reshape(1, -1), bb.reshape(1, -1), cb.reshape(1, -1),
      _pad_lanes(dt_bias), _pad_lanes(a_log), jnp.repeat(d_skip.astype(F32), SSD_HEAD_DIM).reshape(1, -1),
      norm_w.reshape(1, -1), tri)


def _hgrn_kernel(q_ref, f_ref, i_ref, g_ref, lbl_ref, nw_ref, cum_ref, o_ref, state_ref, *, ts, layer):
    s = pl.program_id(1)

    @pl.when(s == 0)
    def _():
        state_ref[...] = jnp.zeros_like(state_ref)

    nck = ts // HGRN_CHUNK
    lg = lbl_ref[...]
    e = jnp.exp(lg - jnp.max(lg, axis=0, keepdims=True))
    sm = e / jnp.sum(e, axis=0, keepdims=True)
    ridx = lax.broadcasted_iota(jnp.int32, lg.shape, 0)
    lb = jnp.sum(jnp.where((ridx >= 1) & (ridx <= layer), sm, 0.0), axis=0, keepdims=True)

    cum = cum_ref[...]
    nw = nw_ref[...]
    ti = lax.broadcasted_iota(jnp.int32, (ts, ts), 0)
    tj = lax.broadcasted_iota(jnp.int32, (ts, ts), 1)
    blockcausal = (ti // HGRN_CHUNK == tj // HGRN_CHUNK) & (ti >= tj)
    tok = lax.broadcasted_iota(jnp.int32, (1, ts), 1)

    def head(h):
        c0 = h * HGRN_DIM
        lbh = lb[:, c0:c0 + HGRN_DIM]
        f = lbh + (1.0 - lbh) * _sigmoid(f_ref[:, c0:c0 + HGRN_DIM])
        logf = jnp.log(f)
        k = 1.0 - f
        q = _silu(q_ref[:, c0:c0 + HGRN_DIM])
        v = i_ref[:, c0:c0 + HGRN_DIM]
        hi, mid, lo = _split3(logf)
        b = _dot(cum, hi) + _dot(cum, mid) + _dot(cum, lo)
        yield
        b3 = b.reshape(nck, HGRN_CHUNK, HGRN_DIM)
        bref = jnp.broadcast_to(b3[:, HGRN_CHUNK // 2:HGRN_CHUNK // 2 + 1, :], b3.shape).reshape(ts, HGRN_DIM)
        blast = jnp.broadcast_to(b3[:, HGRN_CHUNK - 1:HGRN_CHUNK, :], b3.shape).reshape(ts, HGRN_DIM)
        qe = (q * jnp.exp(b - bref)).astype(BF16)
        ke = (k * jnp.exp(bref - b)).astype(BF16)
        kl = (k * jnp.exp(blast - b)).astype(BF16)
        qb = (q * jnp.exp(b)).astype(BF16)
        vb = v.astype(BF16)
        att = _dot_nt(qe, ke)
        vt = v.T
        lhs = jnp.concatenate(
            [jnp.where(tok // HGRN_CHUNK == c, vt, 0.0) for c in range(nck)], axis=0).astype(BF16)
        st = _dot(lhs, kl)
        yield
        o = _dot(jnp.where(blockcausal, att, 0.0).astype(BF16), vb)
        state = state_ref[h]
        outs = []
        for c in range(nck):
            t0 = c * HGRN_CHUNK
            outs.append(_dot_nt(qb[t0:t0 + HGRN_CHUNK], state.astype(BF16)))
            cd = jnp.exp(blast[t0:t0 + 1, :])
            state = state * cd + st[c * HGRN_DIM:(c + 1) * HGRN_DIM]
            if c % 2 == 1:
                yield
        state_ref[h] = state
        o = o + jnp.concatenate(outs, axis=0)
        o = o * lax.rsqrt(jnp.mean(o * o, axis=-1, keepdims=True) + EPS) * nw
        o_ref[:, c0:c0 + HGRN_DIM] = (o * _silu(g_ref[:, c0:c0 + HGRN_DIM])).astype(o_ref.dtype)

    for h0 in range(0, HGRN_HEADS, HGRN_HEADS_INTERLEAVED):
        live = [head(h) for h in range(h0, h0 + HGRN_HEADS_INTERLEAVED)]
        while live:
            live = [gen for gen in live if next(gen, live) is not live]


def _hgrn_cum_matrix(ts):
    t = np.arange(ts)
    same = (t[:, None] // HGRN_CHUNK) == (t[None, :] // HGRN_CHUNK)
    return (same & (t[None, :] <= t[:, None])).astype(np.float32)


def _hgrn_mixer(proj, bsz, seq, lb_logits, norm_w, layer, ts=256):
    nsteps = seq // ts
    row = lambda b, s: b * nsteps + s
    depth = lb_logits.shape[0]
    cum = jnp.asarray(_hgrn_cum_matrix(ts), BF16)
    kern = functools.partial(_hgrn_kernel, ts=ts, layer=layer)
    blk = lambda col: pl.BlockSpec((ts, HGRN_WIDTH), lambda b, s: (row(b, s), col // HGRN_WIDTH))
    return pl.pallas_call(
        kern,
        grid=(bsz, nsteps),
        in_specs=[blk(COL_HQ), blk(COL_HF), blk(COL_HI), blk(COL_HG),
                  pl.BlockSpec((depth, HGRN_WIDTH), lambda b, s: (0, 0)),
                  pl.BlockSpec((1, HGRN_DIM), lambda b, s: (0, 0)),
                  pl.BlockSpec((ts, ts), lambda b, s: (0, 0))],
        out_specs=pl.BlockSpec((ts, HGRN_WIDTH), lambda b, s: (row(b, s), 0)),
        out_shape=jax.ShapeDtypeStruct((bsz * seq, HGRN_WIDTH), BF16),
        scratch_shapes=[pltpu.VMEM((HGRN_HEADS, HGRN_DIM, HGRN_DIM), F32)],
        compiler_params=_params(("arbitrary", "arbitrary")),
        name="hgrn2_mixer",
    )(proj, proj, proj, proj, lb_logits.astype(F32), norm_w.reshape(1, HGRN_DIM).astype(F32), cum)


def _rel_bucket_np(dist):
    max_exact = REL_BUCKETS // 2
    d = np.maximum(dist, 0)
    ratio = np.maximum(d, 1).astype(np.float32) / np.float32(max_exact)
    log_ratio = np.log(ratio).astype(np.float32) / np.float32(math.log(REL_MAX_DIST / max_exact))
    large = np.minimum(max_exact + (log_ratio * np.float32(REL_BUCKETS - max_exact)).astype(np.int32),
                       REL_BUCKETS - 1)
    return np.where(d < max_exact, d, large).astype(np.int32)


def _bias_expand_kernel(rb_ref, bmap_ref, o_ref):
    h = pl.program_id(0)
    bm = bmap_ref[...]
    out = jnp.full(bm.shape, NEG_INF, F32)
    for k in range(REL_BUCKETS):
        out = jnp.where(bm == k, rb_ref[k, h] * LOG2E, out)
    o_ref[...] = out


def _bias_expand(rel_bias, bmap, tr):
    rows, cols = bmap.shape
    return pl.pallas_call(
        _bias_expand_kernel,
        grid=(NSA_HEADS, rows // tr),
        in_specs=[pl.BlockSpec(memory_space=pltpu.SMEM),
                  pl.BlockSpec((tr, cols), lambda h, i: (i, 0))],
        out_specs=pl.BlockSpec((None, tr, cols), lambda h, i: (h, i, 0)),
        out_shape=jax.ShapeDtypeStruct((NSA_HEADS, rows, cols), F32),
        compiler_params=_params(("arbitrary", "arbitrary")),
        name="nsa_bias_expand",
    )(rel_bias.astype(F32), bmap)


def _nsa_bias_tables(rel_bias, seq):
    ncp = seq // CMP_STRIDE
    t = np.arange(seq)[:, None]
    cmp_end = np.arange(ncp)[None, :] * CMP_STRIDE + CMP_BLOCK - 1
    d = t - cmp_end
    bmap_cmp = np.where(d >= 0, _rel_bucket_np(d), -1).astype(np.int32)
    l = np.arange(Q_BLOCK)[:, None]
    j = np.arange(WIN_KEYS)[None, :]
    dist = l - j + WINDOW
    win = np.where((dist >= 0) & (dist < WINDOW), _rel_bucket_np(dist), -1)
    far = _rel_bucket_np(np.arange(Q_BLOCK + 1, 8 * seq))
    assert (far == far[0]).all(), "distances beyond one query block must share a single bucket"
    bmap_tab = np.concatenate([win, np.full((Q_BLOCK, Q_BLOCK), far[0])], axis=1).astype(np.int32)
    bias_cmp = _bias_expand(rel_bias, jnp.asarray(bmap_cmp), 512)
    bias_tab = _bias_expand(rel_bias, jnp.asarray(bmap_tab), Q_BLOCK)
    return bias_cmp, bias_tab


def _cmp_kernel(u_ref, pe_ref, w1_ref, w2_ref, o_ref, *, ncp):
    half = CMP_BLOCK // 2
    pe = pe_ref[...]
    top = jnp.zeros((ncp, CMP_HIDDEN), F32)
    bot = jnp.zeros((ncp, CMP_HIDDEN), F32)
    for l in range(half):
        x = u_ref[pl.ds(l, ncp, stride=half), :]
        top = top + _dot((x + pe[l:l + 1, :]).astype(BF16), w1_ref[l])
        bot = bot + _dot((x + pe[half + l:half + l + 1, :]).astype(BF16), w1_ref[half + l])
    hid = top + pltpu.roll(bot, ncp - 1, 0)
    o_ref[...] = _dot(_silu(hid).astype(BF16), w2_ref[...])


def _nsa_compress(proj, bsz, seq, pe, w1, w2):
    ncp = seq // CMP_STRIDE
    kern = functools.partial(_cmp_kernel, ncp=ncp)
    return pl.pallas_call(
        kern,
        grid=(bsz, 2, NSA_KV_HEADS),
        in_specs=[pl.BlockSpec((seq, NSA_HEAD_DIM), lambda b, t, h: (b, COL_KC // NSA_HEAD_DIM + NSA_KV_HEADS * t + h)),
                  pl.BlockSpec((None, CMP_BLOCK, NSA_HEAD_DIM), lambda b, t, h: (t, 0, 0)),
                  pl.BlockSpec((None, CMP_BLOCK, NSA_HEAD_DIM, CMP_HIDDEN), lambda b, t, h: (t, 0, 0, 0)),
                  pl.BlockSpec((None, CMP_HIDDEN, NSA_HEAD_DIM), lambda b, t, h: (t, 0, 0))],
        out_specs=pl.BlockSpec((None, None, None, ncp, NSA_HEAD_DIM), lambda b, t, h: (b, t, h, 0, 0)),
        out_shape=jax.ShapeDtypeStruct((bsz, 2, NSA_KV_HEADS, ncp, NSA_HEAD_DIM), F32),
        compiler_params=_params(("arbitrary", "arbitrary", "arbitrary")),
        name="nsa_compress",
    )(proj, pe.astype(F32), w1.reshape(2, CMP_BLOCK, NSA_HEAD_DIM, CMP_HIDDEN).astype(BF16), w2.astype(BF16))


def _softmax_start(s, v):
    m = jnp.max(s, axis=-1, keepdims=True)
    p = jnp.exp2(s - m)
    return m, jnp.sum(p, axis=-1, keepdims=True), _dot(p.astype(BF16), v)


def _softmax_update(s, v, carry):
    m, l, acc = carry
    m_new = jnp.maximum(m, jnp.max(s, axis=-1, keepdims=True))
    alpha = jnp.exp2(m - m_new)
    p = jnp.exp2(s - m_new)
    return m_new, alpha * l + jnp.sum(p, axis=-1, keepdims=True), alpha * acc + _dot(p.astype(BF16), v)


def _softmax_finish(carry):
    m, l, acc = carry
    return jnp.where(m > 0.5 * NEG_INF, acc / jnp.maximum(l, 1e-30), 0.0)


def _nsa_kernel(q_ref, gate_ref, kc_ref, vc_ref, ks_ref, vs_ref, kw_ref, vw_ref, bcmp_ref, btab_ref,
                kone_ref, c2s_ref, o_ref, kaug_ref, vsb_ref, kwb_ref, vwb_ref, sa_ref, sb_ref, *, n_sel, nqb):
    hk = pl.program_id(1)
    c0 = pl.program_id(2) * nqb
    rows = NSA_GQA * Q_BLOCK
    lane = lax.broadcasted_iota(jnp.int32, (Q_BLOCK, LANES), 1)

    @pl.when(c0 == 0)
    def _():
        kaug_ref[0:Q_BLOCK, 0:NSA_HEAD_DIM] = jnp.zeros((Q_BLOCK, NSA_HEAD_DIM), BF16)
        kaug_ref[Q_BLOCK:, 0:NSA_HEAD_DIM] = ks_ref[...].astype(BF16)
        kaug_ref[:, NSA_HEAD_DIM:] = kone_ref[...]
        vsb_ref[0:Q_BLOCK, :] = jnp.zeros((Q_BLOCK, NSA_HEAD_DIM), BF16)
        vsb_ref[Q_BLOCK:, :] = vs_ref[...].astype(BF16)
        kwb_ref[0:WINDOW, 0:NSA_HEAD_DIM] = jnp.zeros((WINDOW, NSA_HEAD_DIM), BF16)
        kwb_ref[WINDOW:, 0:NSA_HEAD_DIM] = kw_ref[...].astype(BF16)
        for r0 in range(0, WINDOW, Q_BLOCK):
            kwb_ref[r0:r0 + Q_BLOCK, NSA_HEAD_DIM:] = jnp.where(lane == AUG_PAD, 1.0, 0.0).astype(BF16)
        kwb_ref[WINDOW:, NSA_HEAD_DIM:] = jnp.zeros((kw_ref.shape[0], LANES), BF16)
        vwb_ref[0:WINDOW, :] = jnp.zeros((WINDOW, NSA_HEAD_DIM), BF16)
        vwb_ref[WINDOW:, :] = vw_ref[...].astype(BF16)

    gens = [_nsa_unit(c0 + u, q_ref[u * Q_BLOCK:(u + 1) * Q_BLOCK, :], bcmp_ref[:, u * Q_BLOCK:(u + 1) * Q_BLOCK, :],
                      btab_ref, kc_ref, vc_ref, c2s_ref, kaug_ref, vsb_ref, kwb_ref, vwb_ref, n_sel)
            for u in range(nqb)]
    units = [None] * nqb
    while any(un is None for un in units):
        for u, gen in enumerate(gens):
            if units[u] is None:
                units[u] = next(gen)
    qaug_far = jnp.concatenate([un[3] for un in units], axis=0)
    carry = tuple(jnp.concatenate([un[2][i] for un in units], axis=0) for i in range(3))

    far_blocks = FAR_KEYS // Q_BLOCK
    n_pairs = (c0 + nqb - 2 + 2 * far_blocks - 1) // (2 * far_blocks)
    last_slab = (kaug_ref.shape[0] - Q_BLOCK) // FAR_KEYS - 1

    def far_logits(slab):
        r0 = pl.multiple_of(jnp.minimum(slab, last_slab) * FAR_KEYS + Q_BLOCK, Q_BLOCK)
        return _dot_nt(qaug_far, kaug_ref[pl.ds(r0, FAR_KEYS), :])

    def far_values(slab):
        return vsb_ref[pl.ds(pl.multiple_of(slab * FAR_KEYS + Q_BLOCK, Q_BLOCK), FAR_KEYS), :]

    sa_ref[...] = far_logits(0)

    def far_body(k, carry):
        sb_ref[...] = far_logits(2 * k + 1)
        carry = _softmax_update(sa_ref[...], far_values(2 * k), carry)
        sa_ref[...] = far_logits(2 * k + 2)
        return _softmax_update(sb_ref[...], far_values(2 * k + 1), carry)

    o_sel = _softmax_finish(lax.fori_loop(0, n_pairs, far_body, carry))

    for u in range(nqb):
        o_cmp, o_win = units[u][0], units[u][1]
        gates = _sigmoid(gate_ref[u * Q_BLOCK:(u + 1) * Q_BLOCK, :])
        for g in range(NSA_GQA):
            def gate_col(br):
                return jnp.sum(jnp.where(lane == hk * (3 * NSA_GQA) + g * 3 + br, gates, 0.0), axis=-1, keepdims=True)

            r0 = g * Q_BLOCK
            out = (gate_col(0) * o_cmp[r0:r0 + Q_BLOCK] + gate_col(1) * o_sel[u * rows + r0:u * rows + r0 + Q_BLOCK]
                   + gate_col(2) * o_win[r0:r0 + Q_BLOCK])
            o_ref[u * Q_BLOCK:(u + 1) * Q_BLOCK, g * NSA_HEAD_DIM:(g + 1) * NSA_HEAD_DIM] = out.astype(o_ref.dtype)


def _nsa_unit(c, q_in, bc_in, btab_ref, kc_ref, vc_ref, c2s_ref, kaug_ref, vsb_ref, kwb_ref, vwb_ref, n_sel):
    rows = NSA_GQA * Q_BLOCK
    lane = lax.broadcasted_iota(jnp.int32, (Q_BLOCK, LANES), 1)
    q = q_in * (NSA_HEAD_DIM ** -0.5 * LOG2E)
    q3 = jnp.concatenate([q[:, g * NSA_HEAD_DIM:(g + 1) * NSA_HEAD_DIM] for g in range(NSA_GQA)],
                         axis=0).astype(BF16)

    btab = btab_ref[...].reshape(rows, WIN_KEYS + Q_BLOCK)
    near0 = pl.multiple_of(c * Q_BLOCK, Q_BLOCK)

    qaug_win = jnp.concatenate(
        [q3, jnp.concatenate([jnp.where(lane == AUG_PAD, NEG_INF, 0.0).astype(BF16)] * NSA_GQA, axis=0)], axis=1)
    s_win = _dot_nt(qaug_win, kwb_ref[pl.ds(near0, WIN_KEYS), :]) + btab[:, 0:WIN_KEYS]
    yield None

    bc = bc_in.reshape(rows, bc_in.shape[-1])
    valid = bc > 0.5 * NEG_INF
    s = _dot_nt(q3, kc_ref[...].astype(BF16)) + bc
    yield None
    o_win = _softmax_finish(_softmax_start(s_win, vwb_ref[pl.ds(near0, WIN_KEYS), :]))
    yield None
    m = jnp.max(s, axis=-1, keepdims=True)
    p = jnp.where(valid, jnp.exp2(s - m), 0.0)
    p = p / jnp.maximum(jnp.sum(p, axis=-1, keepdims=True), 1e-30)
    o_cmp = _dot(p.astype(BF16), vc_ref[...].astype(BF16))

    psum = p[0:Q_BLOCK] + p[Q_BLOCK:2 * Q_BLOCK] + p[2 * Q_BLOCK:3 * Q_BLOCK]
    ph = psum.astype(BF16)
    plo = (psum - ph.astype(F32)).astype(BF16)
    c2s = c2s_ref[...]
    imp_t = _dot_nt(c2s, ph) + _dot_nt(c2s, plo)
    nsp = -(-n_sel // SUBLANES) * SUBLANES
    imp_t = imp_t[0:nsp]
    tq = c * Q_BLOCK + lax.broadcasted_iota(jnp.int32, (nsp, Q_BLOCK), 1)
    jj = lax.broadcasted_iota(jnp.int32, (nsp, Q_BLOCK), 0)
    valid_sel = (jj < n_sel) & (jj * SEL_BLOCK <= tq)
    back = tq // SEL_BLOCK - jj
    force = (jj == 0) | ((back >= 0) & (back < SEL_LOCAL))
    score = jnp.where(valid_sel, imp_t + jnp.where(force, FORCE_BONUS, 0.0), NEG_INF)
    yield None
    n_tiles = nsp // SUBLANES
    tiles = [score[v * SUBLANES:(v + 1) * SUBLANES] for v in range(n_tiles)]
    sub = lax.broadcasted_iota(jnp.int32, (SUBLANES, Q_BLOCK), 0)
    cnts = [jnp.zeros((SUBLANES, Q_BLOCK), F32) for _ in range(n_tiles)]
    for jp in range(n_sel):
        v0, r0 = divmod(jp, SUBLANES)
        r = tiles[v0][r0:r0 + 1, :]
        for v in range(n_tiles):
            gt = jnp.where(r > tiles[v], 1.0, 0.0)
            ge = jnp.where(r >= tiles[v], 1.0, 0.0)
            cnts[v] = cnts[v] + (gt if v < v0 else ge if v > v0 else jnp.where(sub > r0, ge, gt))
        if jp % (2 * SUBLANES) == 2 * SUBLANES - 1:
            yield None
    cnt = jnp.concatenate(cnts, axis=0)
    keep = valid_sel & (cnt < float(min(SEL_TOPK, n_sel)))

    ext = lax.broadcasted_iota(jnp.int32, (LANES - nsp, Q_BLOCK), 0) + nsp
    aug = jnp.concatenate([jnp.where(keep, 0.0, NEG_INF), jnp.where(ext == AUG_PAD, NEG_INF, 0.0)], axis=0)
    aug_near = aug.T
    aug_far = jnp.where((lane >= 2 * (c - 1)) & (lane < AUG_BIAS_HI), NEG_INF, aug_near)
    far_parts = []
    for g in range(NSA_GQA):
        bfar = btab[g * Q_BLOCK:(g + 1) * Q_BLOCK, WIN_KEYS:]
        hi = bfar.astype(BF16).astype(F32)
        far_parts.append(jnp.where(lane == AUG_BIAS_HI, hi, jnp.where(lane == AUG_BIAS_LO, bfar - hi, aug_far)))
    qaug_near = jnp.concatenate([q3, jnp.concatenate([aug_near.astype(BF16)] * NSA_GQA, axis=0)], axis=1)
    qaug_far = jnp.concatenate([q3, jnp.concatenate(far_parts, axis=0).astype(BF16)], axis=1)

    s = _dot_nt(qaug_near, kaug_ref[pl.ds(near0, NEAR_KEYS), :]) + btab[:, WIN_KEYS - NEAR_KEYS:WIN_KEYS]
    yield None
    carry = _softmax_start(s, vsb_ref[pl.ds(near0, NEAR_KEYS), :])
    yield o_cmp, o_win, carry, qaug_far


def _nsa_attention(proj, cmp_kv, bias_cmp, bias_tab, bsz, seq):
    nq = seq // Q_BLOCK
    ncp = seq // CMP_STRIDE
    n_sel = seq // SEL_BLOCK
    assert n_sel <= AUG_BIAS_HI and NSA_HEAD_DIM == LANES
    assert seq % (2 * FAR_KEYS) == 0, "far steps walk the keys two 512-key slabs at a time"
    key = np.arange(-Q_BLOCK, seq)[:, None]
    ln = np.arange(LANES)[None, :]
    kone = np.where(key < 0, ln == AUG_PAD,
                    ((ln < AUG_BIAS_HI) & (key // SEL_BLOCK == ln)) | (ln == AUG_BIAS_HI) | (ln == AUG_BIAS_LO))
    kone = jnp.asarray(kone, BF16)
    c_start = np.arange(ncp)[None, :] * CMP_STRIDE
    s_start = np.arange(LANES)[:, None] * SEL_BLOCK
    overlap = np.clip(np.minimum(c_start + CMP_BLOCK, s_start + SEL_BLOCK) - np.maximum(c_start, s_start), 0, None)
    overlap = np.where(np.arange(LANES)[:, None] < n_sel, overlap, 0)
    c2s_t = jnp.asarray(overlap / CMP_BLOCK, BF16)
    gqa_w = NSA_GQA * NSA_HEAD_DIM
    full = lambda col: pl.BlockSpec((seq, NSA_HEAD_DIM), lambda b, h, c: (b, col // NSA_HEAD_DIM + h))
    nqb = NSA_Q_BLOCKS_PER_STEP
    ns = nq // nqb
    qr = nqb * Q_BLOCK
    kern = functools.partial(_nsa_kernel, n_sel=n_sel, nqb=nqb)
    return pl.pallas_call(
        kern,
        grid=(bsz, NSA_KV_HEADS, ns),
        in_specs=[pl.BlockSpec((qr, gqa_w), lambda b, h, c: (b * ns + c, COL_Q // gqa_w + h)),
                  pl.BlockSpec((qr, LANES), lambda b, h, c: (b * ns + c, COL_GATE // LANES)),
                  pl.BlockSpec((None, None, None, ncp, NSA_HEAD_DIM), lambda b, h, c: (b, 0, h, 0, 0)),
                  pl.BlockSpec((None, None, None, ncp, NSA_HEAD_DIM), lambda b, h, c: (b, 1, h, 0, 0)),
                  full(COL_KS), full(COL_VS), full(COL_KW), full(COL_VW),
                  pl.BlockSpec((NSA_GQA, qr, ncp), lambda b, h, c: (h, c, 0)),
                  pl.BlockSpec((NSA_GQA, Q_BLOCK, WIN_KEYS + Q_BLOCK), lambda b, h, c: (h, 0, 0)),
                  pl.BlockSpec((seq + Q_BLOCK, LANES), lambda b, h, c: (0, 0)),
                  pl.BlockSpec((LANES, ncp), lambda b, h, c: (0, 0))],
        out_specs=pl.BlockSpec((qr, gqa_w), lambda b, h, c: (b * ns + c, h)),
        out_shape=jax.ShapeDtypeStruct((bsz * seq, NSA_WIDTH), BF16),
        scratch_shapes=[pltpu.VMEM((seq + Q_BLOCK, 2 * NSA_HEAD_DIM), BF16),
                        pltpu.VMEM((seq + Q_BLOCK, NSA_HEAD_DIM), BF16),
                        pltpu.VMEM((seq + WINDOW, 2 * NSA_HEAD_DIM), BF16),
                        pltpu.VMEM((seq + WINDOW, NSA_HEAD_DIM), BF16),
                        pltpu.VMEM((nqb * NSA_GQA * Q_BLOCK, FAR_KEYS), F32),
                        pltpu.VMEM((nqb * NSA_GQA * Q_BLOCK, FAR_KEYS), F32)],
        compiler_params=_params(("arbitrary", "arbitrary", "arbitrary")),
        name="nsa_attention",
    )(proj, proj, cmp_kv, cmp_kv, proj, proj, proj, proj, bias_cmp, bias_tab, kone, c2s_t)


_IN_OFF = tuple(int(v) for v in np.cumsum((0,) + IN_SIZES))
_W_IN_COPIES = ((_IN_OFF[0], SSD_WIDTH, COL_Z), (_IN_OFF[1], SSD_WIDTH, COL_XS),
                (_IN_OFF[1] + SSD_WIDTH, 512, COL_B), (_IN_OFF[1] + SSD_WIDTH + 512, 512, COL_C),
                (_IN_OFF[3], NSA_WIDTH, COL_Q), (_IN_OFF[4], 6 * NSA_KV_WIDTH, COL_KC),
                (_IN_OFF[11], 4 * HGRN_WIDTH, COL_HQ))
_W_IN_NARROW = ((_IN_OFF[2], IN_SIZES[2], COL_DT, COL_GATE - COL_DT), (_IN_OFF[10], IN_SIZES[10], COL_GATE, COL_HQ - COL_GATE))


def _relayout_kernel(w_ref, o_ref):
    for src, width, dst in _W_IN_COPIES:
        o_ref[:, dst:dst + width] = w_ref[:, src:src + width].astype(BF16)
    for src, valid, dst, padded in _W_IN_NARROW:
        tile = w_ref[:, src:src + LANES]
        lane = lax.broadcasted_iota(jnp.int32, tile.shape, 1)
        o_ref[:, dst:dst + LANES] = jnp.where(lane < valid, tile, 0.0).astype(BF16)
        if padded > LANES:
            o_ref[:, dst + LANES:dst + padded] = jnp.zeros((tile.shape[0], padded - LANES), BF16)


def _relayout_w_in(w, tr=128):
    depth, d, n = w.shape
    return pl.pallas_call(
        _relayout_kernel,
        grid=(depth, d // tr),
        in_specs=[pl.BlockSpec((None, tr, n), lambda l, i: (l, i, 0))],
        out_specs=pl.BlockSpec((None, tr, IN_PAD), lambda l, i: (l, i, 0)),
        out_shape=jax.ShapeDtypeStruct((depth, d, IN_PAD), BF16),
        compiler_params=_params(("parallel", "parallel")),
        name="w_in_relayout",
    )(w)


def _tiles(seq):
    return dict(in_proj=(1024, 1024), out_proj=(1024, 512), gate_up=(min(1024, seq), 256), down=(256, 1024))


def kernel(x, norm_mix_w, w_in, ssd_conv_w, ssd_conv_b, ssd_dt_bias, ssd_a_log, ssd_d, ssd_norm_w, nsa_cmp_pe, nsa_cmp_w1, nsa_cmp_w2, rel_bias, hgrn_lb_logits, hgrn_norm_w, w_out, norm_ffn_w, ffn_w_gu, ffn_conv_w, ffn_conv_b, ffn_w_down, norm_f_w):
    bsz, seq, d = x.shape
    depth = w_in.shape[0]
    xr = x.reshape(bsz * seq, d).astype(F32)
    bias_cmp, bias_tab = _nsa_bias_tables(rel_bias, seq)
    tiles = _tiles(seq)
    w_in_b = _relayout_w_in(jnp.pad(w_in.astype(BF16), ((0, 0), (0, 0), (0, -w_in.shape[2] % LANES))))
    w_out_b = w_out.astype(BF16)
    w_gu_b = ffn_w_gu.astype(BF16)
    w_down_b = ffn_w_down.astype(BF16)
    for l in range(depth):
        n = _rmsnorm(xr, norm_mix_w[l], BF16)
        proj = _matmul(n, w_in_b, l, F32, *tiles["in_proj"], name="in_proj")
        y_ssd = _ssd_mixer(proj, bsz, seq, ssd_conv_w[l].astype(F32), ssd_conv_b[l].astype(F32), ssd_dt_bias[l],
                           ssd_a_log[l], ssd_d[l], ssd_norm_w[l].astype(F32))
        cmp_kv = _nsa_compress(proj, bsz, seq, nsa_cmp_pe[l], nsa_cmp_w1[l], nsa_cmp_w2[l])
        y_nsa = _nsa_attention(proj, cmp_kv, bias_cmp, bias_tab, bsz, seq)
        y_hgrn = _hgrn_mixer(proj, bsz, seq, hgrn_lb_logits, hgrn_norm_w[l], l)
        xr = _out_proj([y_ssd, y_nsa, y_hgrn], w_out_b, l, xr, *tiles["out_proj"])
        hn = _rmsnorm(xr, norm_ffn_w[l], BF16)
        act = _gate_up(hn, w_gu_b, l, ffn_conv_w[l].astype(F32), ffn_conv_b[l].astype(F32), seq, *tiles["gate_up"])
        xr = _matmul_residual_wres(act, w_down_b, l, xr, *tiles["down"], name="ffn_down")
    out = _rmsnorm(xr, norm_f_w, x.dtype)
    return out.reshape(bsz, seq, d)
```

```python
import functools
import math

import numpy as np
import jax
import jax.numpy as jnp
from jax import lax
from jax.experimental import pallas as pl
from jax.experimental.pallas import tpu as pltpu

F32 = jnp.float32
BF16 = jnp.bfloat16

D_MODEL = 4096
SSD_HEAD_DIM = 64
SSD_WIDTH = 1536
SSD_HEADS = 24
SSD_GROUPS = 4
SSD_HPG = 6
SSD_STATE = 128
SSD_CONV = 4
SSD_CHUNK = 128
SSD_GW = SSD_WIDTH // SSD_GROUPS
NSA_HEAD_DIM = 128
NSA_WIDTH = 1536
NSA_HEADS = 12
NSA_KV_HEADS = 4
NSA_GQA = 3
NSA_KV_WIDTH = 512
CMP_BLOCK = 32
CMP_STRIDE = 16
CMP_HIDDEN = 256
SEL_BLOCK = 64
SEL_TOPK = 16
SEL_LOCAL = 2
WINDOW = 512
Q_BLOCK = 128
FORCE_BONUS = 1e4
HGRN_WIDTH = 1024
HGRN_HEADS = 8
HGRN_DIM = 128
HGRN_CHUNK = 32
REL_BUCKETS = 32
REL_MAX_DIST = 128
D_FF = 11008
FFN_CONV = 3
EPS = 1e-6
NEG_INF = -1e30

IN_SIZES = (SSD_WIDTH, SSD_WIDTH + 2 * SSD_GROUPS * SSD_STATE, SSD_HEADS, NSA_WIDTH,
            NSA_KV_WIDTH, NSA_KV_WIDTH, NSA_KV_WIDTH, NSA_KV_WIDTH, NSA_KV_WIDTH, NSA_KV_WIDTH,
            3 * NSA_HEADS, HGRN_WIDTH, HGRN_WIDTH, HGRN_WIDTH, HGRN_WIDTH)

LANES = 128
SUBLANES = 8
VMEM_LIMIT = 56 * 1024 * 1024

COL_Z = 0
COL_XS = 1536
COL_Q = 3072
COL_B = 4608
COL_C = 5120
COL_KC = 5632
COL_VC = 6144
COL_KS = 6656
COL_VS = 7168
COL_KW = 7680
COL_VW = 8192
COL_DT = 8704
COL_GATE = 8832
COL_HQ = 9216
COL_HF = 10240
COL_HI = 11264
COL_HG = 12288
IN_PAD = 13312

LOG2E = math.log2(math.e)
AUG_BIAS_HI = 64
AUG_BIAS_LO = 65
AUG_PAD = 66
WIN_KEYS = WINDOW + Q_BLOCK
NEAR_KEYS = 2 * Q_BLOCK
FAR_KEYS = 4 * Q_BLOCK
HGRN_HEADS_INTERLEAVED = 8
NSA_Q_BLOCKS_PER_STEP = 2


def _params(semantics):
    return pltpu.CompilerParams(dimension_semantics=semantics, vmem_limit_bytes=VMEM_LIMIT)


def _sigmoid(x):
    return 1.0 / (1.0 + jnp.exp(-x))


def _silu(x):
    return x * _sigmoid(x)


def _dot(a, b):
    return jnp.dot(a, b, preferred_element_type=F32)


def _dot_nt(a, b):
    return lax.dot_general(a, b, (((1,), (1,)), ((), ())), preferred_element_type=F32)


def _split3(x):
    hi = x.astype(BF16)
    r1 = x - hi.astype(F32)
    mid = r1.astype(BF16)
    lo = (r1 - mid.astype(F32)).astype(BF16)
    return hi, mid, lo


def _rmsnorm_kernel(x_ref, w_ref, o_ref):
    x = x_ref[...]
    ms = jnp.mean(x * x, axis=-1, keepdims=True)
    o_ref[...] = (x * lax.rsqrt(ms + EPS) * w_ref[...]).astype(o_ref.dtype)


def _rmsnorm(x, w, out_dtype, tm=256):
    m, d = x.shape
    return pl.pallas_call(
        _rmsnorm_kernel,
        grid=(m // tm,),
        in_specs=[pl.BlockSpec((tm, d), lambda i: (i, 0)), pl.BlockSpec((1, d), lambda i: (0, 0))],
        out_specs=pl.BlockSpec((tm, d), lambda i: (i, 0)),
        out_shape=jax.ShapeDtypeStruct((m, d), out_dtype),
        compiler_params=_params(("parallel",)),
        name="rmsnorm",
    )(x, w.reshape(1, d).astype(F32))


def _lane_fold(sq):
    part = sq[:, 0:LANES]
    for c in range(1, sq.shape[1] // LANES):
        part = part + sq[:, c * LANES:(c + 1) * LANES]
    return part


def _rinv_lanes(ssq, d):
    part = ssq[0]
    for p in range(1, ssq.shape[0]):
        part = part + ssq[p]
    tot = jnp.sum(part, axis=-1, keepdims=True)
    return jnp.broadcast_to(lax.rsqrt(tot * (1.0 / d) + EPS), part.shape)


def _scale_rows(acc, rinv):
    return jnp.concatenate([acc[:, c * LANES:(c + 1) * LANES] * rinv for c in range(acc.shape[1] // LANES)], axis=1)


def _prenorm_kernel(x_ref, w_ref, xw_ref, ssq_ref):
    x = x_ref[...]
    xw_ref[...] = (x * w_ref[...]).astype(xw_ref.dtype)
    ssq_ref[0] = _lane_fold(x * x)


def _prenorm(x, w, tm=256):
    m, d = x.shape
    return pl.pallas_call(
        _prenorm_kernel,
        grid=(m // tm,),
        in_specs=[pl.BlockSpec((tm, d), lambda i: (i, 0)), pl.BlockSpec((1, d), lambda i: (0, 0))],
        out_specs=[pl.BlockSpec((tm, d), lambda i: (i, 0)), pl.BlockSpec((1, tm, LANES), lambda i: (0, i, 0))],
        out_shape=[jax.ShapeDtypeStruct((m, d), BF16), jax.ShapeDtypeStruct((1, m, LANES), F32)],
        compiler_params=_params(("parallel",)),
        name="prenorm",
    )(x, w.reshape(1, d).astype(F32))


MM_ROW_CHUNK = 128


def _row_chunks(rows):
    rc = min(MM_ROW_CHUNK, rows)
    return [slice(r, r + rc) for r in range(0, rows, rc)]


def _mm_kernel(a_ref, ssq_ref, w_ref, o_ref, rinv_ref, *, d):
    @pl.when(pl.program_id(1) == 0)
    def _():
        rinv_ref[...] = _rinv_lanes(ssq_ref[...], d)

    for rows in _row_chunks(o_ref.shape[0]):
        o_ref[rows, :] = _scale_rows(_dot(a_ref[rows, :], w_ref[...]), rinv_ref[rows, :]).astype(o_ref.dtype)


def _mm_res_kernel(a_ref, w_ref, r_ref, o_ref):
    for rows in _row_chunks(o_ref.shape[0]):
        o_ref[rows, :] = r_ref[rows, :] + _dot(a_ref[rows, :], w_ref[...])


def _matmul(xw, ssq, w, layer, out_dtype, tm, tn, name):
    m, k = xw.shape
    n = w.shape[2]
    return pl.pallas_call(
        functools.partial(_mm_kernel, d=k),
        grid=(m // tm, n // tn),
        in_specs=[pl.BlockSpec((tm, k), lambda i, j: (i, 0)),
                  pl.BlockSpec((ssq.shape[0], tm, LANES), lambda i, j: (0, i, 0)),
                  pl.BlockSpec((None, k, tn), lambda i, j: (layer, 0, j))],
        out_specs=pl.BlockSpec((tm, tn), lambda i, j: (i, j)),
        out_shape=jax.ShapeDtypeStruct((m, n), out_dtype),
        scratch_shapes=[pltpu.VMEM((tm, LANES), F32)],
        compiler_params=_params(("arbitrary", "arbitrary")),
        name=name,
    )(xw, ssq, w)


def _out_proj_kernel(*refs):
    *a_refs, w_ref, r_ref, nw_ref, o_ref, xw_ref, ssq_ref = refs
    j = pl.program_id(1)
    parts = []
    for rows in _row_chunks(o_ref.shape[0]):
        acc = r_ref[rows, :]
        k0 = 0
        for a_ref in a_refs:
            acc = acc + _dot(a_ref[rows, :], w_ref[k0:k0 + a_ref.shape[1], :])
            k0 += a_ref.shape[1]
        o_ref[rows, :] = acc
        xw_ref[rows, :] = (acc * nw_ref[...]).astype(xw_ref.dtype)
        parts.append(_lane_fold(acc * acc))
    part = jnp.concatenate(parts, axis=0)

    @pl.when(j == 0)
    def _():
        ssq_ref[0] = part

    @pl.when(j > 0)
    def _():
        ssq_ref[0] = ssq_ref[0] + part


def _out_proj(parts, w, layer, res, norm_w, tm, tn):
    m = res.shape[0]
    k, n = w.shape[1], w.shape[2]
    assert sum(p.shape[1] for p in parts) == k
    return pl.pallas_call(
        _out_proj_kernel,
        grid=(m // tm, n // tn),
        in_specs=[pl.BlockSpec((tm, p.shape[1]), lambda i, j: (i, 0)) for p in parts]
        + [pl.BlockSpec((None, k, tn), lambda i, j: (layer, 0, j)),
           pl.BlockSpec((tm, tn), lambda i, j: (i, j)),
           pl.BlockSpec((1, tn), lambda i, j: (0, j))],
        out_specs=[pl.BlockSpec((tm, tn), lambda i, j: (i, j)), pl.BlockSpec((tm, tn), lambda i, j: (i, j)),
                   pl.BlockSpec((1, tm, LANES), lambda i, j: (0, i, 0))],
        out_shape=[jax.ShapeDtypeStruct((m, n), F32), jax.ShapeDtypeStruct((m, n), BF16),
                   jax.ShapeDtypeStruct((1, m, LANES), F32)],
        compiler_params=_params(("arbitrary", "arbitrary")),
        name="out_proj",
    )(*parts, w, res, norm_w.reshape(1, n).astype(F32))


def _down_norm_kernel(a_ref, w_ref, r_ref, nw_ref, o_ref, xw_ref, ssq_ref):
    for rows in _row_chunks(o_ref.shape[0]):
        acc = r_ref[rows, :] + _dot(a_ref[rows, :], w_ref[...])
        o_ref[rows, :] = acc
        xw_ref[rows, :] = (acc * nw_ref[...]).astype(xw_ref.dtype)
        ssq_ref[rows, :] = _lane_fold(acc * acc)


def _down_proj_norm(a, w, layer, res, norm_w, tm, tn):
    m, k = a.shape
    n = w.shape[2]
    return pl.pallas_call(
        _down_norm_kernel,
        grid=(n // tn, m // tm),
        in_specs=[pl.BlockSpec((tm, k), lambda j, i: (i, 0)),
                  pl.BlockSpec((None, k, tn), lambda j, i: (layer, 0, j), pipeline_mode=pl.Buffered(1)),
                  pl.BlockSpec((tm, tn), lambda j, i: (i, j)),
                  pl.BlockSpec((1, tn), lambda j, i: (0, j))],
        out_specs=[pl.BlockSpec((tm, tn), lambda j, i: (i, j)), pl.BlockSpec((tm, tn), lambda j, i: (i, j)),
                   pl.BlockSpec((None, tm, LANES), lambda j, i: (j, i, 0))],
        out_shape=[jax.ShapeDtypeStruct((m, n), F32), jax.ShapeDtypeStruct((m, n), BF16),
                   jax.ShapeDtypeStruct((n // tn, m, LANES), F32)],
        compiler_params=_params(("arbitrary", "arbitrary")),
        name="ffn_down",
    )(a, w, res, norm_w.reshape(1, n).astype(F32))


def _matmul_residual_wres(a, w, layer, res, tm, tn, name):
    m, k = a.shape
    n = w.shape[2]
    return pl.pallas_call(
        _mm_res_kernel,
        grid=(n // tn, m // tm),
        in_specs=[pl.BlockSpec((tm, k), lambda j, i: (i, 0)),
                  pl.BlockSpec((None, k, tn), lambda j, i: (layer, 0, j), pipeline_mode=pl.Buffered(1)),
                  pl.BlockSpec((tm, tn), lambda j, i: (i, j))],
        out_specs=pl.BlockSpec((tm, tn), lambda j, i: (i, j)),
        out_shape=jax.ShapeDtypeStruct((m, n), F32),
        compiler_params=_params(("arbitrary", "arbitrary")),
        name=name,
    )(a, w, res)


GU_ROW_CHUNKS = 8


def _gu_kernel(h_ref, ssq_ref, wg_ref, wu_ref, cw_ref, cb_ref, o_ref, ga_ref, ua_ref, gb_ref, ub_ref, halo_ref,
               rinv_ref, *, tm, nj, tiles_per_seq):
    t = pl.program_id(0)
    tp = jnp.maximum(t - 1, 0)
    ip = tp // nj
    jp = tp % nj
    rc = tm // GU_ROW_CHUNKS

    @pl.when(t == 0)
    def _():
        gb_ref[...] = jnp.zeros_like(gb_ref)
        ub_ref[...] = jnp.zeros_like(ub_ref)
        halo_ref[...] = jnp.zeros_like(halo_ref)

    @pl.when(t % nj == 0)
    def _():
        rinv_ref[...] = _rinv_lanes(ssq_ref[...], h_ref.shape[1])

    def step(g_cur, u_cur, g_prev, u_prev):
        g_prev[0:SUBLANES, :] = jnp.where(ip % tiles_per_seq == 0, 0.0, halo_ref[jp])
        halo_ref[jp] = g_prev[tm:tm + SUBLANES, :]
        cw = cw_ref[...]
        for r in range(GU_ROW_CHUNKS):
            hr = h_ref[r * rc:(r + 1) * rc, :]
            rinv = rinv_ref[r * rc:(r + 1) * rc, :]
            g_cur[SUBLANES + r * rc:SUBLANES + (r + 1) * rc, :] = _scale_rows(_dot(hr, wg_ref[...]), rinv)
            u_cur[r * rc:(r + 1) * rc, :] = _scale_rows(_dot(hr, wu_ref[...]), rinv)
            acc = cb_ref[...]
            for sh in range(FFN_CONV):
                acc = acc + cw[FFN_CONV - 1 - sh:FFN_CONV - sh, :] * g_prev[pl.ds(SUBLANES - sh + r * rc, rc), :]
            o_ref[r * rc:(r + 1) * rc, :] = (_silu(acc) * u_prev[r * rc:(r + 1) * rc, :]).astype(o_ref.dtype)

    @pl.when(t % 2 == 0)
    def _():
        step(ga_ref, ua_ref, gb_ref, ub_ref)

    @pl.when(t % 2 == 1)
    def _():
        step(gb_ref, ub_ref, ga_ref, ua_ref)


def _gate_up(h, ssq, w_gu, layer, conv_w, conv_b, seq, tm, tn):
    m, k = h.shape
    nf = conv_w.shape[1]
    nj = nf // tn
    last = (m // tm) * nj - 1
    cur = lambda t: jnp.minimum(t, last)
    prev = lambda t: jnp.maximum(t - 1, 0)
    kern = functools.partial(_gu_kernel, tm=tm, nj=nj, tiles_per_seq=seq // tm)
    return pl.pallas_call(
        kern,
        grid=(last + 2,),
        in_specs=[pl.BlockSpec((tm, k), lambda t: (cur(t) // nj, 0)),
                  pl.BlockSpec((ssq.shape[0], tm, LANES), lambda t: (0, cur(t) // nj, 0)),
                  pl.BlockSpec((None, k, tn), lambda t: (layer, 0, cur(t) % nj)),
                  pl.BlockSpec((None, k, tn), lambda t: (layer, 0, cur(t) % nj + nj)),
                  pl.BlockSpec((FFN_CONV, tn), lambda t: (0, prev(t) % nj)),
                  pl.BlockSpec((1, tn), lambda t: (0, prev(t) % nj))],
        out_specs=pl.BlockSpec((tm, tn), lambda t: (prev(t) // nj, prev(t) % nj)),
        out_shape=jax.ShapeDtypeStruct((m, nf), BF16),
        scratch_shapes=[pltpu.VMEM((tm + SUBLANES, tn), F32), pltpu.VMEM((tm, tn), F32),
                        pltpu.VMEM((tm + SUBLANES, tn), F32), pltpu.VMEM((tm, tn), F32),
                        pltpu.VMEM((nj, SUBLANES, tn), F32), pltpu.VMEM((tm, LANES), F32)],
        compiler_params=_params(("arbitrary",)),
        name="ffn_gate_up_conv",
    )(h, ssq, w_gu, w_gu, conv_w, conv_b.reshape(1, nf))


def _ssd_kernel(z_ref, xs_ref, b_ref, c_ref, dt_ref, cwx_ref, cwb_ref, cwc_ref, cbx_ref, cbb_ref, cbc_ref,
                dtb_ref, alog_ref, dsk_ref, nw_ref, tri_ref, o_ref,
                extx_ref, extb_ref, extc_ref, state_ref, *, ts):
    s = pl.program_id(1)

    @pl.when(s == 0)
    def _():
        extx_ref[...] = jnp.zeros_like(extx_ref)
        extb_ref[...] = jnp.zeros_like(extb_ref)
        extc_ref[...] = jnp.zeros_like(extc_ref)
        state_ref[...] = jnp.zeros_like(state_ref)

    def conv_silu(u_ref, ext_ref, w_ref, bias_ref):
        ext_ref[0:SUBLANES, :] = ext_ref[ts:ts + SUBLANES, :]
        ext_ref[SUBLANES:, :] = u_ref[...]
        w = w_ref[...]
        acc = bias_ref[...] + w[SSD_CONV - 1:SSD_CONV, :] * u_ref[...]
        for sh in range(1, SSD_CONV):
            acc = acc + w[SSD_CONV - 1 - sh:SSD_CONV - sh, :] * ext_ref[pl.ds(SUBLANES - sh, ts), :]
        return _silu(acc)

    xs = conv_silu(xs_ref, extx_ref, cwx_ref, cbx_ref)
    bm = conv_silu(b_ref, extb_ref, cwb_ref, cbb_ref)
    cm = conv_silu(c_ref, extc_ref, cwc_ref, cbc_ref)
    dtr = dt_ref[...] + dtb_ref[...]
    dt = jnp.maximum(dtr, 0.0) + jnp.log1p(jnp.exp(-jnp.abs(dtr)))
    a = dt * (-jnp.exp(alog_ref[...]))
    z = z_ref[...]
    tri = tri_ref[...]
    dsk = dsk_ref[...]
    nw = nw_ref[...]
    li = lax.broadcasted_iota(jnp.int32, (SSD_CHUNK, SSD_CHUNK), 0)
    si = lax.broadcasted_iota(jnp.int32, (SSD_CHUNK, SSD_CHUNK), 1)
    causal = li >= si
    low_half = si < SSD_HEAD_DIM
    low_half_row = low_half[0:1, :]

    for ck in range(ts // SSD_CHUNK):
        r0 = ck * SSD_CHUNK
        a_c = a[r0:r0 + SSD_CHUNK]
        hi, mid, lo = _split3(a_c)
        acs = _dot(tri, hi) + _dot(tri, mid) + _dot(tri, lo)
        acs_t = acs.T
        dt_t = dt[r0:r0 + SSD_CHUNK].T
        a_last_b = jnp.broadcast_to(acs_t[:, SSD_CHUNK - 1:SSD_CHUNK], acs_t.shape)
        dtd_t = dt_t * jnp.exp(a_last_b - acs_t)
        chunk_decay = jnp.exp(acs[SSD_CHUNK - 1:SSD_CHUNK, :])
        for g in range(SSD_GROUPS):
            cg = cm[r0:r0 + SSD_CHUNK, g * SSD_STATE:(g + 1) * SSD_STATE]
            bg = bm[r0:r0 + SSD_CHUNK, g * SSD_STATE:(g + 1) * SSD_STATE]
            cb = _dot_nt(cg.astype(BF16), bg.astype(BF16))
            bgt = bg.T
            ys = []
            for pr in range(SSD_HPG // 2):
                pair = (g * SSD_HPG) // 2 + pr
                c0 = pair * LANES
                xs_p = xs[r0:r0 + SSD_CHUNK, c0:c0 + LANES].astype(BF16)
                prev = state_ref[pair]
                rhs = jnp.concatenate([xs_p, prev.astype(BF16)], axis=0)
                y_h, st_h, cd_h = [], [], []
                for hh in (2 * pair, 2 * pair + 1):
                    colb = jnp.broadcast_to(acs[:, hh:hh + 1], (SSD_CHUNK, SSD_CHUNK))
                    seg = colb - acs_t[hh:hh + 1, :]
                    decay = jnp.where(causal, jnp.exp(jnp.where(causal, seg, 0.0)), 0.0)
                    intra = cb * decay * dt_t[hh:hh + 1, :]
                    inter = cg * jnp.exp(colb)
                    lhs = jnp.concatenate([intra, inter], axis=1).astype(BF16)
                    y_h.append(_dot(lhs, rhs))
                    st_h.append(_dot((bgt * dtd_t[hh:hh + 1, :]).astype(BF16), xs_p))
                    cd_h.append(jnp.broadcast_to(chunk_decay[:, hh:hh + 1], (1, LANES)))
                ys.append(jnp.where(low_half, y_h[0], y_h[1]))
                state_ref[pair] = (prev * jnp.where(low_half_row, cd_h[0], cd_h[1])
                                   + jnp.where(low_half, st_h[0], st_h[1]))
            yg = jnp.concatenate(ys, axis=1)
            yg = yg + xs[r0:r0 + SSD_CHUNK, g * SSD_GW:(g + 1) * SSD_GW] * dsk[:, g * SSD_GW:(g + 1) * SSD_GW]
            u = yg * _silu(z[r0:r0 + SSD_CHUNK, g * SSD_GW:(g + 1) * SSD_GW])
            u = u * lax.rsqrt(jnp.mean(u * u, axis=-1, keepdims=True) + EPS)
            o_ref[r0:r0 + SSD_CHUNK, g * SSD_GW:(g + 1) * SSD_GW] = (
                u * nw[:, g * SSD_GW:(g + 1) * SSD_GW]).astype(o_ref.dtype)


def _pad_lanes(v, n=LANES):
    v = v.reshape(1, -1).astype(F32)
    return jnp.pad(v, ((0, 0), (0, n - v.shape[1])))


def _ssd_mixer(proj, bsz, seq, conv_w, conv_b, dt_bias, a_log, d_skip, norm_w, ts=256):
    nsteps = seq // ts
    row = lambda b, s: b * nsteps + s
    xw, bw, cw = conv_w[:, :SSD_WIDTH], conv_w[:, SSD_WIDTH:SSD_WIDTH + 512], conv_w[:, SSD_WIDTH + 512:]
    xb, bb, cb = conv_b[:SSD_WIDTH], conv_b[SSD_WIDTH:SSD_WIDTH + 512], conv_b[SSD_WIDTH + 512:]
    tri = jnp.asarray(np.tril(np.ones((SSD_CHUNK, SSD_CHUNK), np.float32)), BF16)
    const = lambda shape: pl.BlockSpec(shape, lambda b, s: (0,) * len(shape))
    kern = functools.partial(_ssd_kernel, ts=ts)
    return pl.pallas_call(
        kern,
        grid=(bsz, nsteps),
        in_specs=[pl.BlockSpec((ts, SSD_WIDTH), lambda b, s: (row(b, s), COL_Z // SSD_WIDTH)),
                  pl.BlockSpec((ts, SSD_WIDTH), lambda b, s: (row(b, s), COL_XS // SSD_WIDTH)),
                  pl.BlockSpec((ts, 512), lambda b, s: (row(b, s), COL_B // 512)),
                  pl.BlockSpec((ts, 512), lambda b, s: (row(b, s), COL_C // 512)),
                  pl.BlockSpec((ts, LANES), lambda b, s: (row(b, s), COL_DT // LANES)),
                  const((SSD_CONV, SSD_WIDTH)), const((SSD_CONV, 512)), const((SSD_CONV, 512)),
                  const((1, SSD_WIDTH)), const((1, 512)), const((1, 512)),
                  const((1, LANES)), const((1, LANES)), const((1, SSD_WIDTH)), const((1, SSD_WIDTH)),
                  const((SSD_CHUNK, SSD_CHUNK))],
        out_specs=pl.BlockSpec((ts, SSD_WIDTH), lambda b, s: (row(b, s), 0)),
        out_shape=jax.ShapeDtypeStruct((bsz * seq, SSD_WIDTH), BF16),
        scratch_shapes=[pltpu.VMEM((ts + SUBLANES, SSD_WIDTH), F32), pltpu.VMEM((ts + SUBLANES, 512), F32),
                        pltpu.VMEM((ts + SUBLANES, 512), F32),
                        pltpu.VMEM((SSD_HEADS // 2, SSD_STATE, 2 * SSD_HEAD_DIM), F32)],
        compiler_params=_params(("arbitrary", "arbitrary")),
        name="ssd_mixer",
    )(proj, proj, proj, proj, proj, xw, bw, cw, xb.reshape(1, -1), bb.reshape(1, -1), cb.reshape(1, -1),
      _pad_lanes(dt_bias), _pad_lanes(a_log), jnp.repeat(d_skip.astype(F32), SSD_HEAD_DIM).reshape(1, -1),
      norm_w.reshape(1, -1), tri)


def _hgrn_kernel(q_ref, f_ref, i_ref, g_ref, lbl_ref, nw_ref, cum_ref, o_ref, state_ref, *, ts, layer):
    s = pl.program_id(1)

    @pl.when(s == 0)
    def _():
        state_ref[...] = jnp.zeros_like(state_ref)

    nck = ts // HGRN_CHUNK
    lg = lbl_ref[...]
    e = jnp.exp(lg - jnp.max(lg, axis=0, keepdims=True))
    sm = e / jnp.sum(e, axis=0, keepdims=True)
    ridx = lax.broadcasted_iota(jnp.int32, lg.shape, 0)
    lb = jnp.sum(jnp.where((ridx >= 1) & (ridx <= layer), sm, 0.0), axis=0, keepdims=True)

    cum = cum_ref[...]
    nw = nw_ref[...]
    ti = lax.broadcasted_iota(jnp.int32, (ts, ts), 0)
    tj = lax.broadcasted_iota(jnp.int32, (ts, ts), 1)
    blockcausal = (ti // HGRN_CHUNK == tj // HGRN_CHUNK) & (ti >= tj)
    tok = lax.broadcasted_iota(jnp.int32, (1, ts), 1)

    def head(h):
        c0 = h * HGRN_DIM
        lbh = lb[:, c0:c0 + HGRN_DIM]
        f = lbh + (1.0 - lbh) * _sigmoid(f_ref[:, c0:c0 + HGRN_DIM])
        logf = jnp.log(f)
        k = 1.0 - f
        q = _silu(q_ref[:, c0:c0 + HGRN_DIM])
        v = i_ref[:, c0:c0 + HGRN_DIM]
        hi, mid, lo = _split3(logf)
        b = _dot(cum, hi) + _dot(cum, mid) + _dot(cum, lo)
        yield
        b3 = b.reshape(nck, HGRN_CHUNK, HGRN_DIM)
        bref = jnp.broadcast_to(b3[:, HGRN_CHUNK // 2:HGRN_CHUNK // 2 + 1, :], b3.shape).reshape(ts, HGRN_DIM)
        blast = jnp.broadcast_to(b3[:, HGRN_CHUNK - 1:HGRN_CHUNK, :], b3.shape).reshape(ts, HGRN_DIM)
        qe = (q * jnp.exp(b - bref)).astype(BF16)
        ke = (k * jnp.exp(bref - b)).astype(BF16)
        kl = (k * jnp.exp(blast - b)).astype(BF16)
        qb = (q * jnp.exp(b)).astype(BF16)
        vb = v.astype(BF16)
        att = _dot_nt(qe, ke)
        vt = v.T
        lhs = jnp.concatenate(
            [jnp.where(tok // HGRN_CHUNK == c, vt, 0.0) for c in range(nck)], axis=0).astype(BF16)
        st = _dot(lhs, kl)
        yield
        o = _dot(jnp.where(blockcausal, att, 0.0).astype(BF16), vb)
        state = state_ref[h]
        outs = []
        for c in range(nck):
            t0 = c * HGRN_CHUNK
            outs.append(_dot_nt(qb[t0:t0 + HGRN_CHUNK], state.astype(BF16)))
            cd = jnp.exp(blast[t0:t0 + 1, :])
            state = state * cd + st[c * HGRN_DIM:(c + 1) * HGRN_DIM]
            if c % 2 == 1:
                yield
        state_ref[h] = state
        o = o + jnp.concatenate(outs, axis=0)
        o = o * lax.rsqrt(jnp.mean(o * o, axis=-1, keepdims=True) + EPS) * nw
        o_ref[:, c0:c0 + HGRN_DIM] = (o * _silu(g_ref[:, c0:c0 + HGRN_DIM])).astype(o_ref.dtype)

    for h0 in range(0, HGRN_HEADS, HGRN_HEADS_INTERLEAVED):
        live = [head(h) for h in range(h0, h0 + HGRN_HEADS_INTERLEAVED)]
        while live:
            live = [gen for gen in live if next(gen, live) is not live]


def _hgrn_cum_matrix(ts):
    t = np.arange(ts)
    same = (t[:, None] // HGRN_CHUNK) == (t[None, :] // HGRN_CHUNK)
    return (same & (t[None, :] <= t[:, None])).astype(np.float32)


def _hgrn_mixer(proj, bsz, seq, lb_logits, norm_w, layer, ts=256):
    nsteps = seq // ts
    row = lambda b, s: b * nsteps + s
    depth = lb_logits.shape[0]
    cum = jnp.asarray(_hgrn_cum_matrix(ts), BF16)
    kern = functools.partial(_hgrn_kernel, ts=ts, layer=layer)
    blk = lambda col: pl.BlockSpec((ts, HGRN_WIDTH), lambda b, s: (row(b, s), col // HGRN_WIDTH))
    return pl.pallas_call(
        kern,
        grid=(bsz, nsteps),
        in_specs=[blk(COL_HQ), blk(COL_HF), blk(COL_HI), blk(COL_HG),
                  pl.BlockSpec((depth, HGRN_WIDTH), lambda b, s: (0, 0)),
                  pl.BlockSpec((1, HGRN_DIM), lambda b, s: (0, 0)),
                  pl.BlockSpec((ts, ts), lambda b, s: (0, 0))],
        out_specs=pl.BlockSpec((ts, HGRN_WIDTH), lambda b, s: (row(b, s), 0)),
        out_shape=jax.ShapeDtypeStruct((bsz * seq, HGRN_WIDTH), BF16),
        scratch_shapes=[pltpu.VMEM((HGRN_HEADS, HGRN_DIM, HGRN_DIM), F32)],
        compiler_params=_params(("arbitrary", "arbitrary")),
        name="hgrn2_mixer",
    )(proj, proj, proj, proj, lb_logits.astype(F32), norm_w.reshape(1, HGRN_DIM).astype(F32), cum)


def _rel_bucket_np(dist):
    max_exact = REL_BUCKETS // 2
    d = np.maximum(dist, 0)
    ratio = np.maximum(d, 1).astype(np.float32) / np.float32(max_exact)
    log_ratio = np.log(ratio).astype(np.float32) / np.float32(math.log(REL_MAX_DIST / max_exact))
    large = np.minimum(max_exact + (log_ratio * np.float32(REL_BUCKETS - max_exact)).astype(np.int32),
                       REL_BUCKETS - 1)
    return np.where(d < max_exact, d, large).astype(np.int32)


def _bias_expand_kernel(rb_ref, bmap_ref, o_ref):
    h = pl.program_id(0)
    bm = bmap_ref[...]
    out = jnp.full(bm.shape, NEG_INF, F32)
    for k in range(REL_BUCKETS):
        out = jnp.where(bm == k, rb_ref[k, h] * LOG2E, out)
    o_ref[...] = out


def _bias_expand(rel_bias, bmap, tr):
    rows, cols = bmap.shape
    return pl.pallas_call(
        _bias_expand_kernel,
        grid=(NSA_HEADS, rows // tr),
        in_specs=[pl.BlockSpec(memory_space=pltpu.SMEM),
                  pl.BlockSpec((tr, cols), lambda h, i: (i, 0))],
        out_specs=pl.BlockSpec((None, tr, cols), lambda h, i: (h, i, 0)),
        out_shape=jax.ShapeDtypeStruct((NSA_HEADS, rows, cols), F32),
        compiler_params=_params(("arbitrary", "arbitrary")),
        name="nsa_bias_expand",
    )(rel_bias.astype(F32), bmap)


def _nsa_bias_tables(rel_bias, seq):
    ncp = seq // CMP_STRIDE
    t = np.arange(seq)[:, None]
    cmp_end = np.arange(ncp)[None, :] * CMP_STRIDE + CMP_BLOCK - 1
    d = t - cmp_end
    bmap_cmp = np.where(d >= 0, _rel_bucket_np(d), -1).astype(np.int32)
    l = np.arange(Q_BLOCK)[:, None]
    j = np.arange(WIN_KEYS)[None, :]
    dist = l - j + WINDOW
    win = np.where((dist >= 0) & (dist < WINDOW), _rel_bucket_np(dist), -1)
    far = _rel_bucket_np(np.arange(Q_BLOCK + 1, 8 * seq))
    assert (far == far[0]).all(), "distances beyond one query block must share a single bucket"
    bmap_tab = np.concatenate([win, np.full((Q_BLOCK, Q_BLOCK), far[0])], axis=1).astype(np.int32)
    bias_cmp = _bias_expand(rel_bias, jnp.asarray(bmap_cmp), 512)
    bias_tab = _bias_expand(rel_bias, jnp.asarray(bmap_tab), Q_BLOCK)
    return bias_cmp, bias_tab


def _cmp_kernel(u_ref, pe_ref, w1_ref, w2_ref, o_ref, *, ncp):
    half = CMP_BLOCK // 2
    pe = pe_ref[...]
    top = jnp.zeros((ncp, CMP_HIDDEN), F32)
    bot = jnp.zeros((ncp, CMP_HIDDEN), F32)
    for l in range(half):
        x = u_ref[pl.ds(l, ncp, stride=half), :]
        top = top + _dot((x + pe[l:l + 1, :]).astype(BF16), w1_ref[l])
        bot = bot + _dot((x + pe[half + l:half + l + 1, :]).astype(BF16), w1_ref[half + l])
    hid = top + pltpu.roll(bot, ncp - 1, 0)
    o_ref[...] = _dot(_silu(hid).astype(BF16), w2_ref[...])


def _nsa_compress(proj, bsz, seq, pe, w1, w2):
    ncp = seq // CMP_STRIDE
    kern = functools.partial(_cmp_kernel, ncp=ncp)
    return pl.pallas_call(
        kern,
        grid=(bsz, 2, NSA_KV_HEADS),
        in_specs=[pl.BlockSpec((seq, NSA_HEAD_DIM), lambda b, t, h: (b, COL_KC // NSA_HEAD_DIM + NSA_KV_HEADS * t + h)),
                  pl.BlockSpec((None, CMP_BLOCK, NSA_HEAD_DIM), lambda b, t, h: (t, 0, 0)),
                  pl.BlockSpec((None, CMP_BLOCK, NSA_HEAD_DIM, CMP_HIDDEN), lambda b, t, h: (t, 0, 0, 0)),
                  pl.BlockSpec((None, CMP_HIDDEN, NSA_HEAD_DIM), lambda b, t, h: (t, 0, 0))],
        out_specs=pl.BlockSpec((None, None, None, ncp, NSA_HEAD_DIM), lambda b, t, h: (b, t, h, 0, 0)),
        out_shape=jax.ShapeDtypeStruct((bsz, 2, NSA_KV_HEADS, ncp, NSA_HEAD_DIM), F32),
        compiler_params=_params(("arbitrary", "arbitrary", "arbitrary")),
        name="nsa_compress",
    )(proj, pe.astype(F32), w1.reshape(2, CMP_BLOCK, NSA_HEAD_DIM, CMP_HIDDEN).astype(BF16), w2.astype(BF16))


def _softmax_start(s, v):
    m = jnp.max(s, axis=-1, keepdims=True)
    p = jnp.exp2(s - m)
    return m, jnp.sum(p, axis=-1, keepdims=True), _dot(p.astype(BF16), v)


def _softmax_update(s, v, carry):
    m, l, acc = carry
    m_new = jnp.maximum(m, jnp.max(s, axis=-1, keepdims=True))
    alpha = jnp.exp2(m - m_new)
    p = jnp.exp2(s - m_new)
    return m_new, alpha * l + jnp.sum(p, axis=-1, keepdims=True), alpha * acc + _dot(p.astype(BF16), v)


def _softmax_finish(carry):
    m, l, acc = carry
    return jnp.where(m > 0.5 * NEG_INF, acc / jnp.maximum(l, 1e-30), 0.0)


def _nsa_kernel(q_ref, gate_ref, kc_ref, vc_ref, ks_ref, vs_ref, kw_ref, vw_ref, bcmp_ref, btab_ref,
                kone_ref, c2s_ref, o_ref, kaug_ref, vsb_ref, kwb_ref, vwb_ref, sa_ref, sb_ref, *, n_sel, nqb):
    hk = pl.program_id(1)
    c0 = pl.program_id(2) * nqb
    rows = NSA_GQA * Q_BLOCK
    lane = lax.broadcasted_iota(jnp.int32, (Q_BLOCK, LANES), 1)

    @pl.when(c0 == 0)
    def _():
        kaug_ref[0:Q_BLOCK, 0:NSA_HEAD_DIM] = jnp.zeros((Q_BLOCK, NSA_HEAD_DIM), BF16)
        kaug_ref[Q_BLOCK:, 0:NSA_HEAD_DIM] = ks_ref[...].astype(BF16)
        kaug_ref[:, NSA_HEAD_DIM:] = kone_ref[...]
        vsb_ref[0:Q_BLOCK, :] = jnp.zeros((Q_BLOCK, NSA_HEAD_DIM), BF16)
        vsb_ref[Q_BLOCK:, :] = vs_ref[...].astype(BF16)
        kwb_ref[0:WINDOW, 0:NSA_HEAD_DIM] = jnp.zeros((WINDOW, NSA_HEAD_DIM), BF16)
        kwb_ref[WINDOW:, 0:NSA_HEAD_DIM] = kw_ref[...].astype(BF16)
        for r0 in range(0, WINDOW, Q_BLOCK):
            kwb_ref[r0:r0 + Q_BLOCK, NSA_HEAD_DIM:] = jnp.where(lane == AUG_PAD, 1.0, 0.0).astype(BF16)
        kwb_ref[WINDOW:, NSA_HEAD_DIM:] = jnp.zeros((kw_ref.shape[0], LANES), BF16)
        vwb_ref[0:WINDOW, :] = jnp.zeros((WINDOW, NSA_HEAD_DIM), BF16)
        vwb_ref[WINDOW:, :] = vw_ref[...].astype(BF16)

    gens = [_nsa_unit(c0 + u, q_ref[u * Q_BLOCK:(u + 1) * Q_BLOCK, :], bcmp_ref[:, u * Q_BLOCK:(u + 1) * Q_BLOCK, :],
                      btab_ref, kc_ref, vc_ref, c2s_ref, kaug_ref, vsb_ref, kwb_ref, vwb_ref, n_sel)
            for u in range(nqb)]
    units = [None] * nqb
    while any(un is None for un in units):
        for u, gen in enumerate(gens):
            if units[u] is None:
                units[u] = next(gen)
    qaug_far = jnp.concatenate([un[3] for un in units], axis=0)
    carry = tuple(jnp.concatenate([un[2][i] for un in units], axis=0) for i in range(3))

    far_blocks = FAR_KEYS // Q_BLOCK
    n_slabs = (c0 + nqb - 2 + far_blocks - 1) // far_blocks
    n_pairs = n_slabs // 2
    last_slab = (kaug_ref.shape[0] - Q_BLOCK) // FAR_KEYS - 1

    def far_logits(slab):
        r0 = pl.multiple_of(jnp.minimum(slab, last_slab) * FAR_KEYS + Q_BLOCK, Q_BLOCK)
        return _dot_nt(qaug_far, kaug_ref[pl.ds(r0, FAR_KEYS), :])

    def far_values(slab):
        return vsb_ref[pl.ds(pl.multiple_of(slab * FAR_KEYS + Q_BLOCK, Q_BLOCK), FAR_KEYS), :]

    sa_ref[...] = far_logits(0)

    def far_body(k, carry):
        sb_ref[...] = far_logits(2 * k + 1)
        carry = _softmax_update(sa_ref[...], far_values(2 * k), carry)
        sa_ref[...] = far_logits(2 * k + 2)
        return _softmax_update(sb_ref[...], far_values(2 * k + 1), carry)

    carry = lax.fori_loop(0, n_pairs, far_body, carry)
    carry = lax.cond(n_slabs % 2 == 1,
                     lambda cr: _softmax_update(sa_ref[...], far_values(2 * n_pairs), cr), lambda cr: cr, carry)
    o_sel = _softmax_finish(carry)

    for u in range(nqb):
        o_cmp, o_win = units[u][0], units[u][1]
        gates = _sigmoid(gate_ref[u * Q_BLOCK:(u + 1) * Q_BLOCK, :])
        for g in range(NSA_GQA):
            def gate_col(br):
                return jnp.sum(jnp.where(lane == hk * (3 * NSA_GQA) + g * 3 + br, gates, 0.0), axis=-1, keepdims=True)

            r0 = g * Q_BLOCK
            out = (gate_col(0) * o_cmp[r0:r0 + Q_BLOCK] + gate_col(1) * o_sel[u * rows + r0:u * rows + r0 + Q_BLOCK]
                   + gate_col(2) * o_win[r0:r0 + Q_BLOCK])
            o_ref[u * Q_BLOCK:(u + 1) * Q_BLOCK, g * NSA_HEAD_DIM:(g + 1) * NSA_HEAD_DIM] = out.astype(o_ref.dtype)


def _nsa_unit(c, q_in, bc_in, btab_ref, kc_ref, vc_ref, c2s_ref, kaug_ref, vsb_ref, kwb_ref, vwb_ref, n_sel):
    rows = NSA_GQA * Q_BLOCK
    lane = lax.broadcasted_iota(jnp.int32, (Q_BLOCK, LANES), 1)
    q = q_in * (NSA_HEAD_DIM ** -0.5 * LOG2E)
    q3 = jnp.concatenate([q[:, g * NSA_HEAD_DIM:(g + 1) * NSA_HEAD_DIM] for g in range(NSA_GQA)],
                         axis=0).astype(BF16)

    btab = btab_ref[...].reshape(rows, WIN_KEYS + Q_BLOCK)
    near0 = pl.multiple_of(c * Q_BLOCK, Q_BLOCK)

    qaug_win = jnp.concatenate(
        [q3, jnp.concatenate([jnp.where(lane == AUG_PAD, NEG_INF, 0.0).astype(BF16)] * NSA_GQA, axis=0)], axis=1)
    s_win = _dot_nt(qaug_win, kwb_ref[pl.ds(near0, WIN_KEYS), :]) + btab[:, 0:WIN_KEYS]
    yield None

    bc = bc_in.reshape(rows, bc_in.shape[-1])
    valid = bc > 0.5 * NEG_INF
    s = _dot_nt(q3, kc_ref[...].astype(BF16)) + bc
    yield None
    o_win = _softmax_finish(_softmax_start(s_win, vwb_ref[pl.ds(near0, WIN_KEYS), :]))
    yield None
    m = jnp.max(s, axis=-1, keepdims=True)
    p = jnp.where(valid, jnp.exp2(s - m), 0.0)
    p = p / jnp.maximum(jnp.sum(p, axis=-1, keepdims=True), 1e-30)
    o_cmp = _dot(p.astype(BF16), vc_ref[...].astype(BF16))

    psum = p[0:Q_BLOCK] + p[Q_BLOCK:2 * Q_BLOCK] + p[2 * Q_BLOCK:3 * Q_BLOCK]
    ph = psum.astype(BF16)
    plo = (psum - ph.astype(F32)).astype(BF16)
    c2s = c2s_ref[...]
    imp_t = _dot_nt(c2s, ph) + _dot_nt(c2s, plo)
    nsp = -(-n_sel // SUBLANES) * SUBLANES
    imp_t = imp_t[0:nsp]
    tq = c * Q_BLOCK + lax.broadcasted_iota(jnp.int32, (nsp, Q_BLOCK), 1)
    jj = lax.broadcasted_iota(jnp.int32, (nsp, Q_BLOCK), 0)
    valid_sel = (jj < n_sel) & (jj * SEL_BLOCK <= tq)
    back = tq // SEL_BLOCK - jj
    force = (jj == 0) | ((back >= 0) & (back < SEL_LOCAL))
    score = jnp.where(valid_sel, imp_t + jnp.where(force, FORCE_BONUS, 0.0), NEG_INF)
    yield None
    n_tiles = nsp // SUBLANES
    tiles = [score[v * SUBLANES:(v + 1) * SUBLANES] for v in range(n_tiles)]
    sub = lax.broadcasted_iota(jnp.int32, (SUBLANES, Q_BLOCK), 0)
    cnts = [jnp.zeros((SUBLANES, Q_BLOCK), F32) for _ in range(n_tiles)]
    for jp in range(n_sel):
        v0, r0 = divmod(jp, SUBLANES)
        r = tiles[v0][r0:r0 + 1, :]
        for v in range(n_tiles):
            gt = jnp.where(r > tiles[v], 1.0, 0.0)
            ge = jnp.where(r >= tiles[v], 1.0, 0.0)
            cnts[v] = cnts[v] + (gt if v < v0 else ge if v > v0 else jnp.where(sub > r0, ge, gt))
        if jp % (2 * SUBLANES) == 2 * SUBLANES - 1:
            yield None
    cnt = jnp.concatenate(cnts, axis=0)
    keep = valid_sel & (cnt < float(min(SEL_TOPK, n_sel)))

    ext = lax.broadcasted_iota(jnp.int32, (LANES - nsp, Q_BLOCK), 0) + nsp
    aug = jnp.concatenate([jnp.where(keep, 0.0, NEG_INF), jnp.where(ext == AUG_PAD, NEG_INF, 0.0)], axis=0)
    aug_near = aug.T
    aug_far = jnp.where((lane >= 2 * (c - 1)) & (lane < AUG_BIAS_HI), NEG_INF, aug_near)
    far_parts = []
    for g in range(NSA_GQA):
        bfar = btab[g * Q_BLOCK:(g + 1) * Q_BLOCK, WIN_KEYS:]
        hi = bfar.astype(BF16).astype(F32)
        far_parts.append(jnp.where(lane == AUG_BIAS_HI, hi, jnp.where(lane == AUG_BIAS_LO, bfar - hi, aug_far)))
    qaug_near = jnp.concatenate([q3, jnp.concatenate([aug_near.astype(BF16)] * NSA_GQA, axis=0)], axis=1)
    qaug_far = jnp.concatenate([q3, jnp.concatenate(far_parts, axis=0).astype(BF16)], axis=1)

    s = _dot_nt(qaug_near, kaug_ref[pl.ds(near0, NEAR_KEYS), :]) + btab[:, WIN_KEYS - NEAR_KEYS:WIN_KEYS]
    yield None
    carry = _softmax_start(s, vsb_ref[pl.ds(near0, NEAR_KEYS), :])
    yield o_cmp, o_win, carry, qaug_far


def _nsa_attention(proj, cmp_kv, bias_cmp, bias_tab, bsz, seq):
    nq = seq // Q_BLOCK
    ncp = seq // CMP_STRIDE
    n_sel = seq // SEL_BLOCK
    assert n_sel <= AUG_BIAS_HI and NSA_HEAD_DIM == LANES
    assert seq % (2 * FAR_KEYS) == 0, "far steps walk the keys two 512-key slabs at a time"
    key = np.arange(-Q_BLOCK, seq)[:, None]
    ln = np.arange(LANES)[None, :]
    kone = np.where(key < 0, ln == AUG_PAD,
                    ((ln < AUG_BIAS_HI) & (key // SEL_BLOCK == ln)) | (ln == AUG_BIAS_HI) | (ln == AUG_BIAS_LO))
    kone = jnp.asarray(kone, BF16)
    c_start = np.arange(ncp)[None, :] * CMP_STRIDE
    s_start = np.arange(LANES)[:, None] * SEL_BLOCK
    overlap = np.clip(np.minimum(c_start + CMP_BLOCK, s_start + SEL_BLOCK) - np.maximum(c_start, s_start), 0, None)
    overlap = np.where(np.arange(LANES)[:, None] < n_sel, overlap, 0)
    c2s_t = jnp.asarray(overlap / CMP_BLOCK, BF16)
    gqa_w = NSA_GQA * NSA_HEAD_DIM
    full = lambda col: pl.BlockSpec((seq, NSA_HEAD_DIM), lambda b, h, c: (b, col // NSA_HEAD_DIM + h))
    nqb = NSA_Q_BLOCKS_PER_STEP
    ns = nq // nqb
    qr = nqb * Q_BLOCK
    kern = functools.partial(_nsa_kernel, n_sel=n_sel, nqb=nqb)
    return pl.pallas_call(
        kern,
        grid=(bsz, NSA_KV_HEADS, ns),
        in_specs=[pl.BlockSpec((qr, gqa_w), lambda b, h, c: (b * ns + c, COL_Q // gqa_w + h)),
                  pl.BlockSpec((qr, LANES), lambda b, h, c: (b * ns + c, COL_GATE // LANES)),
                  pl.BlockSpec((None, None, None, ncp, NSA_HEAD_DIM), lambda b, h, c: (b, 0, h, 0, 0)),
                  pl.BlockSpec((None, None, None, ncp, NSA_HEAD_DIM), lambda b, h, c: (b, 1, h, 0, 0)),
                  full(COL_KS), full(COL_VS), full(COL_KW), full(COL_VW),
                  pl.BlockSpec((NSA_GQA, qr, ncp), lambda b, h, c: (h, c, 0)),
                  pl.BlockSpec((NSA_GQA, Q_BLOCK, WIN_KEYS + Q_BLOCK), lambda b, h, c: (h, 0, 0)),
                  pl.BlockSpec((seq + Q_BLOCK, LANES), lambda b, h, c: (0, 0)),
                  pl.BlockSpec((LANES, ncp), lambda b, h, c: (0, 0))],
        out_specs=pl.BlockSpec((qr, gqa_w), lambda b, h, c: (b * ns + c, h)),
        out_shape=jax.ShapeDtypeStruct((bsz * seq, NSA_WIDTH), BF16),
        scratch_shapes=[pltpu.VMEM((seq + Q_BLOCK, 2 * NSA_HEAD_DIM), BF16),
                        pltpu.VMEM((seq + Q_BLOCK, NSA_HEAD_DIM), BF16),
                        pltpu.VMEM((seq + WINDOW, 2 * NSA_HEAD_DIM), BF16),
                        pltpu.VMEM((seq + WINDOW, NSA_HEAD_DIM), BF16),
                        pltpu.VMEM((nqb * NSA_GQA * Q_BLOCK, FAR_KEYS), F32),
                        pltpu.VMEM((nqb * NSA_GQA * Q_BLOCK, FAR_KEYS), F32)],
        compiler_params=_params(("arbitrary", "arbitrary", "arbitrary")),
        name="nsa_attention",
    )(proj, proj, cmp_kv, cmp_kv, proj, proj, proj, proj, bias_cmp, bias_tab, kone, c2s_t)


_IN_OFF = tuple(int(v) for v in np.cumsum((0,) + IN_SIZES))
_W_IN_COPIES = ((_IN_OFF[0], SSD_WIDTH, COL_Z), (_IN_OFF[1], SSD_WIDTH, COL_XS),
                (_IN_OFF[1] + SSD_WIDTH, 512, COL_B), (_IN_OFF[1] + SSD_WIDTH + 512, 512, COL_C),
                (_IN_OFF[3], NSA_WIDTH, COL_Q), (_IN_OFF[4], 6 * NSA_KV_WIDTH, COL_KC),
                (_IN_OFF[11], 4 * HGRN_WIDTH, COL_HQ))
_W_IN_NARROW = ((_IN_OFF[2], IN_SIZES[2], COL_DT, COL_GATE - COL_DT), (_IN_OFF[10], IN_SIZES[10], COL_GATE, COL_HQ - COL_GATE))


def _relayout_kernel(w_ref, o_ref):
    for src, width, dst in _W_IN_COPIES:
        o_ref[:, dst:dst + width] = w_ref[:, src:src + width].astype(BF16)
    for src, valid, dst, padded in _W_IN_NARROW:
        tile = w_ref[:, src:src + LANES]
        lane = lax.broadcasted_iota(jnp.int32, tile.shape, 1)
        o_ref[:, dst:dst + LANES] = jnp.where(lane < valid, tile, 0.0).astype(BF16)
        if padded > LANES:
            o_ref[:, dst + LANES:dst + padded] = jnp.zeros((tile.shape[0], padded - LANES), BF16)


def _relayout_w_in(w, tr=128):
    depth, d, n = w.shape
    return pl.pallas_call(
        _relayout_kernel,
        grid=(depth, d // tr),
        in_specs=[pl.BlockSpec((None, tr, n), lambda l, i: (l, i, 0))],
        out_specs=pl.BlockSpec((None, tr, IN_PAD), lambda l, i: (l, i, 0)),
        out_shape=jax.ShapeDtypeStruct((depth, d, IN_PAD), BF16),
        compiler_params=_params(("parallel", "parallel")),
        name="w_in_relayout",
    )(w)


def _tiles(seq):
    return dict(in_proj=(1024, 1024), out_proj=(1024, 512), gate_up=(min(1024, seq), 256), down=(256, 1024))


def kernel(x, norm_mix_w, w_in, ssd_conv_w, ssd_conv_b, ssd_dt_bias, ssd_a_log, ssd_d, ssd_norm_w, nsa_cmp_pe, nsa_cmp_w1, nsa_cmp_w2, rel_bias, hgrn_lb_logits, hgrn_norm_w, w_out, norm_ffn_w, ffn_w_gu, ffn_conv_w, ffn_conv_b, ffn_w_down, norm_f_w):
    bsz, seq, d = x.shape
    depth = w_in.shape[0]
    xr = x.reshape(bsz * seq, d).astype(F32)
    bias_cmp, bias_tab = _nsa_bias_tables(rel_bias, seq)
    tiles = _tiles(seq)
    w_in_b = _relayout_w_in(jnp.pad(w_in.astype(BF16), ((0, 0), (0, 0), (0, -w_in.shape[2] % LANES))))
    w_out_b = w_out.astype(BF16)
    w_gu_b = ffn_w_gu.astype(BF16)
    w_down_b = ffn_w_down.astype(BF16)
    xw, ssq = _prenorm(xr, norm_mix_w[0])
    for l in range(depth):
        proj = _matmul(xw, ssq, w_in_b, l, F32, *tiles["in_proj"], name="in_proj")
        y_ssd = _ssd_mixer(proj, bsz, seq, ssd_conv_w[l].astype(F32), ssd_conv_b[l].astype(F32), ssd_dt_bias[l],
                           ssd_a_log[l], ssd_d[l], ssd_norm_w[l].astype(F32))
        cmp_kv = _nsa_compress(proj, bsz, seq, nsa_cmp_pe[l], nsa_cmp_w1[l], nsa_cmp_w2[l])
        y_nsa = _nsa_attention(proj, cmp_kv, bias_cmp, bias_tab, bsz, seq)
        y_hgrn = _hgrn_mixer(proj, bsz, seq, hgrn_lb_logits, hgrn_norm_w[l], l)
        xr, xw, ssq = _out_proj([y_ssd, y_nsa, y_hgrn], w_out_b, l, xr, norm_ffn_w[l], *tiles["out_proj"])
        act = _gate_up(xw, ssq, w_gu_b, l, ffn_conv_w[l].astype(F32), ffn_conv_b[l].astype(F32), seq,
                       *tiles["gate_up"])
        if l + 1 < depth:
            xr, xw, ssq = _down_proj_norm(act, w_down_b, l, xr, norm_mix_w[l + 1], *tiles["down"])
        else:
            xr = _matmul_residual_wres(act, w_down_b, l, xr, *tiles["down"], name="ffn_down")
    out = _rmsnorm(xr, norm_f_w, x.dtype)
    return out.reshape(bsz, seq, d)
```

```python
import functools
import math

import numpy as np
import jax
import jax.numpy as jnp
from jax import lax
from jax.experimental import pallas as pl
from jax.experimental.pallas import tpu as pltpu

F32 = jnp.float32
BF16 = jnp.bfloat16

D_MODEL = 4096
SSD_HEAD_DIM = 64
SSD_WIDTH = 1536
SSD_HEADS = 24
SSD_GROUPS = 4
SSD_HPG = 6
SSD_STATE = 128
SSD_CONV = 4
SSD_CHUNK = 128
SSD_GW = SSD_WIDTH // SSD_GROUPS
NSA_HEAD_DIM = 128
NSA_WIDTH = 1536
NSA_HEADS = 12
NSA_KV_HEADS = 4
NSA_GQA = 3
NSA_KV_WIDTH = 512
CMP_BLOCK = 32
CMP_STRIDE = 16
CMP_HIDDEN = 256
SEL_BLOCK = 64
SEL_TOPK = 16
SEL_LOCAL = 2
WINDOW = 512
Q_BLOCK = 128
FORCE_BONUS = 1e4
HGRN_WIDTH = 1024
HGRN_HEADS = 8
HGRN_DIM = 128
HGRN_CHUNK = 32
REL_BUCKETS = 32
REL_MAX_DIST = 128
D_FF = 11008
FFN_CONV = 3
EPS = 1e-6
NEG_INF = -1e30

IN_SIZES = (SSD_WIDTH, SSD_WIDTH + 2 * SSD_GROUPS * SSD_STATE, SSD_HEADS, NSA_WIDTH,
            NSA_KV_WIDTH, NSA_KV_WIDTH, NSA_KV_WIDTH, NSA_KV_WIDTH, NSA_KV_WIDTH, NSA_KV_WIDTH,
            3 * NSA_HEADS, HGRN_WIDTH, HGRN_WIDTH, HGRN_WIDTH, HGRN_WIDTH)

LANES = 128
SUBLANES = 8
VMEM_LIMIT = 56 * 1024 * 1024

COL_Z = 0
COL_XS = 1536
COL_Q = 3072
COL_B = 4608
COL_C = 5120
COL_KC = 5632
COL_VC = 6144
COL_KS = 6656
COL_VS = 7168
COL_KW = 7680
COL_VW = 8192
COL_DT = 8704
COL_GATE = 8832
COL_HQ = 9216
COL_HF = 10240
COL_HI = 11264
COL_HG = 12288
IN_PAD = 13312

LOG2E = math.log2(math.e)
AUG_BIAS_HI = 64
AUG_BIAS_LO = 65
AUG_PAD = 66
WIN_KEYS = WINDOW + Q_BLOCK
NEAR_KEYS = 2 * Q_BLOCK
FAR_KEYS = 4 * Q_BLOCK
HGRN_HEADS_INTERLEAVED = 8
NSA_Q_BLOCKS_PER_STEP = 2


def _params(semantics):
    return pltpu.CompilerParams(dimension_semantics=semantics, vmem_limit_bytes=VMEM_LIMIT)


def _sigmoid(x):
    return 1.0 / (1.0 + jnp.exp(-x))


def _silu(x):
    return x * _sigmoid(x)


def _dot(a, b):
    return jnp.dot(a, b, preferred_element_type=F32)


def _dot_nt(a, b):
    return lax.dot_general(a, b, (((1,), (1,)), ((), ())), preferred_element_type=F32)


def _split3(x):
    hi = x.astype(BF16)
    r1 = x - hi.astype(F32)
    mid = r1.astype(BF16)
    lo = (r1 - mid.astype(F32)).astype(BF16)
    return hi, mid, lo


def _rmsnorm_kernel(x_ref, w_ref, o_ref):
    x = x_ref[...]
    ms = jnp.mean(x * x, axis=-1, keepdims=True)
    o_ref[...] = (x * lax.rsqrt(ms + EPS) * w_ref[...]).astype(o_ref.dtype)


def _rmsnorm(x, w, out_dtype, tm=256):
    m, d = x.shape
    return pl.pallas_call(
        _rmsnorm_kernel,
        grid=(m // tm,),
        in_specs=[pl.BlockSpec((tm, d), lambda i: (i, 0)), pl.BlockSpec((1, d), lambda i: (0, 0))],
        out_specs=pl.BlockSpec((tm, d), lambda i: (i, 0)),
        out_shape=jax.ShapeDtypeStruct((m, d), out_dtype),
        compiler_params=_params(("parallel",)),
        name="rmsnorm",
    )(x, w.reshape(1, d).astype(F32))


def _lane_fold(sq):
    part = sq[:, 0:LANES]
    for c in range(1, sq.shape[1] // LANES):
        part = part + sq[:, c * LANES:(c + 1) * LANES]
    return part


def _rinv_lanes(ssq, d):
    part = ssq[0]
    for p in range(1, ssq.shape[0]):
        part = part + ssq[p]
    tot = jnp.sum(part, axis=-1, keepdims=True)
    return jnp.broadcast_to(lax.rsqrt(tot * (1.0 / d) + EPS), part.shape)


def _scale_rows(acc, rinv):
    return jnp.concatenate([acc[:, c * LANES:(c + 1) * LANES] * rinv for c in range(acc.shape[1] // LANES)], axis=1)


def _prenorm_kernel(x_ref, w_ref, xw_ref, ssq_ref):
    x = x_ref[...]
    xw_ref[...] = (x * w_ref[...]).astype(xw_ref.dtype)
    ssq_ref[0] = _lane_fold(x * x)


def _prenorm(x, w, tm=256):
    m, d = x.shape
    return pl.pallas_call(
        _prenorm_kernel,
        grid=(m // tm,),
        in_specs=[pl.BlockSpec((tm, d), lambda i: (i, 0)), pl.BlockSpec((1, d), lambda i: (0, 0))],
        out_specs=[pl.BlockSpec((tm, d), lambda i: (i, 0)), pl.BlockSpec((1, tm, LANES), lambda i: (0, i, 0))],
        out_shape=[jax.ShapeDtypeStruct((m, d), BF16), jax.ShapeDtypeStruct((1, m, LANES), F32)],
        compiler_params=_params(("parallel",)),
        name="prenorm",
    )(x, w.reshape(1, d).astype(F32))


MM_ROW_CHUNK = 128


def _row_chunks(rows):
    rc = min(MM_ROW_CHUNK, rows)
    return [slice(r, r + rc) for r in range(0, rows, rc)]


def _mm_kernel(a_ref, ssq_ref, w_ref, o_ref, rinv_ref, *, d):
    @pl.when(pl.program_id(1) == 0)
    def _():
        rinv_ref[...] = _rinv_lanes(ssq_ref[...], d)

    for rows in _row_chunks(o_ref.shape[0]):
        o_ref[rows, :] = _scale_rows(_dot(a_ref[rows, :], w_ref[...]), rinv_ref[rows, :]).astype(o_ref.dtype)


def _mm_res_kernel(a_ref, w_ref, r_ref, o_ref):
    for rows in _row_chunks(o_ref.shape[0]):
        o_ref[rows, :] = r_ref[rows, :] + _dot(a_ref[rows, :], w_ref[...])


def _matmul(xw, ssq, w, layer, out_dtype, tm, tn, name):
    m, k = xw.shape
    n = w.shape[2]
    return pl.pallas_call(
        functools.partial(_mm_kernel, d=k),
        grid=(m // tm, n // tn),
        in_specs=[pl.BlockSpec((tm, k), lambda i, j: (i, 0)),
                  pl.BlockSpec((ssq.shape[0], tm, LANES), lambda i, j: (0, i, 0)),
                  pl.BlockSpec((None, k, tn), lambda i, j: (layer, 0, j))],
        out_specs=pl.BlockSpec((tm, tn), lambda i, j: (i, j)),
        out_shape=jax.ShapeDtypeStruct((m, n), out_dtype),
        scratch_shapes=[pltpu.VMEM((tm, LANES), F32)],
        compiler_params=_params(("arbitrary", "arbitrary")),
        name=name,
    )(xw, ssq, w)


def _out_proj_kernel(*refs):
    *a_refs, w_ref, r_ref, nw_ref, o_ref, xw_ref, ssq_ref = refs
    j = pl.program_id(1)
    parts = []
    for rows in _row_chunks(o_ref.shape[0]):
        acc = r_ref[rows, :]
        k0 = 0
        for a_ref in a_refs:
            acc = acc + _dot(a_ref[rows, :], w_ref[k0:k0 + a_ref.shape[1], :])
            k0 += a_ref.shape[1]
        o_ref[rows, :] = acc
        xw_ref[rows, :] = (acc * nw_ref[...]).astype(xw_ref.dtype)
        parts.append(_lane_fold(acc * acc))
    part = jnp.concatenate(parts, axis=0)

    @pl.when(j == 0)
    def _():
        ssq_ref[0] = part

    @pl.when(j > 0)
    def _():
        ssq_ref[0] = ssq_ref[0] + part


def _out_proj(parts, w, layer, res, norm_w, tm, tn):
    m = res.shape[0]
    k, n = w.shape[1], w.shape[2]
    assert sum(p.shape[1] for p in parts) == k
    return pl.pallas_call(
        _out_proj_kernel,
        grid=(m // tm, n // tn),
        in_specs=[pl.BlockSpec((tm, p.shape[1]), lambda i, j: (i, 0)) for p in parts]
        + [pl.BlockSpec((None, k, tn), lambda i, j: (layer, 0, j)),
           pl.BlockSpec((tm, tn), lambda i, j: (i, j)),
           pl.BlockSpec((1, tn), lambda i, j: (0, j))],
        out_specs=[pl.BlockSpec((tm, tn), lambda i, j: (i, j)), pl.BlockSpec((tm, tn), lambda i, j: (i, j)),
                   pl.BlockSpec((1, tm, LANES), lambda i, j: (0, i, 0))],
        out_shape=[jax.ShapeDtypeStruct((m, n), F32), jax.ShapeDtypeStruct((m, n), BF16),
                   jax.ShapeDtypeStruct((1, m, LANES), F32)],
        compiler_params=_params(("arbitrary", "arbitrary")),
        name="out_proj",
    )(*parts, w, res, norm_w.reshape(1, n).astype(F32))


def _down_norm_kernel(a_ref, w_ref, r_ref, nw_ref, o_ref, xw_ref, ssq_ref):
    for rows in _row_chunks(o_ref.shape[0]):
        acc = r_ref[rows, :] + _dot(a_ref[rows, :], w_ref[...])
        o_ref[rows, :] = acc
        xw_ref[rows, :] = (acc * nw_ref[...]).astype(xw_ref.dtype)
        ssq_ref[rows, :] = _lane_fold(acc * acc)


def _down_proj_norm(a, w, layer, res, norm_w, tm, tn):
    m, k = a.shape
    n = w.shape[2]
    return pl.pallas_call(
        _down_norm_kernel,
        grid=(n // tn, m // tm),
        in_specs=[pl.BlockSpec((tm, k), lambda j, i: (i, 0)),
                  pl.BlockSpec((None, k, tn), lambda j, i: (layer, 0, j), pipeline_mode=pl.Buffered(1)),
                  pl.BlockSpec((tm, tn), lambda j, i: (i, j)),
                  pl.BlockSpec((1, tn), lambda j, i: (0, j))],
        out_specs=[pl.BlockSpec((tm, tn), lambda j, i: (i, j)), pl.BlockSpec((tm, tn), lambda j, i: (i, j)),
                   pl.BlockSpec((None, tm, LANES), lambda j, i: (j, i, 0))],
        out_shape=[jax.ShapeDtypeStruct((m, n), F32), jax.ShapeDtypeStruct((m, n), BF16),
                   jax.ShapeDtypeStruct((n // tn, m, LANES), F32)],
        compiler_params=_params(("arbitrary", "arbitrary")),
        name="ffn_down",
    )(a, w, res, norm_w.reshape(1, n).astype(F32))


def _matmul_residual_wres(a, w, layer, res, tm, tn, name):
    m, k = a.shape
    n = w.shape[2]
    return pl.pallas_call(
        _mm_res_kernel,
        grid=(n // tn, m // tm),
        in_specs=[pl.BlockSpec((tm, k), lambda j, i: (i, 0)),
                  pl.BlockSpec((None, k, tn), lambda j, i: (layer, 0, j), pipeline_mode=pl.Buffered(1)),
                  pl.BlockSpec((tm, tn), lambda j, i: (i, j))],
        out_specs=pl.BlockSpec((tm, tn), lambda j, i: (i, j)),
        out_shape=jax.ShapeDtypeStruct((m, n), F32),
        compiler_params=_params(("arbitrary", "arbitrary")),
        name=name,
    )(a, w, res)


def _gu_kernel(h_ref, ssq_ref, wg_ref, wu_ref, cw_ref, cb_ref, o_ref, ga_ref, ua_ref, gb_ref, ub_ref, halo_ref,
               rinv_ref, *, tm, nj, tiles_per_seq):
    t = pl.program_id(0)
    tp = jnp.maximum(t - 1, 0)
    ip = tp // nj
    jp = tp % nj
    rc = min(MM_ROW_CHUNK, tm)

    @pl.when(t == 0)
    def _():
        gb_ref[...] = jnp.zeros_like(gb_ref)
        ub_ref[...] = jnp.zeros_like(ub_ref)
        halo_ref[...] = jnp.zeros_like(halo_ref)

    @pl.when(t % nj == 0)
    def _():
        rinv_ref[...] = _rinv_lanes(ssq_ref[...], h_ref.shape[1])

    def step(g_cur, u_cur, g_prev, u_prev):
        g_prev[0:SUBLANES, :] = jnp.where(ip % tiles_per_seq == 0, 0.0, halo_ref[jp])
        halo_ref[jp] = g_prev[tm:tm + SUBLANES, :]
        cw = cw_ref[...]
        for r in range(tm // rc):
            hr = h_ref[r * rc:(r + 1) * rc, :]
            rinv = rinv_ref[r * rc:(r + 1) * rc, :]
            g_cur[SUBLANES + r * rc:SUBLANES + (r + 1) * rc, :] = _scale_rows(_dot(hr, wg_ref[...]), rinv)
            u_cur[r * rc:(r + 1) * rc, :] = _scale_rows(_dot(hr, wu_ref[...]), rinv)
            acc = cb_ref[...]
            for sh in range(FFN_CONV):
                acc = acc + cw[FFN_CONV - 1 - sh:FFN_CONV - sh, :] * g_prev[pl.ds(SUBLANES - sh + r * rc, rc), :]
            o_ref[r * rc:(r + 1) * rc, :] = (_silu(acc) * u_prev[r * rc:(r + 1) * rc, :]).astype(o_ref.dtype)

    @pl.when(t % 2 == 0)
    def _():
        step(ga_ref, ua_ref, gb_ref, ub_ref)

    @pl.when(t % 2 == 1)
    def _():
        step(gb_ref, ub_ref, ga_ref, ua_ref)


def _gate_up(h, ssq, w_gu, layer, conv_w, conv_b, seq, tm, tn):
    m, k = h.shape
    nf = conv_w.shape[1]
    nj = nf // tn
    last = (m // tm) * nj - 1
    cur = lambda t: jnp.minimum(t, last)
    prev = lambda t: jnp.maximum(t - 1, 0)
    kern = functools.partial(_gu_kernel, tm=tm, nj=nj, tiles_per_seq=seq // tm)
    return pl.pallas_call(
        kern,
        grid=(last + 2,),
        in_specs=[pl.BlockSpec((tm, k), lambda t: (cur(t) // nj, 0), pipeline_mode=pl.Buffered(1)),
                  pl.BlockSpec((ssq.shape[0], tm, LANES), lambda t: (0, cur(t) // nj, 0)),
                  pl.BlockSpec((None, k, tn), lambda t: (layer, 0, cur(t) % nj)),
                  pl.BlockSpec((None, k, tn), lambda t: (layer, 0, cur(t) % nj + nj)),
                  pl.BlockSpec((FFN_CONV, tn), lambda t: (0, prev(t) % nj)),
                  pl.BlockSpec((1, tn), lambda t: (0, prev(t) % nj))],
        out_specs=pl.BlockSpec((tm, tn), lambda t: (prev(t) // nj, prev(t) % nj)),
        out_shape=jax.ShapeDtypeStruct((m, nf), BF16),
        scratch_shapes=[pltpu.VMEM((tm + SUBLANES, tn), F32), pltpu.VMEM((tm, tn), F32),
                        pltpu.VMEM((tm + SUBLANES, tn), F32), pltpu.VMEM((tm, tn), F32),
                        pltpu.VMEM((nj, SUBLANES, tn), F32), pltpu.VMEM((tm, LANES), F32)],
        compiler_params=_params(("arbitrary",)),
        name="ffn_gate_up_conv",
    )(h, ssq, w_gu, w_gu, conv_w, conv_b.reshape(1, nf))


def _ssd_kernel(z_ref, xs_ref, b_ref, c_ref, dt_ref, cwx_ref, cwb_ref, cwc_ref, cbx_ref, cbb_ref, cbc_ref,
                dtb_ref, alog_ref, dsk_ref, nw_ref, tri_ref, o_ref,
                extx_ref, extb_ref, extc_ref, state_ref, *, ts):
    s = pl.program_id(1)

    @pl.when(s == 0)
    def _():
        extx_ref[...] = jnp.zeros_like(extx_ref)
        extb_ref[...] = jnp.zeros_like(extb_ref)
        extc_ref[...] = jnp.zeros_like(extc_ref)
        state_ref[...] = jnp.zeros_like(state_ref)

    def conv_silu(u_ref, ext_ref, w_ref, bias_ref):
        ext_ref[0:SUBLANES, :] = ext_ref[ts:ts + SUBLANES, :]
        ext_ref[SUBLANES:, :] = u_ref[...]
        w = w_ref[...]
        acc = bias_ref[...] + w[SSD_CONV - 1:SSD_CONV, :] * u_ref[...]
        for sh in range(1, SSD_CONV):
            acc = acc + w[SSD_CONV - 1 - sh:SSD_CONV - sh, :] * ext_ref[pl.ds(SUBLANES - sh, ts), :]
        return _silu(acc)

    xs = conv_silu(xs_ref, extx_ref, cwx_ref, cbx_ref)
    bm = conv_silu(b_ref, extb_ref, cwb_ref, cbb_ref)
    cm = conv_silu(c_ref, extc_ref, cwc_ref, cbc_ref)
    dtr = dt_ref[...] + dtb_ref[...]
    dt = jnp.maximum(dtr, 0.0) + jnp.log1p(jnp.exp(-jnp.abs(dtr)))
    a = dt * (-jnp.exp(alog_ref[...]))
    z = z_ref[...]
    tri = tri_ref[...]
    dsk = dsk_ref[...]
    nw = nw_ref[...]
    li = lax.broadcasted_iota(jnp.int32, (SSD_CHUNK, SSD_CHUNK), 0)
    si = lax.broadcasted_iota(jnp.int32, (SSD_CHUNK, SSD_CHUNK), 1)
    causal = li >= si
    low_half = si < SSD_HEAD_DIM
    low_half_row = low_half[0:1, :]

    for ck in range(ts // SSD_CHUNK):
        r0 = ck * SSD_CHUNK
        a_c = a[r0:r0 + SSD_CHUNK]
        hi, mid, lo = _split3(a_c)
        acs = _dot(tri, hi) + _dot(tri, mid) + _dot(tri, lo)
        acs_t = acs.T
        dt_t = dt[r0:r0 + SSD_CHUNK].T
        a_last_b = jnp.broadcast_to(acs_t[:, SSD_CHUNK - 1:SSD_CHUNK], acs_t.shape)
        dtd_t = dt_t * jnp.exp(a_last_b - acs_t)
        chunk_decay = jnp.exp(acs[SSD_CHUNK - 1:SSD_CHUNK, :])
        for g in range(SSD_GROUPS):
            cg = cm[r0:r0 + SSD_CHUNK, g * SSD_STATE:(g + 1) * SSD_STATE]
            bg = bm[r0:r0 + SSD_CHUNK, g * SSD_STATE:(g + 1) * SSD_STATE]
            cb = _dot_nt(cg.astype(BF16), bg.astype(BF16))
            bgt = bg.T
            ys = []
            for pr in range(SSD_HPG // 2):
                pair = (g * SSD_HPG) // 2 + pr
                c0 = pair * LANES
                xs_p = xs[r0:r0 + SSD_CHUNK, c0:c0 + LANES].astype(BF16)
                prev = state_ref[pair]
                rhs = jnp.concatenate([xs_p, prev.astype(BF16)], axis=0)
                y_h, st_h, cd_h = [], [], []
                for hh in (2 * pair, 2 * pair + 1):
                    colb = jnp.broadcast_to(acs[:, hh:hh + 1], (SSD_CHUNK, SSD_CHUNK))
                    seg = colb - acs_t[hh:hh + 1, :]
                    decay = jnp.where(causal, jnp.exp(jnp.where(causal, seg, 0.0)), 0.0)
                    intra = cb * decay * dt_t[hh:hh + 1, :]
                    inter = cg * jnp.exp(colb)
                    lhs = jnp.concatenate([intra, inter], axis=1).astype(BF16)
                    y_h.append(_dot(lhs, rhs))
                    st_h.append(_dot((bgt * dtd_t[hh:hh + 1, :]).astype(BF16), xs_p))
                    cd_h.append(jnp.broadcast_to(chunk_decay[:, hh:hh + 1], (1, LANES)))
                ys.append(jnp.where(low_half, y_h[0], y_h[1]))
                state_ref[pair] = (prev * jnp.where(low_half_row, cd_h[0], cd_h[1])
                                   + jnp.where(low_half, st_h[0], st_h[1]))
            yg = jnp.concatenate(ys, axis=1)
            yg = yg + xs[r0:r0 + SSD_CHUNK, g * SSD_GW:(g + 1) * SSD_GW] * dsk[:, g * SSD_GW:(g + 1) * SSD_GW]
            u = yg * _silu(z[r0:r0 + SSD_CHUNK, g * SSD_GW:(g + 1) * SSD_GW])
            u = u * lax.rsqrt(jnp.mean(u * u, axis=-1, keepdims=True) + EPS)
            o_ref[r0:r0 + SSD_CHUNK, g * SSD_GW:(g + 1) * SSD_GW] = (
                u * nw[:, g * SSD_GW:(g + 1) * SSD_GW]).astype(o_ref.dtype)


def _pad_lanes(v, n=LANES):
    v = v.reshape(1, -1).astype(F32)
    return jnp.pad(v, ((0, 0), (0, n - v.shape[1])))


def _ssd_mixer(proj, bsz, seq, conv_w, conv_b, dt_bias, a_log, d_skip, norm_w, ts=256):
    nsteps = seq // ts
    row = lambda b, s: b * nsteps + s
    xw, bw, cw = conv_w[:, :SSD_WIDTH], conv_w[:, SSD_WIDTH:SSD_WIDTH + 512], conv_w[:, SSD_WIDTH + 512:]
    xb, bb, cb = conv_b[:SSD_WIDTH], conv_b[SSD_WIDTH:SSD_WIDTH + 512], conv_b[SSD_WIDTH + 512:]
    tri = jnp.asarray(np.tril(np.ones((SSD_CHUNK, SSD_CHUNK), np.float32)), BF16)
    const = lambda shape: pl.BlockSpec(shape, lambda b, s: (0,) * len(shape))
    kern = functools.partial(_ssd_kernel, ts=ts)
    return pl.pallas_call(
        kern,
        grid=(bsz, nsteps),
        in_specs=[pl.BlockSpec((ts, SSD_WIDTH), lambda b, s: (row(b, s), COL_Z // SSD_WIDTH)),
                  pl.BlockSpec((ts, SSD_WIDTH), lambda b, s: (row(b, s), COL_XS // SSD_WIDTH)),
                  pl.BlockSpec((ts, 512), lambda b, s: (row(b, s), COL_B // 512)),
                  pl.BlockSpec((ts, 512), lambda b, s: (row(b, s), COL_C // 512)),
                  pl.BlockSpec((ts, LANES), lambda b, s: (row(b, s), COL_DT // LANES)),
                  const((SSD_CONV, SSD_WIDTH)), const((SSD_CONV, 512)), const((SSD_CONV, 512)),
                  const((1, SSD_WIDTH)), const((1, 512)), const((1, 512)),
                  const((1, LANES)), const((1, LANES)), const((1, SSD_WIDTH)), const((1, SSD_WIDTH)),
                  const((SSD_CHUNK, SSD_CHUNK))],
        out_specs=pl.BlockSpec((ts, SSD_WIDTH), lambda b, s: (row(b, s), 0)),
        out_shape=jax.ShapeDtypeStruct((bsz * seq, SSD_WIDTH), BF16),
        scratch_shapes=[pltpu.VMEM((ts + SUBLANES, SSD_WIDTH), F32), pltpu.VMEM((ts + SUBLANES, 512), F32),
                        pltpu.VMEM((ts + SUBLANES, 512), F32),
                        pltpu.VMEM((SSD_HEADS // 2, SSD_STATE, 2 * SSD_HEAD_DIM), F32)],
        compiler_params=_params(("arbitrary", "arbitrary")),
        name="ssd_mixer",
    )(proj, proj, proj, proj, proj, xw, bw, cw, xb.reshape(1, -1), bb.reshape(1, -1), cb.reshape(1, -1),
      _pad_lanes(dt_bias), _pad_lanes(a_log), jnp.repeat(d_skip.astype(F32), SSD_HEAD_DIM).reshape(1, -1),
      norm_w.reshape(1, -1), tri)


def _hgrn_kernel(q_ref, f_ref, i_ref, g_ref, lbl_ref, nw_ref, cum_ref, o_ref, state_ref, *, ts, layer):
    s = pl.program_id(1)

    @pl.when(s == 0)
    def _():
        state_ref[...] = jnp.zeros_like(state_ref)

    nck = ts // HGRN_CHUNK
    lg = lbl_ref[...]
    e = jnp.exp(lg - jnp.max(lg, axis=0, keepdims=True))
    sm = e / jnp.sum(e, axis=0, keepdims=True)
    ridx = lax.broadcasted_iota(jnp.int32, lg.shape, 0)
    lb = jnp.sum(jnp.where((ridx >= 1) & (ridx <= layer), sm, 0.0), axis=0, keepdims=True)

    cum = cum_ref[...]
    nw = nw_ref[...]
    ti = lax.broadcasted_iota(jnp.int32, (ts, ts), 0)
    tj = lax.broadcasted_iota(jnp.int32, (ts, ts), 1)
    blockcausal = (ti // HGRN_CHUNK == tj // HGRN_CHUNK) & (ti >= tj)
    tok = lax.broadcasted_iota(jnp.int32, (1, ts), 1)

    def head(h):
        c0 = h * HGRN_DIM
        lbh = lb[:, c0:c0 + HGRN_DIM]
        f = lbh + (1.0 - lbh) * _sigmoid(f_ref[:, c0:c0 + HGRN_DIM])
        logf = jnp.log(f)
        k = 1.0 - f
        q = _silu(q_ref[:, c0:c0 + HGRN_DIM])
        v = i_ref[:, c0:c0 + HGRN_DIM]
        hi, mid, lo = _split3(logf)
        b = _dot(cum, hi) + _dot(cum, mid) + _dot(cum, lo)
        yield
        b3 = b.reshape(nck, HGRN_CHUNK, HGRN_DIM)
        bref = jnp.broadcast_to(b3[:, HGRN_CHUNK // 2:HGRN_CHUNK // 2 + 1, :], b3.shape).reshape(ts, HGRN_DIM)
        blast = jnp.broadcast_to(b3[:, HGRN_CHUNK - 1:HGRN_CHUNK, :], b3.shape).reshape(ts, HGRN_DIM)
        qe = (q * jnp.exp(b - bref)).astype(BF16)
        ke = (k * jnp.exp(bref - b)).astype(BF16)
        kl = (k * jnp.exp(blast - b)).astype(BF16)
        qb = (q * jnp.exp(b)).astype(BF16)
        vb = v.astype(BF16)
        att = _dot_nt(qe, ke)
        vt = v.T
        lhs = jnp.concatenate(
            [jnp.where(tok // HGRN_CHUNK == c, vt, 0.0) for c in range(nck)], axis=0).astype(BF16)
        st = _dot(lhs, kl)
        yield
        o = _dot(jnp.where(blockcausal, att, 0.0).astype(BF16), vb)
        state = state_ref[h]
        outs = []
        for c in range(nck):
            t0 = c * HGRN_CHUNK
            outs.append(_dot_nt(qb[t0:t0 + HGRN_CHUNK], state.astype(BF16)))
            cd = jnp.exp(blast[t0:t0 + 1, :])
            state = state * cd + st[c * HGRN_DIM:(c + 1) * HGRN_DIM]
            if c % 2 == 1:
                yield
        state_ref[h] = state
        o = o + jnp.concatenate(outs, axis=0)
        o = o * lax.rsqrt(jnp.mean(o * o, axis=-1, keepdims=True) + EPS) * nw
        o_ref[:, c0:c0 + HGRN_DIM] = (o * _silu(g_ref[:, c0:c0 + HGRN_DIM])).astype(o_ref.dtype)

    for h0 in range(0, HGRN_HEADS, HGRN_HEADS_INTERLEAVED):
        live = [head(h) for h in range(h0, h0 + HGRN_HEADS_INTERLEAVED)]
        while live:
            live = [gen for gen in live if next(gen, live) is not live]


def _hgrn_cum_matrix(ts):
    t = np.arange(ts)
    same = (t[:, None] // HGRN_CHUNK) == (t[None, :] // HGRN_CHUNK)
    return (same & (t[None, :] <= t[:, None])).astype(np.float32)


def _hgrn_mixer(proj, bsz, seq, lb_logits, norm_w, layer, ts=256):
    nsteps = seq // ts
    row = lambda b, s: b * nsteps + s
    depth = lb_logits.shape[0]
    cum = jnp.asarray(_hgrn_cum_matrix(ts), BF16)
    kern = functools.partial(_hgrn_kernel, ts=ts, layer=layer)
    blk = lambda col: pl.BlockSpec((ts, HGRN_WIDTH), lambda b, s: (row(b, s), col // HGRN_WIDTH))
    return pl.pallas_call(
        kern,
        grid=(bsz, nsteps),
        in_specs=[blk(COL_HQ), blk(COL_HF), blk(COL_HI), blk(COL_HG),
                  pl.BlockSpec((depth, HGRN_WIDTH), lambda b, s: (0, 0)),
                  pl.BlockSpec((1, HGRN_DIM), lambda b, s: (0, 0)),
                  pl.BlockSpec((ts, ts), lambda b, s: (0, 0))],
        out_specs=pl.BlockSpec((ts, HGRN_WIDTH), lambda b, s: (row(b, s), 0)),
        out_shape=jax.ShapeDtypeStruct((bsz * seq, HGRN_WIDTH), BF16),
        scratch_shapes=[pltpu.VMEM((HGRN_HEADS, HGRN_DIM, HGRN_DIM), F32)],
        compiler_params=_params(("arbitrary", "arbitrary")),
        name="hgrn2_mixer",
    )(proj, proj, proj, proj, lb_logits.astype(F32), norm_w.reshape(1, HGRN_DIM).astype(F32), cum)


def _rel_bucket_np(dist):
    max_exact = REL_BUCKETS // 2
    d = np.maximum(dist, 0)
    ratio = np.maximum(d, 1).astype(np.float32) / np.float32(max_exact)
    log_ratio = np.log(ratio).astype(np.float32) / np.float32(math.log(REL_MAX_DIST / max_exact))
    large = np.minimum(max_exact + (log_ratio * np.float32(REL_BUCKETS - max_exact)).astype(np.int32),
                       REL_BUCKETS - 1)
    return np.where(d < max_exact, d, large).astype(np.int32)


def _bias_expand_kernel(rb_ref, bmap_ref, o_ref):
    h = pl.program_id(0)
    bm = bmap_ref[...]
    out = jnp.full(bm.shape, NEG_INF, F32)
    for k in range(REL_BUCKETS):
        out = jnp.where(bm == k, rb_ref[k, h] * LOG2E, out)
    o_ref[...] = out


def _bias_expand(rel_bias, bmap, tr):
    rows, cols = bmap.shape
    return pl.pallas_call(
        _bias_expand_kernel,
        grid=(NSA_HEADS, rows // tr),
        in_specs=[pl.BlockSpec(memory_space=pltpu.SMEM),
                  pl.BlockSpec((tr, cols), lambda h, i: (i, 0))],
        out_specs=pl.BlockSpec((None, tr, cols), lambda h, i: (h, i, 0)),
        out_shape=jax.ShapeDtypeStruct((NSA_HEADS, rows, cols), F32),
        compiler_params=_params(("arbitrary", "arbitrary")),
        name="nsa_bias_expand",
    )(rel_bias.astype(F32), bmap)


def _nsa_bias_tables(rel_bias, seq):
    ncp = seq // CMP_STRIDE
    t = np.arange(seq)[:, None]
    cmp_end = np.arange(ncp)[None, :] * CMP_STRIDE + CMP_BLOCK - 1
    d = t - cmp_end
    bmap_cmp = np.where(d >= 0, _rel_bucket_np(d), -1).astype(np.int32)
    l = np.arange(Q_BLOCK)[:, None]
    j = np.arange(WIN_KEYS)[None, :]
    dist = l - j + WINDOW
    win = np.where((dist >= 0) & (dist < WINDOW), _rel_bucket_np(dist), -1)
    far = _rel_bucket_np(np.arange(Q_BLOCK + 1, 8 * seq))
    assert (far == far[0]).all(), "distances beyond one query block must share a single bucket"
    bmap_tab = np.concatenate([win, np.full((Q_BLOCK, Q_BLOCK), far[0])], axis=1).astype(np.int32)
    bias_cmp = _bias_expand(rel_bias, jnp.asarray(bmap_cmp), 512)
    bias_tab = _bias_expand(rel_bias, jnp.asarray(bmap_tab), Q_BLOCK)
    return bias_cmp, bias_tab


def _cmp_kernel(u_ref, pe_ref, w1_ref, w2_ref, o_ref, *, ncp):
    half = CMP_BLOCK // 2
    pe = pe_ref[...]
    top = jnp.zeros((ncp, CMP_HIDDEN), F32)
    bot = jnp.zeros((ncp, CMP_HIDDEN), F32)
    for l in range(half):
        x = u_ref[pl.ds(l, ncp, stride=half), :]
        top = top + _dot((x + pe[l:l + 1, :]).astype(BF16), w1_ref[l])
        bot = bot + _dot((x + pe[half + l:half + l + 1, :]).astype(BF16), w1_ref[half + l])
    hid = top + pltpu.roll(bot, ncp - 1, 0)
    o_ref[...] = _dot(_silu(hid).astype(BF16), w2_ref[...])


def _nsa_compress(proj, bsz, seq, pe, w1, w2):
    ncp = seq // CMP_STRIDE
    kern = functools.partial(_cmp_kernel, ncp=ncp)
    return pl.pallas_call(
        kern,
        grid=(bsz, 2, NSA_KV_HEADS),
        in_specs=[pl.BlockSpec((seq, NSA_HEAD_DIM), lambda b, t, h: (b, COL_KC // NSA_HEAD_DIM + NSA_KV_HEADS * t + h)),
                  pl.BlockSpec((None, CMP_BLOCK, NSA_HEAD_DIM), lambda b, t, h: (t, 0, 0)),
                  pl.BlockSpec((None, CMP_BLOCK, NSA_HEAD_DIM, CMP_HIDDEN), lambda b, t, h: (t, 0, 0, 0)),
                  pl.BlockSpec((None, CMP_HIDDEN, NSA_HEAD_DIM), lambda b, t, h: (t, 0, 0))],
        out_specs=pl.BlockSpec((None, None, None, ncp, NSA_HEAD_DIM), lambda b, t, h: (b, t, h, 0, 0)),
        out_shape=jax.ShapeDtypeStruct((bsz, 2, NSA_KV_HEADS, ncp, NSA_HEAD_DIM), F32),
        compiler_params=_params(("arbitrary", "arbitrary", "arbitrary")),
        name="nsa_compress",
    )(proj, pe.astype(F32), w1.reshape(2, CMP_BLOCK, NSA_HEAD_DIM, CMP_HIDDEN).astype(BF16), w2.astype(BF16))


def _softmax_start(s, v):
    m = jnp.max(s, axis=-1, keepdims=True)
    p = jnp.exp2(s - m)
    return m, jnp.sum(p, axis=-1, keepdims=True), _dot(p.astype(BF16), v)


def _softmax_update(s, v, carry):
    m, l, acc = carry
    m_new = jnp.maximum(m, jnp.max(s, axis=-1, keepdims=True))
    alpha = jnp.exp2(m - m_new)
    p = jnp.exp2(s - m_new)
    return m_new, alpha * l + jnp.sum(p, axis=-1, keepdims=True), alpha * acc + _dot(p.astype(BF16), v)


def _softmax_finish(carry):
    m, l, acc = carry
    return jnp.where(m > 0.5 * NEG_INF, acc / jnp.maximum(l, 1e-30), 0.0)


def _nsa_kernel(q_ref, gate_ref, kc_ref, vc_ref, ks_ref, vs_ref, kw_ref, vw_ref, bcmp_ref, btab_ref,
                kone_ref, c2s_ref, o_ref, kaug_ref, vsb_ref, kwb_ref, vwb_ref, sa_ref, sb_ref, *, n_sel, nqb):
    hk = pl.program_id(1)
    c0 = pl.program_id(2) * nqb
    rows = NSA_GQA * Q_BLOCK
    lane = lax.broadcasted_iota(jnp.int32, (Q_BLOCK, LANES), 1)

    @pl.when(c0 == 0)
    def _():
        kaug_ref[0:Q_BLOCK, 0:NSA_HEAD_DIM] = jnp.zeros((Q_BLOCK, NSA_HEAD_DIM), BF16)
        kaug_ref[Q_BLOCK:, 0:NSA_HEAD_DIM] = ks_ref[...].astype(BF16)
        kaug_ref[:, NSA_HEAD_DIM:] = kone_ref[...]
        vsb_ref[0:Q_BLOCK, :] = jnp.zeros((Q_BLOCK, NSA_HEAD_DIM), BF16)
        vsb_ref[Q_BLOCK:, :] = vs_ref[...].astype(BF16)
        kwb_ref[0:WINDOW, 0:NSA_HEAD_DIM] = jnp.zeros((WINDOW, NSA_HEAD_DIM), BF16)
        kwb_ref[WINDOW:, 0:NSA_HEAD_DIM] = kw_ref[...].astype(BF16)
        for r0 in range(0, WINDOW, Q_BLOCK):
            kwb_ref[r0:r0 + Q_BLOCK, NSA_HEAD_DIM:] = jnp.where(lane == AUG_PAD, 1.0, 0.0).astype(BF16)
        kwb_ref[WINDOW:, NSA_HEAD_DIM:] = jnp.zeros((kw_ref.shape[0], LANES), BF16)
        vwb_ref[0:WINDOW, :] = jnp.zeros((WINDOW, NSA_HEAD_DIM), BF16)
        vwb_ref[WINDOW:, :] = vw_ref[...].astype(BF16)

    gens = [_nsa_unit(c0 + u, q_ref[u * Q_BLOCK:(u + 1) * Q_BLOCK, :], bcmp_ref[:, u * Q_BLOCK:(u + 1) * Q_BLOCK, :],
                      btab_ref, kc_ref, vc_ref, c2s_ref, kaug_ref, vsb_ref, kwb_ref, vwb_ref, n_sel)
            for u in range(nqb)]
    units = [None] * nqb
    while any(un is None for un in units):
        for u, gen in enumerate(gens):
            if units[u] is None:
                units[u] = next(gen)
    qaug_far = jnp.concatenate([un[3] for un in units], axis=0)
    carry = tuple(jnp.concatenate([un[2][i] for un in units], axis=0) for i in range(3))

    far_blocks = FAR_KEYS // Q_BLOCK
    n_slabs = (c0 + nqb - 2 + far_blocks - 1) // far_blocks
    n_pairs = n_slabs // 2
    last_slab = (kaug_ref.shape[0] - Q_BLOCK) // FAR_KEYS - 1

    def far_logits(slab):
        r0 = pl.multiple_of(jnp.minimum(slab, last_slab) * FAR_KEYS + Q_BLOCK, Q_BLOCK)
        return _dot_nt(qaug_far, kaug_ref[pl.ds(r0, FAR_KEYS), :])

    def far_values(slab):
        return vsb_ref[pl.ds(pl.multiple_of(slab * FAR_KEYS + Q_BLOCK, Q_BLOCK), FAR_KEYS), :]

    sa_ref[...] = far_logits(0)

    def far_body(k, carry):
        sb_ref[...] = far_logits(2 * k + 1)
        carry = _softmax_update(sa_ref[...], far_values(2 * k), carry)
        sa_ref[...] = far_logits(2 * k + 2)
        return _softmax_update(sb_ref[...], far_values(2 * k + 1), carry)

    carry = lax.fori_loop(0, n_pairs, far_body, carry)
    carry = lax.cond(n_slabs % 2 == 1,
                     lambda cr: _softmax_update(sa_ref[...], far_values(2 * n_pairs), cr), lambda cr: cr, carry)
    o_sel = _softmax_finish(carry)

    for u in range(nqb):
        o_cmp, o_win = units[u][0], units[u][1]
        gates = _sigmoid(gate_ref[u * Q_BLOCK:(u + 1) * Q_BLOCK, :])
        for g in range(NSA_GQA):
            def gate_col(br):
                return jnp.sum(jnp.where(lane == hk * (3 * NSA_GQA) + g * 3 + br, gates, 0.0), axis=-1, keepdims=True)

            r0 = g * Q_BLOCK
            out = (gate_col(0) * o_cmp[r0:r0 + Q_BLOCK] + gate_col(1) * o_sel[u * rows + r0:u * rows + r0 + Q_BLOCK]
                   + gate_col(2) * o_win[r0:r0 + Q_BLOCK])
            o_ref[u * Q_BLOCK:(u + 1) * Q_BLOCK, g * NSA_HEAD_DIM:(g + 1) * NSA_HEAD_DIM] = out.astype(o_ref.dtype)


def _nsa_unit(c, q_in, bc_in, btab_ref, kc_ref, vc_ref, c2s_ref, kaug_ref, vsb_ref, kwb_ref, vwb_ref, n_sel):
    rows = NSA_GQA * Q_BLOCK
    lane = lax.broadcasted_iota(jnp.int32, (Q_BLOCK, LANES), 1)
    q = q_in * (NSA_HEAD_DIM ** -0.5 * LOG2E)
    q3 = jnp.concatenate([q[:, g * NSA_HEAD_DIM:(g + 1) * NSA_HEAD_DIM] for g in range(NSA_GQA)],
                         axis=0).astype(BF16)

    btab = btab_ref[...].reshape(rows, WIN_KEYS + Q_BLOCK)
    near0 = pl.multiple_of(c * Q_BLOCK, Q_BLOCK)

    qaug_win = jnp.concatenate(
        [q3, jnp.concatenate([jnp.where(lane == AUG_PAD, NEG_INF, 0.0).astype(BF16)] * NSA_GQA, axis=0)], axis=1)
    s_win = _dot_nt(qaug_win, kwb_ref[pl.ds(near0, WIN_KEYS), :]) + btab[:, 0:WIN_KEYS]
    yield None

    bc = bc_in.reshape(rows, bc_in.shape[-1])
    valid = bc > 0.5 * NEG_INF
    s = _dot_nt(q3, kc_ref[...].astype(BF16)) + bc
    yield None
    o_win = _softmax_finish(_softmax_start(s_win, vwb_ref[pl.ds(near0, WIN_KEYS), :]))
    yield None
    m = jnp.max(s, axis=-1, keepdims=True)
    p = jnp.where(valid, jnp.exp2(s - m), 0.0)
    p = p / jnp.maximum(jnp.sum(p, axis=-1, keepdims=True), 1e-30)
    o_cmp = _dot(p.astype(BF16), vc_ref[...].astype(BF16))

    psum = p[0:Q_BLOCK] + p[Q_BLOCK:2 * Q_BLOCK] + p[2 * Q_BLOCK:3 * Q_BLOCK]
    ph = psum.astype(BF16)
    plo = (psum - ph.astype(F32)).astype(BF16)
    c2s = c2s_ref[...]
    imp_t = _dot_nt(c2s, ph) + _dot_nt(c2s, plo)
    nsp = -(-n_sel // SUBLANES) * SUBLANES
    imp_t = imp_t[0:nsp]
    tq = c * Q_BLOCK + lax.broadcasted_iota(jnp.int32, (nsp, Q_BLOCK), 1)
    jj = lax.broadcasted_iota(jnp.int32, (nsp, Q_BLOCK), 0)
    valid_sel = (jj < n_sel) & (jj * SEL_BLOCK <= tq)
    back = tq // SEL_BLOCK - jj
    force = (jj == 0) | ((back >= 0) & (back < SEL_LOCAL))
    score = jnp.where(valid_sel, imp_t + jnp.where(force, FORCE_BONUS, 0.0), NEG_INF)
    yield None
    n_tiles = nsp // SUBLANES
    tiles = [score[v * SUBLANES:(v + 1) * SUBLANES] for v in range(n_tiles)]
    sub = lax.broadcasted_iota(jnp.int32, (SUBLANES, Q_BLOCK), 0)
    cnts = [jnp.zeros((SUBLANES, Q_BLOCK), F32) for _ in range(n_tiles)]
    for jp in range(n_sel):
        v0, r0 = divmod(jp, SUBLANES)
        r = tiles[v0][r0:r0 + 1, :]
        for v in range(n_tiles):
            gt = jnp.where(r > tiles[v], 1.0, 0.0)
            ge = jnp.where(r >= tiles[v], 1.0, 0.0)
            cnts[v] = cnts[v] + (gt if v < v0 else ge if v > v0 else jnp.where(sub > r0, ge, gt))
        if jp % (2 * SUBLANES) == 2 * SUBLANES - 1:
            yield None
    cnt = jnp.concatenate(cnts, axis=0)
    keep = valid_sel & (cnt < float(min(SEL_TOPK, n_sel)))

    ext = lax.broadcasted_iota(jnp.int32, (LANES - nsp, Q_BLOCK), 0) + nsp
    aug = jnp.concatenate([jnp.where(keep, 0.0, NEG_INF), jnp.where(ext == AUG_PAD, NEG_INF, 0.0)], axis=0)
    aug_near = aug.T
    aug_far = jnp.where((lane >= 2 * (c - 1)) & (lane < AUG_BIAS_HI), NEG_INF, aug_near)
    far_parts = []
    for g in range(NSA_GQA):
        bfar = btab[g * Q_BLOCK:(g + 1) * Q_BLOCK, WIN_KEYS:]
        hi = bfar.astype(BF16).astype(F32)
        far_parts.append(jnp.where(lane == AUG_BIAS_HI, hi, jnp.where(lane == AUG_BIAS_LO, bfar - hi, aug_far)))
    qaug_near = jnp.concatenate([q3, jnp.concatenate([aug_near.astype(BF16)] * NSA_GQA, axis=0)], axis=1)
    qaug_far = jnp.concatenate([q3, jnp.concatenate(far_parts, axis=0).astype(BF16)], axis=1)

    s = _dot_nt(qaug_near, kaug_ref[pl.ds(near0, NEAR_KEYS), :]) + btab[:, WIN_KEYS - NEAR_KEYS:WIN_KEYS]
    yield None
    carry = _softmax_start(s, vsb_ref[pl.ds(near0, NEAR_KEYS), :])
    yield o_cmp, o_win, carry, qaug_far


def _nsa_attention(proj, cmp_kv, bias_cmp, bias_tab, bsz, seq):
    nq = seq // Q_BLOCK
    ncp = seq // CMP_STRIDE
    n_sel = seq // SEL_BLOCK
    assert n_sel <= AUG_BIAS_HI and NSA_HEAD_DIM == LANES
    assert seq % (2 * FAR_KEYS) == 0, "far steps walk the keys two 512-key slabs at a time"
    key = np.arange(-Q_BLOCK, seq)[:, None]
    ln = np.arange(LANES)[None, :]
    kone = np.where(key < 0, ln == AUG_PAD,
                    ((ln < AUG_BIAS_HI) & (key // SEL_BLOCK == ln)) | (ln == AUG_BIAS_HI) | (ln == AUG_BIAS_LO))
    kone = jnp.asarray(kone, BF16)
    c_start = np.arange(ncp)[None, :] * CMP_STRIDE
    s_start = np.arange(LANES)[:, None] * SEL_BLOCK
    overlap = np.clip(np.minimum(c_start + CMP_BLOCK, s_start + SEL_BLOCK) - np.maximum(c_start, s_start), 0, None)
    overlap = np.where(np.arange(LANES)[:, None] < n_sel, overlap, 0)
    c2s_t = jnp.asarray(overlap / CMP_BLOCK, BF16)
    gqa_w = NSA_GQA * NSA_HEAD_DIM
    full = lambda col: pl.BlockSpec((seq, NSA_HEAD_DIM), lambda b, h, c: (b, col // NSA_HEAD_DIM + h))
    nqb = NSA_Q_BLOCKS_PER_STEP
    ns = nq // nqb
    qr = nqb * Q_BLOCK
    kern = functools.partial(_nsa_kernel, n_sel=n_sel, nqb=nqb)
    return pl.pallas_call(
        kern,
        grid=(bsz, NSA_KV_HEADS, ns),
        in_specs=[pl.BlockSpec((qr, gqa_w), lambda b, h, c: (b * ns + c, COL_Q // gqa_w + h)),
                  pl.BlockSpec((qr, LANES), lambda b, h, c: (b * ns + c, COL_GATE // LANES)),
                  pl.BlockSpec((None, None, None, ncp, NSA_HEAD_DIM), lambda b, h, c: (b, 0, h, 0, 0)),
                  pl.BlockSpec((None, None, None, ncp, NSA_HEAD_DIM), lambda b, h, c: (b, 1, h, 0, 0)),
                  full(COL_KS), full(COL_VS), full(COL_KW), full(COL_VW),
                  pl.BlockSpec((NSA_GQA, qr, ncp), lambda b, h, c: (h, c, 0)),
                  pl.BlockSpec((NSA_GQA, Q_BLOCK, WIN_KEYS + Q_BLOCK), lambda b, h, c: (h, 0, 0)),
                  pl.BlockSpec((seq + Q_BLOCK, LANES), lambda b, h, c: (0, 0)),
                  pl.BlockSpec((LANES, ncp), lambda b, h, c: (0, 0))],
        out_specs=pl.BlockSpec((qr, gqa_w), lambda b, h, c: (b * ns + c, h)),
        out_shape=jax.ShapeDtypeStruct((bsz * seq, NSA_WIDTH), BF16),
        scratch_shapes=[pltpu.VMEM((seq + Q_BLOCK, 2 * NSA_HEAD_DIM), BF16),
                        pltpu.VMEM((seq + Q_BLOCK, NSA_HEAD_DIM), BF16),
                        pltpu.VMEM((seq + WINDOW, 2 * NSA_HEAD_DIM), BF16),
                        pltpu.VMEM((seq + WINDOW, NSA_HEAD_DIM), BF16),
                        pltpu.VMEM((nqb * NSA_GQA * Q_BLOCK, FAR_KEYS), F32),
                        pltpu.VMEM((nqb * NSA_GQA * Q_BLOCK, FAR_KEYS), F32)],
        compiler_params=_params(("arbitrary", "arbitrary", "arbitrary")),
        name="nsa_attention",
    )(proj, proj, cmp_kv, cmp_kv, proj, proj, proj, proj, bias_cmp, bias_tab, kone, c2s_t)


_IN_OFF = tuple(int(v) for v in np.cumsum((0,) + IN_SIZES))
_W_IN_COPIES = ((_IN_OFF[0], SSD_WIDTH, COL_Z), (_IN_OFF[1], SSD_WIDTH, COL_XS),
                (_IN_OFF[1] + SSD_WIDTH, 512, COL_B), (_IN_OFF[1] + SSD_WIDTH + 512, 512, COL_C),
                (_IN_OFF[3], NSA_WIDTH, COL_Q), (_IN_OFF[4], 6 * NSA_KV_WIDTH, COL_KC),
                (_IN_OFF[11], 4 * HGRN_WIDTH, COL_HQ))
_W_IN_NARROW = ((_IN_OFF[2], IN_SIZES[2], COL_DT, COL_GATE - COL_DT), (_IN_OFF[10], IN_SIZES[10], COL_GATE, COL_HQ - COL_GATE))


def _relayout_kernel(w_ref, o_ref):
    for src, width, dst in _W_IN_COPIES:
        o_ref[:, dst:dst + width] = w_ref[:, src:src + width].astype(BF16)
    for src, valid, dst, padded in _W_IN_NARROW:
        tile = w_ref[:, src:src + LANES]
        lane = lax.broadcasted_iota(jnp.int32, tile.shape, 1)
        o_ref[:, dst:dst + LANES] = jnp.where(lane < valid, tile, 0.0).astype(BF16)
        if padded > LANES:
            o_ref[:, dst + LANES:dst + padded] = jnp.zeros((tile.shape[0], padded - LANES), BF16)


def _relayout_w_in(w, tr=128):
    depth, d, n = w.shape
    return pl.pallas_call(
        _relayout_kernel,
        grid=(depth, d // tr),
        in_specs=[pl.BlockSpec((None, tr, n), lambda l, i: (l, i, 0))],
        out_specs=pl.BlockSpec((None, tr, IN_PAD), lambda l, i: (l, i, 0)),
        out_shape=jax.ShapeDtypeStruct((depth, d, IN_PAD), BF16),
        compiler_params=_params(("parallel", "parallel")),
        name="w_in_relayout",
    )(w)


def _tiles(seq):
    return dict(in_proj=(1024, 1024), out_proj=(1024, 512), gate_up=(min(2048, seq), 256), down=(256, 1024))


def kernel(x, norm_mix_w, w_in, ssd_conv_w, ssd_conv_b, ssd_dt_bias, ssd_a_log, ssd_d, ssd_norm_w, nsa_cmp_pe, nsa_cmp_w1, nsa_cmp_w2, rel_bias, hgrn_lb_logits, hgrn_norm_w, w_out, norm_ffn_w, ffn_w_gu, ffn_conv_w, ffn_conv_b, ffn_w_down, norm_f_w):
    bsz, seq, d = x.shape
    depth = w_in.shape[0]
    xr = x.reshape(bsz * seq, d).astype(F32)
    bias_cmp, bias_tab = _nsa_bias_tables(rel_bias, seq)
    tiles = _tiles(seq)
    w_in_b = _relayout_w_in(jnp.pad(w_in.astype(BF16), ((0, 0), (0, 0), (0, -w_in.shape[2] % LANES))))
    w_out_b = w_out.astype(BF16)
    w_gu_b = ffn_w_gu.astype(BF16)
    w_down_b = ffn_w_down.astype(BF16)
    xw, ssq = _prenorm(xr, norm_mix_w[0])
    for l in range(depth):
        proj = _matmul(xw, ssq, w_in_b, l, F32, *tiles["in_proj"], name="in_proj")
        y_ssd = _ssd_mixer(proj, bsz, seq, ssd_conv_w[l].astype(F32), ssd_conv_b[l].astype(F32), ssd_dt_bias[l],
                           ssd_a_log[l], ssd_d[l], ssd_norm_w[l].astype(F32))
        cmp_kv = _nsa_compress(proj, bsz, seq, nsa_cmp_pe[l], nsa_cmp_w1[l], nsa_cmp_w2[l])
        y_nsa = _nsa_attention(proj, cmp_kv, bias_cmp, bias_tab, bsz, seq)
        y_hgrn = _hgrn_mixer(proj, bsz, seq, hgrn_lb_logits, hgrn_norm_w[l], l)
        xr, xw, ssq = _out_proj([y_ssd, y_nsa, y_hgrn], w_out_b, l, xr, norm_ffn_w[l], *tiles["out_proj"])
        act = _gate_up(xw, ssq, w_gu_b, l, ffn_conv_w[l].astype(F32), ffn_conv_b[l].astype(F32), seq,
                       *tiles["gate_up"])
        if l + 1 < depth:
            xr, xw, ssq = _down_proj_norm(act, w_down_b, l, xr, norm_mix_w[l + 1], *tiles["down"])
        else:
            xr = _matmul_residual_wres(act, w_down_b, l, xr, *tiles["down"], name="ffn_down")
    out = _rmsnorm(xr, norm_f_w, x.dtype)
    return out.reshape(bsz, seq, d)
```

```python
import functools
import math

import numpy as np
import jax
import jax.numpy as jnp
from jax import lax
from jax.experimental import pallas as pl
from jax.experimental.pallas import tpu as pltpu

F32 = jnp.float32
BF16 = jnp.bfloat16

D_MODEL = 4096
SSD_HEAD_DIM = 64
SSD_WIDTH = 1536
SSD_HEADS = 24
SSD_GROUPS = 4
SSD_HPG = 6
SSD_STATE = 128
SSD_CONV = 4
SSD_CHUNK = 128
SSD_GW = SSD_WIDTH // SSD_GROUPS
NSA_HEAD_DIM = 128
NSA_WIDTH = 1536
NSA_HEADS = 12
NSA_KV_HEADS = 4
NSA_GQA = 3
NSA_KV_WIDTH = 512
CMP_BLOCK = 32
CMP_STRIDE = 16
CMP_HIDDEN = 256
SEL_BLOCK = 64
SEL_TOPK = 16
SEL_LOCAL = 2
WINDOW = 512
Q_BLOCK = 128
FORCE_BONUS = 1e4
HGRN_WIDTH = 1024
HGRN_HEADS = 8
HGRN_DIM = 128
HGRN_CHUNK = 32
REL_BUCKETS = 32
REL_MAX_DIST = 128
D_FF = 11008
FFN_CONV = 3
EPS = 1e-6
NEG_INF = -1e30

IN_SIZES = (SSD_WIDTH, SSD_WIDTH + 2 * SSD_GROUPS * SSD_STATE, SSD_HEADS, NSA_WIDTH,
            NSA_KV_WIDTH, NSA_KV_WIDTH, NSA_KV_WIDTH, NSA_KV_WIDTH, NSA_KV_WIDTH, NSA_KV_WIDTH,
            3 * NSA_HEADS, HGRN_WIDTH, HGRN_WIDTH, HGRN_WIDTH, HGRN_WIDTH)

LANES = 128
SUBLANES = 8
VMEM_LIMIT = 56 * 1024 * 1024

COL_Z = 0
COL_XS = 1536
COL_Q = 3072
COL_B = 4608
COL_C = 5120
COL_KC = 5632
COL_VC = 6144
COL_KS = 6656
COL_VS = 7168
COL_KW = 7680
COL_VW = 8192
COL_DT = 8704
COL_GATE = 8832
COL_HQ = 9216
COL_HF = 10240
COL_HI = 11264
COL_HG = 12288
IN_PAD = 13312

LOG2E = math.log2(math.e)
AUG_BIAS_HI = 64
AUG_BIAS_LO = 65
AUG_PAD = 66
WIN_KEYS = WINDOW + Q_BLOCK
NEAR_KEYS = 2 * Q_BLOCK
FAR_KEYS = 4 * Q_BLOCK
HGRN_HEADS_INTERLEAVED = 8
NSA_Q_BLOCKS_PER_STEP = 2


def _params(semantics):
    return pltpu.CompilerParams(dimension_semantics=semantics, vmem_limit_bytes=VMEM_LIMIT)


def _sigmoid(x):
    return 1.0 / (1.0 + jnp.exp(-x))


def _silu(x):
    return x * _sigmoid(x)


def _dot(a, b):
    return jnp.dot(a, b, preferred_element_type=F32)


def _dot_nt(a, b):
    return lax.dot_general(a, b, (((1,), (1,)), ((), ())), preferred_element_type=F32)


def _split3(x):
    hi = x.astype(BF16)
    r1 = x - hi.astype(F32)
    mid = r1.astype(BF16)
    lo = (r1 - mid.astype(F32)).astype(BF16)
    return hi, mid, lo


def _rmsnorm_kernel(x_ref, w_ref, o_ref):
    x = x_ref[...]
    ms = jnp.mean(x * x, axis=-1, keepdims=True)
    o_ref[...] = (x * lax.rsqrt(ms + EPS) * w_ref[...]).astype(o_ref.dtype)


def _rmsnorm(x, w, out_dtype, tm=256):
    m, d = x.shape
    return pl.pallas_call(
        _rmsnorm_kernel,
        grid=(m // tm,),
        in_specs=[pl.BlockSpec((tm, d), lambda i: (i, 0)), pl.BlockSpec((1, d), lambda i: (0, 0))],
        out_specs=pl.BlockSpec((tm, d), lambda i: (i, 0)),
        out_shape=jax.ShapeDtypeStruct((m, d), out_dtype),
        compiler_params=_params(("parallel",)),
        name="rmsnorm",
    )(x, w.reshape(1, d).astype(F32))


def _lane_fold(sq):
    part = sq[:, 0:LANES]
    for c in range(1, sq.shape[1] // LANES):
        part = part + sq[:, c * LANES:(c + 1) * LANES]
    return part


def _rinv_lanes(ssq, d):
    part = ssq[0]
    for p in range(1, ssq.shape[0]):
        part = part + ssq[p]
    tot = jnp.sum(part, axis=-1, keepdims=True)
    return jnp.broadcast_to(lax.rsqrt(tot * (1.0 / d) + EPS), part.shape)


def _scale_rows(acc, rinv):
    return jnp.concatenate([acc[:, c * LANES:(c + 1) * LANES] * rinv for c in range(acc.shape[1] // LANES)], axis=1)


def _prenorm_kernel(x_ref, w_ref, xw_ref, ssq_ref):
    x = x_ref[...]
    xw_ref[...] = (x * w_ref[...]).astype(xw_ref.dtype)
    ssq_ref[0] = _lane_fold(x * x)


def _prenorm(x, w, tm=256):
    m, d = x.shape
    return pl.pallas_call(
        _prenorm_kernel,
        grid=(m // tm,),
        in_specs=[pl.BlockSpec((tm, d), lambda i: (i, 0)), pl.BlockSpec((1, d), lambda i: (0, 0))],
        out_specs=[pl.BlockSpec((tm, d), lambda i: (i, 0)), pl.BlockSpec((1, tm, LANES), lambda i: (0, i, 0))],
        out_shape=[jax.ShapeDtypeStruct((m, d), BF16), jax.ShapeDtypeStruct((1, m, LANES), F32)],
        compiler_params=_params(("parallel",)),
        name="prenorm",
    )(x, w.reshape(1, d).astype(F32))


MM_ROW_CHUNK = 128


def _row_chunks(rows):
    rc = min(MM_ROW_CHUNK, rows)
    return [slice(r, r + rc) for r in range(0, rows, rc)]


def _mm_kernel(a_ref, ssq_ref, w_ref, o_ref, rinv_ref, *, d):
    @pl.when(pl.program_id(1) == 0)
    def _():
        rinv_ref[...] = _rinv_lanes(ssq_ref[...], d)

    for rows in _row_chunks(o_ref.shape[0]):
        o_ref[rows, :] = _scale_rows(_dot(a_ref[rows, :], w_ref[...]), rinv_ref[rows, :]).astype(o_ref.dtype)


def _mm_res_kernel(a_ref, w_ref, r_ref, o_ref):
    for rows in _row_chunks(o_ref.shape[0]):
        o_ref[rows, :] = r_ref[rows, :] + _dot(a_ref[rows, :], w_ref[...])


def _matmul(xw, ssq, w, layer, out_dtype, tm, tn, name):
    m, k = xw.shape
    n = w.shape[2]
    return pl.pallas_call(
        functools.partial(_mm_kernel, d=k),
        grid=(m // tm, n // tn),
        in_specs=[pl.BlockSpec((tm, k), lambda i, j: (i, 0), pipeline_mode=pl.Buffered(1)),
                  pl.BlockSpec((ssq.shape[0], tm, LANES), lambda i, j: (0, i, 0), pipeline_mode=pl.Buffered(1)),
                  pl.BlockSpec((None, k, tn), lambda i, j: (layer, 0, j))],
        out_specs=pl.BlockSpec((tm, tn), lambda i, j: (i, j)),
        out_shape=jax.ShapeDtypeStruct((m, n), out_dtype),
        scratch_shapes=[pltpu.VMEM((tm, LANES), F32)],
        compiler_params=_params(("arbitrary", "arbitrary")),
        name=name,
    )(xw, ssq, w)


def _out_proj_kernel(*refs):
    *a_refs, w_ref, r_ref, nw_ref, o_ref, xw_ref, ssq_ref = refs
    j = pl.program_id(1)
    parts = []
    for rows in _row_chunks(o_ref.shape[0]):
        acc = r_ref[rows, :]
        k0 = 0
        for a_ref in a_refs:
            acc = acc + _dot(a_ref[rows, :], w_ref[k0:k0 + a_ref.shape[1], :])
            k0 += a_ref.shape[1]
        o_ref[rows, :] = acc
        xw_ref[rows, :] = (acc * nw_ref[...]).astype(xw_ref.dtype)
        parts.append(_lane_fold(acc * acc))
    part = jnp.concatenate(parts, axis=0)

    @pl.when(j == 0)
    def _():
        ssq_ref[0] = part

    @pl.when(j > 0)
    def _():
        ssq_ref[0] = ssq_ref[0] + part


def _out_proj(parts, w, layer, res, norm_w, tm, tn):
    m = res.shape[0]
    k, n = w.shape[1], w.shape[2]
    assert sum(p.shape[1] for p in parts) == k
    return pl.pallas_call(
        _out_proj_kernel,
        grid=(m // tm, n // tn),
        in_specs=[pl.BlockSpec((tm, p.shape[1]), lambda i, j: (i, 0), pipeline_mode=pl.Buffered(1)) for p in parts]
        + [pl.BlockSpec((None, k, tn), lambda i, j: (layer, 0, j)),
           pl.BlockSpec((tm, tn), lambda i, j: (i, j)),
           pl.BlockSpec((1, tn), lambda i, j: (0, j))],
        out_specs=[pl.BlockSpec((tm, tn), lambda i, j: (i, j)), pl.BlockSpec((tm, tn), lambda i, j: (i, j)),
                   pl.BlockSpec((1, tm, LANES), lambda i, j: (0, i, 0))],
        out_shape=[jax.ShapeDtypeStruct((m, n), F32), jax.ShapeDtypeStruct((m, n), BF16),
                   jax.ShapeDtypeStruct((1, m, LANES), F32)],
        compiler_params=_params(("arbitrary", "arbitrary")),
        name="out_proj",
    )(*parts, w, res, norm_w.reshape(1, n).astype(F32))


def _down_norm_kernel(a_ref, w_ref, r_ref, nw_ref, o_ref, xw_ref, ssq_ref):
    for rows in _row_chunks(o_ref.shape[0]):
        acc = r_ref[rows, :] + _dot(a_ref[rows, :], w_ref[...])
        o_ref[rows, :] = acc
        xw_ref[rows, :] = (acc * nw_ref[...]).astype(xw_ref.dtype)
        ssq_ref[rows, :] = _lane_fold(acc * acc)


def _down_proj_norm(a, w, layer, res, norm_w, tm, tn):
    m, k = a.shape
    n = w.shape[2]
    return pl.pallas_call(
        _down_norm_kernel,
        grid=(n // tn, m // tm),
        in_specs=[pl.BlockSpec((tm, k), lambda j, i: (i, 0)),
                  pl.BlockSpec((None, k, tn), lambda j, i: (layer, 0, j), pipeline_mode=pl.Buffered(1)),
                  pl.BlockSpec((tm, tn), lambda j, i: (i, j)),
                  pl.BlockSpec((1, tn), lambda j, i: (0, j))],
        out_specs=[pl.BlockSpec((tm, tn), lambda j, i: (i, j)), pl.BlockSpec((tm, tn), lambda j, i: (i, j)),
                   pl.BlockSpec((None, tm, LANES), lambda j, i: (j, i, 0))],
        out_shape=[jax.ShapeDtypeStruct((m, n), F32), jax.ShapeDtypeStruct((m, n), BF16),
                   jax.ShapeDtypeStruct((n // tn, m, LANES), F32)],
        compiler_params=_params(("arbitrary", "arbitrary")),
        name="ffn_down",
    )(a, w, res, norm_w.reshape(1, n).astype(F32))


def _matmul_residual_wres(a, w, layer, res, tm, tn, name):
    m, k = a.shape
    n = w.shape[2]
    return pl.pallas_call(
        _mm_res_kernel,
        grid=(n // tn, m // tm),
        in_specs=[pl.BlockSpec((tm, k), lambda j, i: (i, 0)),
                  pl.BlockSpec((None, k, tn), lambda j, i: (layer, 0, j), pipeline_mode=pl.Buffered(1)),
                  pl.BlockSpec((tm, tn), lambda j, i: (i, j))],
        out_specs=pl.BlockSpec((tm, tn), lambda j, i: (i, j)),
        out_shape=jax.ShapeDtypeStruct((m, n), F32),
        compiler_params=_params(("arbitrary", "arbitrary")),
        name=name,
    )(a, w, res)


def _gu_kernel(h_ref, ssq_ref, wg_ref, wu_ref, cw_ref, cb_ref, o_ref, ga_ref, ua_ref, gb_ref, ub_ref, halo_ref,
               rinv_ref, *, tm, nj, tiles_per_seq):
    t = pl.program_id(0)
    tp = jnp.maximum(t - 1, 0)
    ip = tp // nj
    jp = tp % nj
    rc = min(MM_ROW_CHUNK, tm)

    @pl.when(t == 0)
    def _():
        gb_ref[...] = jnp.zeros_like(gb_ref)
        ub_ref[...] = jnp.zeros_like(ub_ref)
        halo_ref[...] = jnp.zeros_like(halo_ref)

    @pl.when(t % nj == 0)
    def _():
        rinv_ref[...] = _rinv_lanes(ssq_ref[...], h_ref.shape[1])

    def step(g_cur, u_cur, g_prev, u_prev):
        g_prev[0:SUBLANES, :] = jnp.where(ip % tiles_per_seq == 0, 0.0, halo_ref[jp])
        halo_ref[jp] = g_prev[tm:tm + SUBLANES, :]
        cw = cw_ref[...]
        for r in range(tm // rc):
            hr = h_ref[r * rc:(r + 1) * rc, :]
            rinv = rinv_ref[r * rc:(r + 1) * rc, :]
            g_cur[SUBLANES + r * rc:SUBLANES + (r + 1) * rc, :] = _scale_rows(_dot(hr, wg_ref[...]), rinv)
            u_cur[r * rc:(r + 1) * rc, :] = _scale_rows(_dot(hr, wu_ref[...]), rinv)
            acc = cb_ref[...]
            for sh in range(FFN_CONV):
                acc = acc + cw[FFN_CONV - 1 - sh:FFN_CONV - sh, :] * g_prev[pl.ds(SUBLANES - sh + r * rc, rc), :]
            o_ref[r * rc:(r + 1) * rc, :] = (_silu(acc) * u_prev[r * rc:(r + 1) * rc, :]).astype(o_ref.dtype)

    @pl.when(t % 2 == 0)
    def _():
        step(ga_ref, ua_ref, gb_ref, ub_ref)

    @pl.when(t % 2 == 1)
    def _():
        step(gb_ref, ub_ref, ga_ref, ua_ref)


def _gate_up(h, ssq, w_gu, layer, conv_w, conv_b, seq, tm, tn):
    m, k = h.shape
    nf = conv_w.shape[1]
    nj = nf // tn
    last = (m // tm) * nj - 1
    cur = lambda t: jnp.minimum(t, last)
    prev = lambda t: jnp.maximum(t - 1, 0)
    kern = functools.partial(_gu_kernel, tm=tm, nj=nj, tiles_per_seq=seq // tm)
    return pl.pallas_call(
        kern,
        grid=(last + 2,),
        in_specs=[pl.BlockSpec((tm, k), lambda t: (cur(t) // nj, 0), pipeline_mode=pl.Buffered(1)),
                  pl.BlockSpec((ssq.shape[0], tm, LANES), lambda t: (0, cur(t) // nj, 0)),
                  pl.BlockSpec((None, k, tn), lambda t: (layer, 0, cur(t) % nj)),
                  pl.BlockSpec((None, k, tn), lambda t: (layer, 0, cur(t) % nj + nj)),
                  pl.BlockSpec((FFN_CONV, tn), lambda t: (0, prev(t) % nj)),
                  pl.BlockSpec((1, tn), lambda t: (0, prev(t) % nj))],
        out_specs=pl.BlockSpec((tm, tn), lambda t: (prev(t) // nj, prev(t) % nj)),
        out_shape=jax.ShapeDtypeStruct((m, nf), BF16),
        scratch_shapes=[pltpu.VMEM((tm + SUBLANES, tn), F32), pltpu.VMEM((tm, tn), F32),
                        pltpu.VMEM((tm + SUBLANES, tn), F32), pltpu.VMEM((tm, tn), F32),
                        pltpu.VMEM((nj, SUBLANES, tn), F32), pltpu.VMEM((tm, LANES), F32)],
        compiler_params=_params(("arbitrary",)),
        name="ffn_gate_up_conv",
    )(h, ssq, w_gu, w_gu, conv_w, conv_b.reshape(1, nf))


def _ssd_kernel(z_ref, xs_ref, b_ref, c_ref, dt_ref, cwx_ref, cwb_ref, cwc_ref, cbx_ref, cbb_ref, cbc_ref,
                dtb_ref, alog_ref, dsk_ref, nw_ref, tri_ref, o_ref,
                extx_ref, extb_ref, extc_ref, state_ref, *, ts):
    s = pl.program_id(1)

    @pl.when(s == 0)
    def _():
        extx_ref[...] = jnp.zeros_like(extx_ref)
        extb_ref[...] = jnp.zeros_like(extb_ref)
        extc_ref[...] = jnp.zeros_like(extc_ref)
        state_ref[...] = jnp.zeros_like(state_ref)

    def conv_silu(u_ref, ext_ref, w_ref, bias_ref):
        ext_ref[0:SUBLANES, :] = ext_ref[ts:ts + SUBLANES, :]
        ext_ref[SUBLANES:, :] = u_ref[...]
        w = w_ref[...]
        acc = bias_ref[...] + w[SSD_CONV - 1:SSD_CONV, :] * u_ref[...]
        for sh in range(1, SSD_CONV):
            acc = acc + w[SSD_CONV - 1 - sh:SSD_CONV - sh, :] * ext_ref[pl.ds(SUBLANES - sh, ts), :]
        return _silu(acc)

    xs = conv_silu(xs_ref, extx_ref, cwx_ref, cbx_ref)
    bm = conv_silu(b_ref, extb_ref, cwb_ref, cbb_ref)
    cm = conv_silu(c_ref, extc_ref, cwc_ref, cbc_ref)
    dtr = dt_ref[...] + dtb_ref[...]
    dt = jnp.maximum(dtr, 0.0) + jnp.log1p(jnp.exp(-jnp.abs(dtr)))
    a = dt * (-jnp.exp(alog_ref[...]))
    z = z_ref[...]
    tri = tri_ref[...]
    dsk = dsk_ref[...]
    nw = nw_ref[...]
    li = lax.broadcasted_iota(jnp.int32, (SSD_CHUNK, SSD_CHUNK), 0)
    si = lax.broadcasted_iota(jnp.int32, (SSD_CHUNK, SSD_CHUNK), 1)
    causal = li >= si
    low_half = si < SSD_HEAD_DIM
    low_half_row = low_half[0:1, :]

    for ck in range(ts // SSD_CHUNK):
        r0 = ck * SSD_CHUNK
        a_c = a[r0:r0 + SSD_CHUNK]
        hi, mid, lo = _split3(a_c)
        acs = _dot(tri, hi) + _dot(tri, mid) + _dot(tri, lo)
        acs_t = acs.T
        dt_t = dt[r0:r0 + SSD_CHUNK].T
        a_last_b = jnp.broadcast_to(acs_t[:, SSD_CHUNK - 1:SSD_CHUNK], acs_t.shape)
        dtd_t = dt_t * jnp.exp(a_last_b - acs_t)
        chunk_decay = jnp.exp(acs[SSD_CHUNK - 1:SSD_CHUNK, :])
        for g in range(SSD_GROUPS):
            cg = cm[r0:r0 + SSD_CHUNK, g * SSD_STATE:(g + 1) * SSD_STATE]
            bg = bm[r0:r0 + SSD_CHUNK, g * SSD_STATE:(g + 1) * SSD_STATE]
            cb = _dot_nt(cg.astype(BF16), bg.astype(BF16))
            bgt = bg.T
            ys = []
            for pr in range(SSD_HPG // 2):
                pair = (g * SSD_HPG) // 2 + pr
                c0 = pair * LANES
                xs_p = xs[r0:r0 + SSD_CHUNK, c0:c0 + LANES].astype(BF16)
                prev = state_ref[pair]
                rhs = jnp.concatenate([xs_p, prev.astype(BF16)], axis=0)
                y_h, st_h, cd_h = [], [], []
                for hh in (2 * pair, 2 * pair + 1):
                    colb = jnp.broadcast_to(acs[:, hh:hh + 1], (SSD_CHUNK, SSD_CHUNK))
                    seg = colb - acs_t[hh:hh + 1, :]
                    decay = jnp.where(causal, jnp.exp(jnp.where(causal, seg, 0.0)), 0.0)
                    intra = cb * decay * dt_t[hh:hh + 1, :]
                    inter = cg * jnp.exp(colb)
                    lhs = jnp.concatenate([intra, inter], axis=1).astype(BF16)
                    y_h.append(_dot(lhs, rhs))
                    st_h.append(_dot((bgt * dtd_t[hh:hh + 1, :]).astype(BF16), xs_p))
                    cd_h.append(jnp.broadcast_to(chunk_decay[:, hh:hh + 1], (1, LANES)))
                ys.append(jnp.where(low_half, y_h[0], y_h[1]))
                state_ref[pair] = (prev * jnp.where(low_half_row, cd_h[0], cd_h[1])
                                   + jnp.where(low_half, st_h[0], st_h[1]))
            yg = jnp.concatenate(ys, axis=1)
            yg = yg + xs[r0:r0 + SSD_CHUNK, g * SSD_GW:(g + 1) * SSD_GW] * dsk[:, g * SSD_GW:(g + 1) * SSD_GW]
            u = yg * _silu(z[r0:r0 + SSD_CHUNK, g * SSD_GW:(g + 1) * SSD_GW])
            u = u * lax.rsqrt(jnp.mean(u * u, axis=-1, keepdims=True) + EPS)
            o_ref[r0:r0 + SSD_CHUNK, g * SSD_GW:(g + 1) * SSD_GW] = (
                u * nw[:, g * SSD_GW:(g + 1) * SSD_GW]).astype(o_ref.dtype)


def _pad_lanes(v, n=LANES):
    v = v.reshape(1, -1).astype(F32)
    return jnp.pad(v, ((0, 0), (0, n - v.shape[1])))


def _ssd_mixer(proj, bsz, seq, conv_w, conv_b, dt_bias, a_log, d_skip, norm_w, ts=256):
    nsteps = seq // ts
    row = lambda b, s: b * nsteps + s
    xw, bw, cw = conv_w[:, :SSD_WIDTH], conv_w[:, SSD_WIDTH:SSD_WIDTH + 512], conv_w[:, SSD_WIDTH + 512:]
    xb, bb, cb = conv_b[:SSD_WIDTH], conv_b[SSD_WIDTH:SSD_WIDTH + 512], conv_b[SSD_WIDTH + 512:]
    tri = jnp.asarray(np.tril(np.ones((SSD_CHUNK, SSD_CHUNK), np.float32)), BF16)
    const = lambda shape: pl.BlockSpec(shape, lambda b, s: (0,) * len(shape))
    kern = functools.partial(_ssd_kernel, ts=ts)
    return pl.pallas_call(
        kern,
        grid=(bsz, nsteps),
        in_specs=[pl.BlockSpec((ts, SSD_WIDTH), lambda b, s: (row(b, s), COL_Z // SSD_WIDTH)),
                  pl.BlockSpec((ts, SSD_WIDTH), lambda b, s: (row(b, s), COL_XS // SSD_WIDTH)),
                  pl.BlockSpec((ts, 512), lambda b, s: (row(b, s), COL_B // 512)),
                  pl.BlockSpec((ts, 512), lambda b, s: (row(b, s), COL_C // 512)),
                  pl.BlockSpec((ts, LANES), lambda b, s: (row(b, s), COL_DT // LANES)),
                  const((SSD_CONV, SSD_WIDTH)), const((SSD_CONV, 512)), const((SSD_CONV, 512)),
                  const((1, SSD_WIDTH)), const((1, 512)), const((1, 512)),
                  const((1, LANES)), const((1, LANES)), const((1, SSD_WIDTH)), const((1, SSD_WIDTH)),
                  const((SSD_CHUNK, SSD_CHUNK))],
        out_specs=pl.BlockSpec((ts, SSD_WIDTH), lambda b, s: (row(b, s), 0)),
        out_shape=jax.ShapeDtypeStruct((bsz * seq, SSD_WIDTH), BF16),
        scratch_shapes=[pltpu.VMEM((ts + SUBLANES, SSD_WIDTH), F32), pltpu.VMEM((ts + SUBLANES, 512), F32),
                        pltpu.VMEM((ts + SUBLANES, 512), F32),
                        pltpu.VMEM((SSD_HEADS // 2, SSD_STATE, 2 * SSD_HEAD_DIM), F32)],
        compiler_params=_params(("arbitrary", "arbitrary")),
        name="ssd_mixer",
    )(proj, proj, proj, proj, proj, xw, bw, cw, xb.reshape(1, -1), bb.reshape(1, -1), cb.reshape(1, -1),
      _pad_lanes(dt_bias), _pad_lanes(a_log), jnp.repeat(d_skip.astype(F32), SSD_HEAD_DIM).reshape(1, -1),
      norm_w.reshape(1, -1), tri)


def _hgrn_kernel(q_ref, f_ref, i_ref, g_ref, lbl_ref, nw_ref, cum_ref, o_ref, state_ref, *, ts, layer):
    s = pl.program_id(1)

    @pl.when(s == 0)
    def _():
        state_ref[...] = jnp.zeros_like(state_ref)

    nck = ts // HGRN_CHUNK
    lg = lbl_ref[...]
    e = jnp.exp(lg - jnp.max(lg, axis=0, keepdims=True))
    sm = e / jnp.sum(e, axis=0, keepdims=True)
    ridx = lax.broadcasted_iota(jnp.int32, lg.shape, 0)
    lb = jnp.sum(jnp.where((ridx >= 1) & (ridx <= layer), sm, 0.0), axis=0, keepdims=True)

    cum = cum_ref[...]
    nw = nw_ref[...]
    ti = lax.broadcasted_iota(jnp.int32, (ts, ts), 0)
    tj = lax.broadcasted_iota(jnp.int32, (ts, ts), 1)
    blockcausal = (ti // HGRN_CHUNK == tj // HGRN_CHUNK) & (ti >= tj)
    tok = lax.broadcasted_iota(jnp.int32, (1, ts), 1)

    def head(h):
        c0 = h * HGRN_DIM
        lbh = lb[:, c0:c0 + HGRN_DIM]
        f = lbh + (1.0 - lbh) * _sigmoid(f_ref[:, c0:c0 + HGRN_DIM])
        logf = jnp.log(f)
        k = 1.0 - f
        q = _silu(q_ref[:, c0:c0 + HGRN_DIM])
        v = i_ref[:, c0:c0 + HGRN_DIM]
        hi, mid, lo = _split3(logf)
        b = _dot(cum, hi) + _dot(cum, mid) + _dot(cum, lo)
        yield
        b3 = b.reshape(nck, HGRN_CHUNK, HGRN_DIM)
        bref = jnp.broadcast_to(b3[:, HGRN_CHUNK // 2:HGRN_CHUNK // 2 + 1, :], b3.shape).reshape(ts, HGRN_DIM)
        blast = jnp.broadcast_to(b3[:, HGRN_CHUNK - 1:HGRN_CHUNK, :], b3.shape).reshape(ts, HGRN_DIM)
        qe = (q * jnp.exp(b - bref)).astype(BF16)
        ke = (k * jnp.exp(bref - b)).astype(BF16)
        kl = (k * jnp.exp(blast - b)).astype(BF16)
        qb = (q * jnp.exp(b)).astype(BF16)
        vb = v.astype(BF16)
        att = _dot_nt(qe, ke)
        vt = v.T
        lhs = jnp.concatenate(
            [jnp.where(tok // HGRN_CHUNK == c, vt, 0.0) for c in range(nck)], axis=0).astype(BF16)
        st = _dot(lhs, kl)
        yield
        o = _dot(jnp.where(blockcausal, att, 0.0).astype(BF16), vb)
        state = state_ref[h]
        outs = []
        for c in range(nck):
            t0 = c * HGRN_CHUNK
            outs.append(_dot_nt(qb[t0:t0 + HGRN_CHUNK], state.astype(BF16)))
            cd = jnp.exp(blast[t0:t0 + 1, :])
            state = state * cd + st[c * HGRN_DIM:(c + 1) * HGRN_DIM]
            if c % 2 == 1:
                yield
        state_ref[h] = state
        o = o + jnp.concatenate(outs, axis=0)
        o = o * lax.rsqrt(jnp.mean(o * o, axis=-1, keepdims=True) + EPS) * nw
        o_ref[:, c0:c0 + HGRN_DIM] = (o * _silu(g_ref[:, c0:c0 + HGRN_DIM])).astype(o_ref.dtype)

    for h0 in range(0, HGRN_HEADS, HGRN_HEADS_INTERLEAVED):
        live = [head(h) for h in range(h0, h0 + HGRN_HEADS_INTERLEAVED)]
        while live:
            live = [gen for gen in live if next(gen, live) is not live]


def _hgrn_cum_matrix(ts):
    t = np.arange(ts)
    same = (t[:, None] // HGRN_CHUNK) == (t[None, :] // HGRN_CHUNK)
    return (same & (t[None, :] <= t[:, None])).astype(np.float32)


def _hgrn_mixer(proj, bsz, seq, lb_logits, norm_w, layer, ts=256):
    nsteps = seq // ts
    row = lambda b, s: b * nsteps + s
    depth = lb_logits.shape[0]
    cum = jnp.asarray(_hgrn_cum_matrix(ts), BF16)
    kern = functools.partial(_hgrn_kernel, ts=ts, layer=layer)
    blk = lambda col: pl.BlockSpec((ts, HGRN_WIDTH), lambda b, s: (row(b, s), col // HGRN_WIDTH))
    return pl.pallas_call(
        kern,
        grid=(bsz, nsteps),
        in_specs=[blk(COL_HQ), blk(COL_HF), blk(COL_HI), blk(COL_HG),
                  pl.BlockSpec((depth, HGRN_WIDTH), lambda b, s: (0, 0)),
                  pl.BlockSpec((1, HGRN_DIM), lambda b, s: (0, 0)),
                  pl.BlockSpec((ts, ts), lambda b, s: (0, 0))],
        out_specs=pl.BlockSpec((ts, HGRN_WIDTH), lambda b, s: (row(b, s), 0)),
        out_shape=jax.ShapeDtypeStruct((bsz * seq, HGRN_WIDTH), BF16),
        scratch_shapes=[pltpu.VMEM((HGRN_HEADS, HGRN_DIM, HGRN_DIM), F32)],
        compiler_params=_params(("arbitrary", "arbitrary")),
        name="hgrn2_mixer",
    )(proj, proj, proj, proj, lb_logits.astype(F32), norm_w.reshape(1, HGRN_DIM).astype(F32), cum)


def _rel_bucket_np(dist):
    max_exact = REL_BUCKETS // 2
    d = np.maximum(dist, 0)
    ratio = np.maximum(d, 1).astype(np.float32) / np.float32(max_exact)
    log_ratio = np.log(ratio).astype(np.float32) / np.float32(math.log(REL_MAX_DIST / max_exact))
    large = np.minimum(max_exact + (log_ratio * np.float32(REL_BUCKETS - max_exact)).astype(np.int32),
                       REL_BUCKETS - 1)
    return np.where(d < max_exact, d, large).astype(np.int32)


def _bias_expand_kernel(rb_ref, bmap_ref, o_ref):
    h = pl.program_id(0)
    bm = bmap_ref[...]
    out = jnp.full(bm.shape, NEG_INF, F32)
    for k in range(REL_BUCKETS):
        out = jnp.where(bm == k, rb_ref[k, h] * LOG2E, out)
    o_ref[...] = out


def _bias_expand(rel_bias, bmap, tr):
    rows, cols = bmap.shape
    return pl.pallas_call(
        _bias_expand_kernel,
        grid=(NSA_HEADS, rows // tr),
        in_specs=[pl.BlockSpec(memory_space=pltpu.SMEM),
                  pl.BlockSpec((tr, cols), lambda h, i: (i, 0))],
        out_specs=pl.BlockSpec((None, tr, cols), lambda h, i: (h, i, 0)),
        out_shape=jax.ShapeDtypeStruct((NSA_HEADS, rows, cols), F32),
        compiler_params=_params(("arbitrary", "arbitrary")),
        name="nsa_bias_expand",
    )(rel_bias.astype(F32), bmap)


def _nsa_bias_tables(rel_bias, seq):
    ncp = seq // CMP_STRIDE
    t = np.arange(seq)[:, None]
    cmp_end = np.arange(ncp)[None, :] * CMP_STRIDE + CMP_BLOCK - 1
    d = t - cmp_end
    bmap_cmp = np.where(d >= 0, _rel_bucket_np(d), -1).astype(np.int32)
    l = np.arange(Q_BLOCK)[:, None]
    j = np.arange(WIN_KEYS)[None, :]
    dist = l - j + WINDOW
    win = np.where((dist >= 0) & (dist < WINDOW), _rel_bucket_np(dist), -1)
    far = _rel_bucket_np(np.arange(Q_BLOCK + 1, 8 * seq))
    assert (far == far[0]).all(), "distances beyond one query block must share a single bucket"
    bmap_tab = np.concatenate([win, np.full((Q_BLOCK, Q_BLOCK), far[0])], axis=1).astype(np.int32)
    bias_cmp = _bias_expand(rel_bias, jnp.asarray(bmap_cmp), 512)
    bias_tab = _bias_expand(rel_bias, jnp.asarray(bmap_tab), Q_BLOCK)
    return bias_cmp, bias_tab


def _cmp_kernel(u_ref, pe_ref, w1_ref, w2_ref, o_ref, *, ncp):
    half = CMP_BLOCK // 2
    pe = pe_ref[...]
    top = jnp.zeros((ncp, CMP_HIDDEN), F32)
    bot = jnp.zeros((ncp, CMP_HIDDEN), F32)
    for l in range(half):
        x = u_ref[pl.ds(l, ncp, stride=half), :]
        top = top + _dot((x + pe[l:l + 1, :]).astype(BF16), w1_ref[l])
        bot = bot + _dot((x + pe[half + l:half + l + 1, :]).astype(BF16), w1_ref[half + l])
    hid = top + pltpu.roll(bot, ncp - 1, 0)
    o_ref[...] = _dot(_silu(hid).astype(BF16), w2_ref[...])


def _nsa_compress(proj, bsz, seq, pe, w1, w2):
    ncp = seq // CMP_STRIDE
    kern = functools.partial(_cmp_kernel, ncp=ncp)
    return pl.pallas_call(
        kern,
        grid=(bsz, 2, NSA_KV_HEADS),
        in_specs=[pl.BlockSpec((seq, NSA_HEAD_DIM), lambda b, t, h: (b, COL_KC // NSA_HEAD_DIM + NSA_KV_HEADS * t + h)),
                  pl.BlockSpec((None, CMP_BLOCK, NSA_HEAD_DIM), lambda b, t, h: (t, 0, 0)),
                  pl.BlockSpec((None, CMP_BLOCK, NSA_HEAD_DIM, CMP_HIDDEN), lambda b, t, h: (t, 0, 0, 0)),
                  pl.BlockSpec((None, CMP_HIDDEN, NSA_HEAD_DIM), lambda b, t, h: (t, 0, 0))],
        out_specs=pl.BlockSpec((None, None, None, ncp, NSA_HEAD_DIM), lambda b, t, h: (b, t, h, 0, 0)),
        out_shape=jax.ShapeDtypeStruct((bsz, 2, NSA_KV_HEADS, ncp, NSA_HEAD_DIM), F32),
        compiler_params=_params(("arbitrary", "arbitrary", "arbitrary")),
        name="nsa_compress",
    )(proj, pe.astype(F32), w1.reshape(2, CMP_BLOCK, NSA_HEAD_DIM, CMP_HIDDEN).astype(BF16), w2.astype(BF16))


def _softmax_start(s, v):
    m = jnp.max(s, axis=-1, keepdims=True)
    p = jnp.exp2(s - m)
    return m, jnp.sum(p, axis=-1, keepdims=True), _dot(p.astype(BF16), v)


def _softmax_update(s, v, carry):
    m, l, acc = carry
    m_new = jnp.maximum(m, jnp.max(s, axis=-1, keepdims=True))
    alpha = jnp.exp2(m - m_new)
    p = jnp.exp2(s - m_new)
    return m_new, alpha * l + jnp.sum(p, axis=-1, keepdims=True), alpha * acc + _dot(p.astype(BF16), v)


def _softmax_finish(carry):
    m, l, acc = carry
    return jnp.where(m > 0.5 * NEG_INF, acc / jnp.maximum(l, 1e-30), 0.0)


def _nsa_kernel(q_ref, gate_ref, kc_ref, vc_ref, ks_ref, vs_ref, kw_ref, vw_ref, bcmp_ref, btab_ref,
                kone_ref, c2s_ref, o_ref, kaug_ref, vsb_ref, kwb_ref, vwb_ref, sa_ref, sb_ref, *, n_sel, nqb):
    hk = pl.program_id(1)
    c0 = pl.program_id(2) * nqb
    rows = NSA_GQA * Q_BLOCK
    lane = lax.broadcasted_iota(jnp.int32, (Q_BLOCK, LANES), 1)

    @pl.when(c0 == 0)
    def _():
        kaug_ref[0:Q_BLOCK, 0:NSA_HEAD_DIM] = jnp.zeros((Q_BLOCK, NSA_HEAD_DIM), BF16)
        kaug_ref[Q_BLOCK:, 0:NSA_HEAD_DIM] = ks_ref[...].astype(BF16)
        kaug_ref[:, NSA_HEAD_DIM:] = kone_ref[...]
        vsb_ref[0:Q_BLOCK, :] = jnp.zeros((Q_BLOCK, NSA_HEAD_DIM), BF16)
        vsb_ref[Q_BLOCK:, :] = vs_ref[...].astype(BF16)
        kwb_ref[0:WINDOW, 0:NSA_HEAD_DIM] = jnp.zeros((WINDOW, NSA_HEAD_DIM), BF16)
        kwb_ref[WINDOW:, 0:NSA_HEAD_DIM] = kw_ref[...].astype(BF16)
        for r0 in range(0, WINDOW, Q_BLOCK):
            kwb_ref[r0:r0 + Q_BLOCK, NSA_HEAD_DIM:] = jnp.where(lane == AUG_PAD, 1.0, 0.0).astype(BF16)
        kwb_ref[WINDOW:, NSA_HEAD_DIM:] = jnp.zeros((kw_ref.shape[0], LANES), BF16)
        vwb_ref[0:WINDOW, :] = jnp.zeros((WINDOW, NSA_HEAD_DIM), BF16)
        vwb_ref[WINDOW:, :] = vw_ref[...].astype(BF16)

    gens = [_nsa_unit(c0 + u, q_ref[u * Q_BLOCK:(u + 1) * Q_BLOCK, :], bcmp_ref[:, u * Q_BLOCK:(u + 1) * Q_BLOCK, :],
                      btab_ref, kc_ref, vc_ref, c2s_ref, kaug_ref, vsb_ref, kwb_ref, vwb_ref, n_sel)
            for u in range(nqb)]
    units = [None] * nqb
    while any(un is None for un in units):
        for u, gen in enumerate(gens):
            if units[u] is None:
                units[u] = next(gen)
    qaug_far = jnp.concatenate([un[3] for un in units], axis=0)
    carry = tuple(jnp.concatenate([un[2][i] for un in units], axis=0) for i in range(3))

    far_blocks = FAR_KEYS // Q_BLOCK
    n_slabs = (c0 + nqb - 2 + far_blocks - 1) // far_blocks
    n_pairs = n_slabs // 2
    last_slab = (kaug_ref.shape[0] - Q_BLOCK) // FAR_KEYS - 1

    def far_logits(slab):
        r0 = pl.multiple_of(jnp.minimum(slab, last_slab) * FAR_KEYS + Q_BLOCK, Q_BLOCK)
        return _dot_nt(qaug_far, kaug_ref[pl.ds(r0, FAR_KEYS), :])

    def far_values(slab):
        return vsb_ref[pl.ds(pl.multiple_of(slab * FAR_KEYS + Q_BLOCK, Q_BLOCK), FAR_KEYS), :]

    sa_ref[...] = far_logits(0)

    def far_body(k, carry):
        sb_ref[...] = far_logits(2 * k + 1)
        carry = _softmax_update(sa_ref[...], far_values(2 * k), carry)
        sa_ref[...] = far_logits(2 * k + 2)
        return _softmax_update(sb_ref[...], far_values(2 * k + 1), carry)

    carry = lax.fori_loop(0, n_pairs, far_body, carry)
    carry = lax.cond(n_slabs % 2 == 1,
                     lambda cr: _softmax_update(sa_ref[...], far_values(2 * n_pairs), cr), lambda cr: cr, carry)
    o_sel = _softmax_finish(carry)

    for u in range(nqb):
        o_cmp, o_win = units[u][0], units[u][1]
        gates = _sigmoid(gate_ref[u * Q_BLOCK:(u + 1) * Q_BLOCK, :])
        for g in range(NSA_GQA):
            def gate_col(br):
                return jnp.sum(jnp.where(lane == hk * (3 * NSA_GQA) + g * 3 + br, gates, 0.0), axis=-1, keepdims=True)

            r0 = g * Q_BLOCK
            out = (gate_col(0) * o_cmp[r0:r0 + Q_BLOCK] + gate_col(1) * o_sel[u * rows + r0:u * rows + r0 + Q_BLOCK]
                   + gate_col(2) * o_win[r0:r0 + Q_BLOCK])
            o_ref[u * Q_BLOCK:(u + 1) * Q_BLOCK, g * NSA_HEAD_DIM:(g + 1) * NSA_HEAD_DIM] = out.astype(o_ref.dtype)


def _nsa_unit(c, q_in, bc_in, btab_ref, kc_ref, vc_ref, c2s_ref, kaug_ref, vsb_ref, kwb_ref, vwb_ref, n_sel):
    rows = NSA_GQA * Q_BLOCK
    lane = lax.broadcasted_iota(jnp.int32, (Q_BLOCK, LANES), 1)
    q = q_in * (NSA_HEAD_DIM ** -0.5 * LOG2E)
    q3 = jnp.concatenate([q[:, g * NSA_HEAD_DIM:(g + 1) * NSA_HEAD_DIM] for g in range(NSA_GQA)],
                         axis=0).astype(BF16)

    btab = btab_ref[...].reshape(rows, WIN_KEYS + Q_BLOCK)
    near0 = pl.multiple_of(c * Q_BLOCK, Q_BLOCK)

    qaug_win = jnp.concatenate(
        [q3, jnp.concatenate([jnp.where(lane == AUG_PAD, NEG_INF, 0.0).astype(BF16)] * NSA_GQA, axis=0)], axis=1)
    s_win = _dot_nt(qaug_win, kwb_ref[pl.ds(near0, WIN_KEYS), :]) + btab[:, 0:WIN_KEYS]
    yield None

    bc = bc_in.reshape(rows, bc_in.shape[-1])
    valid = bc > 0.5 * NEG_INF
    s = _dot_nt(q3, kc_ref[...].astype(BF16)) + bc
    yield None
    o_win = _softmax_finish(_softmax_start(s_win, vwb_ref[pl.ds(near0, WIN_KEYS), :]))
    yield None
    m = jnp.max(s, axis=-1, keepdims=True)
    p = jnp.where(valid, jnp.exp2(s - m), 0.0)
    p = p / jnp.maximum(jnp.sum(p, axis=-1, keepdims=True), 1e-30)
    o_cmp = _dot(p.astype(BF16), vc_ref[...].astype(BF16))

    psum = p[0:Q_BLOCK] + p[Q_BLOCK:2 * Q_BLOCK] + p[2 * Q_BLOCK:3 * Q_BLOCK]
    ph = psum.astype(BF16)
    plo = (psum - ph.astype(F32)).astype(BF16)
    c2s = c2s_ref[...]
    imp_t = _dot_nt(c2s, ph) + _dot_nt(c2s, plo)
    nsp = -(-n_sel // SUBLANES) * SUBLANES
    imp_t = imp_t[0:nsp]
    tq = c * Q_BLOCK + lax.broadcasted_iota(jnp.int32, (nsp, Q_BLOCK), 1)
    jj = lax.broadcasted_iota(jnp.int32, (nsp, Q_BLOCK), 0)
    valid_sel = (jj < n_sel) & (jj * SEL_BLOCK <= tq)
    back = tq // SEL_BLOCK - jj
    force = (jj == 0) | ((back >= 0) & (back < SEL_LOCAL))
    score = jnp.where(valid_sel, imp_t + jnp.where(force, FORCE_BONUS, 0.0), NEG_INF)
    yield None
    n_tiles = nsp // SUBLANES
    tiles = [score[v * SUBLANES:(v + 1) * SUBLANES] for v in range(n_tiles)]
    sub = lax.broadcasted_iota(jnp.int32, (SUBLANES, Q_BLOCK), 0)
    cnts = [jnp.zeros((SUBLANES, Q_BLOCK), F32) for _ in range(n_tiles)]
    for jp in range(n_sel):
        v0, r0 = divmod(jp, SUBLANES)
        r = tiles[v0][r0:r0 + 1, :]
        for v in range(n_tiles):
            gt = jnp.where(r > tiles[v], 1.0, 0.0)
            ge = jnp.where(r >= tiles[v], 1.0, 0.0)
            cnts[v] = cnts[v] + (gt if v < v0 else ge if v > v0 else jnp.where(sub > r0, ge, gt))
        if jp % (2 * SUBLANES) == 2 * SUBLANES - 1:
            yield None
    cnt = jnp.concatenate(cnts, axis=0)
    keep = valid_sel & (cnt < float(min(SEL_TOPK, n_sel)))

    ext = lax.broadcasted_iota(jnp.int32, (LANES - nsp, Q_BLOCK), 0) + nsp
    aug = jnp.concatenate([jnp.where(keep, 0.0, NEG_INF), jnp.where(ext == AUG_PAD, NEG_INF, 0.0)], axis=0)
    aug_near = aug.T
    aug_far = jnp.where((lane >= 2 * (c - 1)) & (lane < AUG_BIAS_HI), NEG_INF, aug_near)
    far_parts = []
    for g in range(NSA_GQA):
        bfar = btab[g * Q_BLOCK:(g + 1) * Q_BLOCK, WIN_KEYS:]
        hi = bfar.astype(BF16).astype(F32)
        far_parts.append(jnp.where(lane == AUG_BIAS_HI, hi, jnp.where(lane == AUG_BIAS_LO, bfar - hi, aug_far)))
    qaug_near = jnp.concatenate([q3, jnp.concatenate([aug_near.astype(BF16)] * NSA_GQA, axis=0)], axis=1)
    qaug_far = jnp.concatenate([q3, jnp.concatenate(far_parts, axis=0).astype(BF16)], axis=1)

    s = _dot_nt(qaug_near, kaug_ref[pl.ds(near0, NEAR_KEYS), :]) + btab[:, WIN_KEYS - NEAR_KEYS:WIN_KEYS]
    yield None
    carry = _softmax_start(s, vsb_ref[pl.ds(near0, NEAR_KEYS), :])
    yield o_cmp, o_win, carry, qaug_far


def _nsa_attention(proj, cmp_kv, bias_cmp, bias_tab, bsz, seq):
    nq = seq // Q_BLOCK
    ncp = seq // CMP_STRIDE
    n_sel = seq // SEL_BLOCK
    assert n_sel <= AUG_BIAS_HI and NSA_HEAD_DIM == LANES
    assert seq % (2 * FAR_KEYS) == 0, "far steps walk the keys two 512-key slabs at a time"
    key = np.arange(-Q_BLOCK, seq)[:, None]
    ln = np.arange(LANES)[None, :]
    kone = np.where(key < 0, ln == AUG_PAD,
                    ((ln < AUG_BIAS_HI) & (key // SEL_BLOCK == ln)) | (ln == AUG_BIAS_HI) | (ln == AUG_BIAS_LO))
    kone = jnp.asarray(kone, BF16)
    c_start = np.arange(ncp)[None, :] * CMP_STRIDE
    s_start = np.arange(LANES)[:, None] * SEL_BLOCK
    overlap = np.clip(np.minimum(c_start + CMP_BLOCK, s_start + SEL_BLOCK) - np.maximum(c_start, s_start), 0, None)
    overlap = np.where(np.arange(LANES)[:, None] < n_sel, overlap, 0)
    c2s_t = jnp.asarray(overlap / CMP_BLOCK, BF16)
    gqa_w = NSA_GQA * NSA_HEAD_DIM
    full = lambda col: pl.BlockSpec((seq, NSA_HEAD_DIM), lambda b, h, c: (b, col // NSA_HEAD_DIM + h))
    nqb = NSA_Q_BLOCKS_PER_STEP
    ns = nq // nqb
    qr = nqb * Q_BLOCK
    kern = functools.partial(_nsa_kernel, n_sel=n_sel, nqb=nqb)
    return pl.pallas_call(
        kern,
        grid=(bsz, NSA_KV_HEADS, ns),
        in_specs=[pl.BlockSpec((qr, gqa_w), lambda b, h, c: (b * ns + c, COL_Q // gqa_w + h)),
                  pl.BlockSpec((qr, LANES), lambda b, h, c: (b * ns + c, COL_GATE // LANES)),
                  pl.BlockSpec((None, None, None, ncp, NSA_HEAD_DIM), lambda b, h, c: (b, 0, h, 0, 0)),
                  pl.BlockSpec((None, None, None, ncp, NSA_HEAD_DIM), lambda b, h, c: (b, 1, h, 0, 0)),
                  full(COL_KS), full(COL_VS), full(COL_KW), full(COL_VW),
                  pl.BlockSpec((NSA_GQA, qr, ncp), lambda b, h, c: (h, c, 0)),
                  pl.BlockSpec((NSA_GQA, Q_BLOCK, WIN_KEYS + Q_BLOCK), lambda b, h, c: (h, 0, 0)),
                  pl.BlockSpec((seq + Q_BLOCK, LANES), lambda b, h, c: (0, 0)),
                  pl.BlockSpec((LANES, ncp), lambda b, h, c: (0, 0))],
        out_specs=pl.BlockSpec((qr, gqa_w), lambda b, h, c: (b * ns + c, h)),
        out_shape=jax.ShapeDtypeStruct((bsz * seq, NSA_WIDTH), BF16),
        scratch_shapes=[pltpu.VMEM((seq + Q_BLOCK, 2 * NSA_HEAD_DIM), BF16),
                        pltpu.VMEM((seq + Q_BLOCK, NSA_HEAD_DIM), BF16),
                        pltpu.VMEM((seq + WINDOW, 2 * NSA_HEAD_DIM), BF16),
                        pltpu.VMEM((seq + WINDOW, NSA_HEAD_DIM), BF16),
                        pltpu.VMEM((nqb * NSA_GQA * Q_BLOCK, FAR_KEYS), F32),
                        pltpu.VMEM((nqb * NSA_GQA * Q_BLOCK, FAR_KEYS), F32)],
        compiler_params=_params(("arbitrary", "arbitrary", "arbitrary")),
        name="nsa_attention",
    )(proj, proj, cmp_kv, cmp_kv, proj, proj, proj, proj, bias_cmp, bias_tab, kone, c2s_t)


_IN_OFF = tuple(int(v) for v in np.cumsum((0,) + IN_SIZES))
_W_IN_COPIES = ((_IN_OFF[0], SSD_WIDTH, COL_Z), (_IN_OFF[1], SSD_WIDTH, COL_XS),
                (_IN_OFF[1] + SSD_WIDTH, 512, COL_B), (_IN_OFF[1] + SSD_WIDTH + 512, 512, COL_C),
                (_IN_OFF[3], NSA_WIDTH, COL_Q), (_IN_OFF[4], 6 * NSA_KV_WIDTH, COL_KC),
                (_IN_OFF[11], 4 * HGRN_WIDTH, COL_HQ))
_W_IN_NARROW = ((_IN_OFF[2], IN_SIZES[2], COL_DT, COL_GATE - COL_DT), (_IN_OFF[10], IN_SIZES[10], COL_GATE, COL_HQ - COL_GATE))


def _relayout_kernel(w_ref, o_ref):
    for src, width, dst in _W_IN_COPIES:
        o_ref[:, dst:dst + width] = w_ref[:, src:src + width].astype(BF16)
    for src, valid, dst, padded in _W_IN_NARROW:
        tile = w_ref[:, src:src + LANES]
        lane = lax.broadcasted_iota(jnp.int32, tile.shape, 1)
        o_ref[:, dst:dst + LANES] = jnp.where(lane < valid, tile, 0.0).astype(BF16)
        if padded > LANES:
            o_ref[:, dst + LANES:dst + padded] = jnp.zeros((tile.shape[0], padded - LANES), BF16)


def _relayout_w_in(w, tr=128):
    depth, d, n = w.shape
    return pl.pallas_call(
        _relayout_kernel,
        grid=(depth, d // tr),
        in_specs=[pl.BlockSpec((None, tr, n), lambda l, i: (l, i, 0))],
        out_specs=pl.BlockSpec((None, tr, IN_PAD), lambda l, i: (l, i, 0)),
        out_shape=jax.ShapeDtypeStruct((depth, d, IN_PAD), BF16),
        compiler_params=_params(("parallel", "parallel")),
        name="w_in_relayout",
    )(w)


def _tiles(seq):
    return dict(in_proj=(2048, 1024), out_proj=(1024, 1024), gate_up=(min(2048, seq), 256), down=(256, 1024))


def kernel(x, norm_mix_w, w_in, ssd_conv_w, ssd_conv_b, ssd_dt_bias, ssd_a_log, ssd_d, ssd_norm_w, nsa_cmp_pe, nsa_cmp_w1, nsa_cmp_w2, rel_bias, hgrn_lb_logits, hgrn_norm_w, w_out, norm_ffn_w, ffn_w_gu, ffn_conv_w, ffn_conv_b, ffn_w_down, norm_f_w):
    bsz, seq, d = x.shape
    depth = w_in.shape[0]
    xr = x.reshape(bsz * seq, d).astype(F32)
    bias_cmp, bias_tab = _nsa_bias_tables(rel_bias, seq)
    tiles = _tiles(seq)
    w_in_b = _relayout_w_in(jnp.pad(w_in.astype(BF16), ((0, 0), (0, 0), (0, -w_in.shape[2] % LANES))))
    w_out_b = w_out.astype(BF16)
    w_gu_b = ffn_w_gu.astype(BF16)
    w_down_b = ffn_w_down.astype(BF16)
    xw, ssq = _prenorm(xr, norm_mix_w[0])
    for l in range(depth):
        proj = _matmul(xw, ssq, w_in_b, l, F32, *tiles["in_proj"], name="in_proj")
        y_ssd = _ssd_mixer(proj, bsz, seq, ssd_conv_w[l].astype(F32), ssd_conv_b[l].astype(F32), ssd_dt_bias[l],
                           ssd_a_log[l], ssd_d[l], ssd_norm_w[l].astype(F32))
        cmp_kv = _nsa_compress(proj, bsz, seq, nsa_cmp_pe[l], nsa_cmp_w1[l], nsa_cmp_w2[l])
        y_nsa = _nsa_attention(proj, cmp_kv, bias_cmp, bias_tab, bsz, seq)
        y_hgrn = _hgrn_mixer(proj, bsz, seq, hgrn_lb_logits, hgrn_norm_w[l], l)
        xr, xw, ssq = _out_proj([y_ssd, y_nsa, y_hgrn], w_out_b, l, xr, norm_ffn_w[l], *tiles["out_proj"])
        act = _gate_up(xw, ssq, w_gu_b, l, ffn_conv_w[l].astype(F32), ffn_conv_b[l].astype(F32), seq,
                       *tiles["gate_up"])
        if l + 1 < depth:
            xr, xw, ssq = _down_proj_norm(act, w_down_b, l, xr, norm_mix_w[l + 1], *tiles["down"])
        else:
            xr = _matmul_residual_wres(act, w_down_b, l, xr, *tiles["down"], name="ffn_down")
    out = _rmsnorm(xr, norm_f_w, x.dtype)
    return out.reshape(bsz, seq, d)
```

```python
import functools
import math

import numpy as np
import jax
import jax.numpy as jnp
from jax import lax
from jax.experimental import pallas as pl
from jax.experimental.pallas import tpu as pltpu

F32 = jnp.float32
BF16 = jnp.bfloat16

D_MODEL = 4096
SSD_HEAD_DIM = 64
SSD_WIDTH = 1536
SSD_HEADS = 24
SSD_GROUPS = 4
SSD_HPG = 6
SSD_STATE = 128
SSD_CONV = 4
SSD_CHUNK = 128
SSD_GW = SSD_WIDTH // SSD_GROUPS
NSA_HEAD_DIM = 128
NSA_WIDTH = 1536
NSA_HEADS = 12
NSA_KV_HEADS = 4
NSA_GQA = 3
NSA_KV_WIDTH = 512
CMP_BLOCK = 32
CMP_STRIDE = 16
CMP_HIDDEN = 256
SEL_BLOCK = 64
SEL_TOPK = 16
SEL_LOCAL = 2
WINDOW = 512
Q_BLOCK = 128
FORCE_BONUS = 1e4
HGRN_WIDTH = 1024
HGRN_HEADS = 8
HGRN_DIM = 128
HGRN_CHUNK = 32
REL_BUCKETS = 32
REL_MAX_DIST = 128
D_FF = 11008
FFN_CONV = 3
EPS = 1e-6
NEG_INF = -1e30

IN_SIZES = (SSD_WIDTH, SSD_WIDTH + 2 * SSD_GROUPS * SSD_STATE, SSD_HEADS, NSA_WIDTH,
            NSA_KV_WIDTH, NSA_KV_WIDTH, NSA_KV_WIDTH, NSA_KV_WIDTH, NSA_KV_WIDTH, NSA_KV_WIDTH,
            3 * NSA_HEADS, HGRN_WIDTH, HGRN_WIDTH, HGRN_WIDTH, HGRN_WIDTH)

LANES = 128
SUBLANES = 8
VMEM_LIMIT = 56 * 1024 * 1024

COL_Z = 0
COL_XS = 1536
COL_Q = 3072
COL_B = 4608
COL_C = 5120
COL_KC = 5632
COL_VC = 6144
COL_KS = 6656
COL_VS = 7168
COL_KW = 7680
COL_VW = 8192
COL_DT = 8704
COL_GATE = 8832
COL_HQ = 9216
COL_HF = 10240
COL_HI = 11264
COL_HG = 12288
IN_PAD = 13312

LOG2E = math.log2(math.e)
AUG_BIAS_HI = 64
AUG_BIAS_LO = 65
AUG_PAD = 66
WIN_KEYS = WINDOW + Q_BLOCK
NEAR_KEYS = 2 * Q_BLOCK
FAR_KEYS = 4 * Q_BLOCK
HGRN_HEADS_INTERLEAVED = 8
NSA_Q_BLOCKS_PER_STEP = 4
NSA_FAR_GROUP = 2


def _params(semantics):
    return pltpu.CompilerParams(dimension_semantics=semantics, vmem_limit_bytes=VMEM_LIMIT)


def _sigmoid(x):
    return 1.0 / (1.0 + jnp.exp(-x))


def _silu(x):
    return x * _sigmoid(x)


def _dot(a, b):
    return jnp.dot(a, b, preferred_element_type=F32)


def _dot_nt(a, b):
    return lax.dot_general(a, b, (((1,), (1,)), ((), ())), preferred_element_type=F32)


def _split3(x):
    hi = x.astype(BF16)
    r1 = x - hi.astype(F32)
    mid = r1.astype(BF16)
    lo = (r1 - mid.astype(F32)).astype(BF16)
    return hi, mid, lo


def _rmsnorm_kernel(x_ref, w_ref, o_ref):
    x = x_ref[...]
    ms = jnp.mean(x * x, axis=-1, keepdims=True)
    o_ref[...] = (x * lax.rsqrt(ms + EPS) * w_ref[...]).astype(o_ref.dtype)


def _rmsnorm(x, w, out_dtype, tm=256):
    m, d = x.shape
    return pl.pallas_call(
        _rmsnorm_kernel,
        grid=(m // tm,),
        in_specs=[pl.BlockSpec((tm, d), lambda i: (i, 0)), pl.BlockSpec((1, d), lambda i: (0, 0))],
        out_specs=pl.BlockSpec((tm, d), lambda i: (i, 0)),
        out_shape=jax.ShapeDtypeStruct((m, d), out_dtype),
        compiler_params=_params(("parallel",)),
        name="rmsnorm",
    )(x, w.reshape(1, d).astype(F32))


def _lane_fold(sq):
    part = sq[:, 0:LANES]
    for c in range(1, sq.shape[1] // LANES):
        part = part + sq[:, c * LANES:(c + 1) * LANES]
    return part


def _rinv_lanes(ssq, d):
    part = ssq[0]
    for p in range(1, ssq.shape[0]):
        part = part + ssq[p]
    tot = jnp.sum(part, axis=-1, keepdims=True)
    return jnp.broadcast_to(lax.rsqrt(tot * (1.0 / d) + EPS), part.shape)


def _scale_rows(acc, rinv):
    return jnp.concatenate([acc[:, c * LANES:(c + 1) * LANES] * rinv for c in range(acc.shape[1] // LANES)], axis=1)


def _prenorm_kernel(x_ref, w_ref, xw_ref, ssq_ref):
    x = x_ref[...]
    xw_ref[...] = (x * w_ref[...]).astype(xw_ref.dtype)
    ssq_ref[0] = _lane_fold(x * x)


def _prenorm(x, w, tm=256):
    m, d = x.shape
    return pl.pallas_call(
        _prenorm_kernel,
        grid=(m // tm,),
        in_specs=[pl.BlockSpec((tm, d), lambda i: (i, 0)), pl.BlockSpec((1, d), lambda i: (0, 0))],
        out_specs=[pl.BlockSpec((tm, d), lambda i: (i, 0)), pl.BlockSpec((1, tm, LANES), lambda i: (0, i, 0))],
        out_shape=[jax.ShapeDtypeStruct((m, d), BF16), jax.ShapeDtypeStruct((1, m, LANES), F32)],
        compiler_params=_params(("parallel",)),
        name="prenorm",
    )(x, w.reshape(1, d).astype(F32))


MM_ROW_CHUNK = 128


def _row_chunks(rows):
    rc = min(MM_ROW_CHUNK, rows)
    return [slice(r, r + rc) for r in range(0, rows, rc)]


def _mm_kernel(a_ref, ssq_ref, w_ref, o_ref, rinv_ref, *, d):
    @pl.when(pl.program_id(1) == 0)
    def _():
        rinv_ref[...] = _rinv_lanes(ssq_ref[...], d)

    for rows in _row_chunks(o_ref.shape[0]):
        o_ref[rows, :] = _scale_rows(_dot(a_ref[rows, :], w_ref[...]), rinv_ref[rows, :]).astype(o_ref.dtype)


def _mm_res_kernel(a_ref, w_ref, r_ref, o_ref):
    for rows in _row_chunks(o_ref.shape[0]):
        o_ref[rows, :] = r_ref[rows, :] + _dot(a_ref[rows, :], w_ref[...])


def _matmul(xw, ssq, w, layer, out_dtype, tm, tn, name):
    m, k = xw.shape
    n = w.shape[2]
    return pl.pallas_call(
        functools.partial(_mm_kernel, d=k),
        grid=(m // tm, n // tn),
        in_specs=[pl.BlockSpec((tm, k), lambda i, j: (i, 0)),
                  pl.BlockSpec((ssq.shape[0], tm, LANES), lambda i, j: (0, i, 0)),
                  pl.BlockSpec((None, k, tn), lambda i, j: (layer, 0, j))],
        out_specs=pl.BlockSpec((tm, tn), lambda i, j: (i, j)),
        out_shape=jax.ShapeDtypeStruct((m, n), out_dtype),
        scratch_shapes=[pltpu.VMEM((tm, LANES), F32)],
        compiler_params=_params(("arbitrary", "arbitrary")),
        name=name,
    )(xw, ssq, w)


def _out_proj_kernel(*refs):
    *a_refs, w_ref, r_ref, nw_ref, o_ref, xw_ref, ssq_ref = refs
    j = pl.program_id(1)
    parts = []
    for rows in _row_chunks(o_ref.shape[0]):
        acc = r_ref[rows, :]
        k0 = 0
        for a_ref in a_refs:
            acc = acc + _dot(a_ref[rows, :], w_ref[k0:k0 + a_ref.shape[1], :])
            k0 += a_ref.shape[1]
        o_ref[rows, :] = acc
        xw_ref[rows, :] = (acc * nw_ref[...]).astype(xw_ref.dtype)
        parts.append(_lane_fold(acc * acc))
    part = jnp.concatenate(parts, axis=0)

    @pl.when(j == 0)
    def _():
        ssq_ref[0] = part

    @pl.when(j > 0)
    def _():
        ssq_ref[0] = ssq_ref[0] + part


def _out_proj(parts, w, layer, res, norm_w, tm, tn):
    m = res.shape[0]
    k, n = w.shape[1], w.shape[2]
    assert sum(p.shape[1] for p in parts) == k
    return pl.pallas_call(
        _out_proj_kernel,
        grid=(m // tm, n // tn),
        in_specs=[pl.BlockSpec((tm, p.shape[1]), lambda i, j: (i, 0)) for p in parts]
        + [pl.BlockSpec((None, k, tn), lambda i, j: (layer, 0, j)),
           pl.BlockSpec((tm, tn), lambda i, j: (i, j)),
           pl.BlockSpec((1, tn), lambda i, j: (0, j))],
        out_specs=[pl.BlockSpec((tm, tn), lambda i, j: (i, j)), pl.BlockSpec((tm, tn), lambda i, j: (i, j)),
                   pl.BlockSpec((1, tm, LANES), lambda i, j: (0, i, 0))],
        out_shape=[jax.ShapeDtypeStruct((m, n), F32), jax.ShapeDtypeStruct((m, n), BF16),
                   jax.ShapeDtypeStruct((1, m, LANES), F32)],
        compiler_params=_params(("arbitrary", "arbitrary")),
        name="out_proj",
    )(*parts, w, res, norm_w.reshape(1, n).astype(F32))


def _down_norm_kernel(a_ref, w_ref, r_ref, nw_ref, o_ref, xw_ref, ssq_ref):
    for rows in _row_chunks(o_ref.shape[0]):
        acc = r_ref[rows, :] + _dot(a_ref[rows, :], w_ref[...])
        o_ref[rows, :] = acc
        xw_ref[rows, :] = (acc * nw_ref[...]).astype(xw_ref.dtype)
        ssq_ref[rows, :] = _lane_fold(acc * acc)


def _down_proj_norm(a, w, layer, res, norm_w, tm, tn):
    m, k = a.shape
    n = w.shape[2]
    return pl.pallas_call(
        _down_norm_kernel,
        grid=(n // tn, m // tm),
        in_specs=[pl.BlockSpec((tm, k), lambda j, i: (i, 0)),
                  pl.BlockSpec((None, k, tn), lambda j, i: (layer, 0, j), pipeline_mode=pl.Buffered(1)),
                  pl.BlockSpec((tm, tn), lambda j, i: (i, j)),
                  pl.BlockSpec((1, tn), lambda j, i: (0, j))],
        out_specs=[pl.BlockSpec((tm, tn), lambda j, i: (i, j)), pl.BlockSpec((tm, tn), lambda j, i: (i, j)),
                   pl.BlockSpec((None, tm, LANES), lambda j, i: (j, i, 0))],
        out_shape=[jax.ShapeDtypeStruct((m, n), F32), jax.ShapeDtypeStruct((m, n), BF16),
                   jax.ShapeDtypeStruct((n // tn, m, LANES), F32)],
        compiler_params=_params(("arbitrary", "arbitrary")),
        name="ffn_down",
    )(a, w, res, norm_w.reshape(1, n).astype(F32))


def _matmul_residual_wres(a, w, layer, res, tm, tn, name):
    m, k = a.shape
    n = w.shape[2]
    return pl.pallas_call(
        _mm_res_kernel,
        grid=(n // tn, m // tm),
        in_specs=[pl.BlockSpec((tm, k), lambda j, i: (i, 0)),
                  pl.BlockSpec((None, k, tn), lambda j, i: (layer, 0, j), pipeline_mode=pl.Buffered(1)),
                  pl.BlockSpec((tm, tn), lambda j, i: (i, j))],
        out_specs=pl.BlockSpec((tm, tn), lambda j, i: (i, j)),
        out_shape=jax.ShapeDtypeStruct((m, n), F32),
        compiler_params=_params(("arbitrary", "arbitrary")),
        name=name,
    )(a, w, res)


def _gu_kernel(h_ref, ssq_ref, wg_ref, wu_ref, cw_ref, cb_ref, o_ref, ga_ref, ua_ref, gb_ref, ub_ref, halo_ref,
               rinv_ref, *, tm, nj, tiles_per_seq):
    t = pl.program_id(0)
    tp = jnp.maximum(t - 1, 0)
    ip = tp // nj
    jp = tp % nj
    rc = min(MM_ROW_CHUNK, tm)

    @pl.when(t == 0)
    def _():
        gb_ref[...] = jnp.zeros_like(gb_ref)
        ub_ref[...] = jnp.zeros_like(ub_ref)
        halo_ref[...] = jnp.zeros_like(halo_ref)

    @pl.when(t % nj == 0)
    def _():
        rinv_ref[...] = _rinv_lanes(ssq_ref[...], h_ref.shape[1])

    def step(g_cur, u_cur, g_prev, u_prev):
        g_prev[0:SUBLANES, :] = jnp.where(ip % tiles_per_seq == 0, 0.0, halo_ref[jp])
        halo_ref[jp] = g_prev[tm:tm + SUBLANES, :]
        cw = cw_ref[...]
        for r in range(tm // rc):
            hr = h_ref[r * rc:(r + 1) * rc, :]
            rinv = rinv_ref[r * rc:(r + 1) * rc, :]
            g_cur[SUBLANES + r * rc:SUBLANES + (r + 1) * rc, :] = _scale_rows(_dot(hr, wg_ref[...]), rinv)
            u_cur[r * rc:(r + 1) * rc, :] = _scale_rows(_dot(hr, wu_ref[...]), rinv)
            acc = cb_ref[...]
            for sh in range(FFN_CONV):
                acc = acc + cw[FFN_CONV - 1 - sh:FFN_CONV - sh, :] * g_prev[pl.ds(SUBLANES - sh + r * rc, rc), :]
            o_ref[r * rc:(r + 1) * rc, :] = (_silu(acc) * u_prev[r * rc:(r + 1) * rc, :]).astype(o_ref.dtype)

    @pl.when(t % 2 == 0)
    def _():
        step(ga_ref, ua_ref, gb_ref, ub_ref)

    @pl.when(t % 2 == 1)
    def _():
        step(gb_ref, ub_ref, ga_ref, ua_ref)


def _gate_up(h, ssq, w_gu, layer, conv_w, conv_b, seq, tm, tn):
    m, k = h.shape
    nf = conv_w.shape[1]
    nj = nf // tn
    last = (m // tm) * nj - 1
    cur = lambda t: jnp.minimum(t, last)
    prev = lambda t: jnp.maximum(t - 1, 0)
    kern = functools.partial(_gu_kernel, tm=tm, nj=nj, tiles_per_seq=seq // tm)
    return pl.pallas_call(
        kern,
        grid=(last + 2,),
        in_specs=[pl.BlockSpec((tm, k), lambda t: (cur(t) // nj, 0), pipeline_mode=pl.Buffered(1)),
                  pl.BlockSpec((ssq.shape[0], tm, LANES), lambda t: (0, cur(t) // nj, 0)),
                  pl.BlockSpec((None, k, tn), lambda t: (layer, 0, cur(t) % nj)),
                  pl.BlockSpec((None, k, tn), lambda t: (layer, 0, cur(t) % nj + nj)),
                  pl.BlockSpec((FFN_CONV, tn), lambda t: (0, prev(t) % nj)),
                  pl.BlockSpec((1, tn), lambda t: (0, prev(t) % nj))],
        out_specs=pl.BlockSpec((tm, tn), lambda t: (prev(t) // nj, prev(t) % nj)),
        out_shape=jax.ShapeDtypeStruct((m, nf), BF16),
        scratch_shapes=[pltpu.VMEM((tm + SUBLANES, tn), F32), pltpu.VMEM((tm, tn), F32),
                        pltpu.VMEM((tm + SUBLANES, tn), F32), pltpu.VMEM((tm, tn), F32),
                        pltpu.VMEM((nj, SUBLANES, tn), F32), pltpu.VMEM((tm, LANES), F32)],
        compiler_params=_params(("arbitrary",)),
        name="ffn_gate_up_conv",
    )(h, ssq, w_gu, w_gu, conv_w, conv_b.reshape(1, nf))


def _ssd_kernel(z_ref, xs_ref, b_ref, c_ref, dt_ref, cwx_ref, cwb_ref, cwc_ref, cbx_ref, cbb_ref, cbc_ref,
                dtb_ref, alog_ref, dsk_ref, nw_ref, tri_ref, o_ref,
                extx_ref, extb_ref, extc_ref, state_ref, *, ts):
    s = pl.program_id(1)

    @pl.when(s == 0)
    def _():
        extx_ref[...] = jnp.zeros_like(extx_ref)
        extb_ref[...] = jnp.zeros_like(extb_ref)
        extc_ref[...] = jnp.zeros_like(extc_ref)
        state_ref[...] = jnp.zeros_like(state_ref)

    def conv_silu(u_ref, ext_ref, w_ref, bias_ref):
        ext_ref[0:SUBLANES, :] = ext_ref[ts:ts + SUBLANES, :]
        ext_ref[SUBLANES:, :] = u_ref[...]
        w = w_ref[...]
        acc = bias_ref[...] + w[SSD_CONV - 1:SSD_CONV, :] * u_ref[...]
        for sh in range(1, SSD_CONV):
            acc = acc + w[SSD_CONV - 1 - sh:SSD_CONV - sh, :] * ext_ref[pl.ds(SUBLANES - sh, ts), :]
        return _silu(acc)

    xs = conv_silu(xs_ref, extx_ref, cwx_ref, cbx_ref)
    bm = conv_silu(b_ref, extb_ref, cwb_ref, cbb_ref)
    cm = conv_silu(c_ref, extc_ref, cwc_ref, cbc_ref)
    dtr = dt_ref[...] + dtb_ref[...]
    dt = jnp.maximum(dtr, 0.0) + jnp.log1p(jnp.exp(-jnp.abs(dtr)))
    a = dt * (-jnp.exp(alog_ref[...]))
    z = z_ref[...]
    tri = tri_ref[...]
    dsk = dsk_ref[...]
    nw = nw_ref[...]
    li = lax.broadcasted_iota(jnp.int32, (SSD_CHUNK, SSD_CHUNK), 0)
    si = lax.broadcasted_iota(jnp.int32, (SSD_CHUNK, SSD_CHUNK), 1)
    causal = li >= si
    low_half = si < SSD_HEAD_DIM
    low_half_row = low_half[0:1, :]

    for ck in range(ts // SSD_CHUNK):
        r0 = ck * SSD_CHUNK
        a_c = a[r0:r0 + SSD_CHUNK]
        hi, mid, lo = _split3(a_c)
        acs = _dot(tri, hi) + _dot(tri, mid) + _dot(tri, lo)
        acs_t = acs.T
        dt_t = dt[r0:r0 + SSD_CHUNK].T
        a_last_b = jnp.broadcast_to(acs_t[:, SSD_CHUNK - 1:SSD_CHUNK], acs_t.shape)
        dtd_t = dt_t * jnp.exp(a_last_b - acs_t)
        chunk_decay = jnp.exp(acs[SSD_CHUNK - 1:SSD_CHUNK, :])
        for g in range(SSD_GROUPS):
            cg = cm[r0:r0 + SSD_CHUNK, g * SSD_STATE:(g + 1) * SSD_STATE]
            bg = bm[r0:r0 + SSD_CHUNK, g * SSD_STATE:(g + 1) * SSD_STATE]
            cb = _dot_nt(cg.astype(BF16), bg.astype(BF16))
            bgt = bg.T
            ys = []
            for pr in range(SSD_HPG // 2):
                pair = (g * SSD_HPG) // 2 + pr
                c0 = pair * LANES
                xs_p = xs[r0:r0 + SSD_CHUNK, c0:c0 + LANES].astype(BF16)
                prev = state_ref[pair]
                rhs = jnp.concatenate([xs_p, prev.astype(BF16)], axis=0)
                y_h, st_h, cd_h = [], [], []
                for hh in (2 * pair, 2 * pair + 1):
                    colb = jnp.broadcast_to(acs[:, hh:hh + 1], (SSD_CHUNK, SSD_CHUNK))
                    seg = colb - acs_t[hh:hh + 1, :]
                    decay = jnp.where(causal, jnp.exp(jnp.where(causal, seg, 0.0)), 0.0)
                    intra = cb * decay * dt_t[hh:hh + 1, :]
                    inter = cg * jnp.exp(colb)
                    lhs = jnp.concatenate([intra, inter], axis=1).astype(BF16)
                    y_h.append(_dot(lhs, rhs))
                    st_h.append(_dot((bgt * dtd_t[hh:hh + 1, :]).astype(BF16), xs_p))
                    cd_h.append(jnp.broadcast_to(chunk_decay[:, hh:hh + 1], (1, LANES)))
                ys.append(jnp.where(low_half, y_h[0], y_h[1]))
                state_ref[pair] = (prev * jnp.where(low_half_row, cd_h[0], cd_h[1])
                                   + jnp.where(low_half, st_h[0], st_h[1]))
            yg = jnp.concatenate(ys, axis=1)
            yg = yg + xs[r0:r0 + SSD_CHUNK, g * SSD_GW:(g + 1) * SSD_GW] * dsk[:, g * SSD_GW:(g + 1) * SSD_GW]
            u = yg * _silu(z[r0:r0 + SSD_CHUNK, g * SSD_GW:(g + 1) * SSD_GW])
            u = u * lax.rsqrt(jnp.mean(u * u, axis=-1, keepdims=True) + EPS)
            o_ref[r0:r0 + SSD_CHUNK, g * SSD_GW:(g + 1) * SSD_GW] = (
                u * nw[:, g * SSD_GW:(g + 1) * SSD_GW]).astype(o_ref.dtype)


def _pad_lanes(v, n=LANES):
    v = v.reshape(1, -1).astype(F32)
    return jnp.pad(v, ((0, 0), (0, n - v.shape[1])))


def _ssd_mixer(proj, bsz, seq, conv_w, conv_b, dt_bias, a_log, d_skip, norm_w, ts=256):
    nsteps = seq // ts
    row = lambda b, s: b * nsteps + s
    xw, bw, cw = conv_w[:, :SSD_WIDTH], conv_w[:, SSD_WIDTH:SSD_WIDTH + 512], conv_w[:, SSD_WIDTH + 512:]
    xb, bb, cb = conv_b[:SSD_WIDTH], conv_b[SSD_WIDTH:SSD_WIDTH + 512], conv_b[SSD_WIDTH + 512:]
    tri = jnp.asarray(np.tril(np.ones((SSD_CHUNK, SSD_CHUNK), np.float32)), BF16)
    const = lambda shape: pl.BlockSpec(shape, lambda b, s: (0,) * len(shape))
    kern = functools.partial(_ssd_kernel, ts=ts)
    return pl.pallas_call(
        kern,
        grid=(bsz, nsteps),
        in_specs=[pl.BlockSpec((ts, SSD_WIDTH), lambda b, s: (row(b, s), COL_Z // SSD_WIDTH)),
                  pl.BlockSpec((ts, SSD_WIDTH), lambda b, s: (row(b, s), COL_XS // SSD_WIDTH)),
                  pl.BlockSpec((ts, 512), lambda b, s: (row(b, s), COL_B // 512)),
                  pl.BlockSpec((ts, 512), lambda b, s: (row(b, s), COL_C // 512)),
                  pl.BlockSpec((ts, LANES), lambda b, s: (row(b, s), COL_DT // LANES)),
                  const((SSD_CONV, SSD_WIDTH)), const((SSD_CONV, 512)), const((SSD_CONV, 512)),
                  const((1, SSD_WIDTH)), const((1, 512)), const((1, 512)),
                  const((1, LANES)), const((1, LANES)), const((1, SSD_WIDTH)), const((1, SSD_WIDTH)),
                  const((SSD_CHUNK, SSD_CHUNK))],
        out_specs=pl.BlockSpec((ts, SSD_WIDTH), lambda b, s: (row(b, s), 0)),
        out_shape=jax.ShapeDtypeStruct((bsz * seq, SSD_WIDTH), BF16),
        scratch_shapes=[pltpu.VMEM((ts + SUBLANES, SSD_WIDTH), F32), pltpu.VMEM((ts + SUBLANES, 512), F32),
                        pltpu.VMEM((ts + SUBLANES, 512), F32),
                        pltpu.VMEM((SSD_HEADS // 2, SSD_STATE, 2 * SSD_HEAD_DIM), F32)],
        compiler_params=_params(("arbitrary", "arbitrary")),
        name="ssd_mixer",
    )(proj, proj, proj, proj, proj, xw, bw, cw, xb.reshape(1, -1), bb.reshape(1, -1), cb.reshape(1, -1),
      _pad_lanes(dt_bias), _pad_lanes(a_log), jnp.repeat(d_skip.astype(F32), SSD_HEAD_DIM).reshape(1, -1),
      norm_w.reshape(1, -1), tri)


def _hgrn_kernel(q_ref, f_ref, i_ref, g_ref, lbl_ref, nw_ref, cum_ref, o_ref, state_ref, *, ts, layer):
    s = pl.program_id(1)

    @pl.when(s == 0)
    def _():
        state_ref[...] = jnp.zeros_like(state_ref)

    nck = ts // HGRN_CHUNK
    lg = lbl_ref[...]
    e = jnp.exp(lg - jnp.max(lg, axis=0, keepdims=True))
    sm = e / jnp.sum(e, axis=0, keepdims=True)
    ridx = lax.broadcasted_iota(jnp.int32, lg.shape, 0)
    lb = jnp.sum(jnp.where((ridx >= 1) & (ridx <= layer), sm, 0.0), axis=0, keepdims=True)

    cum = cum_ref[...]
    nw = nw_ref[...]
    ti = lax.broadcasted_iota(jnp.int32, (ts, ts), 0)
    tj = lax.broadcasted_iota(jnp.int32, (ts, ts), 1)
    blockcausal = (ti // HGRN_CHUNK == tj // HGRN_CHUNK) & (ti >= tj)
    tok = lax.broadcasted_iota(jnp.int32, (1, ts), 1)

    def head(h):
        c0 = h * HGRN_DIM
        lbh = lb[:, c0:c0 + HGRN_DIM]
        f = lbh + (1.0 - lbh) * _sigmoid(f_ref[:, c0:c0 + HGRN_DIM])
        logf = jnp.log(f)
        k = 1.0 - f
        q = _silu(q_ref[:, c0:c0 + HGRN_DIM])
        v = i_ref[:, c0:c0 + HGRN_DIM]
        hi, mid, lo = _split3(logf)
        b = _dot(cum, hi) + _dot(cum, mid) + _dot(cum, lo)
        yield
        b3 = b.reshape(nck, HGRN_CHUNK, HGRN_DIM)
        bref = jnp.broadcast_to(b3[:, HGRN_CHUNK // 2:HGRN_CHUNK // 2 + 1, :], b3.shape).reshape(ts, HGRN_DIM)
        blast = jnp.broadcast_to(b3[:, HGRN_CHUNK - 1:HGRN_CHUNK, :], b3.shape).reshape(ts, HGRN_DIM)
        qe = (q * jnp.exp(b - bref)).astype(BF16)
        ke = (k * jnp.exp(bref - b)).astype(BF16)
        kl = (k * jnp.exp(blast - b)).astype(BF16)
        qb = (q * jnp.exp(b)).astype(BF16)
        vb = v.astype(BF16)
        att = _dot_nt(qe, ke)
        vt = v.T
        lhs = jnp.concatenate(
            [jnp.where(tok // HGRN_CHUNK == c, vt, 0.0) for c in range(nck)], axis=0).astype(BF16)
        st = _dot(lhs, kl)
        yield
        o = _dot(jnp.where(blockcausal, att, 0.0).astype(BF16), vb)
        state = state_ref[h]
        outs = []
        for c in range(nck):
            t0 = c * HGRN_CHUNK
            outs.append(_dot_nt(qb[t0:t0 + HGRN_CHUNK], state.astype(BF16)))
            cd = jnp.exp(blast[t0:t0 + 1, :])
            state = state * cd + st[c * HGRN_DIM:(c + 1) * HGRN_DIM]
            if c % 2 == 1:
                yield
        state_ref[h] = state
        o = o + jnp.concatenate(outs, axis=0)
        o = o * lax.rsqrt(jnp.mean(o * o, axis=-1, keepdims=True) + EPS) * nw
        o_ref[:, c0:c0 + HGRN_DIM] = (o * _silu(g_ref[:, c0:c0 + HGRN_DIM])).astype(o_ref.dtype)

    for h0 in range(0, HGRN_HEADS, HGRN_HEADS_INTERLEAVED):
        live = [head(h) for h in range(h0, h0 + HGRN_HEADS_INTERLEAVED)]
        while live:
            live = [gen for gen in live if next(gen, live) is not live]


def _hgrn_cum_matrix(ts):
    t = np.arange(ts)
    same = (t[:, None] // HGRN_CHUNK) == (t[None, :] // HGRN_CHUNK)
    return (same & (t[None, :] <= t[:, None])).astype(np.float32)


def _hgrn_mixer(proj, bsz, seq, lb_logits, norm_w, layer, ts=256):
    nsteps = seq // ts
    row = lambda b, s: b * nsteps + s
    depth = lb_logits.shape[0]
    cum = jnp.asarray(_hgrn_cum_matrix(ts), BF16)
    kern = functools.partial(_hgrn_kernel, ts=ts, layer=layer)
    blk = lambda col: pl.BlockSpec((ts, HGRN_WIDTH), lambda b, s: (row(b, s), col // HGRN_WIDTH))
    return pl.pallas_call(
        kern,
        grid=(bsz, nsteps),
        in_specs=[blk(COL_HQ), blk(COL_HF), blk(COL_HI), blk(COL_HG),
                  pl.BlockSpec((depth, HGRN_WIDTH), lambda b, s: (0, 0)),
                  pl.BlockSpec((1, HGRN_DIM), lambda b, s: (0, 0)),
                  pl.BlockSpec((ts, ts), lambda b, s: (0, 0))],
        out_specs=pl.BlockSpec((ts, HGRN_WIDTH), lambda b, s: (row(b, s), 0)),
        out_shape=jax.ShapeDtypeStruct((bsz * seq, HGRN_WIDTH), BF16),
        scratch_shapes=[pltpu.VMEM((HGRN_HEADS, HGRN_DIM, HGRN_DIM), F32)],
        compiler_params=_params(("arbitrary", "arbitrary")),
        name="hgrn2_mixer",
    )(proj, proj, proj, proj, lb_logits.astype(F32), norm_w.reshape(1, HGRN_DIM).astype(F32), cum)


def _rel_bucket_np(dist):
    max_exact = REL_BUCKETS // 2
    d = np.maximum(dist, 0)
    ratio = np.maximum(d, 1).astype(np.float32) / np.float32(max_exact)
    log_ratio = np.log(ratio).astype(np.float32) / np.float32(math.log(REL_MAX_DIST / max_exact))
    large = np.minimum(max_exact + (log_ratio * np.float32(REL_BUCKETS - max_exact)).astype(np.int32),
                       REL_BUCKETS - 1)
    return np.where(d < max_exact, d, large).astype(np.int32)


def _bias_expand_kernel(rb_ref, bmap_ref, o_ref):
    h = pl.program_id(0)
    bm = bmap_ref[...]
    out = jnp.full(bm.shape, NEG_INF, F32)
    for k in range(REL_BUCKETS):
        out = jnp.where(bm == k, rb_ref[k, h] * LOG2E, out)
    o_ref[...] = out


def _bias_expand(rel_bias, bmap, tr):
    rows, cols = bmap.shape
    return pl.pallas_call(
        _bias_expand_kernel,
        grid=(NSA_HEADS, rows // tr),
        in_specs=[pl.BlockSpec(memory_space=pltpu.SMEM),
                  pl.BlockSpec((tr, cols), lambda h, i: (i, 0))],
        out_specs=pl.BlockSpec((None, tr, cols), lambda h, i: (h, i, 0)),
        out_shape=jax.ShapeDtypeStruct((NSA_HEADS, rows, cols), F32),
        compiler_params=_params(("arbitrary", "arbitrary")),
        name="nsa_bias_expand",
    )(rel_bias.astype(F32), bmap)


def _nsa_bias_tables(rel_bias, seq):
    ncp = seq // CMP_STRIDE
    t = np.arange(seq)[:, None]
    cmp_end = np.arange(ncp)[None, :] * CMP_STRIDE + CMP_BLOCK - 1
    d = t - cmp_end
    bmap_cmp = np.where(d >= 0, _rel_bucket_np(d), -1).astype(np.int32)
    l = np.arange(Q_BLOCK)[:, None]
    j = np.arange(WIN_KEYS)[None, :]
    dist = l - j + WINDOW
    win = np.where((dist >= 0) & (dist < WINDOW), _rel_bucket_np(dist), -1)
    far = _rel_bucket_np(np.arange(Q_BLOCK + 1, 8 * seq))
    assert (far == far[0]).all(), "distances beyond one query block must share a single bucket"
    bmap_tab = np.concatenate([win, np.full((Q_BLOCK, Q_BLOCK), far[0])], axis=1).astype(np.int32)
    bias_cmp = _bias_expand(rel_bias, jnp.asarray(bmap_cmp), 512)
    bias_tab = _bias_expand(rel_bias, jnp.asarray(bmap_tab), Q_BLOCK)
    return bias_cmp, bias_tab


def _cmp_kernel(u_ref, pe_ref, w1_ref, w2_ref, o_ref, *, ncp):
    half = CMP_BLOCK // 2
    pe = pe_ref[...]
    top = jnp.zeros((ncp, CMP_HIDDEN), F32)
    bot = jnp.zeros((ncp, CMP_HIDDEN), F32)
    for l in range(half):
        x = u_ref[pl.ds(l, ncp, stride=half), :]
        top = top + _dot((x + pe[l:l + 1, :]).astype(BF16), w1_ref[l])
        bot = bot + _dot((x + pe[half + l:half + l + 1, :]).astype(BF16), w1_ref[half + l])
    hid = top + pltpu.roll(bot, ncp - 1, 0)
    o_ref[...] = _dot(_silu(hid).astype(BF16), w2_ref[...])


def _nsa_compress(proj, bsz, seq, pe, w1, w2):
    ncp = seq // CMP_STRIDE
    kern = functools.partial(_cmp_kernel, ncp=ncp)
    return pl.pallas_call(
        kern,
        grid=(bsz, 2, NSA_KV_HEADS),
        in_specs=[pl.BlockSpec((seq, NSA_HEAD_DIM), lambda b, t, h: (b, COL_KC // NSA_HEAD_DIM + NSA_KV_HEADS * t + h)),
                  pl.BlockSpec((None, CMP_BLOCK, NSA_HEAD_DIM), lambda b, t, h: (t, 0, 0)),
                  pl.BlockSpec((None, CMP_BLOCK, NSA_HEAD_DIM, CMP_HIDDEN), lambda b, t, h: (t, 0, 0, 0)),
                  pl.BlockSpec((None, CMP_HIDDEN, NSA_HEAD_DIM), lambda b, t, h: (t, 0, 0))],
        out_specs=pl.BlockSpec((None, None, None, ncp, NSA_HEAD_DIM), lambda b, t, h: (b, t, h, 0, 0)),
        out_shape=jax.ShapeDtypeStruct((bsz, 2, NSA_KV_HEADS, ncp, NSA_HEAD_DIM), F32),
        compiler_params=_params(("arbitrary", "arbitrary", "arbitrary")),
        name="nsa_compress",
    )(proj, pe.astype(F32), w1.reshape(2, CMP_BLOCK, NSA_HEAD_DIM, CMP_HIDDEN).astype(BF16), w2.astype(BF16))


def _softmax_start(s, v):
    m = jnp.max(s, axis=-1, keepdims=True)
    p = jnp.exp2(s - m)
    return m, jnp.sum(p, axis=-1, keepdims=True), _dot(p.astype(BF16), v)


def _softmax_update(s, v, carry):
    m, l, acc = carry
    m_new = jnp.maximum(m, jnp.max(s, axis=-1, keepdims=True))
    alpha = jnp.exp2(m - m_new)
    p = jnp.exp2(s - m_new)
    return m_new, alpha * l + jnp.sum(p, axis=-1, keepdims=True), alpha * acc + _dot(p.astype(BF16), v)


def _softmax_finish(carry):
    m, l, acc = carry
    return jnp.where(m > 0.5 * NEG_INF, acc / jnp.maximum(l, 1e-30), 0.0)


def _nsa_kernel(q_ref, gate_ref, kc_ref, vc_ref, ks_ref, vs_ref, kw_ref, vw_ref, bcmp_ref, btab_ref,
                kone_ref, c2s_ref, o_ref, kaug_ref, vsb_ref, kwb_ref, vwb_ref, sa_ref, sb_ref, *, n_sel, nqb):
    hk = pl.program_id(1)
    c0 = pl.program_id(2) * nqb
    rows = NSA_GQA * Q_BLOCK
    lane = lax.broadcasted_iota(jnp.int32, (Q_BLOCK, LANES), 1)

    @pl.when(c0 == 0)
    def _():
        kaug_ref[0:Q_BLOCK, 0:NSA_HEAD_DIM] = jnp.zeros((Q_BLOCK, NSA_HEAD_DIM), BF16)
        kaug_ref[Q_BLOCK:, 0:NSA_HEAD_DIM] = ks_ref[...].astype(BF16)
        kaug_ref[:, NSA_HEAD_DIM:] = kone_ref[...]
        vsb_ref[0:Q_BLOCK, :] = jnp.zeros((Q_BLOCK, NSA_HEAD_DIM), BF16)
        vsb_ref[Q_BLOCK:, :] = vs_ref[...].astype(BF16)
        kwb_ref[0:WINDOW, 0:NSA_HEAD_DIM] = jnp.zeros((WINDOW, NSA_HEAD_DIM), BF16)
        kwb_ref[WINDOW:, 0:NSA_HEAD_DIM] = kw_ref[...].astype(BF16)
        for r0 in range(0, WINDOW, Q_BLOCK):
            kwb_ref[r0:r0 + Q_BLOCK, NSA_HEAD_DIM:] = jnp.where(lane == AUG_PAD, 1.0, 0.0).astype(BF16)
        kwb_ref[WINDOW:, NSA_HEAD_DIM:] = jnp.zeros((kw_ref.shape[0], LANES), BF16)
        vwb_ref[0:WINDOW, :] = jnp.zeros((WINDOW, NSA_HEAD_DIM), BF16)
        vwb_ref[WINDOW:, :] = vw_ref[...].astype(BF16)

    gens = [_nsa_unit(c0 + u, q_ref[u * Q_BLOCK:(u + 1) * Q_BLOCK, :], bcmp_ref[:, u * Q_BLOCK:(u + 1) * Q_BLOCK, :],
                      btab_ref, kc_ref, vc_ref, c2s_ref, kaug_ref, vsb_ref, kwb_ref, vwb_ref, n_sel)
            for u in range(nqb)]
    units = [None] * nqb
    while any(un is None for un in units):
        for u, gen in enumerate(gens):
            if units[u] is None:
                units[u] = next(gen)
    far_blocks = FAR_KEYS // Q_BLOCK
    last_slab = (kaug_ref.shape[0] - Q_BLOCK) // FAR_KEYS - 1

    def far_values(slab):
        return vsb_ref[pl.ds(pl.multiple_of(slab * FAR_KEYS + Q_BLOCK, Q_BLOCK), FAR_KEYS), :]

    o_sel = []
    for u0 in range(0, nqb, NSA_FAR_GROUP):
        group = units[u0:u0 + NSA_FAR_GROUP]
        qaug_far = jnp.concatenate([un[3] for un in group], axis=0)
        carry = tuple(jnp.concatenate([un[2][i] for un in group], axis=0) for i in range(3))
        n_slabs = (c0 + u0 + NSA_FAR_GROUP - 2 + far_blocks - 1) // far_blocks
        n_pairs = n_slabs // 2

        def far_logits(slab, qaug_far=qaug_far):
            r0 = pl.multiple_of(jnp.minimum(slab, last_slab) * FAR_KEYS + Q_BLOCK, Q_BLOCK)
            return _dot_nt(qaug_far, kaug_ref[pl.ds(r0, FAR_KEYS), :])

        sa_ref[...] = far_logits(0)

        def far_body(k, carry, far_logits=far_logits):
            sb_ref[...] = far_logits(2 * k + 1)
            carry = _softmax_update(sa_ref[...], far_values(2 * k), carry)
            sa_ref[...] = far_logits(2 * k + 2)
            return _softmax_update(sb_ref[...], far_values(2 * k + 1), carry)

        carry = lax.fori_loop(0, n_pairs, far_body, carry)
        carry = lax.cond(n_slabs % 2 == 1,
                         lambda cr, n_pairs=n_pairs: _softmax_update(sa_ref[...], far_values(2 * n_pairs), cr),
                         lambda cr: cr, carry)
        out = _softmax_finish(carry)
        o_sel += [out[i * rows:(i + 1) * rows] for i in range(NSA_FAR_GROUP)]

    for u in range(nqb):
        o_cmp, o_win = units[u][0], units[u][1]
        gates = _sigmoid(gate_ref[u * Q_BLOCK:(u + 1) * Q_BLOCK, :])
        for g in range(NSA_GQA):
            def gate_col(br):
                return jnp.sum(jnp.where(lane == hk * (3 * NSA_GQA) + g * 3 + br, gates, 0.0), axis=-1, keepdims=True)

            r0 = g * Q_BLOCK
            out = (gate_col(0) * o_cmp[r0:r0 + Q_BLOCK] + gate_col(1) * o_sel[u][r0:r0 + Q_BLOCK]
                   + gate_col(2) * o_win[r0:r0 + Q_BLOCK])
            o_ref[u * Q_BLOCK:(u + 1) * Q_BLOCK, g * NSA_HEAD_DIM:(g + 1) * NSA_HEAD_DIM] = out.astype(o_ref.dtype)


def _nsa_unit(c, q_in, bc_in, btab_ref, kc_ref, vc_ref, c2s_ref, kaug_ref, vsb_ref, kwb_ref, vwb_ref, n_sel):
    rows = NSA_GQA * Q_BLOCK
    lane = lax.broadcasted_iota(jnp.int32, (Q_BLOCK, LANES), 1)
    q = q_in * (NSA_HEAD_DIM ** -0.5 * LOG2E)
    q3 = jnp.concatenate([q[:, g * NSA_HEAD_DIM:(g + 1) * NSA_HEAD_DIM] for g in range(NSA_GQA)],
                         axis=0).astype(BF16)

    btab = btab_ref[...].reshape(rows, WIN_KEYS + Q_BLOCK)
    near0 = pl.multiple_of(c * Q_BLOCK, Q_BLOCK)

    qaug_win = jnp.concatenate(
        [q3, jnp.concatenate([jnp.where(lane == AUG_PAD, NEG_INF, 0.0).astype(BF16)] * NSA_GQA, axis=0)], axis=1)
    s_win = _dot_nt(qaug_win, kwb_ref[pl.ds(near0, WIN_KEYS), :]) + btab[:, 0:WIN_KEYS]
    yield None

    bc = bc_in.reshape(rows, bc_in.shape[-1])
    valid = bc > 0.5 * NEG_INF
    s = _dot_nt(q3, kc_ref[...].astype(BF16)) + bc
    yield None
    o_win = _softmax_finish(_softmax_start(s_win, vwb_ref[pl.ds(near0, WIN_KEYS), :]))
    yield None
    m = jnp.max(s, axis=-1, keepdims=True)
    p = jnp.where(valid, jnp.exp2(s - m), 0.0)
    p = p / jnp.maximum(jnp.sum(p, axis=-1, keepdims=True), 1e-30)
    o_cmp = _dot(p.astype(BF16), vc_ref[...].astype(BF16))

    psum = p[0:Q_BLOCK] + p[Q_BLOCK:2 * Q_BLOCK] + p[2 * Q_BLOCK:3 * Q_BLOCK]
    ph = psum.astype(BF16)
    plo = (psum - ph.astype(F32)).astype(BF16)
    c2s = c2s_ref[...]
    imp_t = _dot_nt(c2s, ph) + _dot_nt(c2s, plo)
    nsp = -(-n_sel // SUBLANES) * SUBLANES
    imp_t = imp_t[0:nsp]
    tq = c * Q_BLOCK + lax.broadcasted_iota(jnp.int32, (nsp, Q_BLOCK), 1)
    jj = lax.broadcasted_iota(jnp.int32, (nsp, Q_BLOCK), 0)
    valid_sel = (jj < n_sel) & (jj * SEL_BLOCK <= tq)
    back = tq // SEL_BLOCK - jj
    force = (jj == 0) | ((back >= 0) & (back < SEL_LOCAL))
    score = jnp.where(valid_sel, imp_t + jnp.where(force, FORCE_BONUS, 0.0), NEG_INF)
    yield None
    n_tiles = nsp // SUBLANES
    tiles = [score[v * SUBLANES:(v + 1) * SUBLANES] for v in range(n_tiles)]
    sub = lax.broadcasted_iota(jnp.int32, (SUBLANES, Q_BLOCK), 0)
    cnts = [jnp.zeros((SUBLANES, Q_BLOCK), F32) for _ in range(n_tiles)]
    for jp in range(n_sel):
        v0, r0 = divmod(jp, SUBLANES)
        r = tiles[v0][r0:r0 + 1, :]
        for v in range(n_tiles):
            gt = jnp.where(r > tiles[v], 1.0, 0.0)
            ge = jnp.where(r >= tiles[v], 1.0, 0.0)
            cnts[v] = cnts[v] + (gt if v < v0 else ge if v > v0 else jnp.where(sub > r0, ge, gt))
        if jp % (2 * SUBLANES) == 2 * SUBLANES - 1:
            yield None
    cnt = jnp.concatenate(cnts, axis=0)
    keep = valid_sel & (cnt < float(min(SEL_TOPK, n_sel)))

    ext = lax.broadcasted_iota(jnp.int32, (LANES - nsp, Q_BLOCK), 0) + nsp
    aug = jnp.concatenate([jnp.where(keep, 0.0, NEG_INF), jnp.where(ext == AUG_PAD, NEG_INF, 0.0)], axis=0)
    aug_near = aug.T
    aug_far = jnp.where((lane >= 2 * (c - 1)) & (lane < AUG_BIAS_HI), NEG_INF, aug_near)
    far_parts = []
    for g in range(NSA_GQA):
        bfar = btab[g * Q_BLOCK:(g + 1) * Q_BLOCK, WIN_KEYS:]
        hi = bfar.astype(BF16).astype(F32)
        far_parts.append(jnp.where(lane == AUG_BIAS_HI, hi, jnp.where(lane == AUG_BIAS_LO, bfar - hi, aug_far)))
    qaug_near = jnp.concatenate([q3, jnp.concatenate([aug_near.astype(BF16)] * NSA_GQA, axis=0)], axis=1)
    qaug_far = jnp.concatenate([q3, jnp.concatenate(far_parts, axis=0).astype(BF16)], axis=1)

    s = _dot_nt(qaug_near, kaug_ref[pl.ds(near0, NEAR_KEYS), :]) + btab[:, WIN_KEYS - NEAR_KEYS:WIN_KEYS]
    yield None
    carry = _softmax_start(s, vsb_ref[pl.ds(near0, NEAR_KEYS), :])
    yield o_cmp, o_win, carry, qaug_far


def _nsa_attention(proj, cmp_kv, bias_cmp, bias_tab, bsz, seq):
    nq = seq // Q_BLOCK
    ncp = seq // CMP_STRIDE
    n_sel = seq // SEL_BLOCK
    assert n_sel <= AUG_BIAS_HI and NSA_HEAD_DIM == LANES
    assert seq % (2 * FAR_KEYS) == 0, "far steps walk the keys two 512-key slabs at a time"
    key = np.arange(-Q_BLOCK, seq)[:, None]
    ln = np.arange(LANES)[None, :]
    kone = np.where(key < 0, ln == AUG_PAD,
                    ((ln < AUG_BIAS_HI) & (key // SEL_BLOCK == ln)) | (ln == AUG_BIAS_HI) | (ln == AUG_BIAS_LO))
    kone = jnp.asarray(kone, BF16)
    c_start = np.arange(ncp)[None, :] * CMP_STRIDE
    s_start = np.arange(LANES)[:, None] * SEL_BLOCK
    overlap = np.clip(np.minimum(c_start + CMP_BLOCK, s_start + SEL_BLOCK) - np.maximum(c_start, s_start), 0, None)
    overlap = np.where(np.arange(LANES)[:, None] < n_sel, overlap, 0)
    c2s_t = jnp.asarray(overlap / CMP_BLOCK, BF16)
    gqa_w = NSA_GQA * NSA_HEAD_DIM
    full = lambda col: pl.BlockSpec((seq, NSA_HEAD_DIM), lambda b, h, c: (b, col // NSA_HEAD_DIM + h))
    nqb = NSA_Q_BLOCKS_PER_STEP
    ns = nq // nqb
    qr = nqb * Q_BLOCK
    kern = functools.partial(_nsa_kernel, n_sel=n_sel, nqb=nqb)
    return pl.pallas_call(
        kern,
        grid=(bsz, NSA_KV_HEADS, ns),
        in_specs=[pl.BlockSpec((qr, gqa_w), lambda b, h, c: (b * ns + c, COL_Q // gqa_w + h)),
                  pl.BlockSpec((qr, LANES), lambda b, h, c: (b * ns + c, COL_GATE // LANES)),
                  pl.BlockSpec((None, None, None, ncp, NSA_HEAD_DIM), lambda b, h, c: (b, 0, h, 0, 0)),
                  pl.BlockSpec((None, None, None, ncp, NSA_HEAD_DIM), lambda b, h, c: (b, 1, h, 0, 0)),
                  full(COL_KS), full(COL_VS), full(COL_KW), full(COL_VW),
                  pl.BlockSpec((NSA_GQA, qr, ncp), lambda b, h, c: (h, c, 0)),
                  pl.BlockSpec((NSA_GQA, Q_BLOCK, WIN_KEYS + Q_BLOCK), lambda b, h, c: (h, 0, 0)),
                  pl.BlockSpec((seq + Q_BLOCK, LANES), lambda b, h, c: (0, 0)),
                  pl.BlockSpec((LANES, ncp), lambda b, h, c: (0, 0))],
        out_specs=pl.BlockSpec((qr, gqa_w), lambda b, h, c: (b * ns + c, h)),
        out_shape=jax.ShapeDtypeStruct((bsz * seq, NSA_WIDTH), BF16),
        scratch_shapes=[pltpu.VMEM((seq + Q_BLOCK, 2 * NSA_HEAD_DIM), BF16),
                        pltpu.VMEM((seq + Q_BLOCK, NSA_HEAD_DIM), BF16),
                        pltpu.VMEM((seq + WINDOW, 2 * NSA_HEAD_DIM), BF16),
                        pltpu.VMEM((seq + WINDOW, NSA_HEAD_DIM), BF16),
                        pltpu.VMEM((NSA_FAR_GROUP * NSA_GQA * Q_BLOCK, FAR_KEYS), F32),
                        pltpu.VMEM((NSA_FAR_GROUP * NSA_GQA * Q_BLOCK, FAR_KEYS), F32)],
        compiler_params=_params(("arbitrary", "arbitrary", "arbitrary")),
        name="nsa_attention",
    )(proj, proj, cmp_kv, cmp_kv, proj, proj, proj, proj, bias_cmp, bias_tab, kone, c2s_t)


_IN_OFF = tuple(int(v) for v in np.cumsum((0,) + IN_SIZES))
_W_IN_COPIES = ((_IN_OFF[0], SSD_WIDTH, COL_Z), (_IN_OFF[1], SSD_WIDTH, COL_XS),
                (_IN_OFF[1] + SSD_WIDTH, 512, COL_B), (_IN_OFF[1] + SSD_WIDTH + 512, 512, COL_C),
                (_IN_OFF[3], NSA_WIDTH, COL_Q), (_IN_OFF[4], 6 * NSA_KV_WIDTH, COL_KC),
                (_IN_OFF[11], 4 * HGRN_WIDTH, COL_HQ))
_W_IN_NARROW = ((_IN_OFF[2], IN_SIZES[2], COL_DT, COL_GATE - COL_DT), (_IN_OFF[10], IN_SIZES[10], COL_GATE, COL_HQ - COL_GATE))


def _relayout_kernel(w_ref, o_ref):
    for src, width, dst in _W_IN_COPIES:
        o_ref[:, dst:dst + width] = w_ref[:, src:src + width].astype(BF16)
    for src, valid, dst, padded in _W_IN_NARROW:
        tile = w_ref[:, src:src + LANES]
        lane = lax.broadcasted_iota(jnp.int32, tile.shape, 1)
        o_ref[:, dst:dst + LANES] = jnp.where(lane < valid, tile, 0.0).astype(BF16)
        if padded > LANES:
            o_ref[:, dst + LANES:dst + padded] = jnp.zeros((tile.shape[0], padded - LANES), BF16)


def _relayout_w_in(w, tr=128):
    depth, d, n = w.shape
    return pl.pallas_call(
        _relayout_kernel,
        grid=(depth, d // tr),
        in_specs=[pl.BlockSpec((None, tr, n), lambda l, i: (l, i, 0))],
        out_specs=pl.BlockSpec((None, tr, IN_PAD), lambda l, i: (l, i, 0)),
        out_shape=jax.ShapeDtypeStruct((depth, d, IN_PAD), BF16),
        compiler_params=_params(("parallel", "parallel")),
        name="w_in_relayout",
    )(w)


def _tiles(seq):
    return dict(in_proj=(1024, 1024), out_proj=(1024, 512), gate_up=(min(2048, seq), 256), down=(256, 1024))


def kernel(x, norm_mix_w, w_in, ssd_conv_w, ssd_conv_b, ssd_dt_bias, ssd_a_log, ssd_d, ssd_norm_w, nsa_cmp_pe, nsa_cmp_w1, nsa_cmp_w2, rel_bias, hgrn_lb_logits, hgrn_norm_w, w_out, norm_ffn_w, ffn_w_gu, ffn_conv_w, ffn_conv_b, ffn_w_down, norm_f_w):
    bsz, seq, d = x.shape
    depth = w_in.shape[0]
    xr = x.reshape(bsz * seq, d).astype(F32)
    bias_cmp, bias_tab = _nsa_bias_tables(rel_bias, seq)
    tiles = _tiles(seq)
    w_in_b = _relayout_w_in(jnp.pad(w_in.astype(BF16), ((0, 0), (0, 0), (0, -w_in.shape[2] % LANES))))
    w_out_b = w_out.astype(BF16)
    w_gu_b = ffn_w_gu.astype(BF16)
    w_down_b = ffn_w_down.astype(BF16)
    xw, ssq = _prenorm(xr, norm_mix_w[0])
    for l in range(depth):
        proj = _matmul(xw, ssq, w_in_b, l, F32, *tiles["in_proj"], name="in_proj")
        y_ssd = _ssd_mixer(proj, bsz, seq, ssd_conv_w[l].astype(F32), ssd_conv_b[l].astype(F32), ssd_dt_bias[l],
                           ssd_a_log[l], ssd_d[l], ssd_norm_w[l].astype(F32))
        cmp_kv = _nsa_compress(proj, bsz, seq, nsa_cmp_pe[l], nsa_cmp_w1[l], nsa_cmp_w2[l])
        y_nsa = _nsa_attention(proj, cmp_kv, bias_cmp, bias_tab, bsz, seq)
        y_hgrn = _hgrn_mixer(proj, bsz, seq, hgrn_lb_logits, hgrn_norm_w[l], l)
        xr, xw, ssq = _out_proj([y_ssd, y_nsa, y_hgrn], w_out_b, l, xr, norm_ffn_w[l], *tiles["out_proj"])
        act = _gate_up(xw, ssq, w_gu_b, l, ffn_conv_w[l].astype(F32), ffn_conv_b[l].astype(F32), seq,
                       *tiles["gate_up"])
        if l + 1 < depth:
            xr, xw, ssq = _down_proj_norm(act, w_down_b, l, xr, norm_mix_w[l + 1], *tiles["down"])
        else:
            xr = _matmul_residual_wres(act, w_down_b, l, xr, *tiles["down"], name="ffn_down")
    out = _rmsnorm(xr, norm_f_w, x.dtype)
    return out.reshape(bsz, seq, d)
```

```python
import functools
import math

import numpy as np
import jax
import jax.numpy as jnp
from jax import lax
from jax.experimental import pallas as pl
from jax.experimental.pallas import tpu as pltpu

F32 = jnp.float32
BF16 = jnp.bfloat16

D_MODEL = 4096
SSD_HEAD_DIM = 64
SSD_WIDTH = 1536
SSD_HEADS = 24
SSD_GROUPS = 4
SSD_HPG = 6
SSD_STATE = 128
SSD_CONV = 4
SSD_CHUNK = 128
SSD_GW = SSD_WIDTH // SSD_GROUPS
NSA_HEAD_DIM = 128
NSA_WIDTH = 1536
NSA_HEADS = 12
NSA_KV_HEADS = 4
NSA_GQA = 3
NSA_KV_WIDTH = 512
CMP_BLOCK = 32
CMP_STRIDE = 16
CMP_HIDDEN = 256
SEL_BLOCK = 64
SEL_TOPK = 16
SEL_LOCAL = 2
WINDOW = 512
Q_BLOCK = 128
FORCE_BONUS = 1e4
HGRN_WIDTH = 1024
HGRN_HEADS = 8
HGRN_DIM = 128
HGRN_CHUNK = 32
REL_BUCKETS = 32
REL_MAX_DIST = 128
D_FF = 11008
FFN_CONV = 3
EPS = 1e-6
NEG_INF = -1e30

IN_SIZES = (SSD_WIDTH, SSD_WIDTH + 2 * SSD_GROUPS * SSD_STATE, SSD_HEADS, NSA_WIDTH,
            NSA_KV_WIDTH, NSA_KV_WIDTH, NSA_KV_WIDTH, NSA_KV_WIDTH, NSA_KV_WIDTH, NSA_KV_WIDTH,
            3 * NSA_HEADS, HGRN_WIDTH, HGRN_WIDTH, HGRN_WIDTH, HGRN_WIDTH)

LANES = 128
SUBLANES = 8
VMEM_LIMIT = 56 * 1024 * 1024

COL_Z = 0
COL_XS = 1536
COL_Q = 3072
COL_B = 4608
COL_C = 5120
COL_KC = 5632
COL_VC = 6144
COL_KS = 6656
COL_VS = 7168
COL_KW = 7680
COL_VW = 8192
COL_DT = 8704
COL_GATE = 8832
COL_HQ = 9216
COL_HF = 10240
COL_HI = 11264
COL_HG = 12288
IN_PAD = 13312

LOG2E = math.log2(math.e)
AUG_BIAS_HI = 64
AUG_BIAS_LO = 65
AUG_PAD = 66
WIN_KEYS = WINDOW + Q_BLOCK
NEAR_KEYS = 2 * Q_BLOCK
FAR_KEYS = 4 * Q_BLOCK
HGRN_HEADS_INTERLEAVED = 8
NSA_Q_BLOCKS_PER_STEP = 4
NSA_FAR_GROUP = 2


def _params(semantics):
    return pltpu.CompilerParams(dimension_semantics=semantics, vmem_limit_bytes=VMEM_LIMIT)


def _sigmoid(x):
    return 1.0 / (1.0 + jnp.exp(-x))


def _silu(x):
    return x * _sigmoid(x)


def _dot(a, b):
    return jnp.dot(a, b, preferred_element_type=F32)


def _dot_nt(a, b):
    return lax.dot_general(a, b, (((1,), (1,)), ((), ())), preferred_element_type=F32)


def _split3(x):
    hi = x.astype(BF16)
    r1 = x - hi.astype(F32)
    mid = r1.astype(BF16)
    lo = (r1 - mid.astype(F32)).astype(BF16)
    return hi, mid, lo


def _rmsnorm_kernel(x_ref, w_ref, o_ref):
    x = x_ref[...]
    ms = jnp.mean(x * x, axis=-1, keepdims=True)
    o_ref[...] = (x * lax.rsqrt(ms + EPS) * w_ref[...]).astype(o_ref.dtype)


def _rmsnorm(x, w, out_dtype, tm=256):
    m, d = x.shape
    return pl.pallas_call(
        _rmsnorm_kernel,
        grid=(m // tm,),
        in_specs=[pl.BlockSpec((tm, d), lambda i: (i, 0)), pl.BlockSpec((1, d), lambda i: (0, 0))],
        out_specs=pl.BlockSpec((tm, d), lambda i: (i, 0)),
        out_shape=jax.ShapeDtypeStruct((m, d), out_dtype),
        compiler_params=_params(("parallel",)),
        name="rmsnorm",
    )(x, w.reshape(1, d).astype(F32))


def _lane_fold(sq):
    part = sq[:, 0:LANES]
    for c in range(1, sq.shape[1] // LANES):
        part = part + sq[:, c * LANES:(c + 1) * LANES]
    return part


def _rinv_lanes(ssq, d):
    part = ssq[0]
    for p in range(1, ssq.shape[0]):
        part = part + ssq[p]
    tot = jnp.sum(part, axis=-1, keepdims=True)
    return jnp.broadcast_to(lax.rsqrt(tot * (1.0 / d) + EPS), part.shape)


def _scale_rows(acc, rinv):
    return jnp.concatenate([acc[:, c * LANES:(c + 1) * LANES] * rinv for c in range(acc.shape[1] // LANES)], axis=1)


def _prenorm_kernel(x_ref, w_ref, xw_ref, ssq_ref):
    x = x_ref[...]
    xw_ref[...] = (x * w_ref[...]).astype(xw_ref.dtype)
    ssq_ref[0] = _lane_fold(x * x)


def _prenorm(x, w, tm=256):
    m, d = x.shape
    return pl.pallas_call(
        _prenorm_kernel,
        grid=(m // tm,),
        in_specs=[pl.BlockSpec((tm, d), lambda i: (i, 0)), pl.BlockSpec((1, d), lambda i: (0, 0))],
        out_specs=[pl.BlockSpec((tm, d), lambda i: (i, 0)), pl.BlockSpec((1, tm, LANES), lambda i: (0, i, 0))],
        out_shape=[jax.ShapeDtypeStruct((m, d), BF16), jax.ShapeDtypeStruct((1, m, LANES), F32)],
        compiler_params=_params(("parallel",)),
        name="prenorm",
    )(x, w.reshape(1, d).astype(F32))


def _mm_kernel(a_ref, ssq_ref, w_ref, o_ref, rinv_ref, *, d):
    @pl.when(pl.program_id(1) == 0)
    def _():
        rinv_ref[...] = _rinv_lanes(ssq_ref[...], d)

    o_ref[...] = _scale_rows(_dot(a_ref[...], w_ref[...]), rinv_ref[...]).astype(o_ref.dtype)


def _mm_res_kernel(a_ref, w_ref, r_ref, o_ref):
    o_ref[...] = r_ref[...] + _dot(a_ref[...], w_ref[...])


def _matmul(xw, ssq, w, layer, out_dtype, tm, tn, name):
    m, k = xw.shape
    n = w.shape[2]
    return pl.pallas_call(
        functools.partial(_mm_kernel, d=k),
        grid=(m // tm, n // tn),
        in_specs=[pl.BlockSpec((tm, k), lambda i, j: (i, 0)),
                  pl.BlockSpec((ssq.shape[0], tm, LANES), lambda i, j: (0, i, 0)),
                  pl.BlockSpec((None, k, tn), lambda i, j: (layer, 0, j))],
        out_specs=pl.BlockSpec((tm, tn), lambda i, j: (i, j)),
        out_shape=jax.ShapeDtypeStruct((m, n), out_dtype),
        scratch_shapes=[pltpu.VMEM((tm, LANES), F32)],
        compiler_params=_params(("arbitrary", "arbitrary")),
        name=name,
    )(xw, ssq, w)


def _out_proj_kernel(*refs):
    *a_refs, w_ref, r_ref, nw_ref, o_ref, xw_ref, ssq_ref = refs
    j = pl.program_id(1)
    acc = r_ref[...]
    k0 = 0
    for a_ref in a_refs:
        acc = acc + _dot(a_ref[...], w_ref[k0:k0 + a_ref.shape[1], :])
        k0 += a_ref.shape[1]
    o_ref[...] = acc
    xw_ref[...] = (acc * nw_ref[...]).astype(xw_ref.dtype)
    part = _lane_fold(acc * acc)

    @pl.when(j == 0)
    def _():
        ssq_ref[0] = part

    @pl.when(j > 0)
    def _():
        ssq_ref[0] = ssq_ref[0] + part


def _out_proj(parts, w, layer, res, norm_w, tm, tn):
    m = res.shape[0]
    k, n = w.shape[1], w.shape[2]
    assert sum(p.shape[1] for p in parts) == k
    return pl.pallas_call(
        _out_proj_kernel,
        grid=(m // tm, n // tn),
        in_specs=[pl.BlockSpec((tm, p.shape[1]), lambda i, j: (i, 0)) for p in parts]
        + [pl.BlockSpec((None, k, tn), lambda i, j: (layer, 0, j)),
           pl.BlockSpec((tm, tn), lambda i, j: (i, j)),
           pl.BlockSpec((1, tn), lambda i, j: (0, j))],
        out_specs=[pl.BlockSpec((tm, tn), lambda i, j: (i, j)), pl.BlockSpec((tm, tn), lambda i, j: (i, j)),
                   pl.BlockSpec((1, tm, LANES), lambda i, j: (0, i, 0))],
        out_shape=[jax.ShapeDtypeStruct((m, n), F32), jax.ShapeDtypeStruct((m, n), BF16),
                   jax.ShapeDtypeStruct((1, m, LANES), F32)],
        compiler_params=_params(("arbitrary", "arbitrary")),
        name="out_proj",
    )(*parts, w, res, norm_w.reshape(1, n).astype(F32))


def _down_norm_kernel(a_ref, w_ref, r_ref, nw_ref, o_ref, xw_ref, ssq_ref):
    acc = r_ref[...] + _dot(a_ref[...], w_ref[...])
    o_ref[...] = acc
    xw_ref[...] = (acc * nw_ref[...]).astype(xw_ref.dtype)
    ssq_ref[...] = _lane_fold(acc * acc)


def _down_proj_norm(a, w, layer, res, norm_w, tm, tn):
    m, k = a.shape
    n = w.shape[2]
    return pl.pallas_call(
        _down_norm_kernel,
        grid=(n // tn, m // tm),
        in_specs=[pl.BlockSpec((tm, k), lambda j, i: (i, 0)),
                  pl.BlockSpec((None, k, tn), lambda j, i: (layer, 0, j), pipeline_mode=pl.Buffered(1)),
                  pl.BlockSpec((tm, tn), lambda j, i: (i, j)),
                  pl.BlockSpec((1, tn), lambda j, i: (0, j))],
        out_specs=[pl.BlockSpec((tm, tn), lambda j, i: (i, j)), pl.BlockSpec((tm, tn), lambda j, i: (i, j)),
                   pl.BlockSpec((None, tm, LANES), lambda j, i: (j, i, 0))],
        out_shape=[jax.ShapeDtypeStruct((m, n), F32), jax.ShapeDtypeStruct((m, n), BF16),
                   jax.ShapeDtypeStruct((n // tn, m, LANES), F32)],
        compiler_params=_params(("arbitrary", "arbitrary")),
        name="ffn_down",
    )(a, w, res, norm_w.reshape(1, n).astype(F32))


def _matmul_residual_wres(a, w, layer, res, tm, tn, name):
    m, k = a.shape
    n = w.shape[2]
    return pl.pallas_call(
        _mm_res_kernel,
        grid=(n // tn, m // tm),
        in_specs=[pl.BlockSpec((tm, k), lambda j, i: (i, 0)),
                  pl.BlockSpec((None, k, tn), lambda j, i: (layer, 0, j), pipeline_mode=pl.Buffered(1)),
                  pl.BlockSpec((tm, tn), lambda j, i: (i, j))],
        out_specs=pl.BlockSpec((tm, tn), lambda j, i: (i, j)),
        out_shape=jax.ShapeDtypeStruct((m, n), F32),
        compiler_params=_params(("arbitrary", "arbitrary")),
        name=name,
    )(a, w, res)


GU_ROW_CHUNK = 128


def _gu_kernel(h_ref, ssq_ref, wg_ref, wu_ref, cw_ref, cb_ref, o_ref, ga_ref, ua_ref, gb_ref, ub_ref, halo_ref,
               rinv_ref, *, tm, nj, tiles_per_seq):
    t = pl.program_id(0)
    tp = jnp.maximum(t - 1, 0)
    ip = tp // nj
    jp = tp % nj
    rc = min(GU_ROW_CHUNK, tm)

    @pl.when(t == 0)
    def _():
        gb_ref[...] = jnp.zeros_like(gb_ref)
        ub_ref[...] = jnp.zeros_like(ub_ref)
        halo_ref[...] = jnp.zeros_like(halo_ref)

    @pl.when(t % nj == 0)
    def _():
        rinv_ref[...] = _rinv_lanes(ssq_ref[...], h_ref.shape[1])

    def step(g_cur, u_cur, g_prev, u_prev):
        g_prev[0:SUBLANES, :] = jnp.where(ip % tiles_per_seq == 0, 0.0, halo_ref[jp])
        halo_ref[jp] = g_prev[tm:tm + SUBLANES, :]
        cw = cw_ref[...]
        for r in range(tm // rc):
            hr = h_ref[r * rc:(r + 1) * rc, :]
            rinv = rinv_ref[r * rc:(r + 1) * rc, :]
            g_cur[SUBLANES + r * rc:SUBLANES + (r + 1) * rc, :] = _scale_rows(_dot(hr, wg_ref[...]), rinv)
            u_cur[r * rc:(r + 1) * rc, :] = _scale_rows(_dot(hr, wu_ref[...]), rinv)
            acc = cb_ref[...]
            for sh in range(FFN_CONV):
                acc = acc + cw[FFN_CONV - 1 - sh:FFN_CONV - sh, :] * g_prev[pl.ds(SUBLANES - sh + r * rc, rc), :]
            o_ref[r * rc:(r + 1) * rc, :] = (_silu(acc) * u_prev[r * rc:(r + 1) * rc, :]).astype(o_ref.dtype)

    @pl.when(t % 2 == 0)
    def _():
        step(ga_ref, ua_ref, gb_ref, ub_ref)

    @pl.when(t % 2 == 1)
    def _():
        step(gb_ref, ub_ref, ga_ref, ua_ref)


def _gate_up(h, ssq, w_gu, layer, conv_w, conv_b, seq, tm, tn):
    m, k = h.shape
    nf = conv_w.shape[1]
    nj = nf // tn
    last = (m // tm) * nj - 1
    cur = lambda t: jnp.minimum(t, last)
    prev = lambda t: jnp.maximum(t - 1, 0)
    kern = functools.partial(_gu_kernel, tm=tm, nj=nj, tiles_per_seq=seq // tm)
    return pl.pallas_call(
        kern,
        grid=(last + 2,),
        in_specs=[pl.BlockSpec((tm, k), lambda t: (cur(t) // nj, 0), pipeline_mode=pl.Buffered(1)),
                  pl.BlockSpec((ssq.shape[0], tm, LANES), lambda t: (0, cur(t) // nj, 0)),
                  pl.BlockSpec((None, k, tn), lambda t: (layer, 0, cur(t) % nj)),
                  pl.BlockSpec((None, k, tn), lambda t: (layer, 0, cur(t) % nj + nj)),
                  pl.BlockSpec((FFN_CONV, tn), lambda t: (0, prev(t) % nj)),
                  pl.BlockSpec((1, tn), lambda t: (0, prev(t) % nj))],
        out_specs=pl.BlockSpec((tm, tn), lambda t: (prev(t) // nj, prev(t) % nj)),
        out_shape=jax.ShapeDtypeStruct((m, nf), BF16),
        scratch_shapes=[pltpu.VMEM((tm + SUBLANES, tn), F32), pltpu.VMEM((tm, tn), F32),
                        pltpu.VMEM((tm + SUBLANES, tn), F32), pltpu.VMEM((tm, tn), F32),
                        pltpu.VMEM((nj, SUBLANES, tn), F32), pltpu.VMEM((tm, LANES), F32)],
        compiler_params=_params(("arbitrary",)),
        name="ffn_gate_up_conv",
    )(h, ssq, w_gu, w_gu, conv_w, conv_b.reshape(1, nf))


def _ssd_kernel(z_ref, xs_ref, b_ref, c_ref, dt_ref, cwx_ref, cwb_ref, cwc_ref, cbx_ref, cbb_ref, cbc_ref,
                dtb_ref, alog_ref, dsk_ref, nw_ref, tri_ref, o_ref,
                extx_ref, extb_ref, extc_ref, state_ref, *, ts):
    s = pl.program_id(1)

    @pl.when(s == 0)
    def _():
        extx_ref[...] = jnp.zeros_like(extx_ref)
        extb_ref[...] = jnp.zeros_like(extb_ref)
        extc_ref[...] = jnp.zeros_like(extc_ref)
        state_ref[...] = jnp.zeros_like(state_ref)

    def conv_silu(u_ref, ext_ref, w_ref, bias_ref):
        ext_ref[0:SUBLANES, :] = ext_ref[ts:ts + SUBLANES, :]
        ext_ref[SUBLANES:, :] = u_ref[...]
        w = w_ref[...]
        acc = bias_ref[...] + w[SSD_CONV - 1:SSD_CONV, :] * u_ref[...]
        for sh in range(1, SSD_CONV):
            acc = acc + w[SSD_CONV - 1 - sh:SSD_CONV - sh, :] * ext_ref[pl.ds(SUBLANES - sh, ts), :]
        return _silu(acc)

    xs = conv_silu(xs_ref, extx_ref, cwx_ref, cbx_ref)
    bm = conv_silu(b_ref, extb_ref, cwb_ref, cbb_ref)
    cm = conv_silu(c_ref, extc_ref, cwc_ref, cbc_ref)
    dtr = dt_ref[...] + dtb_ref[...]
    dt = jnp.maximum(dtr, 0.0) + jnp.log1p(jnp.exp(-jnp.abs(dtr)))
    a = dt * (-jnp.exp(alog_ref[...]))
    z = z_ref[...]
    tri = tri_ref[...]
    dsk = dsk_ref[...]
    nw = nw_ref[...]
    li = lax.broadcasted_iota(jnp.int32, (SSD_CHUNK, SSD_CHUNK), 0)
    si = lax.broadcasted_iota(jnp.int32, (SSD_CHUNK, SSD_CHUNK), 1)
    causal = li >= si
    low_half = si < SSD_HEAD_DIM
    low_half_row = low_half[0:1, :]

    for ck in range(ts // SSD_CHUNK):
        r0 = ck * SSD_CHUNK
        a_c = a[r0:r0 + SSD_CHUNK]
        hi, mid, lo = _split3(a_c)
        acs = _dot(tri, hi) + _dot(tri, mid) + _dot(tri, lo)
        acs_t = acs.T
        dt_t = dt[r0:r0 + SSD_CHUNK].T
        a_last_b = jnp.broadcast_to(acs_t[:, SSD_CHUNK - 1:SSD_CHUNK], acs_t.shape)
        dtd_t = dt_t * jnp.exp(a_last_b - acs_t)
        chunk_decay = jnp.exp(acs[SSD_CHUNK - 1:SSD_CHUNK, :])
        for g in range(SSD_GROUPS):
            cg = cm[r0:r0 + SSD_CHUNK, g * SSD_STATE:(g + 1) * SSD_STATE]
            bg = bm[r0:r0 + SSD_CHUNK, g * SSD_STATE:(g + 1) * SSD_STATE]
            cb = _dot_nt(cg.astype(BF16), bg.astype(BF16))
            bgt = bg.T
            ys = []
            for pr in range(SSD_HPG // 2):
                pair = (g * SSD_HPG) // 2 + pr
                c0 = pair * LANES
                xs_p = xs[r0:r0 + SSD_CHUNK, c0:c0 + LANES].astype(BF16)
                prev = state_ref[pair]
                rhs = jnp.concatenate([xs_p, prev.astype(BF16)], axis=0)
                y_h, st_h, cd_h = [], [], []
                for hh in (2 * pair, 2 * pair + 1):
                    colb = jnp.broadcast_to(acs[:, hh:hh + 1], (SSD_CHUNK, SSD_CHUNK))
                    seg = colb - acs_t[hh:hh + 1, :]
                    decay = jnp.where(causal, jnp.exp(jnp.where(causal, seg, 0.0)), 0.0)
                    intra = cb * decay * dt_t[hh:hh + 1, :]
                    inter = cg * jnp.exp(colb)
                    lhs = jnp.concatenate([intra, inter], axis=1).astype(BF16)
                    y_h.append(_dot(lhs, rhs))
                    st_h.append(_dot((bgt * dtd_t[hh:hh + 1, :]).astype(BF16), xs_p))
                    cd_h.append(jnp.broadcast_to(chunk_decay[:, hh:hh + 1], (1, LANES)))
                ys.append(jnp.where(low_half, y_h[0], y_h[1]))
                state_ref[pair] = (prev * jnp.where(low_half_row, cd_h[0], cd_h[1])
                                   + jnp.where(low_half, st_h[0], st_h[1]))
            yg = jnp.concatenate(ys, axis=1)
            yg = yg + xs[r0:r0 + SSD_CHUNK, g * SSD_GW:(g + 1) * SSD_GW] * dsk[:, g * SSD_GW:(g + 1) * SSD_GW]
            u = yg * _silu(z[r0:r0 + SSD_CHUNK, g * SSD_GW:(g + 1) * SSD_GW])
            u = u * lax.rsqrt(jnp.mean(u * u, axis=-1, keepdims=True) + EPS)
            o_ref[r0:r0 + SSD_CHUNK, g * SSD_GW:(g + 1) * SSD_GW] = (
                u * nw[:, g * SSD_GW:(g + 1) * SSD_GW]).astype(o_ref.dtype)


def _pad_lanes(v, n=LANES):
    v = v.reshape(1, -1).astype(F32)
    return jnp.pad(v, ((0, 0), (0, n - v.shape[1])))


def _ssd_mixer(proj, bsz, seq, conv_w, conv_b, dt_bias, a_log, d_skip, norm_w, ts=256):
    nsteps = seq // ts
    row = lambda b, s: b * nsteps + s
    xw, bw, cw = conv_w[:, :SSD_WIDTH], conv_w[:, SSD_WIDTH:SSD_WIDTH + 512], conv_w[:, SSD_WIDTH + 512:]
    xb, bb, cb = conv_b[:SSD_WIDTH], conv_b[SSD_WIDTH:SSD_WIDTH + 512], conv_b[SSD_WIDTH + 512:]
    tri = jnp.asarray(np.tril(np.ones((SSD_CHUNK, SSD_CHUNK), np.float32)), BF16)
    const = lambda shape: pl.BlockSpec(shape, lambda b, s: (0,) * len(shape))
    kern = functools.partial(_ssd_kernel, ts=ts)
    return pl.pallas_call(
        kern,
        grid=(bsz, nsteps),
        in_specs=[pl.BlockSpec((ts, SSD_WIDTH), lambda b, s: (row(b, s), COL_Z // SSD_WIDTH)),
                  pl.BlockSpec((ts, SSD_WIDTH), lambda b, s: (row(b, s), COL_XS // SSD_WIDTH)),
                  pl.BlockSpec((ts, 512), lambda b, s: (row(b, s), COL_B // 512)),
                  pl.BlockSpec((ts, 512), lambda b, s: (row(b, s), COL_C // 512)),
                  pl.BlockSpec((ts, LANES), lambda b, s: (row(b, s), COL_DT // LANES)),
                  const((SSD_CONV, SSD_WIDTH)), const((SSD_CONV, 512)), const((SSD_CONV, 512)),
                  const((1, SSD_WIDTH)), const((1, 512)), const((1, 512)),
                  const((1, LANES)), const((1, LANES)), const((1, SSD_WIDTH)), const((1, SSD_WIDTH)),
                  const((SSD_CHUNK, SSD_CHUNK))],
        out_specs=pl.BlockSpec((ts, SSD_WIDTH), lambda b, s: (row(b, s), 0)),
        out_shape=jax.ShapeDtypeStruct((bsz * seq, SSD_WIDTH), BF16),
        scratch_shapes=[pltpu.VMEM((ts + SUBLANES, SSD_WIDTH), F32), pltpu.VMEM((ts + SUBLANES, 512), F32),
                        pltpu.VMEM((ts + SUBLANES, 512), F32),
                        pltpu.VMEM((SSD_HEADS // 2, SSD_STATE, 2 * SSD_HEAD_DIM), F32)],
        compiler_params=_params(("arbitrary", "arbitrary")),
        name="ssd_mixer",
    )(proj, proj, proj, proj, proj, xw, bw, cw, xb.reshape(1, -1), bb.reshape(1, -1), cb.reshape(1, -1),
      _pad_lanes(dt_bias), _pad_lanes(a_log), jnp.repeat(d_skip.astype(F32), SSD_HEAD_DIM).reshape(1, -1),
      norm_w.reshape(1, -1), tri)


def _hgrn_kernel(q_ref, f_ref, i_ref, g_ref, lbl_ref, nw_ref, cum_ref, o_ref, state_ref, *, ts, layer):
    s = pl.program_id(1)

    @pl.when(s == 0)
    def _():
        state_ref[...] = jnp.zeros_like(state_ref)

    nck = ts // HGRN_CHUNK
    lg = lbl_ref[...]
    e = jnp.exp(lg - jnp.max(lg, axis=0, keepdims=True))
    sm = e / jnp.sum(e, axis=0, keepdims=True)
    ridx = lax.broadcasted_iota(jnp.int32, lg.shape, 0)
    lb = jnp.sum(jnp.where((ridx >= 1) & (ridx <= layer), sm, 0.0), axis=0, keepdims=True)

    cum = cum_ref[...]
    nw = nw_ref[...]
    ti = lax.broadcasted_iota(jnp.int32, (ts, ts), 0)
    tj = lax.broadcasted_iota(jnp.int32, (ts, ts), 1)
    blockcausal = (ti // HGRN_CHUNK == tj // HGRN_CHUNK) & (ti >= tj)
    tok = lax.broadcasted_iota(jnp.int32, (1, ts), 1)

    def head(h):
        c0 = h * HGRN_DIM
        lbh = lb[:, c0:c0 + HGRN_DIM]
        f = lbh + (1.0 - lbh) * _sigmoid(f_ref[:, c0:c0 + HGRN_DIM])
        logf = jnp.log(f)
        k = 1.0 - f
        q = _silu(q_ref[:, c0:c0 + HGRN_DIM])
        v = i_ref[:, c0:c0 + HGRN_DIM]
        hi, mid, lo = _split3(logf)
        b = _dot(cum, hi) + _dot(cum, mid) + _dot(cum, lo)
        yield
        b3 = b.reshape(nck, HGRN_CHUNK, HGRN_DIM)
        bref = jnp.broadcast_to(b3[:, HGRN_CHUNK // 2:HGRN_CHUNK // 2 + 1, :], b3.shape).reshape(ts, HGRN_DIM)
        blast = jnp.broadcast_to(b3[:, HGRN_CHUNK - 1:HGRN_CHUNK, :], b3.shape).reshape(ts, HGRN_DIM)
        qe = (q * jnp.exp(b - bref)).astype(BF16)
        ke = (k * jnp.exp(bref - b)).astype(BF16)
        kl = (k * jnp.exp(blast - b)).astype(BF16)
        qb = (q * jnp.exp(b)).astype(BF16)
        vb = v.astype(BF16)
        att = _dot_nt(qe, ke)
        vt = v.T
        lhs = jnp.concatenate(
            [jnp.where(tok // HGRN_CHUNK == c, vt, 0.0) for c in range(nck)], axis=0).astype(BF16)
        st = _dot(lhs, kl)
        yield
        o = _dot(jnp.where(blockcausal, att, 0.0).astype(BF16), vb)
        state = state_ref[h]
        outs = []
        for c in range(nck):
            t0 = c * HGRN_CHUNK
            outs.append(_dot_nt(qb[t0:t0 + HGRN_CHUNK], state.astype(BF16)))
            cd = jnp.exp(blast[t0:t0 + 1, :])
            state = state * cd + st[c * HGRN_DIM:(c + 1) * HGRN_DIM]
            if c % 2 == 1:
                yield
        state_ref[h] = state
        o = o + jnp.concatenate(outs, axis=0)
        o = o * lax.rsqrt(jnp.mean(o * o, axis=-1, keepdims=True) + EPS) * nw
        o_ref[:, c0:c0 + HGRN_DIM] = (o * _silu(g_ref[:, c0:c0 + HGRN_DIM])).astype(o_ref.dtype)

    for h0 in range(0, HGRN_HEADS, HGRN_HEADS_INTERLEAVED):
        live = [head(h) for h in range(h0, h0 + HGRN_HEADS_INTERLEAVED)]
        while live:
            live = [gen for gen in live if next(gen, live) is not live]


def _hgrn_cum_matrix(ts):
    t = np.arange(ts)
    same = (t[:, None] // HGRN_CHUNK) == (t[None, :] // HGRN_CHUNK)
    return (same & (t[None, :] <= t[:, None])).astype(np.float32)


def _hgrn_mixer(proj, bsz, seq, lb_logits, norm_w, layer, ts=256):
    nsteps = seq // ts
    row = lambda b, s: b * nsteps + s
    depth = lb_logits.shape[0]
    cum = jnp.asarray(_hgrn_cum_matrix(ts), BF16)
    kern = functools.partial(_hgrn_kernel, ts=ts, layer=layer)
    blk = lambda col: pl.BlockSpec((ts, HGRN_WIDTH), lambda b, s: (row(b, s), col // HGRN_WIDTH))
    return pl.pallas_call(
        kern,
        grid=(bsz, nsteps),
        in_specs=[blk(COL_HQ), blk(COL_HF), blk(COL_HI), blk(COL_HG),
                  pl.BlockSpec((depth, HGRN_WIDTH), lambda b, s: (0, 0)),
                  pl.BlockSpec((1, HGRN_DIM), lambda b, s: (0, 0)),
                  pl.BlockSpec((ts, ts), lambda b, s: (0, 0))],
        out_specs=pl.BlockSpec((ts, HGRN_WIDTH), lambda b, s: (row(b, s), 0)),
        out_shape=jax.ShapeDtypeStruct((bsz * seq, HGRN_WIDTH), BF16),
        scratch_shapes=[pltpu.VMEM((HGRN_HEADS, HGRN_DIM, HGRN_DIM), F32)],
        compiler_params=_params(("arbitrary", "arbitrary")),
        name="hgrn2_mixer",
    )(proj, proj, proj, proj, lb_logits.astype(F32), norm_w.reshape(1, HGRN_DIM).astype(F32), cum)


def _rel_bucket_np(dist):
    max_exact = REL_BUCKETS // 2
    d = np.maximum(dist, 0)
    ratio = np.maximum(d, 1).astype(np.float32) / np.float32(max_exact)
    log_ratio = np.log(ratio).astype(np.float32) / np.float32(math.log(REL_MAX_DIST / max_exact))
    large = np.minimum(max_exact + (log_ratio * np.float32(REL_BUCKETS - max_exact)).astype(np.int32),
                       REL_BUCKETS - 1)
    return np.where(d < max_exact, d, large).astype(np.int32)


def _bias_expand_kernel(rb_ref, bmap_ref, o_ref):
    h = pl.program_id(0)
    bm = bmap_ref[...]
    out = jnp.full(bm.shape, NEG_INF, F32)
    for k in range(REL_BUCKETS):
        out = jnp.where(bm == k, rb_ref[k, h] * LOG2E, out)
    o_ref[...] = out


def _bias_expand(rel_bias, bmap, tr):
    rows, cols = bmap.shape
    return pl.pallas_call(
        _bias_expand_kernel,
        grid=(NSA_HEADS, rows // tr),
        in_specs=[pl.BlockSpec(memory_space=pltpu.SMEM),
                  pl.BlockSpec((tr, cols), lambda h, i: (i, 0))],
        out_specs=pl.BlockSpec((None, tr, cols), lambda h, i: (h, i, 0)),
        out_shape=jax.ShapeDtypeStruct((NSA_HEADS, rows, cols), F32),
        compiler_params=_params(("arbitrary", "arbitrary")),
        name="nsa_bias_expand",
    )(rel_bias.astype(F32), bmap)


def _nsa_bias_tables(rel_bias, seq):
    ncp = seq // CMP_STRIDE
    t = np.arange(seq)[:, None]
    cmp_end = np.arange(ncp)[None, :] * CMP_STRIDE + CMP_BLOCK - 1
    d = t - cmp_end
    bmap_cmp = np.where(d >= 0, _rel_bucket_np(d), -1).astype(np.int32)
    l = np.arange(Q_BLOCK)[:, None]
    j = np.arange(WIN_KEYS)[None, :]
    dist = l - j + WINDOW
    win = np.where((dist >= 0) & (dist < WINDOW), _rel_bucket_np(dist), -1)
    far = _rel_bucket_np(np.arange(Q_BLOCK + 1, 8 * seq))
    assert (far == far[0]).all(), "distances beyond one query block must share a single bucket"
    bmap_tab = np.concatenate([win, np.full((Q_BLOCK, Q_BLOCK), far[0])], axis=1).astype(np.int32)
    bias_cmp = _bias_expand(rel_bias, jnp.asarray(bmap_cmp), 512)
    bias_tab = _bias_expand(rel_bias, jnp.asarray(bmap_tab), Q_BLOCK)
    return bias_cmp, bias_tab


def _cmp_kernel(u_ref, pe_ref, w1_ref, w2_ref, o_ref, *, ncp):
    half = CMP_BLOCK // 2
    pe = pe_ref[...]
    top = jnp.zeros((ncp, CMP_HIDDEN), F32)
    bot = jnp.zeros((ncp, CMP_HIDDEN), F32)
    for l in range(half):
        x = u_ref[pl.ds(l, ncp, stride=half), :]
        top = top + _dot((x + pe[l:l + 1, :]).astype(BF16), w1_ref[l])
        bot = bot + _dot((x + pe[half + l:half + l + 1, :]).astype(BF16), w1_ref[half + l])
    hid = top + pltpu.roll(bot, ncp - 1, 0)
    o_ref[...] = _dot(_silu(hid).astype(BF16), w2_ref[...])


def _nsa_compress(proj, bsz, seq, pe, w1, w2):
    ncp = seq // CMP_STRIDE
    kern = functools.partial(_cmp_kernel, ncp=ncp)
    return pl.pallas_call(
        kern,
        grid=(bsz, 2, NSA_KV_HEADS),
        in_specs=[pl.BlockSpec((seq, NSA_HEAD_DIM), lambda b, t, h: (b, COL_KC // NSA_HEAD_DIM + NSA_KV_HEADS * t + h)),
                  pl.BlockSpec((None, CMP_BLOCK, NSA_HEAD_DIM), lambda b, t, h: (t, 0, 0)),
                  pl.BlockSpec((None, CMP_BLOCK, NSA_HEAD_DIM, CMP_HIDDEN), lambda b, t, h: (t, 0, 0, 0)),
                  pl.BlockSpec((None, CMP_HIDDEN, NSA_HEAD_DIM), lambda b, t, h: (t, 0, 0))],
        out_specs=pl.BlockSpec((None, None, None, ncp, NSA_HEAD_DIM), lambda b, t, h: (b, t, h, 0, 0)),
        out_shape=jax.ShapeDtypeStruct((bsz, 2, NSA_KV_HEADS, ncp, NSA_HEAD_DIM), F32),
        compiler_params=_params(("arbitrary", "arbitrary", "arbitrary")),
        name="nsa_compress",
    )(proj, pe.astype(F32), w1.reshape(2, CMP_BLOCK, NSA_HEAD_DIM, CMP_HIDDEN).astype(BF16), w2.astype(BF16))


def _softmax_start(s, v):
    m = jnp.max(s, axis=-1, keepdims=True)
    p = jnp.exp2(s - m)
    return m, jnp.sum(p, axis=-1, keepdims=True), _dot(p.astype(BF16), v)


def _softmax_update(s, v, carry):
    m, l, acc = carry
    m_new = jnp.maximum(m, jnp.max(s, axis=-1, keepdims=True))
    alpha = jnp.exp2(m - m_new)
    p = jnp.exp2(s - m_new)
    return m_new, alpha * l + jnp.sum(p, axis=-1, keepdims=True), alpha * acc + _dot(p.astype(BF16), v)


def _softmax_finish(carry):
    m, l, acc = carry
    return jnp.where(m > 0.5 * NEG_INF, acc / jnp.maximum(l, 1e-30), 0.0)


def _nsa_kernel(q_ref, gate_ref, kc_ref, vc_ref, ks_ref, vs_ref, kw_ref, vw_ref, bcmp_ref, btab_ref,
                kone_ref, c2s_ref, o_ref, kaug_ref, vsb_ref, kwb_ref, vwb_ref, sa_ref, sb_ref, *, n_sel, nqb):
    hk = pl.program_id(1)
    c0 = pl.program_id(2) * nqb
    rows = NSA_GQA * Q_BLOCK
    lane = lax.broadcasted_iota(jnp.int32, (Q_BLOCK, LANES), 1)

    @pl.when(c0 == 0)
    def _():
        kaug_ref[0:Q_BLOCK, 0:NSA_HEAD_DIM] = jnp.zeros((Q_BLOCK, NSA_HEAD_DIM), BF16)
        kaug_ref[Q_BLOCK:, 0:NSA_HEAD_DIM] = ks_ref[...].astype(BF16)
        kaug_ref[:, NSA_HEAD_DIM:] = kone_ref[...]
        vsb_ref[0:Q_BLOCK, :] = jnp.zeros((Q_BLOCK, NSA_HEAD_DIM), BF16)
        vsb_ref[Q_BLOCK:, :] = vs_ref[...].astype(BF16)
        kwb_ref[0:WINDOW, 0:NSA_HEAD_DIM] = jnp.zeros((WINDOW, NSA_HEAD_DIM), BF16)
        kwb_ref[WINDOW:, 0:NSA_HEAD_DIM] = kw_ref[...].astype(BF16)
        for r0 in range(0, WINDOW, Q_BLOCK):
            kwb_ref[r0:r0 + Q_BLOCK, NSA_HEAD_DIM:] = jnp.where(lane == AUG_PAD, 1.0, 0.0).astype(BF16)
        kwb_ref[WINDOW:, NSA_HEAD_DIM:] = jnp.zeros((kw_ref.shape[0], LANES), BF16)
        vwb_ref[0:WINDOW, :] = jnp.zeros((WINDOW, NSA_HEAD_DIM), BF16)
        vwb_ref[WINDOW:, :] = vw_ref[...].astype(BF16)

    gens = [_nsa_unit(c0 + u, q_ref[u * Q_BLOCK:(u + 1) * Q_BLOCK, :], bcmp_ref[:, u * Q_BLOCK:(u + 1) * Q_BLOCK, :],
                      btab_ref, kc_ref, vc_ref, c2s_ref, kaug_ref, vsb_ref, kwb_ref, vwb_ref, n_sel)
            for u in range(nqb)]
    units = [None] * nqb
    while any(un is None for un in units):
        for u, gen in enumerate(gens):
            if units[u] is None:
                units[u] = next(gen)
    far_blocks = FAR_KEYS // Q_BLOCK
    last_slab = (kaug_ref.shape[0] - Q_BLOCK) // FAR_KEYS - 1

    def far_values(slab):
        return vsb_ref[pl.ds(pl.multiple_of(slab * FAR_KEYS + Q_BLOCK, Q_BLOCK), FAR_KEYS), :]

    o_sel = []
    for u0 in range(0, nqb, NSA_FAR_GROUP):
        group = units[u0:u0 + NSA_FAR_GROUP]
        qaug_far = jnp.concatenate([un[3] for un in group], axis=0)
        carry = tuple(jnp.concatenate([un[2][i] for un in group], axis=0) for i in range(3))
        n_slabs = (c0 + u0 + NSA_FAR_GROUP - 2 + far_blocks - 1) // far_blocks
        n_pairs = n_slabs // 2

        def far_logits(slab, qaug_far=qaug_far):
            r0 = pl.multiple_of(jnp.minimum(slab, last_slab) * FAR_KEYS + Q_BLOCK, Q_BLOCK)
            return _dot_nt(qaug_far, kaug_ref[pl.ds(r0, FAR_KEYS), :])

        sa_ref[...] = far_logits(0)

        def far_body(k, carry, far_logits=far_logits):
            sb_ref[...] = far_logits(2 * k + 1)
            carry = _softmax_update(sa_ref[...], far_values(2 * k), carry)
            sa_ref[...] = far_logits(2 * k + 2)
            return _softmax_update(sb_ref[...], far_values(2 * k + 1), carry)

        carry = lax.fori_loop(0, n_pairs, far_body, carry)
        carry = lax.cond(n_slabs % 2 == 1,
                         lambda cr, n_pairs=n_pairs: _softmax_update(sa_ref[...], far_values(2 * n_pairs), cr),
                         lambda cr: cr, carry)
        out = _softmax_finish(carry)
        o_sel += [out[i * rows:(i + 1) * rows] for i in range(NSA_FAR_GROUP)]

    for u in range(nqb):
        o_cmp, o_win = units[u][0], units[u][1]
        gates = _sigmoid(gate_ref[u * Q_BLOCK:(u + 1) * Q_BLOCK, :])
        for g in range(NSA_GQA):
            def gate_col(br):
                return jnp.sum(jnp.where(lane == hk * (3 * NSA_GQA) + g * 3 + br, gates, 0.0), axis=-1, keepdims=True)

            r0 = g * Q_BLOCK
            out = (gate_col(0) * o_cmp[r0:r0 + Q_BLOCK] + gate_col(1) * o_sel[u][r0:r0 + Q_BLOCK]
                   + gate_col(2) * o_win[r0:r0 + Q_BLOCK])
            o_ref[u * Q_BLOCK:(u + 1) * Q_BLOCK, g * NSA_HEAD_DIM:(g + 1) * NSA_HEAD_DIM] = out.astype(o_ref.dtype)


def _nsa_unit(c, q_in, bc_in, btab_ref, kc_ref, vc_ref, c2s_ref, kaug_ref, vsb_ref, kwb_ref, vwb_ref, n_sel):
    rows = NSA_GQA * Q_BLOCK
    lane = lax.broadcasted_iota(jnp.int32, (Q_BLOCK, LANES), 1)
    q = q_in * (NSA_HEAD_DIM ** -0.5 * LOG2E)
    q3 = jnp.concatenate([q[:, g * NSA_HEAD_DIM:(g + 1) * NSA_HEAD_DIM] for g in range(NSA_GQA)],
                         axis=0).astype(BF16)

    btab = btab_ref[...].reshape(rows, WIN_KEYS + Q_BLOCK)
    near0 = pl.multiple_of(c * Q_BLOCK, Q_BLOCK)

    qaug_win = jnp.concatenate(
        [q3, jnp.concatenate([jnp.where(lane == AUG_PAD, NEG_INF, 0.0).astype(BF16)] * NSA_GQA, axis=0)], axis=1)
    s_win = _dot_nt(qaug_win, kwb_ref[pl.ds(near0, WIN_KEYS), :]) + btab[:, 0:WIN_KEYS]
    yield None

    bc = bc_in.reshape(rows, bc_in.shape[-1])
    valid = bc > 0.5 * NEG_INF
    s = _dot_nt(q3, kc_ref[...].astype(BF16)) + bc
    yield None
    o_win = _softmax_finish(_softmax_start(s_win, vwb_ref[pl.ds(near0, WIN_KEYS), :]))
    yield None
    m = jnp.max(s, axis=-1, keepdims=True)
    p = jnp.where(valid, jnp.exp2(s - m), 0.0)
    p = p / jnp.maximum(jnp.sum(p, axis=-1, keepdims=True), 1e-30)
    o_cmp = _dot(p.astype(BF16), vc_ref[...].astype(BF16))

    psum = p[0:Q_BLOCK] + p[Q_BLOCK:2 * Q_BLOCK] + p[2 * Q_BLOCK:3 * Q_BLOCK]
    ph = psum.astype(BF16)
    plo = (psum - ph.astype(F32)).astype(BF16)
    c2s = c2s_ref[...]
    imp_t = _dot_nt(c2s, ph) + _dot_nt(c2s, plo)
    nsp = -(-n_sel // SUBLANES) * SUBLANES
    imp_t = imp_t[0:nsp]
    tq = c * Q_BLOCK + lax.broadcasted_iota(jnp.int32, (nsp, Q_BLOCK), 1)
    jj = lax.broadcasted_iota(jnp.int32, (nsp, Q_BLOCK), 0)
    valid_sel = (jj < n_sel) & (jj * SEL_BLOCK <= tq)
    back = tq // SEL_BLOCK - jj
    force = (jj == 0) | ((back >= 0) & (back < SEL_LOCAL))
    score = jnp.where(valid_sel, imp_t + jnp.where(force, FORCE_BONUS, 0.0), NEG_INF)
    yield None
    n_tiles = nsp // SUBLANES
    tiles = [score[v * SUBLANES:(v + 1) * SUBLANES] for v in range(n_tiles)]
    sub = lax.broadcasted_iota(jnp.int32, (SUBLANES, Q_BLOCK), 0)
    cnts = [jnp.zeros((SUBLANES, Q_BLOCK), F32) for _ in range(n_tiles)]
    for jp in range(n_sel):
        v0, r0 = divmod(jp, SUBLANES)
        r = tiles[v0][r0:r0 + 1, :]
        for v in range(n_tiles):
            gt = jnp.where(r > tiles[v], 1.0, 0.0)
            ge = jnp.where(r >= tiles[v], 1.0, 0.0)
            cnts[v] = cnts[v] + (gt if v < v0 else ge if v > v0 else jnp.where(sub > r0, ge, gt))
        if jp % (2 * SUBLANES) == 2 * SUBLANES - 1:
            yield None
    cnt = jnp.concatenate(cnts, axis=0)
    keep = valid_sel & (cnt < float(min(SEL_TOPK, n_sel)))

    ext = lax.broadcasted_iota(jnp.int32, (LANES - nsp, Q_BLOCK), 0) + nsp
    aug = jnp.concatenate([jnp.where(keep, 0.0, NEG_INF), jnp.where(ext == AUG_PAD, NEG_INF, 0.0)], axis=0)
    aug_near = aug.T
    aug_far = jnp.where((lane >= 2 * (c - 1)) & (lane < AUG_BIAS_HI), NEG_INF, aug_near)
    far_parts = []
    for g in range(NSA_GQA):
        bfar = btab[g * Q_BLOCK:(g + 1) * Q_BLOCK, WIN_KEYS:]
        hi = bfar.astype(BF16).astype(F32)
        far_parts.append(jnp.where(lane == AUG_BIAS_HI, hi, jnp.where(lane == AUG_BIAS_LO, bfar - hi, aug_far)))
    qaug_near = jnp.concatenate([q3, jnp.concatenate([aug_near.astype(BF16)] * NSA_GQA, axis=0)], axis=1)
    qaug_far = jnp.concatenate([q3, jnp.concatenate(far_parts, axis=0).astype(BF16)], axis=1)

    s = _dot_nt(qaug_near, kaug_ref[pl.ds(near0, NEAR_KEYS), :]) + btab[:, WIN_KEYS - NEAR_KEYS:WIN_KEYS]
    yield None
    carry = _softmax_start(s, vsb_ref[pl.ds(near0, NEAR_KEYS), :])
    yield o_cmp, o_win, carry, qaug_far


def _nsa_attention(proj, cmp_kv, bias_cmp, bias_tab, bsz, seq):
    nq = seq // Q_BLOCK
    ncp = seq // CMP_STRIDE
    n_sel = seq // SEL_BLOCK
    assert n_sel <= AUG_BIAS_HI and NSA_HEAD_DIM == LANES
    assert seq % (2 * FAR_KEYS) == 0, "far steps walk the keys two 512-key slabs at a time"
    key = np.arange(-Q_BLOCK, seq)[:, None]
    ln = np.arange(LANES)[None, :]
    kone = np.where(key < 0, ln == AUG_PAD,
                    ((ln < AUG_BIAS_HI) & (key // SEL_BLOCK == ln)) | (ln == AUG_BIAS_HI) | (ln == AUG_BIAS_LO))
    kone = jnp.asarray(kone, BF16)
    c_start = np.arange(ncp)[None, :] * CMP_STRIDE
    s_start = np.arange(LANES)[:, None] * SEL_BLOCK
    overlap = np.clip(np.minimum(c_start + CMP_BLOCK, s_start + SEL_BLOCK) - np.maximum(c_start, s_start), 0, None)
    overlap = np.where(np.arange(LANES)[:, None] < n_sel, overlap, 0)
    c2s_t = jnp.asarray(overlap / CMP_BLOCK, BF16)
    gqa_w = NSA_GQA * NSA_HEAD_DIM
    full = lambda col: pl.BlockSpec((seq, NSA_HEAD_DIM), lambda b, h, c: (b, col // NSA_HEAD_DIM + h))
    nqb = NSA_Q_BLOCKS_PER_STEP
    ns = nq // nqb
    qr = nqb * Q_BLOCK
    kern = functools.partial(_nsa_kernel, n_sel=n_sel, nqb=nqb)
    return pl.pallas_call(
        kern,
        grid=(bsz, NSA_KV_HEADS, ns),
        in_specs=[pl.BlockSpec((qr, gqa_w), lambda b, h, c: (b * ns + c, COL_Q // gqa_w + h)),
                  pl.BlockSpec((qr, LANES), lambda b, h, c: (b * ns + c, COL_GATE // LANES)),
                  pl.BlockSpec((None, None, None, ncp, NSA_HEAD_DIM), lambda b, h, c: (b, 0, h, 0, 0)),
                  pl.BlockSpec((None, None, None, ncp, NSA_HEAD_DIM), lambda b, h, c: (b, 1, h, 0, 0)),
                  full(COL_KS), full(COL_VS), full(COL_KW), full(COL_VW),
                  pl.BlockSpec((NSA_GQA, qr, ncp), lambda b, h, c: (h, c, 0)),
                  pl.BlockSpec((NSA_GQA, Q_BLOCK, WIN_KEYS + Q_BLOCK), lambda b, h, c: (h, 0, 0)),
                  pl.BlockSpec((seq + Q_BLOCK, LANES), lambda b, h, c: (0, 0)),
                  pl.BlockSpec((LANES, ncp), lambda b, h, c: (0, 0))],
        out_specs=pl.BlockSpec((qr, gqa_w), lambda b, h, c: (b * ns + c, h)),
        out_shape=jax.ShapeDtypeStruct((bsz * seq, NSA_WIDTH), BF16),
        scratch_shapes=[pltpu.VMEM((seq + Q_BLOCK, 2 * NSA_HEAD_DIM), BF16),
                        pltpu.VMEM((seq + Q_BLOCK, NSA_HEAD_DIM), BF16),
                        pltpu.VMEM((seq + WINDOW, 2 * NSA_HEAD_DIM), BF16),
                        pltpu.VMEM((seq + WINDOW, NSA_HEAD_DIM), BF16),
                        pltpu.VMEM((NSA_FAR_GROUP * NSA_GQA * Q_BLOCK, FAR_KEYS), F32),
                        pltpu.VMEM((NSA_FAR_GROUP * NSA_GQA * Q_BLOCK, FAR_KEYS), F32)],
        compiler_params=_params(("arbitrary", "arbitrary", "arbitrary")),
        name="nsa_attention",
    )(proj, proj, cmp_kv, cmp_kv, proj, proj, proj, proj, bias_cmp, bias_tab, kone, c2s_t)


_IN_OFF = tuple(int(v) for v in np.cumsum((0,) + IN_SIZES))
_W_IN_COPIES = ((_IN_OFF[0], SSD_WIDTH, COL_Z), (_IN_OFF[1], SSD_WIDTH, COL_XS),
                (_IN_OFF[1] + SSD_WIDTH, 512, COL_B), (_IN_OFF[1] + SSD_WIDTH + 512, 512, COL_C),
                (_IN_OFF[3], NSA_WIDTH, COL_Q), (_IN_OFF[4], 6 * NSA_KV_WIDTH, COL_KC),
                (_IN_OFF[11], 4 * HGRN_WIDTH, COL_HQ))
_W_IN_NARROW = ((_IN_OFF[2], IN_SIZES[2], COL_DT, COL_GATE - COL_DT), (_IN_OFF[10], IN_SIZES[10], COL_GATE, COL_HQ - COL_GATE))


def _relayout_kernel(w_ref, o_ref):
    for src, width, dst in _W_IN_COPIES:
        o_ref[:, dst:dst + width] = w_ref[:, src:src + width].astype(BF16)
    for src, valid, dst, padded in _W_IN_NARROW:
        tile = w_ref[:, src:src + LANES]
        lane = lax.broadcasted_iota(jnp.int32, tile.shape, 1)
        o_ref[:, dst:dst + LANES] = jnp.where(lane < valid, tile, 0.0).astype(BF16)
        if padded > LANES:
            o_ref[:, dst + LANES:dst + padded] = jnp.zeros((tile.shape[0], padded - LANES), BF16)


def _relayout_w_in(w, tr=128):
    depth, d, n = w.shape
    return pl.pallas_call(
        _relayout_kernel,
        grid=(depth, d // tr),
        in_specs=[pl.BlockSpec((None, tr, n), lambda l, i: (l, i, 0))],
        out_specs=pl.BlockSpec((None, tr, IN_PAD), lambda l, i: (l, i, 0)),
        out_shape=jax.ShapeDtypeStruct((depth, d, IN_PAD), BF16),
        compiler_params=_params(("parallel", "parallel")),
        name="w_in_relayout",
    )(w)


def _tiles(seq):
    return dict(in_proj=(1024, 1024), out_proj=(1024, 512), gate_up=(min(2048, seq), 256), down=(256, 1024))


def kernel(x, norm_mix_w, w_in, ssd_conv_w, ssd_conv_b, ssd_dt_bias, ssd_a_log, ssd_d, ssd_norm_w, nsa_cmp_pe, nsa_cmp_w1, nsa_cmp_w2, rel_bias, hgrn_lb_logits, hgrn_norm_w, w_out, norm_ffn_w, ffn_w_gu, ffn_conv_w, ffn_conv_b, ffn_w_down, norm_f_w):
    bsz, seq, d = x.shape
    depth = w_in.shape[0]
    xr = x.reshape(bsz * seq, d).astype(F32)
    bias_cmp, bias_tab = _nsa_bias_tables(rel_bias, seq)
    tiles = _tiles(seq)
    w_in_b = _relayout_w_in(jnp.pad(w_in.astype(BF16), ((0, 0), (0, 0), (0, -w_in.shape[2] % LANES))))
    w_out_b = w_out.astype(BF16)
    w_gu_b = ffn_w_gu.astype(BF16)
    w_down_b = ffn_w_down.astype(BF16)
    xw, ssq = _prenorm(xr, norm_mix_w[0])
    for l in range(depth):
        proj = _matmul(xw, ssq, w_in_b, l, F32, *tiles["in_proj"], name="in_proj")
        y_ssd = _ssd_mixer(proj, bsz, seq, ssd_conv_w[l].astype(F32), ssd_conv_b[l].astype(F32), ssd_dt_bias[l],
                           ssd_a_log[l], ssd_d[l], ssd_norm_w[l].astype(F32))
        cmp_kv = _nsa_compress(proj, bsz, seq, nsa_cmp_pe[l], nsa_cmp_w1[l], nsa_cmp_w2[l])
        y_nsa = _nsa_attention(proj, cmp_kv, bias_cmp, bias_tab, bsz, seq)
        y_hgrn = _hgrn_mixer(proj, bsz, seq, hgrn_lb_logits, hgrn_norm_w[l], l)
        xr, xw, ssq = _out_proj([y_ssd, y_nsa, y_hgrn], w_out_b, l, xr, norm_ffn_w[l], *tiles["out_proj"])
        act = _gate_up(xw, ssq, w_gu_b, l, ffn_conv_w[l].astype(F32), ffn_conv_b[l].astype(F32), seq,
                       *tiles["gate_up"])
        if l + 1 < depth:
            xr, xw, ssq = _down_proj_norm(act, w_down_b, l, xr, norm_mix_w[l + 1], *tiles["down"])
        else:
            xr = _matmul_residual_wres(act, w_down_b, l, xr, *tiles["down"], name="ffn_down")
    out = _rmsnorm(xr, norm_f_w, x.dtype)
    return out.reshape(bsz, seq, d)
```

```python
import functools
import math

import numpy as np
import jax
import jax.numpy as jnp
from jax import lax
from jax.experimental import pallas as pl
from jax.experimental.pallas import tpu as pltpu

F32 = jnp.float32
BF16 = jnp.bfloat16

D_MODEL = 4096
SSD_HEAD_DIM = 64
SSD_WIDTH = 1536
SSD_HEADS = 24
SSD_GROUPS = 4
SSD_HPG = 6
SSD_STATE = 128
SSD_CONV = 4
SSD_CHUNK = 128
SSD_GW = SSD_WIDTH // SSD_GROUPS
NSA_HEAD_DIM = 128
NSA_WIDTH = 1536
NSA_HEADS = 12
NSA_KV_HEADS = 4
NSA_GQA = 3
NSA_KV_WIDTH = 512
CMP_BLOCK = 32
CMP_STRIDE = 16
CMP_HIDDEN = 256
SEL_BLOCK = 64
SEL_TOPK = 16
SEL_LOCAL = 2
WINDOW = 512
Q_BLOCK = 128
FORCE_BONUS = 1e4
HGRN_WIDTH = 1024
HGRN_HEADS = 8
HGRN_DIM = 128
HGRN_CHUNK = 32
REL_BUCKETS = 32
REL_MAX_DIST = 128
D_FF = 11008
FFN_CONV = 3
EPS = 1e-6
NEG_INF = -1e30

IN_SIZES = (SSD_WIDTH, SSD_WIDTH + 2 * SSD_GROUPS * SSD_STATE, SSD_HEADS, NSA_WIDTH,
            NSA_KV_WIDTH, NSA_KV_WIDTH, NSA_KV_WIDTH, NSA_KV_WIDTH, NSA_KV_WIDTH, NSA_KV_WIDTH,
            3 * NSA_HEADS, HGRN_WIDTH, HGRN_WIDTH, HGRN_WIDTH, HGRN_WIDTH)

LANES = 128
SUBLANES = 8
VMEM_LIMIT = 56 * 1024 * 1024

COL_Z = 0
COL_XS = 1536
COL_Q = 3072
COL_B = 4608
COL_C = 5120
COL_KC = 5632
COL_VC = 6144
COL_KS = 6656
COL_VS = 7168
COL_KW = 7680
COL_VW = 8192
COL_DT = 8704
COL_GATE = 8832
COL_HQ = 9216
COL_HF = 10240
COL_HI = 11264
COL_HG = 12288
IN_PAD = 13312

LOG2E = math.log2(math.e)
AUG_BIAS_HI = 64
AUG_BIAS_LO = 65
AUG_PAD = 66
WIN_KEYS = WINDOW + Q_BLOCK
NEAR_KEYS = 2 * Q_BLOCK
FAR_KEYS = 4 * Q_BLOCK
HGRN_HEADS_INTERLEAVED = 8
NSA_Q_BLOCKS_PER_STEP = 4
NSA_FAR_GROUP = 2


def _params(semantics):
    return pltpu.CompilerParams(dimension_semantics=semantics, vmem_limit_bytes=VMEM_LIMIT)


def _sigmoid(x):
    return 1.0 / (1.0 + jnp.exp(-x))


def _silu(x):
    return x * _sigmoid(x)


def _dot(a, b):
    return jnp.dot(a, b, preferred_element_type=F32)


def _dot_nt(a, b):
    return lax.dot_general(a, b, (((1,), (1,)), ((), ())), preferred_element_type=F32)


def _split3(x):
    hi = x.astype(BF16)
    r1 = x - hi.astype(F32)
    mid = r1.astype(BF16)
    lo = (r1 - mid.astype(F32)).astype(BF16)
    return hi, mid, lo


def _rmsnorm_kernel(x_ref, w_ref, o_ref):
    x = x_ref[...]
    ms = jnp.mean(x * x, axis=-1, keepdims=True)
    o_ref[...] = (x * lax.rsqrt(ms + EPS) * w_ref[...]).astype(o_ref.dtype)


def _rmsnorm(x, w, out_dtype, tm=256):
    m, d = x.shape
    return pl.pallas_call(
        _rmsnorm_kernel,
        grid=(m // tm,),
        in_specs=[pl.BlockSpec((tm, d), lambda i: (i, 0)), pl.BlockSpec((1, d), lambda i: (0, 0))],
        out_specs=pl.BlockSpec((tm, d), lambda i: (i, 0)),
        out_shape=jax.ShapeDtypeStruct((m, d), out_dtype),
        compiler_params=_params(("parallel",)),
        name="rmsnorm",
    )(x, w.reshape(1, d).astype(F32))


def _lane_fold(sq):
    part = sq[:, 0:LANES]
    for c in range(1, sq.shape[1] // LANES):
        part = part + sq[:, c * LANES:(c + 1) * LANES]
    return part


def _rinv_lanes(ssq, d):
    part = ssq[0]
    for p in range(1, ssq.shape[0]):
        part = part + ssq[p]
    tot = jnp.sum(part, axis=-1, keepdims=True)
    return jnp.broadcast_to(lax.rsqrt(tot * (1.0 / d) + EPS), part.shape)


def _scale_rows(acc, rinv):
    return jnp.concatenate([acc[:, c * LANES:(c + 1) * LANES] * rinv for c in range(acc.shape[1] // LANES)], axis=1)


def _prenorm_kernel(x_ref, w_ref, xw_ref, ssq_ref):
    x = x_ref[...]
    xw_ref[...] = (x * w_ref[...]).astype(xw_ref.dtype)
    ssq_ref[0] = _lane_fold(x * x)


def _prenorm(x, w, tm=256):
    m, d = x.shape
    return pl.pallas_call(
        _prenorm_kernel,
        grid=(m // tm,),
        in_specs=[pl.BlockSpec((tm, d), lambda i: (i, 0)), pl.BlockSpec((1, d), lambda i: (0, 0))],
        out_specs=[pl.BlockSpec((tm, d), lambda i: (i, 0)), pl.BlockSpec((1, tm, LANES), lambda i: (0, i, 0))],
        out_shape=[jax.ShapeDtypeStruct((m, d), BF16), jax.ShapeDtypeStruct((1, m, LANES), F32)],
        compiler_params=_params(("parallel",)),
        name="prenorm",
    )(x, w.reshape(1, d).astype(F32))


def _mm_kernel(a_ref, ssq_ref, w_ref, o_ref, rinv_ref, *, d):
    @pl.when(pl.program_id(1) == 0)
    def _():
        rinv_ref[...] = _rinv_lanes(ssq_ref[...], d)

    o_ref[...] = _scale_rows(_dot(a_ref[...], w_ref[...]), rinv_ref[...]).astype(o_ref.dtype)


def _mm_res_kernel(a_ref, w_ref, r_ref, o_ref):
    o_ref[...] = r_ref[...] + _dot(a_ref[...], w_ref[...])


def _matmul(xw, ssq, w, layer, out_dtype, tm, tn, name):
    m, k = xw.shape
    n = w.shape[2]
    return pl.pallas_call(
        functools.partial(_mm_kernel, d=k),
        grid=(m // tm, n // tn),
        in_specs=[pl.BlockSpec((tm, k), lambda i, j: (i, 0)),
                  pl.BlockSpec((ssq.shape[0], tm, LANES), lambda i, j: (0, i, 0)),
                  pl.BlockSpec((None, k, tn), lambda i, j: (layer, 0, j))],
        out_specs=pl.BlockSpec((tm, tn), lambda i, j: (i, j)),
        out_shape=jax.ShapeDtypeStruct((m, n), out_dtype),
        scratch_shapes=[pltpu.VMEM((tm, LANES), F32)],
        compiler_params=_params(("arbitrary", "arbitrary")),
        name=name,
    )(xw, ssq, w)


def _out_proj_kernel(*refs):
    *a_refs, w_ref, r_ref, nw_ref, o_ref, xw_ref, ssq_ref = refs
    j = pl.program_id(1)
    acc = r_ref[...]
    k0 = 0
    for a_ref in a_refs:
        acc = acc + _dot(a_ref[...], w_ref[k0:k0 + a_ref.shape[1], :])
        k0 += a_ref.shape[1]
    o_ref[...] = acc
    xw_ref[...] = (acc * nw_ref[...]).astype(xw_ref.dtype)
    part = _lane_fold(acc * acc)

    @pl.when(j == 0)
    def _():
        ssq_ref[0] = part

    @pl.when(j > 0)
    def _():
        ssq_ref[0] = ssq_ref[0] + part


def _out_proj(parts, w, layer, res, norm_w, tm, tn):
    m = res.shape[0]
    k, n = w.shape[1], w.shape[2]
    assert sum(p.shape[1] for p in parts) == k
    return pl.pallas_call(
        _out_proj_kernel,
        grid=(m // tm, n // tn),
        in_specs=[pl.BlockSpec((tm, p.shape[1]), lambda i, j: (i, 0)) for p in parts]
        + [pl.BlockSpec((None, k, tn), lambda i, j: (layer, 0, j)),
           pl.BlockSpec((tm, tn), lambda i, j: (i, j)),
           pl.BlockSpec((1, tn), lambda i, j: (0, j))],
        out_specs=[pl.BlockSpec((tm, tn), lambda i, j: (i, j)), pl.BlockSpec((tm, tn), lambda i, j: (i, j)),
                   pl.BlockSpec((1, tm, LANES), lambda i, j: (0, i, 0))],
        out_shape=[jax.ShapeDtypeStruct((m, n), F32), jax.ShapeDtypeStruct((m, n), BF16),
                   jax.ShapeDtypeStruct((1, m, LANES), F32)],
        compiler_params=_params(("arbitrary", "arbitrary")),
        name="out_proj",
    )(*parts, w, res, norm_w.reshape(1, n).astype(F32))


def _down_norm_kernel(a_ref, w_ref, r_ref, nw_ref, o_ref, xw_ref, ssq_ref):
    acc = r_ref[...] + _dot(a_ref[...], w_ref[...])
    o_ref[...] = acc
    xw_ref[...] = (acc * nw_ref[...]).astype(xw_ref.dtype)
    ssq_ref[...] = _lane_fold(acc * acc)


def _down_proj_norm(a, w, layer, res, norm_w, tm, tn):
    m, k = a.shape
    n = w.shape[2]
    return pl.pallas_call(
        _down_norm_kernel,
        grid=(n // tn, m // tm),
        in_specs=[pl.BlockSpec((tm, k), lambda j, i: (i, 0)),
                  pl.BlockSpec((None, k, tn), lambda j, i: (layer, 0, j), pipeline_mode=pl.Buffered(1)),
                  pl.BlockSpec((tm, tn), lambda j, i: (i, j)),
                  pl.BlockSpec((1, tn), lambda j, i: (0, j))],
        out_specs=[pl.BlockSpec((tm, tn), lambda j, i: (i, j)), pl.BlockSpec((tm, tn), lambda j, i: (i, j)),
                   pl.BlockSpec((None, tm, LANES), lambda j, i: (j, i, 0))],
        out_shape=[jax.ShapeDtypeStruct((m, n), F32), jax.ShapeDtypeStruct((m, n), BF16),
                   jax.ShapeDtypeStruct((n // tn, m, LANES), F32)],
        compiler_params=_params(("arbitrary", "arbitrary")),
        name="ffn_down",
    )(a, w, res, norm_w.reshape(1, n).astype(F32))


def _matmul_residual_wres(a, w, layer, res, tm, tn, name):
    m, k = a.shape
    n = w.shape[2]
    return pl.pallas_call(
        _mm_res_kernel,
        grid=(n // tn, m // tm),
        in_specs=[pl.BlockSpec((tm, k), lambda j, i: (i, 0)),
                  pl.BlockSpec((None, k, tn), lambda j, i: (layer, 0, j), pipeline_mode=pl.Buffered(1)),
                  pl.BlockSpec((tm, tn), lambda j, i: (i, j))],
        out_specs=pl.BlockSpec((tm, tn), lambda j, i: (i, j)),
        out_shape=jax.ShapeDtypeStruct((m, n), F32),
        compiler_params=_params(("arbitrary", "arbitrary")),
        name=name,
    )(a, w, res)


GU_ROW_CHUNK = 128


def _gu_kernel(h_ref, ssq_ref, wg_ref, wu_ref, cw_ref, cb_ref, o_ref, ga_ref, ua_ref, gb_ref, ub_ref, halo_ref,
               rinv_ref, *, tm, nj, tiles_per_seq):
    t = pl.program_id(0)
    tp = jnp.maximum(t - 1, 0)
    ip = tp // nj
    jp = tp % nj
    rc = min(GU_ROW_CHUNK, tm)

    @pl.when(t == 0)
    def _():
        gb_ref[...] = jnp.zeros_like(gb_ref)
        ub_ref[...] = jnp.zeros_like(ub_ref)
        halo_ref[...] = jnp.zeros_like(halo_ref)

    @pl.when(t % nj == 0)
    def _():
        rinv_ref[...] = _rinv_lanes(ssq_ref[...], h_ref.shape[1])

    def step(g_cur, u_cur, g_prev, u_prev):
        g_prev[0:SUBLANES, :] = jnp.where(ip % tiles_per_seq == 0, 0.0, halo_ref[jp])
        halo_ref[jp] = g_prev[tm:tm + SUBLANES, :]
        cw = cw_ref[...]
        for r in range(tm // rc):
            hr = h_ref[r * rc:(r + 1) * rc, :]
            rinv = rinv_ref[r * rc:(r + 1) * rc, :]
            g_cur[SUBLANES + r * rc:SUBLANES + (r + 1) * rc, :] = _scale_rows(_dot(hr, wg_ref[...]), rinv)
            u_cur[r * rc:(r + 1) * rc, :] = _scale_rows(_dot(hr, wu_ref[...]), rinv)
            acc = cb_ref[...]
            for sh in range(FFN_CONV):
                acc = acc + cw[FFN_CONV - 1 - sh:FFN_CONV - sh, :] * g_prev[pl.ds(SUBLANES - sh + r * rc, rc), :]
            o_ref[r * rc:(r + 1) * rc, :] = (_silu(acc) * u_prev[r * rc:(r + 1) * rc, :]).astype(o_ref.dtype)

    @pl.when(t % 2 == 0)
    def _():
        step(ga_ref, ua_ref, gb_ref, ub_ref)

    @pl.when(t % 2 == 1)
    def _():
        step(gb_ref, ub_ref, ga_ref, ua_ref)


def _gate_up(h, ssq, w_gu, layer, conv_w, conv_b, seq, tm, tn):
    m, k = h.shape
    nf = conv_w.shape[1]
    nj = nf // tn
    last = (m // tm) * nj - 1
    cur = lambda t: jnp.minimum(t, last)
    prev = lambda t: jnp.maximum(t - 1, 0)
    kern = functools.partial(_gu_kernel, tm=tm, nj=nj, tiles_per_seq=seq // tm)
    return pl.pallas_call(
        kern,
        grid=(last + 2,),
        in_specs=[pl.BlockSpec((tm, k), lambda t: (cur(t) // nj, 0), pipeline_mode=pl.Buffered(1)),
                  pl.BlockSpec((ssq.shape[0], tm, LANES), lambda t: (0, cur(t) // nj, 0)),
                  pl.BlockSpec((None, k, tn), lambda t: (layer, 0, cur(t) % nj)),
                  pl.BlockSpec((None, k, tn), lambda t: (layer, 0, cur(t) % nj + nj)),
                  pl.BlockSpec((FFN_CONV, tn), lambda t: (0, prev(t) % nj)),
                  pl.BlockSpec((1, tn), lambda t: (0, prev(t) % nj))],
        out_specs=pl.BlockSpec((tm, tn), lambda t: (prev(t) // nj, prev(t) % nj)),
        out_shape=jax.ShapeDtypeStruct((m, nf), BF16),
        scratch_shapes=[pltpu.VMEM((tm + SUBLANES, tn), F32), pltpu.VMEM((tm, tn), F32),
                        pltpu.VMEM((tm + SUBLANES, tn), F32), pltpu.VMEM((tm, tn), F32),
                        pltpu.VMEM((nj, SUBLANES, tn), F32), pltpu.VMEM((tm, LANES), F32)],
        compiler_params=_params(("arbitrary",)),
        name="ffn_gate_up_conv",
    )(h, ssq, w_gu, w_gu, conv_w, conv_b.reshape(1, nf))


def _ssd_kernel(z_ref, xs_ref, b_ref, c_ref, dt_ref, cwx_ref, cwb_ref, cwc_ref, cbx_ref, cbb_ref, cbc_ref,
                dtb_ref, alog_ref, dsk_ref, nw_ref, tri_ref, o_ref,
                extx_ref, extb_ref, extc_ref, state_ref, *, ts):
    s = pl.program_id(1)

    @pl.when(s == 0)
    def _():
        extx_ref[...] = jnp.zeros_like(extx_ref)
        extb_ref[...] = jnp.zeros_like(extb_ref)
        extc_ref[...] = jnp.zeros_like(extc_ref)
        state_ref[...] = jnp.zeros_like(state_ref)

    def conv_silu(u_ref, ext_ref, w_ref, bias_ref):
        ext_ref[0:SUBLANES, :] = ext_ref[ts:ts + SUBLANES, :]
        ext_ref[SUBLANES:, :] = u_ref[...]
        w = w_ref[...]
        acc = bias_ref[...] + w[SSD_CONV - 1:SSD_CONV, :] * u_ref[...]
        for sh in range(1, SSD_CONV):
            acc = acc + w[SSD_CONV - 1 - sh:SSD_CONV - sh, :] * ext_ref[pl.ds(SUBLANES - sh, ts), :]
        return _silu(acc)

    xs = conv_silu(xs_ref, extx_ref, cwx_ref, cbx_ref)
    bm = conv_silu(b_ref, extb_ref, cwb_ref, cbb_ref)
    cm = conv_silu(c_ref, extc_ref, cwc_ref, cbc_ref)
    dtr = dt_ref[...] + dtb_ref[...]
    dt = jnp.maximum(dtr, 0.0) + jnp.log1p(jnp.exp(-jnp.abs(dtr)))
    a = dt * (-jnp.exp(alog_ref[...]))
    z = z_ref[...]
    tri = tri_ref[...]
    dsk = dsk_ref[...]
    nw = nw_ref[...]
    li = lax.broadcasted_iota(jnp.int32, (SSD_CHUNK, SSD_CHUNK), 0)
    si = lax.broadcasted_iota(jnp.int32, (SSD_CHUNK, SSD_CHUNK), 1)
    causal = li >= si
    low_half = si < SSD_HEAD_DIM
    low_half_row = low_half[0:1, :]

    for ck in range(ts // SSD_CHUNK):
        r0 = ck * SSD_CHUNK
        a_c = a[r0:r0 + SSD_CHUNK]
        hi, mid, lo = _split3(a_c)
        acs = _dot(tri, hi) + _dot(tri, mid) + _dot(tri, lo)
        acs_t = acs.T
        dt_t = dt[r0:r0 + SSD_CHUNK].T
        a_last_b = jnp.broadcast_to(acs_t[:, SSD_CHUNK - 1:SSD_CHUNK], acs_t.shape)
        dtd_t = dt_t * jnp.exp(a_last_b - acs_t)
        chunk_decay = jnp.exp(acs[SSD_CHUNK - 1:SSD_CHUNK, :])
        for g in range(SSD_GROUPS):
            cg = cm[r0:r0 + SSD_CHUNK, g * SSD_STATE:(g + 1) * SSD_STATE]
            bg = bm[r0:r0 + SSD_CHUNK, g * SSD_STATE:(g + 1) * SSD_STATE]
            cb = jnp.where(causal, _dot_nt(cg.astype(BF16), bg.astype(BF16)), 0.0)
            bgt = bg.T
            ys = []
            for pr in range(SSD_HPG // 2):
                pair = (g * SSD_HPG) // 2 + pr
                c0 = pair * LANES
                xs_p = xs[r0:r0 + SSD_CHUNK, c0:c0 + LANES].astype(BF16)
                prev = state_ref[pair]
                rhs = jnp.concatenate([xs_p, prev.astype(BF16)], axis=0)
                y_h, st_h, cd_h = [], [], []
                for hh in (2 * pair, 2 * pair + 1):
                    colb = jnp.broadcast_to(acs[:, hh:hh + 1], (SSD_CHUNK, SSD_CHUNK))
                    seg = colb - acs_t[hh:hh + 1, :]
                    decay = jnp.exp(jnp.where(causal, seg, 0.0))
                    intra = cb * decay * dt_t[hh:hh + 1, :]
                    inter = cg * jnp.exp(colb)
                    lhs = jnp.concatenate([intra, inter], axis=1).astype(BF16)
                    y_h.append(_dot(lhs, rhs))
                    st_h.append(_dot((bgt * dtd_t[hh:hh + 1, :]).astype(BF16), xs_p))
                    cd_h.append(jnp.broadcast_to(chunk_decay[:, hh:hh + 1], (1, LANES)))
                ys.append(jnp.where(low_half, y_h[0], y_h[1]))
                state_ref[pair] = (prev * jnp.where(low_half_row, cd_h[0], cd_h[1])
                                   + jnp.where(low_half, st_h[0], st_h[1]))
            yg = jnp.concatenate(ys, axis=1)
            yg = yg + xs[r0:r0 + SSD_CHUNK, g * SSD_GW:(g + 1) * SSD_GW] * dsk[:, g * SSD_GW:(g + 1) * SSD_GW]
            u = yg * _silu(z[r0:r0 + SSD_CHUNK, g * SSD_GW:(g + 1) * SSD_GW])
            u = u * lax.rsqrt(jnp.mean(u * u, axis=-1, keepdims=True) + EPS)
            o_ref[r0:r0 + SSD_CHUNK, g * SSD_GW:(g + 1) * SSD_GW] = (
                u * nw[:, g * SSD_GW:(g + 1) * SSD_GW]).astype(o_ref.dtype)


def _pad_lanes(v, n=LANES):
    v = v.reshape(1, -1).astype(F32)
    return jnp.pad(v, ((0, 0), (0, n - v.shape[1])))


def _ssd_mixer(proj, bsz, seq, conv_w, conv_b, dt_bias, a_log, d_skip, norm_w, ts=256):
    nsteps = seq // ts
    row = lambda b, s: b * nsteps + s
    xw, bw, cw = conv_w[:, :SSD_WIDTH], conv_w[:, SSD_WIDTH:SSD_WIDTH + 512], conv_w[:, SSD_WIDTH + 512:]
    xb, bb, cb = conv_b[:SSD_WIDTH], conv_b[SSD_WIDTH:SSD_WIDTH + 512], conv_b[SSD_WIDTH + 512:]
    tri = jnp.asarray(np.tril(np.ones((SSD_CHUNK, SSD_CHUNK), np.float32)), BF16)
    const = lambda shape: pl.BlockSpec(shape, lambda b, s: (0,) * len(shape))
    kern = functools.partial(_ssd_kernel, ts=ts)
    return pl.pallas_call(
        kern,
        grid=(bsz, nsteps),
        in_specs=[pl.BlockSpec((ts, SSD_WIDTH), lambda b, s: (row(b, s), COL_Z // SSD_WIDTH)),
                  pl.BlockSpec((ts, SSD_WIDTH), lambda b, s: (row(b, s), COL_XS // SSD_WIDTH)),
                  pl.BlockSpec((ts, 512), lambda b, s: (row(b, s), COL_B // 512)),
                  pl.BlockSpec((ts, 512), lambda b, s: (row(b, s), COL_C // 512)),
                  pl.BlockSpec((ts, LANES), lambda b, s: (row(b, s), COL_DT // LANES)),
                  const((SSD_CONV, SSD_WIDTH)), const((SSD_CONV, 512)), const((SSD_CONV, 512)),
                  const((1, SSD_WIDTH)), const((1, 512)), const((1, 512)),
                  const((1, LANES)), const((1, LANES)), const((1, SSD_WIDTH)), const((1, SSD_WIDTH)),
                  const((SSD_CHUNK, SSD_CHUNK))],
        out_specs=pl.BlockSpec((ts, SSD_WIDTH), lambda b, s: (row(b, s), 0)),
        out_shape=jax.ShapeDtypeStruct((bsz * seq, SSD_WIDTH), BF16),
        scratch_shapes=[pltpu.VMEM((ts + SUBLANES, SSD_WIDTH), F32), pltpu.VMEM((ts + SUBLANES, 512), F32),
                        pltpu.VMEM((ts + SUBLANES, 512), F32),
                        pltpu.VMEM((SSD_HEADS // 2, SSD_STATE, 2 * SSD_HEAD_DIM), F32)],
        compiler_params=_params(("arbitrary", "arbitrary")),
        name="ssd_mixer",
    )(proj, proj, proj, proj, proj, xw, bw, cw, xb.reshape(1, -1), bb.reshape(1, -1), cb.reshape(1, -1),
      _pad_lanes(dt_bias), _pad_lanes(a_log), jnp.repeat(d_skip.astype(F32), SSD_HEAD_DIM).reshape(1, -1),
      norm_w.reshape(1, -1), tri)


def _hgrn_kernel(q_ref, f_ref, i_ref, g_ref, lbl_ref, nw_ref, cum_ref, o_ref, state_ref, *, ts, layer):
    s = pl.program_id(1)

    @pl.when(s == 0)
    def _():
        state_ref[...] = jnp.zeros_like(state_ref)

    nck = ts // HGRN_CHUNK
    lg = lbl_ref[...]
    e = jnp.exp(lg - jnp.max(lg, axis=0, keepdims=True))
    sm = e / jnp.sum(e, axis=0, keepdims=True)
    ridx = lax.broadcasted_iota(jnp.int32, lg.shape, 0)
    lb = jnp.sum(jnp.where((ridx >= 1) & (ridx <= layer), sm, 0.0), axis=0, keepdims=True)

    cum = cum_ref[...]
    nw = nw_ref[...]
    ti = lax.broadcasted_iota(jnp.int32, (ts, ts), 0)
    tj = lax.broadcasted_iota(jnp.int32, (ts, ts), 1)
    blockcausal = (ti // HGRN_CHUNK == tj // HGRN_CHUNK) & (ti >= tj)
    tok = lax.broadcasted_iota(jnp.int32, (1, ts), 1)

    def head(h):
        c0 = h * HGRN_DIM
        lbh = lb[:, c0:c0 + HGRN_DIM]
        f = lbh + (1.0 - lbh) * _sigmoid(f_ref[:, c0:c0 + HGRN_DIM])
        logf = jnp.log(f)
        k = 1.0 - f
        q = _silu(q_ref[:, c0:c0 + HGRN_DIM])
        v = i_ref[:, c0:c0 + HGRN_DIM]
        hi, mid, lo = _split3(logf)
        b = _dot(cum, hi) + _dot(cum, mid) + _dot(cum, lo)
        yield
        b3 = b.reshape(nck, HGRN_CHUNK, HGRN_DIM)
        bref = jnp.broadcast_to(b3[:, HGRN_CHUNK // 2:HGRN_CHUNK // 2 + 1, :], b3.shape).reshape(ts, HGRN_DIM)
        blast = jnp.broadcast_to(b3[:, HGRN_CHUNK - 1:HGRN_CHUNK, :], b3.shape).reshape(ts, HGRN_DIM)
        qe = (q * jnp.exp(b - bref)).astype(BF16)
        ke = (k * jnp.exp(bref - b)).astype(BF16)
        kl = (k * jnp.exp(blast - b)).astype(BF16)
        qb = (q * jnp.exp(b)).astype(BF16)
        vb = v.astype(BF16)
        att = _dot_nt(qe, ke)
        vt = v.T
        lhs = jnp.concatenate(
            [jnp.where(tok // HGRN_CHUNK == c, vt, 0.0) for c in range(nck)], axis=0).astype(BF16)
        st = _dot(lhs, kl)
        yield
        o = _dot(jnp.where(blockcausal, att, 0.0).astype(BF16), vb)
        state = state_ref[h]
        outs = []
        for c in range(nck):
            t0 = c * HGRN_CHUNK
            outs.append(_dot_nt(qb[t0:t0 + HGRN_CHUNK], state.astype(BF16)))
            cd = jnp.exp(blast[t0:t0 + 1, :])
            state = state * cd + st[c * HGRN_DIM:(c + 1) * HGRN_DIM]
            if c % 2 == 1:
                yield
        state_ref[h] = state
        o = o + jnp.concatenate(outs, axis=0)
        o = o * lax.rsqrt(jnp.mean(o * o, axis=-1, keepdims=True) + EPS) * nw
        o_ref[:, c0:c0 + HGRN_DIM] = (o * _silu(g_ref[:, c0:c0 + HGRN_DIM])).astype(o_ref.dtype)

    for h0 in range(0, HGRN_HEADS, HGRN_HEADS_INTERLEAVED):
        live = [head(h) for h in range(h0, h0 + HGRN_HEADS_INTERLEAVED)]
        while live:
            live = [gen for gen in live if next(gen, live) is not live]


def _hgrn_cum_matrix(ts):
    t = np.arange(ts)
    same = (t[:, None] // HGRN_CHUNK) == (t[None, :] // HGRN_CHUNK)
    return (same & (t[None, :] <= t[:, None])).astype(np.float32)


def _hgrn_mixer(proj, bsz, seq, lb_logits, norm_w, layer, ts=256):
    nsteps = seq // ts
    row = lambda b, s: b * nsteps + s
    depth = lb_logits.shape[0]
    cum = jnp.asarray(_hgrn_cum_matrix(ts), BF16)
    kern = functools.partial(_hgrn_kernel, ts=ts, layer=layer)
    blk = lambda col: pl.BlockSpec((ts, HGRN_WIDTH), lambda b, s: (row(b, s), col // HGRN_WIDTH))
    return pl.pallas_call(
        kern,
        grid=(bsz, nsteps),
        in_specs=[blk(COL_HQ), blk(COL_HF), blk(COL_HI), blk(COL_HG),
                  pl.BlockSpec((depth, HGRN_WIDTH), lambda b, s: (0, 0)),
                  pl.BlockSpec((1, HGRN_DIM), lambda b, s: (0, 0)),
                  pl.BlockSpec((ts, ts), lambda b, s: (0, 0))],
        out_specs=pl.BlockSpec((ts, HGRN_WIDTH), lambda b, s: (row(b, s), 0)),
        out_shape=jax.ShapeDtypeStruct((bsz * seq, HGRN_WIDTH), BF16),
        scratch_shapes=[pltpu.VMEM((HGRN_HEADS, HGRN_DIM, HGRN_DIM), F32)],
        compiler_params=_params(("arbitrary", "arbitrary")),
        name="hgrn2_mixer",
    )(proj, proj, proj, proj, lb_logits.astype(F32), norm_w.reshape(1, HGRN_DIM).astype(F32), cum)


def _rel_bucket_np(dist):
    max_exact = REL_BUCKETS // 2
    d = np.maximum(dist, 0)
    ratio = np.maximum(d, 1).astype(np.float32) / np.float32(max_exact)
    log_ratio = np.log(ratio).astype(np.float32) / np.float32(math.log(REL_MAX_DIST / max_exact))
    large = np.minimum(max_exact + (log_ratio * np.float32(REL_BUCKETS - max_exact)).astype(np.int32),
                       REL_BUCKETS - 1)
    return np.where(d < max_exact, d, large).astype(np.int32)


def _bias_expand_kernel(rb_ref, bmap_ref, o_ref):
    h = pl.program_id(0)
    bm = bmap_ref[...]
    out = jnp.full(bm.shape, NEG_INF, F32)
    for k in range(REL_BUCKETS):
        out = jnp.where(bm == k, rb_ref[k, h] * LOG2E, out)
    o_ref[...] = out


def _bias_expand(rel_bias, bmap, tr):
    rows, cols = bmap.shape
    return pl.pallas_call(
        _bias_expand_kernel,
        grid=(NSA_HEADS, rows // tr),
        in_specs=[pl.BlockSpec(memory_space=pltpu.SMEM),
                  pl.BlockSpec((tr, cols), lambda h, i: (i, 0))],
        out_specs=pl.BlockSpec((None, tr, cols), lambda h, i: (h, i, 0)),
        out_shape=jax.ShapeDtypeStruct((NSA_HEADS, rows, cols), F32),
        compiler_params=_params(("arbitrary", "arbitrary")),
        name="nsa_bias_expand",
    )(rel_bias.astype(F32), bmap)


def _nsa_bias_tables(rel_bias, seq):
    ncp = seq // CMP_STRIDE
    t = np.arange(seq)[:, None]
    cmp_end = np.arange(ncp)[None, :] * CMP_STRIDE + CMP_BLOCK - 1
    d = t - cmp_end
    bmap_cmp = np.where(d >= 0, _rel_bucket_np(d), -1).astype(np.int32)
    l = np.arange(Q_BLOCK)[:, None]
    j = np.arange(WIN_KEYS)[None, :]
    dist = l - j + WINDOW
    win = np.where((dist >= 0) & (dist < WINDOW), _rel_bucket_np(dist), -1)
    far = _rel_bucket_np(np.arange(Q_BLOCK + 1, 8 * seq))
    assert (far == far[0]).all(), "distances beyond one query block must share a single bucket"
    bmap_tab = np.concatenate([win, np.full((Q_BLOCK, Q_BLOCK), far[0])], axis=1).astype(np.int32)
    bias_cmp = _bias_expand(rel_bias, jnp.asarray(bmap_cmp), 512)
    bias_tab = _bias_expand(rel_bias, jnp.asarray(bmap_tab), Q_BLOCK)
    return bias_cmp, bias_tab


def _cmp_kernel(u_ref, pe_ref, w1_ref, w2_ref, o_ref, *, ncp):
    half = CMP_BLOCK // 2
    pe = pe_ref[...]
    top = jnp.zeros((ncp, CMP_HIDDEN), F32)
    bot = jnp.zeros((ncp, CMP_HIDDEN), F32)
    for l in range(half):
        x = u_ref[pl.ds(l, ncp, stride=half), :]
        top = top + _dot((x + pe[l:l + 1, :]).astype(BF16), w1_ref[l])
        bot = bot + _dot((x + pe[half + l:half + l + 1, :]).astype(BF16), w1_ref[half + l])
    hid = top + pltpu.roll(bot, ncp - 1, 0)
    o_ref[...] = _dot(_silu(hid).astype(BF16), w2_ref[...])


def _nsa_compress(proj, bsz, seq, pe, w1, w2):
    ncp = seq // CMP_STRIDE
    kern = functools.partial(_cmp_kernel, ncp=ncp)
    return pl.pallas_call(
        kern,
        grid=(bsz, 2, NSA_KV_HEADS),
        in_specs=[pl.BlockSpec((seq, NSA_HEAD_DIM), lambda b, t, h: (b, COL_KC // NSA_HEAD_DIM + NSA_KV_HEADS * t + h)),
                  pl.BlockSpec((None, CMP_BLOCK, NSA_HEAD_DIM), lambda b, t, h: (t, 0, 0)),
                  pl.BlockSpec((None, CMP_BLOCK, NSA_HEAD_DIM, CMP_HIDDEN), lambda b, t, h: (t, 0, 0, 0)),
                  pl.BlockSpec((None, CMP_HIDDEN, NSA_HEAD_DIM), lambda b, t, h: (t, 0, 0))],
        out_specs=pl.BlockSpec((None, None, None, ncp, NSA_HEAD_DIM), lambda b, t, h: (b, t, h, 0, 0)),
        out_shape=jax.ShapeDtypeStruct((bsz, 2, NSA_KV_HEADS, ncp, NSA_HEAD_DIM), F32),
        compiler_params=_params(("arbitrary", "arbitrary", "arbitrary")),
        name="nsa_compress",
    )(proj, pe.astype(F32), w1.reshape(2, CMP_BLOCK, NSA_HEAD_DIM, CMP_HIDDEN).astype(BF16), w2.astype(BF16))


def _softmax_start(s, v):
    m = jnp.max(s, axis=-1, keepdims=True)
    p = jnp.exp2(s - m)
    return m, jnp.sum(p, axis=-1, keepdims=True), _dot(p.astype(BF16), v)


def _softmax_update(s, v, carry):
    m, l, acc = carry
    m_new = jnp.maximum(m, jnp.max(s, axis=-1, keepdims=True))
    alpha = jnp.exp2(m - m_new)
    p = jnp.exp2(s - m_new)
    return m_new, alpha * l + jnp.sum(p, axis=-1, keepdims=True), alpha * acc + _dot(p.astype(BF16), v)


def _softmax_finish(carry):
    m, l, acc = carry
    return jnp.where(m > 0.5 * NEG_INF, acc / jnp.maximum(l, 1e-30), 0.0)


def _nsa_kernel(q_ref, gate_ref, kc_ref, vc_ref, ks_ref, vs_ref, kw_ref, vw_ref, bcmp_ref, btab_ref,
                kone_ref, c2s_ref, o_ref, kaug_ref, vsb_ref, kwb_ref, vwb_ref, sa_ref, sb_ref, *, n_sel, nqb):
    hk = pl.program_id(1)
    c0 = pl.program_id(2) * nqb
    rows = NSA_GQA * Q_BLOCK
    lane = lax.broadcasted_iota(jnp.int32, (Q_BLOCK, LANES), 1)

    @pl.when(c0 == 0)
    def _():
        kaug_ref[0:Q_BLOCK, 0:NSA_HEAD_DIM] = jnp.zeros((Q_BLOCK, NSA_HEAD_DIM), BF16)
        kaug_ref[Q_BLOCK:, 0:NSA_HEAD_DIM] = ks_ref[...].astype(BF16)
        kaug_ref[:, NSA_HEAD_DIM:] = kone_ref[...]
        vsb_ref[0:Q_BLOCK, :] = jnp.zeros((Q_BLOCK, NSA_HEAD_DIM), BF16)
        vsb_ref[Q_BLOCK:, :] = vs_ref[...].astype(BF16)
        kwb_ref[0:WINDOW, 0:NSA_HEAD_DIM] = jnp.zeros((WINDOW, NSA_HEAD_DIM), BF16)
        kwb_ref[WINDOW:, 0:NSA_HEAD_DIM] = kw_ref[...].astype(BF16)
        for r0 in range(0, WINDOW, Q_BLOCK):
            kwb_ref[r0:r0 + Q_BLOCK, NSA_HEAD_DIM:] = jnp.where(lane == AUG_PAD, 1.0, 0.0).astype(BF16)
        kwb_ref[WINDOW:, NSA_HEAD_DIM:] = jnp.zeros((kw_ref.shape[0], LANES), BF16)
        vwb_ref[0:WINDOW, :] = jnp.zeros((WINDOW, NSA_HEAD_DIM), BF16)
        vwb_ref[WINDOW:, :] = vw_ref[...].astype(BF16)

    gens = [_nsa_unit(c0 + u, q_ref[u * Q_BLOCK:(u + 1) * Q_BLOCK, :], bcmp_ref[:, u * Q_BLOCK:(u + 1) * Q_BLOCK, :],
                      btab_ref, kc_ref, vc_ref, c2s_ref, kaug_ref, vsb_ref, kwb_ref, vwb_ref, n_sel)
            for u in range(nqb)]
    units = [None] * nqb
    while any(un is None for un in units):
        for u, gen in enumerate(gens):
            if units[u] is None:
                units[u] = next(gen)
    far_blocks = FAR_KEYS // Q_BLOCK
    last_slab = (kaug_ref.shape[0] - Q_BLOCK) // FAR_KEYS - 1

    def far_values(slab):
        return vsb_ref[pl.ds(pl.multiple_of(slab * FAR_KEYS + Q_BLOCK, Q_BLOCK), FAR_KEYS), :]

    o_sel = []
    for u0 in range(0, nqb, NSA_FAR_GROUP):
        group = units[u0:u0 + NSA_FAR_GROUP]
        qaug_far = jnp.concatenate([un[3] for un in group], axis=0)
        carry = tuple(jnp.concatenate([un[2][i] for un in group], axis=0) for i in range(3))
        n_slabs = (c0 + u0 + NSA_FAR_GROUP - 2 + far_blocks - 1) // far_blocks
        n_pairs = n_slabs // 2

        def far_logits(slab, qaug_far=qaug_far):
            r0 = pl.multiple_of(jnp.minimum(slab, last_slab) * FAR_KEYS + Q_BLOCK, Q_BLOCK)
            return _dot_nt(qaug_far, kaug_ref[pl.ds(r0, FAR_KEYS), :])

        sa_ref[...] = far_logits(0)

        def far_body(k, carry, far_logits=far_logits):
            sb_ref[...] = far_logits(2 * k + 1)
            carry = _softmax_update(sa_ref[...], far_values(2 * k), carry)
            sa_ref[...] = far_logits(2 * k + 2)
            return _softmax_update(sb_ref[...], far_values(2 * k + 1), carry)

        carry = lax.fori_loop(0, n_pairs, far_body, carry)
        carry = lax.cond(n_slabs % 2 == 1,
                         lambda cr, n_pairs=n_pairs: _softmax_update(sa_ref[...], far_values(2 * n_pairs), cr),
                         lambda cr: cr, carry)
        out = _softmax_finish(carry)
        o_sel += [out[i * rows:(i + 1) * rows] for i in range(NSA_FAR_GROUP)]

    for u in range(nqb):
        o_cmp, o_win = units[u][0], units[u][1]
        gates = _sigmoid(gate_ref[u * Q_BLOCK:(u + 1) * Q_BLOCK, :])
        for g in range(NSA_GQA):
            def gate_col(br):
                return jnp.sum(jnp.where(lane == hk * (3 * NSA_GQA) + g * 3 + br, gates, 0.0), axis=-1, keepdims=True)

            r0 = g * Q_BLOCK
            out = (gate_col(0) * o_cmp[r0:r0 + Q_BLOCK] + gate_col(1) * o_sel[u][r0:r0 + Q_BLOCK]
                   + gate_col(2) * o_win[r0:r0 + Q_BLOCK])
            o_ref[u * Q_BLOCK:(u + 1) * Q_BLOCK, g * NSA_HEAD_DIM:(g + 1) * NSA_HEAD_DIM] = out.astype(o_ref.dtype)


def _nsa_unit(c, q_in, bc_in, btab_ref, kc_ref, vc_ref, c2s_ref, kaug_ref, vsb_ref, kwb_ref, vwb_ref, n_sel):
    rows = NSA_GQA * Q_BLOCK
    lane = lax.broadcasted_iota(jnp.int32, (Q_BLOCK, LANES), 1)
    q = q_in * (NSA_HEAD_DIM ** -0.5 * LOG2E)
    q3 = jnp.concatenate([q[:, g * NSA_HEAD_DIM:(g + 1) * NSA_HEAD_DIM] for g in range(NSA_GQA)],
                         axis=0).astype(BF16)

    btab = btab_ref[...].reshape(rows, WIN_KEYS + Q_BLOCK)
    near0 = pl.multiple_of(c * Q_BLOCK, Q_BLOCK)

    qaug_win = jnp.concatenate(
        [q3, jnp.concatenate([jnp.where(lane == AUG_PAD, NEG_INF, 0.0).astype(BF16)] * NSA_GQA, axis=0)], axis=1)
    s_win = _dot_nt(qaug_win, kwb_ref[pl.ds(near0, WIN_KEYS), :]) + btab[:, 0:WIN_KEYS]
    yield None

    bc = bc_in.reshape(rows, bc_in.shape[-1])
    valid = bc > 0.5 * NEG_INF
    s = _dot_nt(q3, kc_ref[...].astype(BF16)) + bc
    yield None
    o_win = _softmax_finish(_softmax_start(s_win, vwb_ref[pl.ds(near0, WIN_KEYS), :]))
    yield None
    m = jnp.max(s, axis=-1, keepdims=True)
    p = jnp.where(valid, jnp.exp2(s - m), 0.0)
    p = p / jnp.maximum(jnp.sum(p, axis=-1, keepdims=True), 1e-30)
    o_cmp = _dot(p.astype(BF16), vc_ref[...].astype(BF16))

    psum = p[0:Q_BLOCK] + p[Q_BLOCK:2 * Q_BLOCK] + p[2 * Q_BLOCK:3 * Q_BLOCK]
    ph = psum.astype(BF16)
    plo = (psum - ph.astype(F32)).astype(BF16)
    c2s = c2s_ref[...]
    imp_t = _dot_nt(c2s, ph) + _dot_nt(c2s, plo)
    nsp = -(-n_sel // SUBLANES) * SUBLANES
    imp_t = imp_t[0:nsp]
    tq = c * Q_BLOCK + lax.broadcasted_iota(jnp.int32, (nsp, Q_BLOCK), 1)
    jj = lax.broadcasted_iota(jnp.int32, (nsp, Q_BLOCK), 0)
    valid_sel = (jj < n_sel) & (jj * SEL_BLOCK <= tq)
    back = tq // SEL_BLOCK - jj
    force = (jj == 0) | ((back >= 0) & (back < SEL_LOCAL))
    score = jnp.where(valid_sel, imp_t + jnp.where(force, FORCE_BONUS, 0.0), NEG_INF)
    yield None
    n_tiles = nsp // SUBLANES
    tiles = [score[v * SUBLANES:(v + 1) * SUBLANES] for v in range(n_tiles)]
    sub = lax.broadcasted_iota(jnp.int32, (SUBLANES, Q_BLOCK), 0)
    cnts = [jnp.zeros((SUBLANES, Q_BLOCK), F32) for _ in range(n_tiles)]
    for jp in range(n_sel):
        v0, r0 = divmod(jp, SUBLANES)
        r = tiles[v0][r0:r0 + 1, :]
        for v in range(n_tiles):
            gt = jnp.where(r > tiles[v], 1.0, 0.0)
            ge = jnp.where(r >= tiles[v], 1.0, 0.0)
            cnts[v] = cnts[v] + (gt if v < v0 else ge if v > v0 else jnp.where(sub > r0, ge, gt))
        if jp % (2 * SUBLANES) == 2 * SUBLANES - 1:
            yield None
    cnt = jnp.concatenate(cnts, axis=0)
    keep = valid_sel & (cnt < float(min(SEL_TOPK, n_sel)))

    ext = lax.broadcasted_iota(jnp.int32, (LANES - nsp, Q_BLOCK), 0) + nsp
    aug = jnp.concatenate([jnp.where(keep, 0.0, NEG_INF), jnp.where(ext == AUG_PAD, NEG_INF, 0.0)], axis=0)
    aug_near = aug.T
    aug_far = jnp.where((lane >= 2 * (c - 1)) & (lane < AUG_BIAS_HI), NEG_INF, aug_near)
    far_parts = []
    for g in range(NSA_GQA):
        bfar = btab[g * Q_BLOCK:(g + 1) * Q_BLOCK, WIN_KEYS:]
        hi = bfar.astype(BF16).astype(F32)
        far_parts.append(jnp.where(lane == AUG_BIAS_HI, hi, jnp.where(lane == AUG_BIAS_LO, bfar - hi, aug_far)))
    qaug_near = jnp.concatenate([q3, jnp.concatenate([aug_near.astype(BF16)] * NSA_GQA, axis=0)], axis=1)
    qaug_far = jnp.concatenate([q3, jnp.concatenate(far_parts, axis=0).astype(BF16)], axis=1)

    s = _dot_nt(qaug_near, kaug_ref[pl.ds(near0, NEAR_KEYS), :]) + btab[:, WIN_KEYS - NEAR_KEYS:WIN_KEYS]
    yield None
    carry = _softmax_start(s, vsb_ref[pl.ds(near0, NEAR_KEYS), :])
    yield o_cmp, o_win, carry, qaug_far


def _nsa_attention(proj, cmp_kv, bias_cmp, bias_tab, bsz, seq):
    nq = seq // Q_BLOCK
    ncp = seq // CMP_STRIDE
    n_sel = seq // SEL_BLOCK
    assert n_sel <= AUG_BIAS_HI and NSA_HEAD_DIM == LANES
    assert seq % (2 * FAR_KEYS) == 0, "far steps walk the keys two 512-key slabs at a time"
    key = np.arange(-Q_BLOCK, seq)[:, None]
    ln = np.arange(LANES)[None, :]
    kone = np.where(key < 0, ln == AUG_PAD,
                    ((ln < AUG_BIAS_HI) & (key // SEL_BLOCK == ln)) | (ln == AUG_BIAS_HI) | (ln == AUG_BIAS_LO))
    kone = jnp.asarray(kone, BF16)
    c_start = np.arange(ncp)[None, :] * CMP_STRIDE
    s_start = np.arange(LANES)[:, None] * SEL_BLOCK
    overlap = np.clip(np.minimum(c_start + CMP_BLOCK, s_start + SEL_BLOCK) - np.maximum(c_start, s_start), 0, None)
    overlap = np.where(np.arange(LANES)[:, None] < n_sel, overlap, 0)
    c2s_t = jnp.asarray(overlap / CMP_BLOCK, BF16)
    gqa_w = NSA_GQA * NSA_HEAD_DIM
    full = lambda col: pl.BlockSpec((seq, NSA_HEAD_DIM), lambda b, h, c: (b, col // NSA_HEAD_DIM + h))
    nqb = NSA_Q_BLOCKS_PER_STEP
    ns = nq // nqb
    qr = nqb * Q_BLOCK
    kern = functools.partial(_nsa_kernel, n_sel=n_sel, nqb=nqb)
    return pl.pallas_call(
        kern,
        grid=(bsz, NSA_KV_HEADS, ns),
        in_specs=[pl.BlockSpec((qr, gqa_w), lambda b, h, c: (b * ns + c, COL_Q // gqa_w + h)),
                  pl.BlockSpec((qr, LANES), lambda b, h, c: (b * ns + c, COL_GATE // LANES)),
                  pl.BlockSpec((None, None, None, ncp, NSA_HEAD_DIM), lambda b, h, c: (b, 0, h, 0, 0)),
                  pl.BlockSpec((None, None, None, ncp, NSA_HEAD_DIM), lambda b, h, c: (b, 1, h, 0, 0)),
                  full(COL_KS), full(COL_VS), full(COL_KW), full(COL_VW),
                  pl.BlockSpec((NSA_GQA, qr, ncp), lambda b, h, c: (h, c, 0)),
                  pl.BlockSpec((NSA_GQA, Q_BLOCK, WIN_KEYS + Q_BLOCK), lambda b, h, c: (h, 0, 0)),
                  pl.BlockSpec((seq + Q_BLOCK, LANES), lambda b, h, c: (0, 0)),
                  pl.BlockSpec((LANES, ncp), lambda b, h, c: (0, 0))],
        out_specs=pl.BlockSpec((qr, gqa_w), lambda b, h, c: (b * ns + c, h)),
        out_shape=jax.ShapeDtypeStruct((bsz * seq, NSA_WIDTH), BF16),
        scratch_shapes=[pltpu.VMEM((seq + Q_BLOCK, 2 * NSA_HEAD_DIM), BF16),
                        pltpu.VMEM((seq + Q_BLOCK, NSA_HEAD_DIM), BF16),
                        pltpu.VMEM((seq + WINDOW, 2 * NSA_HEAD_DIM), BF16),
                        pltpu.VMEM((seq + WINDOW, NSA_HEAD_DIM), BF16),
                        pltpu.VMEM((NSA_FAR_GROUP * NSA_GQA * Q_BLOCK, FAR_KEYS), F32),
                        pltpu.VMEM((NSA_FAR_GROUP * NSA_GQA * Q_BLOCK, FAR_KEYS), F32)],
        compiler_params=_params(("arbitrary", "arbitrary", "arbitrary")),
        name="nsa_attention",
    )(proj, proj, cmp_kv, cmp_kv, proj, proj, proj, proj, bias_cmp, bias_tab, kone, c2s_t)


_IN_OFF = tuple(int(v) for v in np.cumsum((0,) + IN_SIZES))
_W_IN_COPIES = ((_IN_OFF[0], SSD_WIDTH, COL_Z), (_IN_OFF[1], SSD_WIDTH, COL_XS),
                (_IN_OFF[1] + SSD_WIDTH, 512, COL_B), (_IN_OFF[1] + SSD_WIDTH + 512, 512, COL_C),
                (_IN_OFF[3], NSA_WIDTH, COL_Q), (_IN_OFF[4], 6 * NSA_KV_WIDTH, COL_KC),
                (_IN_OFF[11], 4 * HGRN_WIDTH, COL_HQ))
_W_IN_NARROW = ((_IN_OFF[2], IN_SIZES[2], COL_DT, COL_GATE - COL_DT), (_IN_OFF[10], IN_SIZES[10], COL_GATE, COL_HQ - COL_GATE))


def _relayout_kernel(w_ref, o_ref):
    for src, width, dst in _W_IN_COPIES:
        o_ref[:, dst:dst + width] = w_ref[:, src:src + width].astype(BF16)
    for src, valid, dst, padded in _W_IN_NARROW:
        tile = w_ref[:, src:src + LANES]
        lane = lax.broadcasted_iota(jnp.int32, tile.shape, 1)
        o_ref[:, dst:dst + LANES] = jnp.where(lane < valid, tile, 0.0).astype(BF16)
        if padded > LANES:
            o_ref[:, dst + LANES:dst + padded] = jnp.zeros((tile.shape[0], padded - LANES), BF16)


def _relayout_w_in(w, tr=128):
    depth, d, n = w.shape
    return pl.pallas_call(
        _relayout_kernel,
        grid=(depth, d // tr),
        in_specs=[pl.BlockSpec((None, tr, n), lambda l, i: (l, i, 0))],
        out_specs=pl.BlockSpec((None, tr, IN_PAD), lambda l, i: (l, i, 0)),
        out_shape=jax.ShapeDtypeStruct((depth, d, IN_PAD), BF16),
        compiler_params=_params(("parallel", "parallel")),
        name="w_in_relayout",
    )(w)


def _tiles(seq):
    return dict(in_proj=(1024, 1024), out_proj=(1024, 512), gate_up=(min(2048, seq), 256), down=(256, 1024))


def kernel(x, norm_mix_w, w_in, ssd_conv_w, ssd_conv_b, ssd_dt_bias, ssd_a_log, ssd_d, ssd_norm_w, nsa_cmp_pe, nsa_cmp_w1, nsa_cmp_w2, rel_bias, hgrn_lb_logits, hgrn_norm_w, w_out, norm_ffn_w, ffn_w_gu, ffn_conv_w, ffn_conv_b, ffn_w_down, norm_f_w):
    bsz, seq, d = x.shape
    depth = w_in.shape[0]
    xr = x.reshape(bsz * seq, d).astype(F32)
    bias_cmp, bias_tab = _nsa_bias_tables(rel_bias, seq)
    tiles = _tiles(seq)
    w_in_b = _relayout_w_in(jnp.pad(w_in.astype(BF16), ((0, 0), (0, 0), (0, -w_in.shape[2] % LANES))))
    w_out_b = w_out.astype(BF16)
    w_gu_b = ffn_w_gu.astype(BF16)
    w_down_b = ffn_w_down.astype(BF16)
    xw, ssq = _prenorm(xr, norm_mix_w[0])
    for l in range(depth):
        proj = _matmul(xw, ssq, w_in_b, l, F32, *tiles["in_proj"], name="in_proj")
        y_ssd = _ssd_mixer(proj, bsz, seq, ssd_conv_w[l].astype(F32), ssd_conv_b[l].astype(F32), ssd_dt_bias[l],
                           ssd_a_log[l], ssd_d[l], ssd_norm_w[l].astype(F32))
        cmp_kv = _nsa_compress(proj, bsz, seq, nsa_cmp_pe[l], nsa_cmp_w1[l], nsa_cmp_w2[l])
        y_nsa = _nsa_attention(proj, cmp_kv, bias_cmp, bias_tab, bsz, seq)
        y_hgrn = _hgrn_mixer(proj, bsz, seq, hgrn_lb_logits, hgrn_norm_w[l], l)
        xr, xw, ssq = _out_proj([y_ssd, y_nsa, y_hgrn], w_out_b, l, xr, norm_ffn_w[l], *tiles["out_proj"])
        act = _gate_up(xw, ssq, w_gu_b, l, ffn_conv_w[l].astype(F32), ffn_conv_b[l].astype(F32), seq,
                       *tiles["gate_up"])
        if l + 1 < depth:
            xr, xw, ssq = _down_proj_norm(act, w_down_b, l, xr, norm_mix_w[l + 1], *tiles["down"])
        else:
            xr = _matmul_residual_wres(act, w_down_b, l, xr, *tiles["down"], name="ffn_down")
    out = _rmsnorm(xr, norm_f_w, x.dtype)
    return out.reshape(bsz, seq, d)
```

```python
import functools
import math

import numpy as np
import jax
import jax.numpy as jnp
from jax import lax
from jax.experimental import pallas as pl
from jax.experimental.pallas import tpu as pltpu

F32 = jnp.float32
BF16 = jnp.bfloat16

D_MODEL = 4096
SSD_HEAD_DIM = 64
SSD_WIDTH = 1536
SSD_HEADS = 24
SSD_GROUPS = 4
SSD_HPG = 6
SSD_STATE = 128
SSD_CONV = 4
SSD_CHUNK = 128
SSD_GW = SSD_WIDTH // SSD_GROUPS
NSA_HEAD_DIM = 128
NSA_WIDTH = 1536
NSA_HEADS = 12
NSA_KV_HEADS = 4
NSA_GQA = 3
NSA_KV_WIDTH = 512
CMP_BLOCK = 32
CMP_STRIDE = 16
CMP_HIDDEN = 256
SEL_BLOCK = 64
SEL_TOPK = 16
SEL_LOCAL = 2
WINDOW = 512
Q_BLOCK = 128
FORCE_BONUS = 1e4
HGRN_WIDTH = 1024
HGRN_HEADS = 8
HGRN_DIM = 128
HGRN_CHUNK = 32
REL_BUCKETS = 32
REL_MAX_DIST = 128
D_FF = 11008
FFN_CONV = 3
EPS = 1e-6
NEG_INF = -1e30

IN_SIZES = (SSD_WIDTH, SSD_WIDTH + 2 * SSD_GROUPS * SSD_STATE, SSD_HEADS, NSA_WIDTH,
            NSA_KV_WIDTH, NSA_KV_WIDTH, NSA_KV_WIDTH, NSA_KV_WIDTH, NSA_KV_WIDTH, NSA_KV_WIDTH,
            3 * NSA_HEADS, HGRN_WIDTH, HGRN_WIDTH, HGRN_WIDTH, HGRN_WIDTH)

LANES = 128
SUBLANES = 8
VMEM_LIMIT = 56 * 1024 * 1024

COL_Z = 0
COL_XS = 1536
COL_Q = 3072
COL_B = 4608
COL_C = 5120
COL_KC = 5632
COL_VC = 6144
COL_KS = 6656
COL_VS = 7168
COL_KW = 7680
COL_VW = 8192
COL_DT = 8704
COL_GATE = 8832
COL_HQ = 9216
COL_HF = 10240
COL_HI = 11264
COL_HG = 12288
IN_PAD = 13312

LOG2E = math.log2(math.e)
AUG_BIAS_HI = 64
AUG_BIAS_LO = 65
AUG_PAD = 66
WIN_KEYS = WINDOW + Q_BLOCK
NEAR_KEYS = 2 * Q_BLOCK
FAR_KEYS = 4 * Q_BLOCK
HGRN_HEADS_INTERLEAVED = 8
NSA_Q_BLOCKS_PER_STEP = 4
NSA_FAR_GROUP = 2


def _params(semantics):
    return pltpu.CompilerParams(dimension_semantics=semantics, vmem_limit_bytes=VMEM_LIMIT)


def _sigmoid(x):
    return 1.0 / (1.0 + jnp.exp(-x))


def _silu(x):
    return x * _sigmoid(x)


def _dot(a, b):
    return jnp.dot(a, b, preferred_element_type=F32)


def _dot_nt(a, b):
    return lax.dot_general(a, b, (((1,), (1,)), ((), ())), preferred_element_type=F32)


def _split3(x):
    hi = x.astype(BF16)
    r1 = x - hi.astype(F32)
    mid = r1.astype(BF16)
    lo = (r1 - mid.astype(F32)).astype(BF16)
    return hi, mid, lo


def _rmsnorm_kernel(x_ref, w_ref, o_ref):
    x = x_ref[...]
    ms = jnp.mean(x * x, axis=-1, keepdims=True)
    o_ref[...] = (x * lax.rsqrt(ms + EPS) * w_ref[...]).astype(o_ref.dtype)


def _rmsnorm(x, w, out_dtype, tm=256):
    m, d = x.shape
    return pl.pallas_call(
        _rmsnorm_kernel,
        grid=(m // tm,),
        in_specs=[pl.BlockSpec((tm, d), lambda i: (i, 0)), pl.BlockSpec((1, d), lambda i: (0, 0))],
        out_specs=pl.BlockSpec((tm, d), lambda i: (i, 0)),
        out_shape=jax.ShapeDtypeStruct((m, d), out_dtype),
        compiler_params=_params(("parallel",)),
        name="rmsnorm",
    )(x, w.reshape(1, d).astype(F32))


def _lane_fold(sq):
    part = sq[:, 0:LANES]
    for c in range(1, sq.shape[1] // LANES):
        part = part + sq[:, c * LANES:(c + 1) * LANES]
    return part


def _rinv_lanes(ssq, d):
    part = ssq[0]
    for p in range(1, ssq.shape[0]):
        part = part + ssq[p]
    tot = jnp.sum(part, axis=-1, keepdims=True)
    return jnp.broadcast_to(lax.rsqrt(tot * (1.0 / d) + EPS), part.shape)


def _scale_rows(acc, rinv):
    return jnp.concatenate([acc[:, c * LANES:(c + 1) * LANES] * rinv for c in range(acc.shape[1] // LANES)], axis=1)


def _prenorm_kernel(x_ref, w_ref, xw_ref, ssq_ref):
    x = x_ref[...]
    xw_ref[...] = (x * w_ref[...]).astype(xw_ref.dtype)
    ssq_ref[0] = _lane_fold(x * x)


def _prenorm(x, w, tm=256):
    m, d = x.shape
    return pl.pallas_call(
        _prenorm_kernel,
        grid=(m // tm,),
        in_specs=[pl.BlockSpec((tm, d), lambda i: (i, 0)), pl.BlockSpec((1, d), lambda i: (0, 0))],
        out_specs=[pl.BlockSpec((tm, d), lambda i: (i, 0)), pl.BlockSpec((1, tm, LANES), lambda i: (0, i, 0))],
        out_shape=[jax.ShapeDtypeStruct((m, d), BF16), jax.ShapeDtypeStruct((1, m, LANES), F32)],
        compiler_params=_params(("parallel",)),
        name="prenorm",
    )(x, w.reshape(1, d).astype(F32))


def _mm_kernel(a_ref, ssq_ref, w_ref, o_ref, rinv_ref, *, d):
    @pl.when(pl.program_id(1) == 0)
    def _():
        rinv_ref[...] = _rinv_lanes(ssq_ref[...], d)

    o_ref[...] = _scale_rows(_dot(a_ref[...], w_ref[...]), rinv_ref[...]).astype(o_ref.dtype)


def _mm_res_kernel(a_ref, w_ref, r_ref, o_ref):
    o_ref[...] = r_ref[...] + _dot(a_ref[...], w_ref[...])


def _matmul(xw, ssq, w, layer, out_dtype, tm, tn, name):
    m, k = xw.shape
    n = w.shape[2]
    return pl.pallas_call(
        functools.partial(_mm_kernel, d=k),
        grid=(m // tm, n // tn),
        in_specs=[pl.BlockSpec((tm, k), lambda i, j: (i, 0)),
                  pl.BlockSpec((ssq.shape[0], tm, LANES), lambda i, j: (0, i, 0)),
                  pl.BlockSpec((None, k, tn), lambda i, j: (layer, 0, j))],
        out_specs=pl.BlockSpec((tm, tn), lambda i, j: (i, j)),
        out_shape=jax.ShapeDtypeStruct((m, n), out_dtype),
        scratch_shapes=[pltpu.VMEM((tm, LANES), F32)],
        compiler_params=_params(("arbitrary", "arbitrary")),
        name=name,
    )(xw, ssq, w)


def _out_proj_kernel(*refs):
    *a_refs, w_ref, r_ref, nw_ref, o_ref, xw_ref, ssq_ref = refs
    j = pl.program_id(1)
    acc = r_ref[...]
    k0 = 0
    for a_ref in a_refs:
        acc = acc + _dot(a_ref[...], w_ref[k0:k0 + a_ref.shape[1], :])
        k0 += a_ref.shape[1]
    o_ref[...] = acc
    xw_ref[...] = (acc * nw_ref[...]).astype(xw_ref.dtype)
    part = _lane_fold(acc * acc)

    @pl.when(j == 0)
    def _():
        ssq_ref[0] = part

    @pl.when(j > 0)
    def _():
        ssq_ref[0] = ssq_ref[0] + part


def _out_proj(parts, w, layer, res, norm_w, tm, tn):
    m = res.shape[0]
    k, n = w.shape[1], w.shape[2]
    assert sum(p.shape[1] for p in parts) == k
    return pl.pallas_call(
        _out_proj_kernel,
        grid=(m // tm, n // tn),
        in_specs=[pl.BlockSpec((tm, p.shape[1]), lambda i, j: (i, 0)) for p in parts]
        + [pl.BlockSpec((None, k, tn), lambda i, j: (layer, 0, j)),
           pl.BlockSpec((tm, tn), lambda i, j: (i, j)),
           pl.BlockSpec((1, tn), lambda i, j: (0, j))],
        out_specs=[pl.BlockSpec((tm, tn), lambda i, j: (i, j)), pl.BlockSpec((tm, tn), lambda i, j: (i, j)),
                   pl.BlockSpec((1, tm, LANES), lambda i, j: (0, i, 0))],
        out_shape=[jax.ShapeDtypeStruct((m, n), F32), jax.ShapeDtypeStruct((m, n), BF16),
                   jax.ShapeDtypeStruct((1, m, LANES), F32)],
        compiler_params=_params(("arbitrary", "arbitrary")),
        name="out_proj",
    )(*parts, w, res, norm_w.reshape(1, n).astype(F32))


def _down_norm_kernel(a_ref, w_ref, r_ref, nw_ref, o_ref, xw_ref, ssq_ref):
    acc = r_ref[...] + _dot(a_ref[...], w_ref[...])
    o_ref[...] = acc
    xw_ref[...] = (acc * nw_ref[...]).astype(xw_ref.dtype)
    ssq_ref[...] = _lane_fold(acc * acc)


def _down_proj_norm(a, w, layer, res, norm_w, tm, tn):
    m, k = a.shape
    n = w.shape[2]
    return pl.pallas_call(
        _down_norm_kernel,
        grid=(n // tn, m // tm),
        in_specs=[pl.BlockSpec((tm, k), lambda j, i: (i, 0)),
                  pl.BlockSpec((None, k, tn), lambda j, i: (layer, 0, j), pipeline_mode=pl.Buffered(1)),
                  pl.BlockSpec((tm, tn), lambda j, i: (i, j)),
                  pl.BlockSpec((1, tn), lambda j, i: (0, j))],
        out_specs=[pl.BlockSpec((tm, tn), lambda j, i: (i, j)), pl.BlockSpec((tm, tn), lambda j, i: (i, j)),
                   pl.BlockSpec((None, tm, LANES), lambda j, i: (j, i, 0))],
        out_shape=[jax.ShapeDtypeStruct((m, n), F32), jax.ShapeDtypeStruct((m, n), BF16),
                   jax.ShapeDtypeStruct((n // tn, m, LANES), F32)],
        compiler_params=_params(("arbitrary", "arbitrary")),
        name="ffn_down",
    )(a, w, res, norm_w.reshape(1, n).astype(F32))


def _matmul_residual_wres(a, w, layer, res, tm, tn, name):
    m, k = a.shape
    n = w.shape[2]
    return pl.pallas_call(
        _mm_res_kernel,
        grid=(n // tn, m // tm),
        in_specs=[pl.BlockSpec((tm, k), lambda j, i: (i, 0)),
                  pl.BlockSpec((None, k, tn), lambda j, i: (layer, 0, j), pipeline_mode=pl.Buffered(1)),
                  pl.BlockSpec((tm, tn), lambda j, i: (i, j))],
        out_specs=pl.BlockSpec((tm, tn), lambda j, i: (i, j)),
        out_shape=jax.ShapeDtypeStruct((m, n), F32),
        compiler_params=_params(("arbitrary", "arbitrary")),
        name=name,
    )(a, w, res)


GU_ROW_CHUNK = 128


def _gu_kernel(h_ref, ssq_ref, wg_ref, wu_ref, cw_ref, cb_ref, o_ref, ga_ref, ua_ref, gb_ref, ub_ref, halo_ref,
               rinv_ref, *, tm, nj, tiles_per_seq):
    t = pl.program_id(0)
    tp = jnp.maximum(t - 1, 0)
    ip = tp // nj
    jp = tp % nj
    rc = min(GU_ROW_CHUNK, tm)

    @pl.when(t == 0)
    def _():
        gb_ref[...] = jnp.zeros_like(gb_ref)
        ub_ref[...] = jnp.zeros_like(ub_ref)
        halo_ref[...] = jnp.zeros_like(halo_ref)

    @pl.when(t % nj == 0)
    def _():
        rinv_ref[...] = _rinv_lanes(ssq_ref[...], h_ref.shape[1])

    def step(g_cur, u_cur, g_prev, u_prev):
        g_prev[0:SUBLANES, :] = jnp.where(ip % tiles_per_seq == 0, 0.0, halo_ref[jp])
        halo_ref[jp] = g_prev[tm:tm + SUBLANES, :]
        cw = cw_ref[...]
        for r in range(tm // rc):
            hr = h_ref[r * rc:(r + 1) * rc, :]
            rinv = rinv_ref[r * rc:(r + 1) * rc, :]
            g_cur[SUBLANES + r * rc:SUBLANES + (r + 1) * rc, :] = _scale_rows(_dot(hr, wg_ref[...]), rinv)
            u_cur[r * rc:(r + 1) * rc, :] = _scale_rows(_dot(hr, wu_ref[...]), rinv)
            acc = cb_ref[...]
            for sh in range(FFN_CONV):
                acc = acc + cw[FFN_CONV - 1 - sh:FFN_CONV - sh, :] * g_prev[pl.ds(SUBLANES - sh + r * rc, rc), :]
            o_ref[r * rc:(r + 1) * rc, :] = (_silu(acc) * u_prev[r * rc:(r + 1) * rc, :]).astype(o_ref.dtype)

    @pl.when(t % 2 == 0)
    def _():
        step(ga_ref, ua_ref, gb_ref, ub_ref)

    @pl.when(t % 2 == 1)
    def _():
        step(gb_ref, ub_ref, ga_ref, ua_ref)


def _gate_up(h, ssq, w_gu, layer, conv_w, conv_b, seq, tm, tn):
    m, k = h.shape
    nf = conv_w.shape[1]
    nj = nf // tn
    last = (m // tm) * nj - 1
    cur = lambda t: jnp.minimum(t, last)
    prev = lambda t: jnp.maximum(t - 1, 0)
    kern = functools.partial(_gu_kernel, tm=tm, nj=nj, tiles_per_seq=seq // tm)
    return pl.pallas_call(
        kern,
        grid=(last + 2,),
        in_specs=[pl.BlockSpec((tm, k), lambda t: (cur(t) // nj, 0), pipeline_mode=pl.Buffered(1)),
                  pl.BlockSpec((ssq.shape[0], tm, LANES), lambda t: (0, cur(t) // nj, 0)),
                  pl.BlockSpec((None, k, tn), lambda t: (layer, 0, cur(t) % nj)),
                  pl.BlockSpec((None, k, tn), lambda t: (layer, 0, cur(t) % nj + nj)),
                  pl.BlockSpec((FFN_CONV, tn), lambda t: (0, prev(t) % nj)),
                  pl.BlockSpec((1, tn), lambda t: (0, prev(t) % nj))],
        out_specs=pl.BlockSpec((tm, tn), lambda t: (prev(t) // nj, prev(t) % nj)),
        out_shape=jax.ShapeDtypeStruct((m, nf), BF16),
        scratch_shapes=[pltpu.VMEM((tm + SUBLANES, tn), F32), pltpu.VMEM((tm, tn), F32),
                        pltpu.VMEM((tm + SUBLANES, tn), F32), pltpu.VMEM((tm, tn), F32),
                        pltpu.VMEM((nj, SUBLANES, tn), F32), pltpu.VMEM((tm, LANES), F32)],
        compiler_params=_params(("arbitrary",)),
        name="ffn_gate_up_conv",
    )(h, ssq, w_gu, w_gu, conv_w, conv_b.reshape(1, nf))


def _ssd_kernel(z_ref, xs_ref, b_ref, c_ref, dt_ref, cwx_ref, cwb_ref, cwc_ref, cbx_ref, cbb_ref, cbc_ref,
                dtb_ref, alog_ref, dsk_ref, nw_ref, tri_ref, o_ref,
                extx_ref, extb_ref, extc_ref, state_ref, *, ts):
    s = pl.program_id(1)

    @pl.when(s == 0)
    def _():
        extx_ref[...] = jnp.zeros_like(extx_ref)
        extb_ref[...] = jnp.zeros_like(extb_ref)
        extc_ref[...] = jnp.zeros_like(extc_ref)
        state_ref[...] = jnp.zeros_like(state_ref)

    def conv_silu(u_ref, ext_ref, w_ref, bias_ref):
        ext_ref[0:SUBLANES, :] = ext_ref[ts:ts + SUBLANES, :]
        ext_ref[SUBLANES:, :] = u_ref[...]
        w = w_ref[...]
        acc = bias_ref[...] + w[SSD_CONV - 1:SSD_CONV, :] * u_ref[...]
        for sh in range(1, SSD_CONV):
            acc = acc + w[SSD_CONV - 1 - sh:SSD_CONV - sh, :] * ext_ref[pl.ds(SUBLANES - sh, ts), :]
        return _silu(acc)

    xs = conv_silu(xs_ref, extx_ref, cwx_ref, cbx_ref)
    bm = conv_silu(b_ref, extb_ref, cwb_ref, cbb_ref)
    cm = conv_silu(c_ref, extc_ref, cwc_ref, cbc_ref)
    dtr = dt_ref[...] + dtb_ref[...]
    dt = jnp.maximum(dtr, 0.0) + jnp.log1p(jnp.exp(-jnp.abs(dtr)))
    a = dt * (-jnp.exp(alog_ref[...]))
    z = z_ref[...]
    tri = tri_ref[...]
    dsk = dsk_ref[...]
    nw = nw_ref[...]
    li = lax.broadcasted_iota(jnp.int32, (SSD_CHUNK, SSD_CHUNK), 0)
    si = lax.broadcasted_iota(jnp.int32, (SSD_CHUNK, SSD_CHUNK), 1)
    causal = li >= si
    low_half = si < SSD_HEAD_DIM
    low_half_row = low_half[0:1, :]

    for ck in range(ts // SSD_CHUNK):
        r0 = ck * SSD_CHUNK
        a_c = a[r0:r0 + SSD_CHUNK]
        hi, mid, lo = _split3(a_c)
        acs = _dot(tri, hi) + _dot(tri, mid) + _dot(tri, lo)
        acs_t = acs.T
        dt_t = dt[r0:r0 + SSD_CHUNK].T
        a_last_b = jnp.broadcast_to(acs_t[:, SSD_CHUNK - 1:SSD_CHUNK], acs_t.shape)
        dtd_t = dt_t * jnp.exp(a_last_b - acs_t)
        chunk_decay = jnp.exp(acs[SSD_CHUNK - 1:SSD_CHUNK, :])
        for g in range(SSD_GROUPS):
            cg = cm[r0:r0 + SSD_CHUNK, g * SSD_STATE:(g + 1) * SSD_STATE]
            bg = bm[r0:r0 + SSD_CHUNK, g * SSD_STATE:(g + 1) * SSD_STATE]
            cb = jnp.where(causal, _dot_nt(cg.astype(BF16), bg.astype(BF16)), 0.0)
            bgt = bg.T
            ys = []
            for pr in range(SSD_HPG // 2):
                pair = (g * SSD_HPG) // 2 + pr
                c0 = pair * LANES
                xs_p = xs[r0:r0 + SSD_CHUNK, c0:c0 + LANES].astype(BF16)
                prev = state_ref[pair]
                rhs = jnp.concatenate([xs_p, prev.astype(BF16)], axis=0)
                y_h, st_h, cd_h = [], [], []
                for hh in (2 * pair, 2 * pair + 1):
                    colb = jnp.broadcast_to(acs[:, hh:hh + 1], (SSD_CHUNK, SSD_CHUNK))
                    seg = colb - acs_t[hh:hh + 1, :]
                    decay = jnp.exp(jnp.where(causal, seg, 0.0))
                    intra = cb * decay * dt_t[hh:hh + 1, :]
                    inter = cg * jnp.exp(colb)
                    lhs = jnp.concatenate([intra, inter], axis=1).astype(BF16)
                    y_h.append(_dot(lhs, rhs))
                    st_h.append(_dot((bgt * dtd_t[hh:hh + 1, :]).astype(BF16), xs_p))
                    cd_h.append(jnp.broadcast_to(chunk_decay[:, hh:hh + 1], (1, LANES)))
                ys.append(jnp.where(low_half, y_h[0], y_h[1]))
                state_ref[pair] = (prev * jnp.where(low_half_row, cd_h[0], cd_h[1])
                                   + jnp.where(low_half, st_h[0], st_h[1]))
            yg = jnp.concatenate(ys, axis=1)
            yg = yg + xs[r0:r0 + SSD_CHUNK, g * SSD_GW:(g + 1) * SSD_GW] * dsk[:, g * SSD_GW:(g + 1) * SSD_GW]
            u = yg * _silu(z[r0:r0 + SSD_CHUNK, g * SSD_GW:(g + 1) * SSD_GW])
            u = u * lax.rsqrt(jnp.mean(u * u, axis=-1, keepdims=True) + EPS)
            o_ref[r0:r0 + SSD_CHUNK, g * SSD_GW:(g + 1) * SSD_GW] = (
                u * nw[:, g * SSD_GW:(g + 1) * SSD_GW]).astype(o_ref.dtype)


def _pad_lanes(v, n=LANES):
    v = v.reshape(1, -1).astype(F32)
    return jnp.pad(v, ((0, 0), (0, n - v.shape[1])))


def _ssd_mixer(proj, bsz, seq, conv_w, conv_b, dt_bias, a_log, d_skip, norm_w, ts=256):
    nsteps = seq // ts
    row = lambda b, s: b * nsteps + s
    xw, bw, cw = conv_w[:, :SSD_WIDTH], conv_w[:, SSD_WIDTH:SSD_WIDTH + 512], conv_w[:, SSD_WIDTH + 512:]
    xb, bb, cb = conv_b[:SSD_WIDTH], conv_b[SSD_WIDTH:SSD_WIDTH + 512], conv_b[SSD_WIDTH + 512:]
    tri = jnp.asarray(np.tril(np.ones((SSD_CHUNK, SSD_CHUNK), np.float32)), BF16)
    const = lambda shape: pl.BlockSpec(shape, lambda b, s: (0,) * len(shape))
    kern = functools.partial(_ssd_kernel, ts=ts)
    return pl.pallas_call(
        kern,
        grid=(bsz, nsteps),
        in_specs=[pl.BlockSpec((ts, SSD_WIDTH), lambda b, s: (row(b, s), COL_Z // SSD_WIDTH)),
                  pl.BlockSpec((ts, SSD_WIDTH), lambda b, s: (row(b, s), COL_XS // SSD_WIDTH)),
                  pl.BlockSpec((ts, 512), lambda b, s: (row(b, s), COL_B // 512)),
                  pl.BlockSpec((ts, 512), lambda b, s: (row(b, s), COL_C // 512)),
                  pl.BlockSpec((ts, LANES), lambda b, s: (row(b, s), COL_DT // LANES)),
                  const((SSD_CONV, SSD_WIDTH)), const((SSD_CONV, 512)), const((SSD_CONV, 512)),
                  const((1, SSD_WIDTH)), const((1, 512)), const((1, 512)),
                  const((1, LANES)), const((1, LANES)), const((1, SSD_WIDTH)), const((1, SSD_WIDTH)),
                  const((SSD_CHUNK, SSD_CHUNK))],
        out_specs=pl.BlockSpec((ts, SSD_WIDTH), lambda b, s: (row(b, s), 0)),
        out_shape=jax.ShapeDtypeStruct((bsz * seq, SSD_WIDTH), BF16),
        scratch_shapes=[pltpu.VMEM((ts + SUBLANES, SSD_WIDTH), F32), pltpu.VMEM((ts + SUBLANES, 512), F32),
                        pltpu.VMEM((ts + SUBLANES, 512), F32),
                        pltpu.VMEM((SSD_HEADS // 2, SSD_STATE, 2 * SSD_HEAD_DIM), F32)],
        compiler_params=_params(("arbitrary", "arbitrary")),
        name="ssd_mixer",
    )(proj, proj, proj, proj, proj, xw, bw, cw, xb.reshape(1, -1), bb.reshape(1, -1), cb.reshape(1, -1),
      _pad_lanes(dt_bias), _pad_lanes(a_log), jnp.repeat(d_skip.astype(F32), SSD_HEAD_DIM).reshape(1, -1),
      norm_w.reshape(1, -1), tri)


def _hgrn_kernel(q_ref, f_ref, i_ref, g_ref, lbl_ref, nw_ref, cum_ref, o_ref, state_ref, *, ts, layer):
    s = pl.program_id(1)

    @pl.when(s == 0)
    def _():
        state_ref[...] = jnp.zeros_like(state_ref)

    nck = ts // HGRN_CHUNK
    lg = lbl_ref[...]
    e = jnp.exp(lg - jnp.max(lg, axis=0, keepdims=True))
    sm = e / jnp.sum(e, axis=0, keepdims=True)
    ridx = lax.broadcasted_iota(jnp.int32, lg.shape, 0)
    lb = jnp.sum(jnp.where((ridx >= 1) & (ridx <= layer), sm, 0.0), axis=0, keepdims=True)

    cum = cum_ref[...]
    nw = nw_ref[...]
    ti = lax.broadcasted_iota(jnp.int32, (ts, ts), 0)
    tj = lax.broadcasted_iota(jnp.int32, (ts, ts), 1)
    blockcausal = (ti // HGRN_CHUNK == tj // HGRN_CHUNK) & (ti >= tj)
    tok = lax.broadcasted_iota(jnp.int32, (1, ts), 1)

    def head(h):
        c0 = h * HGRN_DIM
        lbh = lb[:, c0:c0 + HGRN_DIM]
        f = lbh + (1.0 - lbh) * _sigmoid(f_ref[:, c0:c0 + HGRN_DIM])
        logf = jnp.log(f)
        k = 1.0 - f
        q = _silu(q_ref[:, c0:c0 + HGRN_DIM])
        v = i_ref[:, c0:c0 + HGRN_DIM]
        hi, mid, lo = _split3(logf)
        b = _dot(cum, hi) + _dot(cum, mid) + _dot(cum, lo)
        yield
        b3 = b.reshape(nck, HGRN_CHUNK, HGRN_DIM)
        bref = jnp.broadcast_to(b3[:, HGRN_CHUNK // 2:HGRN_CHUNK // 2 + 1, :], b3.shape).reshape(ts, HGRN_DIM)
        blast = jnp.broadcast_to(b3[:, HGRN_CHUNK - 1:HGRN_CHUNK, :], b3.shape).reshape(ts, HGRN_DIM)
        qe = (q * jnp.exp(b - bref)).astype(BF16)
        ke = (k * jnp.exp(bref - b)).astype(BF16)
        kl = (k * jnp.exp(blast - b)).astype(BF16)
        qb = (q * jnp.exp(b)).astype(BF16)
        vb = v.astype(BF16)
        att = _dot_nt(qe, ke)
        vt = v.T
        lhs = jnp.concatenate(
            [jnp.where(tok // HGRN_CHUNK == c, vt, 0.0) for c in range(nck)], axis=0).astype(BF16)
        st = _dot(lhs, kl)
        yield
        o = _dot(jnp.where(blockcausal, att, 0.0).astype(BF16), vb)
        state = state_ref[h]
        outs = []
        for c in range(nck):
            t0 = c * HGRN_CHUNK
            outs.append(_dot_nt(qb[t0:t0 + HGRN_CHUNK], state.astype(BF16)))
            cd = jnp.exp(blast[t0:t0 + 1, :])
            state = state * cd + st[c * HGRN_DIM:(c + 1) * HGRN_DIM]
            if c % 2 == 1:
                yield
        state_ref[h] = state
        o = o + jnp.concatenate(outs, axis=0)
        o = o * lax.rsqrt(jnp.mean(o * o, axis=-1, keepdims=True) + EPS) * nw
        o_ref[:, c0:c0 + HGRN_DIM] = (o * _silu(g_ref[:, c0:c0 + HGRN_DIM])).astype(o_ref.dtype)

    for h0 in range(0, HGRN_HEADS, HGRN_HEADS_INTERLEAVED):
        live = [head(h) for h in range(h0, h0 + HGRN_HEADS_INTERLEAVED)]
        while live:
            live = [gen for gen in live if next(gen, live) is not live]


def _hgrn_cum_matrix(ts):
    t = np.arange(ts)
    same = (t[:, None] // HGRN_CHUNK) == (t[None, :] // HGRN_CHUNK)
    return (same & (t[None, :] <= t[:, None])).astype(np.float32)


def _hgrn_mixer(proj, bsz, seq, lb_logits, norm_w, layer, ts=256):
    nsteps = seq // ts
    row = lambda b, s: b * nsteps + s
    depth = lb_logits.shape[0]
    cum = jnp.asarray(_hgrn_cum_matrix(ts), BF16)
    kern = functools.partial(_hgrn_kernel, ts=ts, layer=layer)
    blk = lambda col: pl.BlockSpec((ts, HGRN_WIDTH), lambda b, s: (row(b, s), col // HGRN_WIDTH))
    return pl.pallas_call(
        kern,
        grid=(bsz, nsteps),
        in_specs=[blk(COL_HQ), blk(COL_HF), blk(COL_HI), blk(COL_HG),
                  pl.BlockSpec((depth, HGRN_WIDTH), lambda b, s: (0, 0)),
                  pl.BlockSpec((1, HGRN_DIM), lambda b, s: (0, 0)),
                  pl.BlockSpec((ts, ts), lambda b, s: (0, 0))],
        out_specs=pl.BlockSpec((ts, HGRN_WIDTH), lambda b, s: (row(b, s), 0)),
        out_shape=jax.ShapeDtypeStruct((bsz * seq, HGRN_WIDTH), BF16),
        scratch_shapes=[pltpu.VMEM((HGRN_HEADS, HGRN_DIM, HGRN_DIM), F32)],
        compiler_params=_params(("arbitrary", "arbitrary")),
        name="hgrn2_mixer",
    )(proj, proj, proj, proj, lb_logits.astype(F32), norm_w.reshape(1, HGRN_DIM).astype(F32), cum)


def _rel_bucket_np(dist):
    max_exact = REL_BUCKETS // 2
    d = np.maximum(dist, 0)
    ratio = np.maximum(d, 1).astype(np.float32) / np.float32(max_exact)
    log_ratio = np.log(ratio).astype(np.float32) / np.float32(math.log(REL_MAX_DIST / max_exact))
    large = np.minimum(max_exact + (log_ratio * np.float32(REL_BUCKETS - max_exact)).astype(np.int32),
                       REL_BUCKETS - 1)
    return np.where(d < max_exact, d, large).astype(np.int32)


def _bias_expand_kernel(rb_ref, bmap_ref, o_ref):
    h = pl.program_id(0)
    bm = bmap_ref[...]
    out = jnp.full(bm.shape, NEG_INF, F32)
    for k in range(REL_BUCKETS):
        out = jnp.where(bm == k, rb_ref[k, h] * LOG2E, out)
    o_ref[...] = out


def _bias_expand(rel_bias, bmap, tr):
    rows, cols = bmap.shape
    return pl.pallas_call(
        _bias_expand_kernel,
        grid=(NSA_HEADS, rows // tr),
        in_specs=[pl.BlockSpec(memory_space=pltpu.SMEM),
                  pl.BlockSpec((tr, cols), lambda h, i: (i, 0))],
        out_specs=pl.BlockSpec((None, tr, cols), lambda h, i: (h, i, 0)),
        out_shape=jax.ShapeDtypeStruct((NSA_HEADS, rows, cols), F32),
        compiler_params=_params(("arbitrary", "arbitrary")),
        name="nsa_bias_expand",
    )(rel_bias.astype(F32), bmap)


def _nsa_bias_tables(rel_bias, seq):
    ncp = seq // CMP_STRIDE
    t = np.arange(seq)[:, None]
    cmp_end = np.arange(ncp)[None, :] * CMP_STRIDE + CMP_BLOCK - 1
    d = t - cmp_end
    bmap_cmp = np.where(d >= 0, _rel_bucket_np(d), -1).astype(np.int32)
    l = np.arange(Q_BLOCK)[:, None]
    j = np.arange(WIN_KEYS)[None, :]
    dist = l - j + WINDOW
    win = np.where((dist >= 0) & (dist < WINDOW), _rel_bucket_np(dist), -1)
    far = _rel_bucket_np(np.arange(Q_BLOCK + 1, 8 * seq))
    assert (far == far[0]).all(), "distances beyond one query block must share a single bucket"
    bmap_tab = np.concatenate([win, np.full((Q_BLOCK, Q_BLOCK), far[0])], axis=1).astype(np.int32)
    bias_cmp = _bias_expand(rel_bias, jnp.asarray(bmap_cmp), 512)
    bias_tab = _bias_expand(rel_bias, jnp.asarray(bmap_tab), Q_BLOCK)
    return bias_cmp, bias_tab


def _cmp_kernel(u_ref, pe_ref, w1_ref, w2_ref, o_ref, *, ncp):
    half = CMP_BLOCK // 2
    pe = pe_ref[...]
    top = jnp.zeros((ncp, CMP_HIDDEN), F32)
    bot = jnp.zeros((ncp, CMP_HIDDEN), F32)
    for l in range(half):
        x = u_ref[pl.ds(l, ncp, stride=half), :]
        top = top + _dot((x + pe[l:l + 1, :]).astype(BF16), w1_ref[l])
        bot = bot + _dot((x + pe[half + l:half + l + 1, :]).astype(BF16), w1_ref[half + l])
    hid = top + pltpu.roll(bot, ncp - 1, 0)
    o_ref[...] = _dot(_silu(hid).astype(BF16), w2_ref[...])


def _nsa_compress(proj, bsz, seq, pe, w1, w2):
    ncp = seq // CMP_STRIDE
    kern = functools.partial(_cmp_kernel, ncp=ncp)
    return pl.pallas_call(
        kern,
        grid=(bsz, 2, NSA_KV_HEADS),
        in_specs=[pl.BlockSpec((seq, NSA_HEAD_DIM), lambda b, t, h: (b, COL_KC // NSA_HEAD_DIM + NSA_KV_HEADS * t + h)),
                  pl.BlockSpec((None, CMP_BLOCK, NSA_HEAD_DIM), lambda b, t, h: (t, 0, 0)),
                  pl.BlockSpec((None, CMP_BLOCK, NSA_HEAD_DIM, CMP_HIDDEN), lambda b, t, h: (t, 0, 0, 0)),
                  pl.BlockSpec((None, CMP_HIDDEN, NSA_HEAD_DIM), lambda b, t, h: (t, 0, 0))],
        out_specs=pl.BlockSpec((None, None, None, ncp, NSA_HEAD_DIM), lambda b, t, h: (b, t, h, 0, 0)),
        out_shape=jax.ShapeDtypeStruct((bsz, 2, NSA_KV_HEADS, ncp, NSA_HEAD_DIM), F32),
        compiler_params=_params(("arbitrary", "arbitrary", "arbitrary")),
        name="nsa_compress",
    )(proj, pe.astype(F32), w1.reshape(2, CMP_BLOCK, NSA_HEAD_DIM, CMP_HIDDEN).astype(BF16), w2.astype(BF16))


def _softmax_start(s, v):
    m = jnp.max(s, axis=-1, keepdims=True)
    p = jnp.exp2(s - m)
    return m, jnp.sum(p, axis=-1, keepdims=True), _dot(p.astype(BF16), v)


def _softmax_update(s, v, carry):
    m, l, acc = carry
    m_new = jnp.maximum(m, jnp.max(s, axis=-1, keepdims=True))
    alpha = jnp.exp2(m - m_new)
    p = jnp.exp2(s - m_new)
    return m_new, alpha * l + jnp.sum(p, axis=-1, keepdims=True), alpha * acc + _dot(p.astype(BF16), v)


def _softmax_finish(carry):
    m, l, acc = carry
    return jnp.where(m > 0.5 * NEG_INF, acc / jnp.maximum(l, 1e-30), 0.0)


def _nsa_kernel(q_ref, gate_ref, kc_ref, vc_ref, ks_ref, vs_ref, kw_ref, vw_ref, bcmp_ref, btab_ref,
                kone_ref, c2s_ref, o_ref, kaug_ref, vsb_ref, kwb_ref, vwb_ref, sa_ref, sb_ref, *, n_sel, nqb):
    hk = pl.program_id(1)
    c0 = pl.program_id(2) * nqb
    rows = NSA_GQA * Q_BLOCK
    lane = lax.broadcasted_iota(jnp.int32, (Q_BLOCK, LANES), 1)

    @pl.when(c0 == 0)
    def _():
        kaug_ref[0:Q_BLOCK, 0:NSA_HEAD_DIM] = jnp.zeros((Q_BLOCK, NSA_HEAD_DIM), BF16)
        kaug_ref[Q_BLOCK:, 0:NSA_HEAD_DIM] = ks_ref[...].astype(BF16)
        kaug_ref[:, NSA_HEAD_DIM:] = kone_ref[...]
        vsb_ref[0:Q_BLOCK, :] = jnp.zeros((Q_BLOCK, NSA_HEAD_DIM), BF16)
        vsb_ref[Q_BLOCK:, :] = vs_ref[...].astype(BF16)
        kwb_ref[0:WINDOW, 0:NSA_HEAD_DIM] = jnp.zeros((WINDOW, NSA_HEAD_DIM), BF16)
        kwb_ref[WINDOW:, 0:NSA_HEAD_DIM] = kw_ref[...].astype(BF16)
        for r0 in range(0, WINDOW, Q_BLOCK):
            kwb_ref[r0:r0 + Q_BLOCK, NSA_HEAD_DIM:] = jnp.where(lane == AUG_PAD, 1.0, 0.0).astype(BF16)
        kwb_ref[WINDOW:, NSA_HEAD_DIM:] = jnp.zeros((kw_ref.shape[0], LANES), BF16)
        vwb_ref[0:WINDOW, :] = jnp.zeros((WINDOW, NSA_HEAD_DIM), BF16)
        vwb_ref[WINDOW:, :] = vw_ref[...].astype(BF16)

    gens = [_nsa_unit(c0 + u, q_ref[u * Q_BLOCK:(u + 1) * Q_BLOCK, :], bcmp_ref[:, u * Q_BLOCK:(u + 1) * Q_BLOCK, :],
                      btab_ref, kc_ref, vc_ref, c2s_ref, kaug_ref, vsb_ref, kwb_ref, vwb_ref, n_sel)
            for u in range(nqb)]
    units = [None] * nqb
    while any(un is None for un in units):
        for u, gen in enumerate(gens):
            if units[u] is None:
                units[u] = next(gen)
    far_blocks = FAR_KEYS // Q_BLOCK
    last_slab = (kaug_ref.shape[0] - Q_BLOCK) // FAR_KEYS - 1

    def far_values(slab):
        return vsb_ref[pl.ds(pl.multiple_of(slab * FAR_KEYS + Q_BLOCK, Q_BLOCK), FAR_KEYS), :]

    o_sel = []
    for u0 in range(0, nqb, NSA_FAR_GROUP):
        group = units[u0:u0 + NSA_FAR_GROUP]
        qaug_far = jnp.concatenate([un[3] for un in group], axis=0)
        carry = tuple(un[2] for un in group)
        n_slabs = (c0 + u0 + NSA_FAR_GROUP - 2 + far_blocks - 1) // far_blocks
        n_pairs = n_slabs // 2

        def far_logits(slab, qaug_far=qaug_far):
            r0 = pl.multiple_of(jnp.minimum(slab, last_slab) * FAR_KEYS + Q_BLOCK, Q_BLOCK)
            return _dot_nt(qaug_far, kaug_ref[pl.ds(r0, FAR_KEYS), :])

        def update(s_ref, slab, carry):
            v = far_values(slab)
            return tuple(_softmax_update(s_ref[i * rows:(i + 1) * rows, :], v, carry[i])
                         for i in range(NSA_FAR_GROUP))

        sa_ref[...] = far_logits(0)

        def far_body(k, carry, far_logits=far_logits, update=update):
            sb_ref[...] = far_logits(2 * k + 1)
            carry = update(sa_ref, 2 * k, carry)
            sa_ref[...] = far_logits(2 * k + 2)
            return update(sb_ref, 2 * k + 1, carry)

        carry = lax.fori_loop(0, n_pairs, far_body, carry)
        carry = lax.cond(n_slabs % 2 == 1,
                         lambda cr, n_pairs=n_pairs, update=update: update(sa_ref, 2 * n_pairs, cr),
                         lambda cr: cr, carry)
        o_sel += [_softmax_finish(cr) for cr in carry]

    for u in range(nqb):
        o_cmp, o_win = units[u][0], units[u][1]
        gates = _sigmoid(gate_ref[u * Q_BLOCK:(u + 1) * Q_BLOCK, :])
        for g in range(NSA_GQA):
            def gate_col(br):
                return jnp.sum(jnp.where(lane == hk * (3 * NSA_GQA) + g * 3 + br, gates, 0.0), axis=-1, keepdims=True)

            r0 = g * Q_BLOCK
            out = (gate_col(0) * o_cmp[r0:r0 + Q_BLOCK] + gate_col(1) * o_sel[u][r0:r0 + Q_BLOCK]
                   + gate_col(2) * o_win[r0:r0 + Q_BLOCK])
            o_ref[u * Q_BLOCK:(u + 1) * Q_BLOCK, g * NSA_HEAD_DIM:(g + 1) * NSA_HEAD_DIM] = out.astype(o_ref.dtype)


def _nsa_unit(c, q_in, bc_in, btab_ref, kc_ref, vc_ref, c2s_ref, kaug_ref, vsb_ref, kwb_ref, vwb_ref, n_sel):
    rows = NSA_GQA * Q_BLOCK
    lane = lax.broadcasted_iota(jnp.int32, (Q_BLOCK, LANES), 1)
    q = q_in * (NSA_HEAD_DIM ** -0.5 * LOG2E)
    q3 = jnp.concatenate([q[:, g * NSA_HEAD_DIM:(g + 1) * NSA_HEAD_DIM] for g in range(NSA_GQA)],
                         axis=0).astype(BF16)

    btab = btab_ref[...].reshape(rows, WIN_KEYS + Q_BLOCK)
    near0 = pl.multiple_of(c * Q_BLOCK, Q_BLOCK)

    qaug_win = jnp.concatenate(
        [q3, jnp.concatenate([jnp.where(lane == AUG_PAD, NEG_INF, 0.0).astype(BF16)] * NSA_GQA, axis=0)], axis=1)
    s_win = _dot_nt(qaug_win, kwb_ref[pl.ds(near0, WIN_KEYS), :]) + btab[:, 0:WIN_KEYS]
    yield None

    bc = bc_in.reshape(rows, bc_in.shape[-1])
    valid = bc > 0.5 * NEG_INF
    s = _dot_nt(q3, kc_ref[...].astype(BF16)) + bc
    yield None
    o_win = _softmax_finish(_softmax_start(s_win, vwb_ref[pl.ds(near0, WIN_KEYS), :]))
    yield None
    m = jnp.max(s, axis=-1, keepdims=True)
    p = jnp.where(valid, jnp.exp2(s - m), 0.0)
    p = p / jnp.maximum(jnp.sum(p, axis=-1, keepdims=True), 1e-30)
    o_cmp = _dot(p.astype(BF16), vc_ref[...].astype(BF16))

    psum = p[0:Q_BLOCK] + p[Q_BLOCK:2 * Q_BLOCK] + p[2 * Q_BLOCK:3 * Q_BLOCK]
    ph = psum.astype(BF16)
    plo = (psum - ph.astype(F32)).astype(BF16)
    c2s = c2s_ref[...]
    imp_t = _dot_nt(c2s, ph) + _dot_nt(c2s, plo)
    nsp = -(-n_sel // SUBLANES) * SUBLANES
    imp_t = imp_t[0:nsp]
    tq = c * Q_BLOCK + lax.broadcasted_iota(jnp.int32, (nsp, Q_BLOCK), 1)
    jj = lax.broadcasted_iota(jnp.int32, (nsp, Q_BLOCK), 0)
    valid_sel = (jj < n_sel) & (jj * SEL_BLOCK <= tq)
    back = tq // SEL_BLOCK - jj
    force = (jj == 0) | ((back >= 0) & (back < SEL_LOCAL))
    score = jnp.where(valid_sel, imp_t + jnp.where(force, FORCE_BONUS, 0.0), NEG_INF)
    yield None
    n_tiles = nsp // SUBLANES
    tiles = [score[v * SUBLANES:(v + 1) * SUBLANES] for v in range(n_tiles)]
    sub = lax.broadcasted_iota(jnp.int32, (SUBLANES, Q_BLOCK), 0)
    cnts = [jnp.zeros((SUBLANES, Q_BLOCK), F32) for _ in range(n_tiles)]
    for jp in range(n_sel):
        v0, r0 = divmod(jp, SUBLANES)
        r = tiles[v0][r0:r0 + 1, :]
        for v in range(n_tiles):
            gt = jnp.where(r > tiles[v], 1.0, 0.0)
            ge = jnp.where(r >= tiles[v], 1.0, 0.0)
            cnts[v] = cnts[v] + (gt if v < v0 else ge if v > v0 else jnp.where(sub > r0, ge, gt))
        if jp % (2 * SUBLANES) == 2 * SUBLANES - 1:
            yield None
    cnt = jnp.concatenate(cnts, axis=0)
    keep = valid_sel & (cnt < float(min(SEL_TOPK, n_sel)))

    ext = lax.broadcasted_iota(jnp.int32, (LANES - nsp, Q_BLOCK), 0) + nsp
    aug = jnp.concatenate([jnp.where(keep, 0.0, NEG_INF), jnp.where(ext == AUG_PAD, NEG_INF, 0.0)], axis=0)
    aug_near = aug.T
    aug_far = jnp.where((lane >= 2 * (c - 1)) & (lane < AUG_BIAS_HI), NEG_INF, aug_near)
    far_parts = []
    for g in range(NSA_GQA):
        bfar = btab[g * Q_BLOCK:(g + 1) * Q_BLOCK, WIN_KEYS:]
        hi = bfar.astype(BF16).astype(F32)
        far_parts.append(jnp.where(lane == AUG_BIAS_HI, hi, jnp.where(lane == AUG_BIAS_LO, bfar - hi, aug_far)))
    qaug_near = jnp.concatenate([q3, jnp.concatenate([aug_near.astype(BF16)] * NSA_GQA, axis=0)], axis=1)
    qaug_far = jnp.concatenate([q3, jnp.concatenate(far_parts, axis=0).astype(BF16)], axis=1)

    s = _dot_nt(qaug_near, kaug_ref[pl.ds(near0, NEAR_KEYS), :]) + btab[:, WIN_KEYS - NEAR_KEYS:WIN_KEYS]
    yield None
    carry = _softmax_start(s, vsb_ref[pl.ds(near0, NEAR_KEYS), :])
    yield o_cmp, o_win, carry, qaug_far


def _nsa_attention(proj, cmp_kv, bias_cmp, bias_tab, bsz, seq):
    nq = seq // Q_BLOCK
    ncp = seq // CMP_STRIDE
    n_sel = seq // SEL_BLOCK
    assert n_sel <= AUG_BIAS_HI and NSA_HEAD_DIM == LANES
    assert seq % (2 * FAR_KEYS) == 0, "far steps walk the keys two 512-key slabs at a time"
    key = np.arange(-Q_BLOCK, seq)[:, None]
    ln = np.arange(LANES)[None, :]
    kone = np.where(key < 0, ln == AUG_PAD,
                    ((ln < AUG_BIAS_HI) & (key // SEL_BLOCK == ln)) | (ln == AUG_BIAS_HI) | (ln == AUG_BIAS_LO))
    kone = jnp.asarray(kone, BF16)
    c_start = np.arange(ncp)[None, :] * CMP_STRIDE
    s_start = np.arange(LANES)[:, None] * SEL_BLOCK
    overlap = np.clip(np.minimum(c_start + CMP_BLOCK, s_start + SEL_BLOCK) - np.maximum(c_start, s_start), 0, None)
    overlap = np.where(np.arange(LANES)[:, None] < n_sel, overlap, 0)
    c2s_t = jnp.asarray(overlap / CMP_BLOCK, BF16)
    gqa_w = NSA_GQA * NSA_HEAD_DIM
    full = lambda col: pl.BlockSpec((seq, NSA_HEAD_DIM), lambda b, h, c: (b, col // NSA_HEAD_DIM + h))
    nqb = NSA_Q_BLOCKS_PER_STEP
    ns = nq // nqb
    qr = nqb * Q_BLOCK
    kern = functools.partial(_nsa_kernel, n_sel=n_sel, nqb=nqb)
    return pl.pallas_call(
        kern,
        grid=(bsz, NSA_KV_HEADS, ns),
        in_specs=[pl.BlockSpec((qr, gqa_w), lambda b, h, c: (b * ns + c, COL_Q // gqa_w + h)),
                  pl.BlockSpec((qr, LANES), lambda b, h, c: (b * ns + c, COL_GATE // LANES)),
                  pl.BlockSpec((None, None, None, ncp, NSA_HEAD_DIM), lambda b, h, c: (b, 0, h, 0, 0)),
                  pl.BlockSpec((None, None, None, ncp, NSA_HEAD_DIM), lambda b, h, c: (b, 1, h, 0, 0)),
                  full(COL_KS), full(COL_VS), full(COL_KW), full(COL_VW),
                  pl.BlockSpec((NSA_GQA, qr, ncp), lambda b, h, c: (h, c, 0)),
                  pl.BlockSpec((NSA_GQA, Q_BLOCK, WIN_KEYS + Q_BLOCK), lambda b, h, c: (h, 0, 0)),
                  pl.BlockSpec((seq + Q_BLOCK, LANES), lambda b, h, c: (0, 0)),
                  pl.BlockSpec((LANES, ncp), lambda b, h, c: (0, 0))],
        out_specs=pl.BlockSpec((qr, gqa_w), lambda b, h, c: (b * ns + c, h)),
        out_shape=jax.ShapeDtypeStruct((bsz * seq, NSA_WIDTH), BF16),
        scratch_shapes=[pltpu.VMEM((seq + Q_BLOCK, 2 * NSA_HEAD_DIM), BF16),
                        pltpu.VMEM((seq + Q_BLOCK, NSA_HEAD_DIM), BF16),
                        pltpu.VMEM((seq + WINDOW, 2 * NSA_HEAD_DIM), BF16),
                        pltpu.VMEM((seq + WINDOW, NSA_HEAD_DIM), BF16),
                        pltpu.VMEM((NSA_FAR_GROUP * NSA_GQA * Q_BLOCK, FAR_KEYS), F32),
                        pltpu.VMEM((NSA_FAR_GROUP * NSA_GQA * Q_BLOCK, FAR_KEYS), F32)],
        compiler_params=_params(("arbitrary", "arbitrary", "arbitrary")),
        name="nsa_attention",
    )(proj, proj, cmp_kv, cmp_kv, proj, proj, proj, proj, bias_cmp, bias_tab, kone, c2s_t)


_IN_OFF = tuple(int(v) for v in np.cumsum((0,) + IN_SIZES))
_W_IN_COPIES = ((_IN_OFF[0], SSD_WIDTH, COL_Z), (_IN_OFF[1], SSD_WIDTH, COL_XS),
                (_IN_OFF[1] + SSD_WIDTH, 512, COL_B), (_IN_OFF[1] + SSD_WIDTH + 512, 512, COL_C),
                (_IN_OFF[3], NSA_WIDTH, COL_Q), (_IN_OFF[4], 6 * NSA_KV_WIDTH, COL_KC),
                (_IN_OFF[11], 4 * HGRN_WIDTH, COL_HQ))
_W_IN_NARROW = ((_IN_OFF[2], IN_SIZES[2], COL_DT, COL_GATE - COL_DT), (_IN_OFF[10], IN_SIZES[10], COL_GATE, COL_HQ - COL_GATE))


def _relayout_kernel(w_ref, o_ref):
    for src, width, dst in _W_IN_COPIES:
        o_ref[:, dst:dst + width] = w_ref[:, src:src + width].astype(BF16)
    for src, valid, dst, padded in _W_IN_NARROW:
        tile = w_ref[:, src:src + LANES]
        lane = lax.broadcasted_iota(jnp.int32, tile.shape, 1)
        o_ref[:, dst:dst + LANES] = jnp.where(lane < valid, tile, 0.0).astype(BF16)
        if padded > LANES:
            o_ref[:, dst + LANES:dst + padded] = jnp.zeros((tile.shape[0], padded - LANES), BF16)


def _relayout_w_in(w, tr=128):
    depth, d, n = w.shape
    return pl.pallas_call(
        _relayout_kernel,
        grid=(depth, d // tr),
        in_specs=[pl.BlockSpec((None, tr, n), lambda l, i: (l, i, 0))],
        out_specs=pl.BlockSpec((None, tr, IN_PAD), lambda l, i: (l, i, 0)),
        out_shape=jax.ShapeDtypeStruct((depth, d, IN_PAD), BF16),
        compiler_params=_params(("parallel", "parallel")),
        name="w_in_relayout",
    )(w)


def _tiles(seq):
    return dict(in_proj=(1024, 1024), out_proj=(1024, 512), gate_up=(min(2048, seq), 256), down=(256, 1024))


def kernel(x, norm_mix_w, w_in, ssd_conv_w, ssd_conv_b, ssd_dt_bias, ssd_a_log, ssd_d, ssd_norm_w, nsa_cmp_pe, nsa_cmp_w1, nsa_cmp_w2, rel_bias, hgrn_lb_logits, hgrn_norm_w, w_out, norm_ffn_w, ffn_w_gu, ffn_conv_w, ffn_conv_b, ffn_w_down, norm_f_w):
    bsz, seq, d = x.shape
    depth = w_in.shape[0]
    xr = x.reshape(bsz * seq, d).astype(F32)
    bias_cmp, bias_tab = _nsa_bias_tables(rel_bias, seq)
    tiles = _tiles(seq)
    w_in_b = _relayout_w_in(jnp.pad(w_in.astype(BF16), ((0, 0), (0, 0), (0, -w_in.shape[2] % LANES))))
    w_out_b = w_out.astype(BF16)
    w_gu_b = ffn_w_gu.astype(BF16)
    w_down_b = ffn_w_down.astype(BF16)
    xw, ssq = _prenorm(xr, norm_mix_w[0])
    for l in range(depth):
        proj = _matmul(xw, ssq, w_in_b, l, F32, *tiles["in_proj"], name="in_proj")
        y_ssd = _ssd_mixer(proj, bsz, seq, ssd_conv_w[l].astype(F32), ssd_conv_b[l].astype(F32), ssd_dt_bias[l],
                           ssd_a_log[l], ssd_d[l], ssd_norm_w[l].astype(F32))
        cmp_kv = _nsa_compress(proj, bsz, seq, nsa_cmp_pe[l], nsa_cmp_w1[l], nsa_cmp_w2[l])
        y_nsa = _nsa_attention(proj, cmp_kv, bias_cmp, bias_tab, bsz, seq)
        y_hgrn = _hgrn_mixer(proj, bsz, seq, hgrn_lb_logits, hgrn_norm_w[l], l)
        xr, xw, ssq = _out_proj([y_ssd, y_nsa, y_hgrn], w_out_b, l, xr, norm_ffn_w[l], *tiles["out_proj"])
        act = _gate_up(xw, ssq, w_gu_b, l, ffn_conv_w[l].astype(F32), ffn_conv_b[l].astype(F32), seq,
                       *tiles["gate_up"])
        if l + 1 < depth:
            xr, xw, ssq = _down_proj_norm(act, w_down_b, l, xr, norm_mix_w[l + 1], *tiles["down"])
        else:
            xr = _matmul_residual_wres(act, w_down_b, l, xr, *tiles["down"], name="ffn_down")
    out = _rmsnorm(xr, norm_f_w, x.dtype)
    return out.reshape(bsz, seq, d)
```
